```python
import jax, jax.numpy as jnp
from jax import lax
import numpy as np

D_MODEL = 1024
BATCH = 1
SEQ = 16384
DEPTH = 4
DEC_BATCH = 8
DEC_SEQ = 64
PAST_LEN = 1024

CHUNK = 64
N_EVEN = (DEPTH + 1) // 2
N_ODD = DEPTH // 2
H_A = 4
DH_A = 128
D_A = H_A * DH_A
CONV_W = 4
H_B = 4
DH_B = 128
D_B = H_B * DH_B
ROPE_BASE = 10000.0
H_C = 8
DH_C = 128
D_C = H_C * DH_C
D_IN_EVEN = 3 * D_A + 2 * H_A + 4 * D_B
D_OUT_EVEN = D_A + D_B
D_IN_ODD = 4 * D_C
N_EXPERTS = 32
TOP_K = 4
D_FF = D_MODEL
SWIGLU_LIMIT = 7.0
SWIGLU_ALPHA = 1.702
MOE_BLOCK = 128
EPS = 1e-6

kernel_name = 'hybrid_mlstm_retention_hgrn2_moe_stream_step'


def rmsnorm(x, w):
    xf = x.astype(jnp.float32)
    y = xf * lax.rsqrt(jnp.mean(xf * xf, axis=-1, keepdims=True) + EPS)
    return (y * w.astype(jnp.float32)).astype(x.dtype)


def head_rmsnorm(x, w):
    B, T, H, d = x.shape
    xf = x.astype(jnp.float32)
    y = xf * lax.rsqrt(jnp.mean(xf * xf, axis=-1, keepdims=True) + EPS)
    return y.reshape(B, T, H * d) * w.astype(jnp.float32)


def chunk_len(T):
    L = min(CHUNK, T)
    assert T % L == 0
    return L


def to_chunks(x, L):
    B, T, H = x.shape[:3]
    rest = x.shape[3:]
    x = x.reshape((B, T // L, L, H) + rest)
    return x.transpose((1, 0, 3, 2) + tuple(range(4, 4 + len(rest))))


def from_chunks(y):
    nc, B, H, L, d = y.shape
    return y.transpose(1, 0, 3, 2, 4).reshape(B, nc * L, H, d)


def causal_mask(L):
    return jnp.tril(jnp.ones((L, L), dtype=bool))


def causal_conv(x, state, w, b):
    T = x.shape[1]
    xp = jnp.concatenate([state, x], axis=1)
    out = b + sum(xp[:, j:j + T] * w[j] for j in range(CONV_W))
    return out, xp[:, -(CONV_W - 1):]


def rotary(x, pos):
    half = x.shape[-1] // 2
    inv = ROPE_BASE ** (-jnp.arange(half, dtype=jnp.float32) / half)
    ang = pos[:, None] * inv[None, :]
    cos = jnp.cos(ang)[None, :, None, :]
    sin = jnp.sin(ang)[None, :, None, :]
    x1, x2 = x[..., :half], x[..., half:]
    return jnp.concatenate([x1 * cos - x2 * sin, x1 * sin + x2 * cos], axis=-1)


def mlstm_scan(q, k, v, logi, logf, C0, n0, m0):
    L = chunk_len(q.shape[1])
    mask = causal_mask(L)

    def step(carry, inp):
        C, n, m = carry
        qc, kc, vc, ic, fc = inp
        b = jnp.cumsum(fc, axis=-1)
        D = jnp.where(mask, b[..., :, None] - b[..., None, :] + ic[..., None, :], -jnp.inf)
        inter = b + m[..., None]
        m_t = jnp.maximum(inter, jnp.max(D, axis=-1))
        w_intra = jnp.exp(D - m_t[..., None])
        w_inter = jnp.exp(inter - m_t)
        A = jnp.einsum('bhtd,bhsd->bhts', qc, kc) * w_intra
        num = jnp.einsum('bhts,bhsv->bhtv', A, vc) + w_inter[..., None] * jnp.einsum('bhtd,bhdv->bhtv', qc, C)
        den = A.sum(-1) + w_inter * jnp.einsum('bhtd,bhd->bht', qc, n)
        h = num / jnp.maximum(jnp.abs(den), jnp.exp(-m_t))[..., None]
        m_new = m_t[..., -1]
        w_s = jnp.exp(b[..., -1:] - b + ic - m_new[..., None])
        decay = jnp.exp(b[..., -1] + m - m_new)
        C_new = decay[..., None, None] * C + jnp.einsum('bhs,bhsd,bhsv->bhdv', w_s, kc, vc)
        n_new = decay[..., None] * n + jnp.einsum('bhs,bhsd->bhd', w_s, kc)
        return (C_new, n_new, m_new), h

    xs = (to_chunks(q, L), to_chunks(k, L), to_chunks(v, L), to_chunks(logi, L), to_chunks(logf, L))
    (C, n, m), hs = lax.scan(step, (C0, n0, m0), xs)
    return from_chunks(hs), C, n, m


def retention_scan(q, k, v, S0, log_gamma):
    L = chunk_len(q.shape[1])
    mask = causal_mask(L)
    pos = jnp.arange(L, dtype=jnp.float32)
    lg = log_gamma[:, None]
    decay_in = jnp.where(mask, jnp.exp((pos[:, None] - pos[None, :])[None] * lg[..., None]), 0.0)
    inter_w = jnp.exp((pos + 1.0)[None, :] * lg)
    state_w = jnp.exp((L - 1.0 - pos)[None, :] * lg)
    chunk_decay = jnp.exp(L * log_gamma)

    def step(S, inp):
        qc, kc, vc = inp
        A = jnp.einsum('bhtd,bhsd->bhts', qc, kc) * decay_in
        o = jnp.einsum('bhts,bhsv->bhtv', A, vc) + inter_w[..., None] * jnp.einsum('bhtd,bhdv->bhtv', qc, S)
        S_new = chunk_decay[:, None, None] * S + jnp.einsum('hs,bhsd,bhsv->bhdv', state_w, kc, vc)
        return S_new, o

    S, os_ = lax.scan(step, S0, (to_chunks(q, L), to_chunks(k, L), to_chunks(v, L)))
    return from_chunks(os_), S


def hgrn2_scan(q, k, v, logf, S0):
    L = chunk_len(q.shape[1])
    mask = causal_mask(L)[:, :, None]

    def step(S, inp):
        qc, kc, vc, gc = inp
        Bc = jnp.cumsum(gc, axis=2)
        diff = jnp.where(mask, Bc[:, :, :, None, :] - Bc[:, :, None, :, :], -jnp.inf)
        A = jnp.einsum('bhtc,bhsc,bhtsc->bhts', qc, kc, jnp.exp(diff))
        o = jnp.einsum('bhts,bhsv->bhtv', A, vc) + jnp.einsum('bhtc,bhcv->bhtv', qc * jnp.exp(Bc), S)
        last = Bc[:, :, -1]
        S_new = jnp.exp(last)[..., None] * S + jnp.einsum('bhsc,bhsv->bhcv', kc * jnp.exp(last[:, :, None, :] - Bc), vc)
        return S_new, o

    S, os_ = lax.scan(step, S0, (to_chunks(q, L), to_chunks(k, L), to_chunks(v, L), to_chunks(logf, L)))
    return from_chunks(os_), S


def even_mixer(h, pos0, C, n, m, conv, S, p, j):
    f32 = jnp.float32
    B, T, _ = h.shape
    proj = (h @ p['w_in_even'][j]).astype(f32)
    splits = np.cumsum([D_A, D_A, D_A, H_A, H_A, D_B, D_B, D_B]).tolist()
    xm, va, oa, ia, fa, qb, kb, vb, gb = jnp.split(proj, splits, axis=-1)
    xc_pre, conv_new = causal_conv(xm, conv.astype(f32), p['w_mlstm_conv'][j].astype(f32), p['b_mlstm_conv'][j].astype(f32))
    xc = jax.nn.silu(xc_pre)
    xch = xc.reshape(B, T, H_A, DH_A)
    qa = jnp.einsum('bthd,hde->bthe', xch, p['w_mlstm_q'][j].astype(f32))
    ka = jnp.einsum('bthd,hde->bthe', xch, p['w_mlstm_k'][j].astype(f32)) * (DH_A ** -0.5)
    logi = ia + p['b_mlstm_i'][j].astype(f32)
    logf = jax.nn.log_sigmoid(fa + p['b_mlstm_f'][j].astype(f32))
    ha, C_new, n_new, m_new = mlstm_scan(qa, ka, va.reshape(B, T, H_A, DH_A), logi, logf,
                                         C.astype(f32), n.astype(f32), m.astype(f32))
    out_a = head_rmsnorm(jax.nn.sigmoid(oa).reshape(B, T, H_A, DH_A) * ha, p['mlstm_norm_w'][j]) \
        + p['mlstm_skip'][j].astype(f32) * xc
    pos = pos0 + jnp.arange(T, dtype=f32)
    qr = rotary(qb.reshape(B, T, H_B, DH_B), pos)
    kr = rotary(kb.reshape(B, T, H_B, DH_B), pos) * (DH_B ** -0.5)
    log_gamma = jnp.log1p(-jnp.exp2(-5.0 - jnp.arange(H_B, dtype=f32)))
    hb, S_new = retention_scan(qr, kr, vb.reshape(B, T, H_B, DH_B), S.astype(f32), log_gamma)
    out_b = head_rmsnorm(hb, p['ret_norm_w'][j]) * jax.nn.silu(gb)
    y = jnp.concatenate([out_a, out_b], axis=-1).astype(h.dtype) @ p['w_out_even'][j]
    return y, C_new, n_new, m_new, conv_new, S_new


def odd_mixer(h, S, lb, p, j):
    f32 = jnp.float32
    B, T, _ = h.shape
    proj = (h @ p['w_in_odd'][j]).astype(f32)
    qc, fc, ic, gc = jnp.split(proj, 4, axis=-1)
    logf = jnp.logaddexp(jnp.log(lb), jnp.log1p(-lb) + jax.nn.log_sigmoid(fc))
    kc = -jnp.expm1(logf)
    shp = (B, T, H_C, DH_C)
    o, S_new = hgrn2_scan(qc.reshape(shp), kc.reshape(shp), ic.reshape(shp), logf.reshape(shp), S.astype(f32))
    y = (head_rmsnorm(o, p['hgrn_norm_w'][j]) * jax.nn.sigmoid(gc)).astype(h.dtype) @ p['w_out_odd'][j]
    return y, S_new


def moe_ffn(h, w_r, b_r, w_gu, b_gu, w_dn, b_dn):
    B, T, D = h.shape
    x = h.reshape(-1, D)
    n_tok = x.shape[0]
    logits = (x @ w_r + b_r).astype(jnp.float32)
    top_val, top_idx = lax.top_k(logits, TOP_K)
    gate = jax.nn.softmax(top_val, axis=-1)
    n_assign = n_tok * TOP_K
    e_flat = top_idx.reshape(-1)
    tok_flat = jnp.arange(n_assign, dtype=jnp.int32) // TOP_K
    g_flat = gate.reshape(-1)
    order = jnp.argsort(e_flat)
    e_sorted = e_flat[order]
    counts = jnp.zeros((N_EXPERTS,), jnp.int32).at[e_flat].add(1)
    starts = jnp.cumsum(counts) - counts
    padded = (counts + MOE_BLOCK - 1) // MOE_BLOCK * MOE_BLOCK
    pends = jnp.cumsum(padded)
    pstarts = pends - padded
    dest = pstarts[e_sorted] + (jnp.arange(n_assign, dtype=jnp.int32) - starts[e_sorted])
    n_blocks = -(-n_assign // MOE_BLOCK) + N_EXPERTS
    n_rows = n_blocks * MOE_BLOCK
    tok_buf = jnp.full((n_rows,), n_tok, jnp.int32).at[dest].set(tok_flat[order])
    g_buf = jnp.zeros((n_rows,), jnp.float32).at[dest].set(g_flat[order])
    blk_expert = jnp.minimum(jnp.searchsorted(pends, jnp.arange(n_blocks, dtype=jnp.int32) * MOE_BLOCK, side='right'),
                             N_EXPERTS - 1)
    x_pad = jnp.concatenate([x, jnp.zeros((1, D), x.dtype)], axis=0)

    def expert_block(args):
        tok, e = args
        gu = (x_pad[tok] @ w_gu[e] + b_gu[e]).astype(jnp.float32)
        g, u = gu[:, :D_FF], gu[:, D_FF:]
        g = jnp.minimum(g, SWIGLU_LIMIT)
        u = jnp.clip(u, -SWIGLU_LIMIT, SWIGLU_LIMIT)
        act = (u + 1.0) * (g * jax.nn.sigmoid(SWIGLU_ALPHA * g))
        return (act.astype(w_dn.dtype) @ w_dn[e] + b_dn[e]).astype(jnp.float32)

    y_blocks = lax.map(expert_block, (tok_buf.reshape(n_blocks, MOE_BLOCK), blk_expert))
    y = jnp.zeros((n_tok + 1, D), jnp.float32).at[tok_buf].add(y_blocks.reshape(n_rows, D) * g_buf[:, None])
    return y[:n_tok].reshape(B, T, D).astype(h.dtype)


def trunk(x, c, pos0, C_st, n_st, m_st, conv_st, ret_st, hgrn_st, p):
    c_act = jax.nn.silu(c.astype(jnp.float32))
    lb_p = jax.nn.softmax(p['hgrn_lb_logits'].astype(jnp.float32), axis=0)
    lbs = jnp.cumsum(lb_p, axis=0) - lb_p[0]
    even_states = []
    odd_states = []
    for l in range(DEPTH):
        mod = c_act @ p['w_ada'][l] + p['b_ada'][l]
        sh_m, sc_m, g_m, sh_f, sc_f, g_f = [t[:, None, :] for t in jnp.split(mod, 6, axis=-1)]
        h = rmsnorm(x, p['norm_mix_w'][l]) * (1.0 + sc_m) + sh_m
        j = l // 2
        if l % 2 == 0:
            y, C_new, n_new, m_new, conv_new, S_new = even_mixer(h, pos0, C_st[j], n_st[j], m_st[j], conv_st[j],
                                                                 ret_st[j], p, j)
            even_states.append((C_new, n_new, m_new, conv_new, S_new))
        else:
            y, S_new = odd_mixer(h, hgrn_st[j], lbs[l], p, j)
            odd_states.append(S_new)
        x = x + g_m * y
        h = rmsnorm(x, p['norm_ffn_w'][l]) * (1.0 + sc_f) + sh_f
        x = x + g_f * moe_ffn(h, p['moe_router_w'][l], p['moe_router_b'][l], p['moe_w_gate_up'][l],
                              p['moe_b_gate_up'][l], p['moe_w_down'][l], p['moe_b_down'][l])
    y = rmsnorm(x, p['final_norm_w'])
    C_out = jnp.stack([s[0] for s in even_states])
    n_out = jnp.stack([s[1] for s in even_states])
    m_out = jnp.stack([s[2] for s in even_states])
    conv_out = jnp.stack([s[3] for s in even_states])
    ret_out = jnp.stack([s[4] for s in even_states])
    hgrn_out = jnp.stack(odd_states)
    return y, C_out, n_out, m_out, conv_out, ret_out, hgrn_out


def _nrm(key, shape, scale):
    return scale * jax.random.normal(key, shape, jnp.float32)


def setup_inputs(seed: int = 0) -> dict:
    key = jax.random.key(seed)
    ks = jax.random.split(key, 40)
    f = D_MODEL ** -0.5
    return {
        'x_prompt': _nrm(ks[0], (BATCH, SEQ, D_MODEL), 1.0),
        'x_sample': _nrm(ks[1], (DEC_BATCH, DEC_SEQ, D_MODEL), 1.0),
        'c_prompt': _nrm(ks[2], (BATCH, D_MODEL), 1.0),
        'c_sample': _nrm(ks[3], (DEC_BATCH, D_MODEL), 1.0),
        'state_mlstm_C': _nrm(ks[4], (N_EVEN, DEC_BATCH, H_A, DH_A, DH_A), 0.5),
        'state_mlstm_n': _nrm(ks[5], (N_EVEN, DEC_BATCH, H_A, DH_A), 0.5),
        'state_mlstm_m': _nrm(ks[6], (N_EVEN, DEC_BATCH, H_A), 1.0),
        'state_mlstm_conv': _nrm(ks[7], (N_EVEN, DEC_BATCH, CONV_W - 1, D_A), 1.0),
        'state_ret_S': _nrm(ks[8], (N_EVEN, DEC_BATCH, H_B, DH_B, DH_B), 0.5),
        'state_hgrn_S': _nrm(ks[9], (N_ODD, DEC_BATCH, H_C, DH_C, DH_C), 0.5),
        'w_ada': _nrm(ks[10], (DEPTH, D_MODEL, 6 * D_MODEL), 0.5 * f),
        'b_ada': _nrm(ks[11], (DEPTH, 6 * D_MODEL), 0.01),
        'norm_mix_w': 1.0 + _nrm(ks[12], (DEPTH, D_MODEL), 0.01),
        'norm_ffn_w': 1.0 + _nrm(ks[13], (DEPTH, D_MODEL), 0.01),
        'final_norm_w': 1.0 + _nrm(ks[14], (D_MODEL,), 0.01),
        'w_in_even': _nrm(ks[15], (N_EVEN, D_MODEL, D_IN_EVEN), f),
        'b_mlstm_i': _nrm(ks[16], (N_EVEN, H_A), 0.1),
        'b_mlstm_f': jnp.linspace(3.0, 6.0, H_A, dtype=jnp.float32) + _nrm(ks[17], (N_EVEN, H_A), 0.1),
        'w_mlstm_conv': _nrm(ks[18], (N_EVEN, CONV_W, D_A), CONV_W ** -0.5),
        'b_mlstm_conv': _nrm(ks[19], (N_EVEN, D_A), 0.01),
        'w_mlstm_q': _nrm(ks[20], (N_EVEN, H_A, DH_A, DH_A), DH_A ** -0.5),
        'w_mlstm_k': _nrm(ks[21], (N_EVEN, H_A, DH_A, DH_A), DH_A ** -0.5),
        'mlstm_skip': 1.0 + _nrm(ks[22], (N_EVEN, D_A), 0.1),
        'mlstm_norm_w': 1.0 + _nrm(ks[23], (N_EVEN, D_A), 0.01),
        'ret_norm_w': 1.0 + _nrm(ks[24], (N_EVEN, D_B), 0.01),
        'w_out_even': _nrm(ks[25], (N_EVEN, D_OUT_EVEN, D_MODEL), D_OUT_EVEN ** -0.5),
        'w_in_odd': _nrm(ks[26], (N_ODD, D_MODEL, D_IN_ODD), f),
        'hgrn_lb_logits': _nrm(ks[27], (DEPTH, D_C), 0.5),
        'hgrn_norm_w': 1.0 + _nrm(ks[28], (N_ODD, D_C), 0.01),
        'w_out_odd': _nrm(ks[29], (N_ODD, D_C, D_MODEL), D_C ** -0.5),
        'moe_router_w': _nrm(ks[30], (DEPTH, D_MODEL, N_EXPERTS), f),
        'moe_router_b': _nrm(ks[31], (DEPTH, N_EXPERTS), 0.01),
        'moe_w_gate_up': _nrm(ks[32], (DEPTH, N_EXPERTS, D_MODEL, 2 * D_FF), f),
        'moe_b_gate_up': _nrm(ks[33], (DEPTH, N_EXPERTS, 2 * D_FF), 0.01),
        'moe_w_down': _nrm(ks[34], (DEPTH, N_EXPERTS, D_FF, D_MODEL), D_FF ** -0.5),
        'moe_b_down': _nrm(ks[35], (DEPTH, N_EXPERTS, D_MODEL), 0.01),
    }


def reference(x_prompt, x_sample, c_prompt, c_sample, state_mlstm_C, state_mlstm_n, state_mlstm_m,
              state_mlstm_conv, state_ret_S, state_hgrn_S, w_ada, b_ada, norm_mix_w, norm_ffn_w, final_norm_w,
              w_in_even, b_mlstm_i, b_mlstm_f, w_mlstm_conv, b_mlstm_conv, w_mlstm_q, w_mlstm_k, mlstm_skip,
              mlstm_norm_w, ret_norm_w, w_out_even, w_in_odd, hgrn_lb_logits, hgrn_norm_w, w_out_odd,
              moe_router_w, moe_router_b, moe_w_gate_up, moe_b_gate_up, moe_w_down, moe_b_down):
    p = dict(w_ada=w_ada, b_ada=b_ada, norm_mix_w=norm_mix_w, norm_ffn_w=norm_ffn_w, final_norm_w=final_norm_w,
             w_in_even=w_in_even, b_mlstm_i=b_mlstm_i, b_mlstm_f=b_mlstm_f, w_mlstm_conv=w_mlstm_conv,
             b_mlstm_conv=b_mlstm_conv, w_mlstm_q=w_mlstm_q, w_mlstm_k=w_mlstm_k, mlstm_skip=mlstm_skip,
             mlstm_norm_w=mlstm_norm_w, ret_norm_w=ret_norm_w, w_out_even=w_out_even, w_in_odd=w_in_odd,
             hgrn_lb_logits=hgrn_lb_logits, hgrn_norm_w=hgrn_norm_w, w_out_odd=w_out_odd,
             moe_router_w=moe_router_w, moe_router_b=moe_router_b, moe_w_gate_up=moe_w_gate_up,
             moe_b_gate_up=moe_b_gate_up, moe_w_down=moe_w_down, moe_b_down=moe_b_down)
    f32 = jnp.float32
    bp = x_prompt.shape[0]
    z_C = jnp.zeros((N_EVEN, bp, H_A, DH_A, DH_A), f32)
    z_n = jnp.zeros((N_EVEN, bp, H_A, DH_A), f32)
    z_m = jnp.zeros((N_EVEN, bp, H_A), f32)
    z_conv = jnp.zeros((N_EVEN, bp, CONV_W - 1, D_A), f32)
    z_ret = jnp.zeros((N_EVEN, bp, H_B, DH_B, DH_B), f32)
    z_hgrn = jnp.zeros((N_ODD, bp, H_C, DH_C, DH_C), f32)
    y_prompt, pC, pn, pm, pconv, pret, phgrn = trunk(x_prompt, c_prompt, 0, z_C, z_n, z_m, z_conv, z_ret, z_hgrn, p)
    y_sample, sC, sn, sm, sconv, sret, shgrn = trunk(x_sample, c_sample, PAST_LEN, state_mlstm_C, state_mlstm_n,
                                                     state_mlstm_m, state_mlstm_conv, state_ret_S, state_hgrn_S, p)
    return (y_prompt, y_sample, pC, pn, pm, pconv, pret, phgrn, sC, sn, sm, sconv, sret, shgrn)
```

```python
import functools
import math

import jax
import jax.numpy as jnp
from jax import lax
from jax.experimental import pallas as pl
from jax.experimental.pallas import tpu as pltpu

F32 = jnp.float32
BF16 = jnp.bfloat16

CHUNK = 64
LANES = 128
SUBLANES = 8
D_MODEL = 1024
N_HEADS_EVEN = 4
N_HEADS_ODD = 8
CONV_W = 4
N_EXPERTS = 32
TOP_K = 4
SWIGLU_LIMIT = 7.0
SWIGLU_ALPHA = 1.702
EPS = 1e-6
ROPE_BASE = 10000.0

TOKEN_TILE = 512
INPROJ_TILE = 256
COMBINE_TILE = 256
EXPERT_ROWS = 256
VMEM_LIMIT = 56 * 1024 * 1024


def _cparams(sem, vmem=VMEM_LIMIT):
    return pltpu.CompilerParams(dimension_semantics=sem, vmem_limit_bytes=vmem)


def _dot(a, b):
    return jnp.dot(a, b, preferred_element_type=F32)


def _dot_nt(a, b):
    return lax.dot_general(a, b, (((1,), (1,)), ((), ())), preferred_element_type=F32)


def _dot_tn(a, b):
    return lax.dot_general(a, b, (((0,), (0,)), ((), ())), preferred_element_type=F32)


def _split3(x):
    p1 = x.astype(BF16)
    r1 = x - p1.astype(F32)
    p2 = r1.astype(BF16)
    p3 = (r1 - p2.astype(F32)).astype(BF16)
    return p1, p2, p3


def _dot3(a, b):
    ah = a.astype(BF16)
    al = (a - ah.astype(F32)).astype(BF16)
    bh = b.astype(BF16)
    bl = (b - bh.astype(F32)).astype(BF16)
    return _dot(ah, bh) + (_dot(ah, bl) + _dot(al, bh))


def _dot_sel_lhs(sel_bf16, x):
    p1, p2, p3 = _split3(x)
    return _dot(sel_bf16, p1) + (_dot(sel_bf16, p2) + _dot(sel_bf16, p3))


def _dot_sel_nt(sel_bf16, x):
    p1, p2, p3 = _split3(x)
    return _dot_nt(sel_bf16, p1) + (_dot_nt(sel_bf16, p2) + _dot_nt(sel_bf16, p3))


def _sigmoid(x):
    return 1.0 / (1.0 + jnp.exp(-x))


def _log_sigmoid(x):
    return jnp.minimum(x, 0.0) - jnp.log1p(jnp.exp(-jnp.abs(x)))


def _logaddexp(a, b):
    return jnp.maximum(a, b) + jnp.log1p(jnp.exp(-jnp.abs(a - b)))


def _rms(x):
    return x * lax.rsqrt(jnp.mean(x * x, axis=-1, keepdims=True) + EPS)


def _chunk_tri(n):
    r = lax.broadcasted_iota(jnp.int32, (n, n), 0)
    c = lax.broadcasted_iota(jnp.int32, (n, n), 1)
    return jnp.where((r // CHUNK == c // CHUNK) & (c <= r), 1.0, 0.0).astype(BF16)


def _seq_row(tile_idx, chunks_per_tile, c, n_prompt_chunks):
    return jnp.maximum(tile_idx * chunks_per_tile + c - (n_prompt_chunks - 1), 0)


def _ada_kernel(c_ref, w_ref, b_ref, o_ref):
    c = c_ref[...]
    o_ref[0] = _dot3(c * _sigmoid(c), w_ref[0]) + b_ref[0]


def _ada_call(c_all, w_ada, b_ada):
    depth = w_ada.shape[0]
    nrow = c_all.shape[0]
    ncol = w_ada.shape[2] // D_MODEL
    return pl.pallas_call(
        _ada_kernel,
        grid=(depth, ncol),
        in_specs=[pl.BlockSpec((nrow, D_MODEL), lambda l, j: (0, 0)),
                  pl.BlockSpec((1, D_MODEL, D_MODEL), lambda l, j: (l, 0, j)),
                  pl.BlockSpec((1, 1, D_MODEL), lambda l, j: (l, 0, j))],
        out_specs=pl.BlockSpec((1, nrow, D_MODEL), lambda l, j: (l, 0, j)),
        out_shape=jax.ShapeDtypeStruct((depth, nrow, ncol * D_MODEL), F32),
        compiler_params=_cparams(("arbitrary", "arbitrary")),
        name="ada",
    )(c_all, w_ada, b_ada.reshape(depth, 1, -1))


def _inproj_kernel(*refs, tm, ng, npc, has_gates):
    if has_gates:
        x_ref, mod_ref, nw_ref, w_ref, wg_ref, proj_ref, gates_ref, h_scr = refs
    else:
        x_ref, mod_ref, nw_ref, w_ref, proj_ref, h_scr = refs
    i = pl.program_id(0)
    nch = tm // CHUNK
    for c in range(nch):
        seq = _seq_row(i, nch, c, npc)
        sh = mod_ref[pl.ds(seq, 1), 0:D_MODEL]
        sc = mod_ref[pl.ds(seq, 1), D_MODEL:2 * D_MODEL]
        xc = x_ref[c * CHUNK:(c + 1) * CHUNK, :]
        h_scr[c * CHUNK:(c + 1) * CHUNK, :] = _rms(xc) * nw_ref[...] * (1.0 + sc) + sh
    h = h_scr[...]
    hb = h.astype(BF16)
    for g in range(0, ng, 4):
        res = _dot(hb, w_ref[:, g * LANES:(g + 4) * LANES])
        for jj in range(4):
            proj_ref[g + jj] = res[:, jj * LANES:(jj + 1) * LANES]
    if has_gates:
        gates_ref[...] = _dot3(h, wg_ref[...])


def _inproj_call(x, mod_l, nw, w_main, w_gate, npc):
    ttot = x.shape[0]
    tm = INPROJ_TILE
    ng = w_main.shape[1] // LANES
    has_gates = w_gate is not None
    in_specs = [pl.BlockSpec((tm, D_MODEL), lambda i: (i, 0)),
                pl.BlockSpec(mod_l.shape, lambda i: (0, 0)),
                pl.BlockSpec((1, D_MODEL), lambda i: (0, 0)),
                pl.BlockSpec(w_main.shape, lambda i: (0, 0))]
    out_specs = [pl.BlockSpec((ng, tm, LANES), lambda i: (0, i, 0))]
    out_shape = [jax.ShapeDtypeStruct((ng, ttot, LANES), F32)]
    args = [x, mod_l, nw, w_main]
    if has_gates:
        in_specs.append(pl.BlockSpec(w_gate.shape, lambda i: (0, 0)))
        out_specs.append(pl.BlockSpec((tm, LANES), lambda i: (i, 0)))
        out_shape.append(jax.ShapeDtypeStruct((ttot, LANES), F32))
        args.append(w_gate)
    return pl.pallas_call(
        functools.partial(_inproj_kernel, tm=tm, ng=ng, npc=npc, has_gates=has_gates),
        grid=(ttot // tm,),
        in_specs=in_specs, out_specs=out_specs, out_shape=out_shape,
        scratch_shapes=[pltpu.VMEM((tm, D_MODEL), F32)],
        compiler_params=_cparams(("arbitrary",)),
        name="inproj",
    )(*args)


_G_XM, _G_VA, _G_OA, _G_QB, _G_KB, _G_VB, _G_GB = 0, 4, 8, 12, 16, 20, 24


def _even_scan_kernel(proj_ref, gates_ref, cos_ref, sin_ref,
                      c0_ref, n0_ref, m0_ref, conv0_ref, s0_ref,
                      cw_ref, cb_ref, wq_ref, wk_ref, gbias_ref, skip_ref, nwa_ref, nwb_ref,
                      u_ref, co_ref, no_ref, mo_ref, convo_ref, so_ref,
                      c_scr, n_scr, m_scr, conv_scr, s_scr,
                      xbuf, xc_scr, q_scr, k_scr, qr_scr, kr_scr,
                      gl_scr, bc_scr, rowi_scr, rowb_scr, dec_scr, *, ts):
    H = N_HEADS_EVEN
    j = pl.program_id(1)
    nj = pl.num_programs(1)
    nc = ts // CHUNK

    @pl.when(j == 0)
    def _():
        c_scr[...] = c0_ref[0]
        n_scr[...] = n0_ref[0]
        m_scr[...] = m0_ref[0]
        conv_scr[...] = conv0_ref[0]
        s_scr[...] = s0_ref[0]

    for g in range(H):
        lo, hi = g * LANES, (g + 1) * LANES
        x_g = proj_ref[_G_XM + g]
        xbuf[0:SUBLANES, :] = conv_scr[:, lo:hi]
        xbuf[SUBLANES:SUBLANES + ts, :] = x_g
        acc = cb_ref[:, lo:hi] + cw_ref[CONV_W - 1:CONV_W, lo:hi] * x_g
        for t in range(CONV_W - 1):
            off = SUBLANES - (CONV_W - 1) + t
            acc = acc + cw_ref[t:t + 1, lo:hi] * xbuf[off:off + ts, :]
        conv_scr[:, lo:hi] = xbuf[ts:ts + SUBLANES, :]
        xc = acc * _sigmoid(acc)
        xc_scr[g] = xc
        xcb = xc.astype(BF16)
        q_scr[g] = _dot(xcb, wq_ref[g])
        k_scr[g] = _dot(xcb, wk_ref[g]) * (LANES ** -0.5)

    cosv = cos_ref[...]
    sinv = sin_ref[...]
    for g in range(H):
        qb = proj_ref[_G_QB + g]
        kb = proj_ref[_G_KB + g]
        qr_scr[g] = qb * cosv + pltpu.roll(qb, LANES // 2, 1) * sinv
        kr_scr[g] = (kb * cosv + pltpu.roll(kb, LANES // 2, 1) * sinv) * (LANES ** -0.5)

    gpre = gates_ref[...] + gbias_ref[...]
    lane = lax.broadcasted_iota(jnp.int32, (ts, LANES), 1)
    gl = jnp.where(lane < H, gpre, _log_sigmoid(gpre))
    gl_scr[...] = gl
    bc = _dot_sel_lhs(_chunk_tri(ts), gl)
    bc_scr[...] = bc
    eye8 = jnp.where(lax.broadcasted_iota(jnp.int32, (SUBLANES, LANES), 0)
                     == lax.broadcasted_iota(jnp.int32, (SUBLANES, LANES), 1), 1.0, 0.0).astype(BF16)
    for c in range(nc):
        rowi_scr[c] = _dot_sel_nt(eye8, gl[c * CHUNK:(c + 1) * CHUNK, :])
        rowb_scr[c] = _dot_sel_nt(eye8, bc[c * CHUNK:(c + 1) * CHUNK, :])

    ti = lax.broadcasted_iota(jnp.int32, (CHUNK, CHUNK), 0)
    si = lax.broadcasted_iota(jnp.int32, (CHUNK, CHUNK), 1)
    tril = ti >= si
    tcol = lax.broadcasted_iota(jnp.int32, (CHUNK, 1), 0).astype(F32)
    log_gamma = [math.log1p(-2.0 ** (-5 - h)) for h in range(H)]
    for h in range(H):
        dec_scr[h] = jnp.where(tril, jnp.exp((ti - si).astype(F32) * log_gamma[h]), 0.0)

    def chunk_body(c, carry):
        r0 = pl.multiple_of(c * CHUNK, CHUNK)
        rows = pl.ds(r0, CHUNK)
        rowi = rowi_scr[c]
        rowb = rowb_scr[c]
        glc = gl_scr[rows, :]
        bcc = bc_scr[rows, :]
        for h in range(H):
            lo, hi = h * LANES, (h + 1) * LANES
            q = q_scr[h, rows, :]
            k = k_scr[h, rows, :]
            vb = proj_ref[_G_VA + h, rows, :].astype(BF16)
            b_col = bcc[:, H + h:H + h + 1]
            i_col = glc[:, h:h + 1]
            b_row = rowb[H + h:H + h + 1, :]
            i_row = rowi[h:h + 1, :]
            dmat = jnp.where(tril, b_col - b_row + i_row, -jnp.inf)
            a = jnp.max(dmat, axis=-1, keepdims=True)
            m_prev = m_scr[:, h:h + 1]
            inter = b_col + m_prev
            m_t = jnp.maximum(inter, a)
            w_intra = jnp.exp(dmat - m_t)
            w_inter = jnp.exp(inter - m_t)
            qb = q.astype(BF16)
            amat = _dot_nt(qb, k.astype(BF16)) * w_intra
            cst = c_scr[h]
            n_row = n_scr[h]
            num = _dot(amat.astype(BF16), vb) + w_inter * _dot(qb, cst.astype(BF16))
            den = (jnp.sum(amat, axis=-1, keepdims=True)
                   + w_inter * jnp.sum(q * n_row, axis=-1, keepdims=True))
            hout = num / jnp.maximum(jnp.abs(den), jnp.exp(-m_t))
            m_new = m_t[CHUNK - 1:CHUNK, :]
            b_last = b_col[CHUNK - 1:CHUNK, :]
            w_s = jnp.exp(b_last - b_col + i_col - m_new)
            decay = jnp.exp(b_last + m_prev - m_new)
            kw = k * w_s
            c_scr[h] = decay * cst + _dot_tn(kw.astype(BF16), vb)
            n_scr[h] = decay * n_row + jnp.sum(kw, axis=0, keepdims=True)
            m_scr[:, h:h + 1] = m_new
            z = _sigmoid(proj_ref[_G_OA + h, rows, :]) * hout
            u_ref[rows, lo:hi] = (_rms(z) * nwa_ref[:, lo:hi]
                                  + skip_ref[:, lo:hi] * xc_scr[h, rows, :])
        for h in range(H):
            lo, hi = h * LANES, (h + 1) * LANES
            lg = log_gamma[h]
            qrb = qr_scr[h, rows, :].astype(BF16)
            kr = kr_scr[h, rows, :]
            vb = proj_ref[_G_VB + h, rows, :].astype(BF16)
            amat = _dot_nt(qrb, kr.astype(BF16)) * dec_scr[h]
            sst = s_scr[h]
            o = (_dot(amat.astype(BF16), vb)
                 + jnp.exp((tcol + 1.0) * lg) * _dot(qrb, sst.astype(BF16)))
            kws = kr * jnp.exp((CHUNK - 1.0 - tcol) * lg)
            s_scr[h] = math.exp(CHUNK * lg) * sst + _dot_tn(kws.astype(BF16), vb)
            gate = proj_ref[_G_GB + h, rows, :]
            u_ref[rows, D_MODEL // 2 + lo:D_MODEL // 2 + hi] = (
                _rms(o) * nwb_ref[:, lo:hi] * (gate * _sigmoid(gate)))
        return carry

    lax.fori_loop(0, nc, chunk_body, 0)

    @pl.when(j == nj - 1)
    def _():
        co_ref[0] = c_scr[...]
        no_ref[0] = n_scr[...]
        mo_ref[0] = m_scr[...]
        convo_ref[0] = conv_scr[...]
        so_ref[0] = s_scr[...]


def _even_scan_call(proj, gates, cos_t, sin_t, states, weights, *, ts, n_seq, steps, row_off):
    H = N_HEADS_EVEN
    c0, n0, m0, conv0, s0 = states
    ng = proj.shape[0]
    rows_idx = lambda b, j: (row_off + b * steps + j, 0)
    state_specs = [pl.BlockSpec((1, H, LANES, LANES), lambda b, j: (b, 0, 0, 0)),
                   pl.BlockSpec((1, H, 1, LANES), lambda b, j: (b, 0, 0, 0)),
                   pl.BlockSpec((1, 1, LANES), lambda b, j: (b, 0, 0)),
                   pl.BlockSpec((1, SUBLANES, H * LANES), lambda b, j: (b, 0, 0)),
                   pl.BlockSpec((1, H, LANES, LANES), lambda b, j: (b, 0, 0, 0))]
    in_specs = [pl.BlockSpec((ng, ts, LANES), lambda b, j: (0, row_off + b * steps + j, 0)),
                pl.BlockSpec((ts, LANES), rows_idx),
                pl.BlockSpec((ts, LANES), rows_idx),
                pl.BlockSpec((ts, LANES), rows_idx)] + state_specs
    for w in weights:
        in_specs.append(pl.BlockSpec(w.shape, functools.partial(lambda nd, b, j: (0,) * nd, w.ndim)))
    out_specs = [pl.BlockSpec((ts, D_MODEL), lambda b, j: (b * steps + j, 0))] + state_specs
    out_shape = [jax.ShapeDtypeStruct((n_seq * steps * ts, D_MODEL), F32),
                 jax.ShapeDtypeStruct(c0.shape, F32), jax.ShapeDtypeStruct(n0.shape, F32),
                 jax.ShapeDtypeStruct(m0.shape, F32), jax.ShapeDtypeStruct(conv0.shape, F32),
                 jax.ShapeDtypeStruct(s0.shape, F32)]
    nc = ts // CHUNK
    scratch = [pltpu.VMEM((H, LANES, LANES), F32), pltpu.VMEM((H, 1, LANES), F32),
               pltpu.VMEM((1, LANES), F32), pltpu.VMEM((SUBLANES, H * LANES), F32),
               pltpu.VMEM((H, LANES, LANES), F32),
               pltpu.VMEM((ts + 2 * SUBLANES, LANES), F32),
               pltpu.VMEM((H, ts, LANES), F32), pltpu.VMEM((H, ts, LANES), F32),
               pltpu.VMEM((H, ts, LANES), F32), pltpu.VMEM((H, ts, LANES), F32),
               pltpu.VMEM((H, ts, LANES), F32),
               pltpu.VMEM((ts, LANES), F32), pltpu.VMEM((ts, LANES), F32),
               pltpu.VMEM((nc, SUBLANES, CHUNK), F32), pltpu.VMEM((nc, SUBLANES, CHUNK), F32),
               pltpu.VMEM((H, CHUNK, CHUNK), F32)]
    return pl.pallas_call(
        functools.partial(_even_scan_kernel, ts=ts),
        grid=(n_seq, steps),
        in_specs=in_specs, out_specs=out_specs, out_shape=out_shape,
        scratch_shapes=scratch,
        compiler_params=_cparams(("arbitrary", "arbitrary")),
        name="even_scan",
    )(proj, gates, cos_t, sin_t, c0, n0, m0, conv0, s0, *weights)


_G_Q, _G_F, _G_I, _G_G = 0, 8, 16, 24


def _hgrn_intra(q, k, bcum, ti, si, tcol_i):
    amat = jnp.zeros((CHUNK, CHUNK), F32)
    for b in (32, 16, 8):
        nb2 = CHUNK // (2 * b)
        parts = [jnp.broadcast_to(bcum[m * 2 * b + b - 1:m * 2 * b + b, :], (2 * b, LANES))
                 for m in range(nb2)]
        ref = parts[0] if nb2 == 1 else jnp.concatenate(parts, axis=0)
        upper = ((tcol_i // b) % 2) == 1
        ql = jnp.where(upper, q * jnp.exp(jnp.minimum(bcum - ref, 0.0)), 0.0)
        kl = jnp.where(upper, 0.0, k * jnp.exp(jnp.minimum(ref - bcum, 0.0)))
        al = _dot_nt(ql.astype(BF16), kl.astype(BF16))
        amat = amat + jnp.where((ti // (2 * b)) == (si // (2 * b)), al, 0.0)
    nblk = CHUNK // SUBLANES
    b3 = bcum.reshape(nblk, SUBLANES, LANES)
    k3 = k.reshape(nblk, SUBLANES, LANES)
    for jj in range(SUBLANES):
        bj = jnp.broadcast_to(b3[:, jj:jj + 1, :], (nblk, SUBLANES, LANES)).reshape(CHUNK, LANES)
        kj = jnp.broadcast_to(k3[:, jj:jj + 1, :], (nblk, SUBLANES, LANES)).reshape(CHUNK, LANES)
        col = jnp.sum(q * kj * jnp.exp(jnp.minimum(bcum - bj, 0.0)), axis=-1, keepdims=True)
        sel = (si == (ti // SUBLANES) * SUBLANES + jj) & ((ti % SUBLANES) >= jj)
        amat = jnp.where(sel, col, amat)
    return amat


def _odd_scan_kernel(proj_ref, s0_ref, lb_ref, nw_ref, u_ref, so_ref,
                     st_scr, k_scr, bc_scr, *, ts):
    H = N_HEADS_ODD
    j = pl.program_id(1)
    nj = pl.num_programs(1)
    nc = ts // CHUNK

    @pl.when(j == 0)
    def _():
        for h in range(H):
            st_scr[h] = s0_ref[0, h].T

    tri = _chunk_tri(ts)
    for h in range(H):
        lo, hi = h * LANES, (h + 1) * LANES
        lbv = lb_ref[:, lo:hi]
        fpre = proj_ref[_G_F + h]
        logf = _logaddexp(jnp.log(lbv), jnp.log1p(-lbv) + _log_sigmoid(fpre))
        k_scr[h] = (1.0 - lbv) * _sigmoid(-fpre)
        bc_scr[h] = _dot_sel_lhs(tri, logf)

    ti = lax.broadcasted_iota(jnp.int32, (CHUNK, CHUNK), 0)
    si = lax.broadcasted_iota(jnp.int32, (CHUNK, CHUNK), 1)
    tcol_i = lax.broadcasted_iota(jnp.int32, (CHUNK, 1), 0)

    def chunk_body(c, carry):
        r0 = pl.multiple_of(c * CHUNK, CHUNK)
        rows = pl.ds(r0, CHUNK)
        for h in range(H):
            lo, hi = h * LANES, (h + 1) * LANES
            q = proj_ref[_G_Q + h, rows, :]
            k = k_scr[h, rows, :]
            vb = proj_ref[_G_I + h, rows, :].astype(BF16)
            bcum = bc_scr[h, rows, :]
            amat = _hgrn_intra(q, k, bcum, ti, si, tcol_i)
            st = st_scr[h]
            o = (_dot(amat.astype(BF16), vb)
                 + _dot_nt((q * jnp.exp(bcum)).astype(BF16), st.astype(BF16)))
            last = bcum[CHUNK - 1:CHUNK, :]
            kd = k * jnp.exp(last - bcum)
            st_scr[h] = st * jnp.exp(last) + _dot_tn(vb, kd.astype(BF16))
            u_ref[rows, lo:hi] = (_rms(o) * nw_ref[:, lo:hi]
                                  * _sigmoid(proj_ref[_G_G + h, rows, :]))
        return carry

    lax.fori_loop(0, nc, chunk_body, 0)

    @pl.when(j == nj - 1)
    def _():
        for h in range(H):
            so_ref[0, h] = st_scr[h].T


def _odd_scan_call(proj, s0, lb, nw, *, ts, n_seq, steps, row_off):
    H = N_HEADS_ODD
    ng = proj.shape[0]
    st_spec = pl.BlockSpec((1, H, LANES, LANES), lambda b, j: (b, 0, 0, 0))
    return pl.pallas_call(
        functools.partial(_odd_scan_kernel, ts=ts),
        grid=(n_seq, steps),
        in_specs=[pl.BlockSpec((ng, ts, LANES), lambda b, j: (0, row_off + b * steps + j, 0)),
                  st_spec,
                  pl.BlockSpec((1, D_MODEL), lambda b, j: (0, 0)),
                  pl.BlockSpec((1, D_MODEL), lambda b, j: (0, 0))],
        out_specs=[pl.BlockSpec((ts, D_MODEL), lambda b, j: (b * steps + j, 0)), st_spec],
        out_shape=[jax.ShapeDtypeStruct((n_seq * steps * ts, D_MODEL), F32),
                   jax.ShapeDtypeStruct(s0.shape, F32)],
        scratch_shapes=[pltpu.VMEM((H, LANES, LANES), F32),
                        pltpu.VMEM((H, ts, LANES), F32),
                        pltpu.VMEM((H, ts, LANES), F32)],
        compiler_params=_cparams(("arbitrary", "arbitrary")),
        name="odd_scan",
    )(proj, s0, lb, nw)


def _post_kernel(x_ref, up_ref, us_ref, mod_ref, wout_ref, nw_ref, wr_ref, br_ref,
                 xo_ref, hf_ref, eidx_ref, gate_ref, rank_ref, cnt_ref, run_scr,
                 *, tm, npc, n_prompt_tiles):
    i = pl.program_id(0)
    nch = tm // CHUNK

    @pl.when(i == 0)
    def _():
        run_scr[...] = jnp.zeros_like(run_scr)

    u = jnp.where(i < n_prompt_tiles, up_ref[...], us_ref[...])
    y = _dot(u.astype(BF16), wout_ref[...])
    for c in range(nch):
        seq = _seq_row(i, nch, c, npc)
        gm = mod_ref[pl.ds(seq, 1), 2 * D_MODEL:3 * D_MODEL]
        shf = mod_ref[pl.ds(seq, 1), 3 * D_MODEL:4 * D_MODEL]
        scf = mod_ref[pl.ds(seq, 1), 4 * D_MODEL:5 * D_MODEL]
        rs = slice(c * CHUNK, (c + 1) * CHUNK)
        xn = x_ref[rs, :] + gm * y[rs, :]
        xo_ref[rs, :] = xn
        hf_ref[rs, :] = _rms(xn) * nw_ref[...] * (1.0 + scf) + shf

    logits = _dot3(hf_ref[...], wr_ref[...]) + br_ref[...]
    lane_i = lax.broadcasted_iota(jnp.int32, (tm, LANES), 1)
    lane_f = lane_i.astype(F32)
    vals, idxs = [], []
    cur = logits
    for _ in range(TOP_K):
        m = jnp.max(cur, axis=-1, keepdims=True)
        idx = jnp.min(jnp.where(cur == m, lane_f, float(LANES)), axis=-1, keepdims=True)
        vals.append(m)
        idxs.append(idx)
        cur = jnp.where(lane_f == idx, -jnp.inf, cur)
    exps = [jnp.exp(v - vals[0]) for v in vals]
    denom = exps[0] + exps[1] + exps[2] + exps[3]
    onehot = jnp.zeros((tm, LANES), F32)
    for idx in idxs:
        onehot = onehot + jnp.where(lane_f == idx, 1.0, 0.0)
    r = lax.broadcasted_iota(jnp.int32, (tm, tm), 0)
    cidx = lax.broadcasted_iota(jnp.int32, (tm, tm), 1)
    strict = jnp.where(cidx < r, 1.0, 0.0).astype(BF16)
    before = _dot(strict, onehot.astype(BF16)) + run_scr[...]
    eidx_o = jnp.zeros((tm, LANES), F32)
    gate_o = jnp.zeros((tm, LANES), F32)
    rank_o = jnp.zeros((tm, LANES), F32)
    for kk in range(TOP_K):
        rk = jnp.sum(jnp.where(lane_f == idxs[kk], before, 0.0), axis=-1, keepdims=True)
        eidx_o = jnp.where(lane_i == kk, idxs[kk], eidx_o)
        gate_o = jnp.where(lane_i == kk, exps[kk] / denom, gate_o)
        rank_o = jnp.where(lane_i == kk, rk, rank_o)
    eidx_ref[...] = eidx_o
    gate_ref[...] = gate_o
    rank_ref[...] = rank_o
    run_scr[...] = run_scr[...] + jnp.sum(onehot, axis=0, keepdims=True)
    cnt_ref[...] = run_scr[...]


def _post_call(x, u_p, u_s, mod_l, w_out, nw, w_r, b_r, npc):
    ttot = x.shape[0]
    tm = TOKEN_TILE
    npt = u_p.shape[0] // tm
    tile = lambda i: (i, 0)
    const = lambda i: (0, 0)
    return pl.pallas_call(
        functools.partial(_post_kernel, tm=tm, npc=npc, n_prompt_tiles=npt),
        grid=(ttot // tm,),
        in_specs=[pl.BlockSpec((tm, D_MODEL), tile),
                  pl.BlockSpec((tm, D_MODEL), lambda i: (jnp.minimum(i, npt - 1), 0)),
                  pl.BlockSpec((tm, D_MODEL), lambda i: (jnp.maximum(i - npt, 0), 0)),
                  pl.BlockSpec(mod_l.shape, const),
                  pl.BlockSpec(w_out.shape, const),
                  pl.BlockSpec((1, D_MODEL), const),
                  pl.BlockSpec(w_r.shape, const),
                  pl.BlockSpec((1, LANES), const)],
        out_specs=[pl.BlockSpec((tm, D_MODEL), tile), pl.BlockSpec((tm, D_MODEL), tile),
                   pl.BlockSpec((tm, LANES), tile), pl.BlockSpec((tm, LANES), tile),
                   pl.BlockSpec((tm, LANES), tile), pl.BlockSpec((1, LANES), const)],
        out_shape=[jax.ShapeDtypeStruct((ttot, D_MODEL), F32), jax.ShapeDtypeStruct((ttot, D_MODEL), F32),
                   jax.ShapeDtypeStruct((ttot, LANES), F32), jax.ShapeDtypeStruct((ttot, LANES), F32),
                   jax.ShapeDtypeStruct((ttot, LANES), F32), jax.ShapeDtypeStruct((1, LANES), F32)],
        scratch_shapes=[pltpu.VMEM((1, LANES), F32)],
        compiler_params=_cparams(("arbitrary",)),
        name="post",
    )(x, u_p, u_s, mod_l, w_out, nw, w_r, b_r)


def _plan_kernel(eidx_ref, rank_ref, cnt_ref, pos_ref, blk_ref, ends_ref, *, tm, nbp):
    cnt = jnp.broadcast_to(cnt_ref[...], (SUBLANES, LANES))
    nblk = jnp.floor((cnt + (EXPERT_ROWS - 1.0)) * (1.0 / EXPERT_ROWS))
    r = lax.broadcasted_iota(jnp.int32, (LANES, LANES), 0)
    c = lax.broadcasted_iota(jnp.int32, (LANES, LANES), 1)
    upper = jnp.where(r <= c, 1.0, 0.0).astype(BF16)
    p1, p2, p3 = _split3(nblk)
    ends = _dot(p1, upper) + (_dot(p2, upper) + _dot(p3, upper))
    start_row = (ends[0:1, :] - nblk[0:1, :]) * float(EXPERT_ROWS)
    lane_i = lax.broadcasted_iota(jnp.int32, (tm, LANES), 1)
    lane_f = lane_i.astype(F32)
    eidx = eidx_ref[...]
    rank = rank_ref[...]
    pos = jnp.zeros((tm, LANES), F32)
    for kk in range(TOP_K):
        base = jnp.sum(jnp.where(lane_f == eidx[:, kk:kk + 1], start_row, 0.0), axis=-1, keepdims=True)
        pos = jnp.where(lane_i == kk, base + rank[:, kk:kk + 1], pos)
    pos_ref[...] = pos.astype(jnp.int32)

    @pl.when(pl.program_id(0) == 0)
    def _():
        bi = lax.broadcasted_iota(jnp.int32, (nbp, LANES), 0).astype(F32)
        li = lax.broadcasted_iota(jnp.int32, (nbp, LANES), 1)
        done = jnp.where((li < N_EXPERTS) & (ends[0:1, :] <= bi), 1.0, 0.0)
        be = jnp.minimum(jnp.sum(done, axis=-1, keepdims=True), N_EXPERTS - 1.0)
        blk_ref[...] = jnp.broadcast_to(be, (nbp, LANES)).astype(jnp.int32)
        ends_ref[...] = ends.astype(jnp.int32)


def _plan_call(eidx, rank, cnt, nbp):
    ttot = eidx.shape[0]
    tm = TOKEN_TILE
    tile = lambda i: (i, 0)
    const = lambda i: (0, 0)
    return pl.pallas_call(
        functools.partial(_plan_kernel, tm=tm, nbp=nbp),
        grid=(ttot // tm,),
        in_specs=[pl.BlockSpec((tm, LANES), tile), pl.BlockSpec((tm, LANES), tile),
                  pl.BlockSpec((1, LANES), const)],
        out_specs=[pl.BlockSpec((tm, LANES), tile), pl.BlockSpec((nbp, LANES), const),
                   pl.BlockSpec((SUBLANES, LANES), const)],
        out_shape=[jax.ShapeDtypeStruct((ttot, LANES), jnp.int32),
                   jax.ShapeDtypeStruct((nbp, LANES), jnp.int32),
                   jax.ShapeDtypeStruct((SUBLANES, LANES), jnp.int32)],
        compiler_params=_cparams(("arbitrary",)),
        name="plan",
    )(eidx, rank, cnt)


def _dispatch_kernel(pos_ref, hf_ref, xs_in_ref, xs_ref, sem, *, tm):
    del xs_in_ref

    def row_copy(t, p):
        return pltpu.make_async_copy(hf_ref.at[pl.ds(t, 1), :], xs_ref.at[pl.ds(p, 1), :], sem)

    def issue(t, carry):
        for kk in range(TOP_K):
            row_copy(t, pos_ref[t * TOP_K + kk]).start()
        return carry

    lax.fori_loop(0, tm, issue, 0)

    def drain(t, carry):
        for kk in range(TOP_K):
            row_copy(t, pos_ref[t * TOP_K + kk]).wait()
        return carry

    lax.fori_loop(0, tm, drain, 0)


def _dispatch_call(pos_flat, hf, xs):
    ttot = hf.shape[0]
    tm = TOKEN_TILE
    return pl.pallas_call(
        functools.partial(_dispatch_kernel, tm=tm),
        grid=(ttot // tm,),
        in_specs=[pl.BlockSpec((tm * TOP_K,), lambda i: (i,), memory_space=pltpu.SMEM),
                  pl.BlockSpec((tm, D_MODEL), lambda i: (i, 0)),
                  pl.BlockSpec(memory_space=pl.ANY)],
        out_specs=pl.BlockSpec(memory_space=pl.ANY),
        out_shape=jax.ShapeDtypeStruct(xs.shape, xs.dtype),
        scratch_shapes=[pltpu.SemaphoreType.DMA(())],
        input_output_aliases={2: 0},
        compiler_params=pltpu.CompilerParams(dimension_semantics=("arbitrary",),
                                             vmem_limit_bytes=VMEM_LIMIT, has_side_effects=True),
        name="dispatch",
    )(pos_flat, hf, xs)


def _expert_kernel(be_ref, nu_ref, xs_ref, wgu_ref, bgu_ref, wdn_ref, bdn_ref, y_ref,
                   wgu_bf, wdn_bf):
    b = pl.program_id(0)

    @pl.when(b < nu_ref[0])
    def _():
        prev = be_ref[jnp.maximum(b - 1, 0)]

        @pl.when((b == 0) | (be_ref[b] != prev))
        def _():
            wgu_bf[...] = wgu_ref[0].astype(BF16)
            wdn_bf[...] = wdn_ref[0].astype(BF16)

        gu = _dot(xs_ref[...].astype(BF16), wgu_bf[...]) + bgu_ref[0]
        g = jnp.minimum(gu[:, :D_MODEL], SWIGLU_LIMIT)
        u = jnp.clip(gu[:, D_MODEL:], -SWIGLU_LIMIT, SWIGLU_LIMIT)
        act = (u + 1.0) * (g * _sigmoid(SWIGLU_ALPHA * g))
        y_ref[...] = _dot(act.astype(BF16), wdn_bf[...]) + bdn_ref[0]

    @pl.when(b >= nu_ref[0])
    def _():
        y_ref[...] = jnp.zeros_like(y_ref)


def _expert_call(blk_expert, n_used, xs, w_gu, b_gu, w_dn, b_dn):
    nb = xs.shape[0] // EXPERT_ROWS
    d_ff2 = w_gu.shape[2]
    blk = lambda b, be, nu: (jnp.minimum(b, nu[0] - 1), 0)
    blk_out = lambda b, be, nu: (b, 0)
    exp3 = lambda b, be, nu: (be[jnp.minimum(b, nu[0] - 1)], 0, 0)
    grid_spec = pltpu.PrefetchScalarGridSpec(
        num_scalar_prefetch=2, grid=(nb,),
        in_specs=[pl.BlockSpec((EXPERT_ROWS, D_MODEL), blk),
                  pl.BlockSpec((1, D_MODEL, d_ff2), exp3),
                  pl.BlockSpec((1, 1, d_ff2), exp3),
                  pl.BlockSpec((1, D_MODEL, D_MODEL), exp3),
                  pl.BlockSpec((1, 1, D_MODEL), exp3)],
        out_specs=pl.BlockSpec((EXPERT_ROWS, D_MODEL), blk_out),
        scratch_shapes=[pltpu.VMEM((D_MODEL, d_ff2), BF16), pltpu.VMEM((D_MODEL, D_MODEL), BF16)])
    return pl.pallas_call(
        _expert_kernel,
        grid_spec=grid_spec,
        out_shape=jax.ShapeDtypeStruct(xs.shape, F32),
        compiler_params=_cparams(("arbitrary",)),
        name="experts",
    )(blk_expert, n_used, xs, w_gu, b_gu, w_dn, b_dn)


def _combine_kernel(*refs, tm, npc, final):
    if final:
        pos_ref, x_ref, gate_ref, mod_ref, fnw_ref, y_hbm, xo_ref, yf_ref, buf, sem = refs
    else:
        pos_ref, x_ref, gate_ref, mod_ref, y_hbm, xo_ref, buf, sem = refs
    i = pl.program_id(0)
    nch = tm // CHUNK

    def row_copy(t, kk, p):
        return pltpu.make_async_copy(y_hbm.at[pl.ds(p, 1), :], buf.at[kk, pl.ds(t, 1), :], sem)

    def issue(t, carry):
        for kk in range(TOP_K):
            row_copy(t, kk, pos_ref[t * TOP_K + kk]).start()
        return carry

    lax.fori_loop(0, tm, issue, 0)

    def drain(t, carry):
        for kk in range(TOP_K):
            row_copy(t, kk, pos_ref[t * TOP_K + kk]).wait()
        return carry

    lax.fori_loop(0, tm, drain, 0)

    gate = gate_ref[...]
    acc = gate[:, 0:1] * buf[0]
    for kk in range(1, TOP_K):
        acc = acc + gate[:, kk:kk + 1] * buf[kk]
    for c in range(nch):
        seq = _seq_row(i, nch, c, npc)
        gf = mod_ref[pl.ds(seq, 1), 5 * D_MODEL:6 * D_MODEL]
        rs = slice(c * CHUNK, (c + 1) * CHUNK)
        xn = x_ref[rs, :] + gf * acc[rs, :]
        xo_ref[rs, :] = xn
        if final:
            yf_ref[rs, :] = _rms(xn) * fnw_ref[...]


def _combine_call(pos_flat, x, gate, mod_l, y_sorted, npc, final_w):
    ttot = x.shape[0]
    tm = COMBINE_TILE
    final = final_w is not None
    tile = lambda i: (i, 0)
    const = lambda i: (0, 0)
    in_specs = [pl.BlockSpec((tm * TOP_K,), lambda i: (i,), memory_space=pltpu.SMEM),
                pl.BlockSpec((tm, D_MODEL), tile),
                pl.BlockSpec((tm, LANES), tile),
                pl.BlockSpec(mod_l.shape, const)]
    args = [pos_flat, x, gate, mod_l]
    if final:
        in_specs.append(pl.BlockSpec((1, D_MODEL), const))
        args.append(final_w)
    in_specs.append(pl.BlockSpec(memory_space=pl.ANY))
    args.append(y_sorted)
    out_specs = [pl.BlockSpec((tm, D_MODEL), tile)]
    out_shape = [jax.ShapeDtypeStruct((ttot, D_MODEL), F32)]
    if final:
        out_specs.append(pl.BlockSpec((tm, D_MODEL), tile))
        out_shape.append(jax.ShapeDtypeStruct((ttot, D_MODEL), F32))
    return pl.pallas_call(
        functools.partial(_combine_kernel, tm=tm, npc=npc, final=final),
        grid=(ttot // tm,),
        in_specs=in_specs, out_specs=out_specs, out_shape=out_shape,
        scratch_shapes=[pltpu.VMEM((TOP_K, tm, D_MODEL), F32), pltpu.SemaphoreType.DMA(())],
        compiler_params=_cparams(("arbitrary",)),
        name="combine",
    )(*args)


def kernel(x_prompt, x_sample, c_prompt, c_sample, state_mlstm_C, state_mlstm_n, state_mlstm_m, state_mlstm_conv, state_ret_S, state_hgrn_S, w_ada, b_ada, norm_mix_w, norm_ffn_w, final_norm_w, w_in_even, b_mlstm_i, b_mlstm_f, w_mlstm_conv, b_mlstm_conv, w_mlstm_q, w_mlstm_k, mlstm_skip, mlstm_norm_w, ret_norm_w, w_out_even, w_in_odd, hgrn_lb_logits, hgrn_norm_w, w_out_odd, moe_router_w, moe_router_b, moe_w_gate_up, moe_b_gate_up, moe_w_down, moe_b_down):
    bp, seq, d = x_prompt.shape
    bs, dseq, _ = x_sample.shape
    assert bp == 1 and d == D_MODEL and dseq == CHUNK
    assert seq % TOKEN_TILE == 0 and (bs * dseq) % TOKEN_TILE == 0
    depth = w_ada.shape[0]
    tp, tsmp = seq, bs * dseq
    ttot = tp + tsmp
    npc = tp // CHUNK
    he, ho = N_HEADS_EVEN, N_HEADS_ODD
    da = he * LANES
    past_len = 1024

    x = jnp.concatenate([x_prompt.reshape(tp, d), x_sample.reshape(tsmp, d)], axis=0)
    n_mod_rows = 2 * SUBLANES
    assert 1 + bs <= n_mod_rows
    c_all = jnp.zeros((n_mod_rows, d), F32).at[0:1].set(c_prompt).at[1:1 + bs].set(c_sample)
    mod = _ada_call(c_all, w_ada, b_ada)

    half = LANES // 2
    inv = ROPE_BASE ** (-jnp.arange(half, dtype=F32) / half)
    pos_all = jnp.concatenate([jnp.arange(tp, dtype=F32),
                               jnp.tile(past_len + jnp.arange(dseq, dtype=F32), bs)])
    ang = pos_all[:, None] * inv[None, :]
    cos_t = jnp.concatenate([jnp.cos(ang), jnp.cos(ang)], axis=-1)
    sin_t = jnp.concatenate([-jnp.sin(ang), jnp.sin(ang)], axis=-1)

    lb_p = jax.nn.softmax(hgrn_lb_logits.astype(F32), axis=0)
    lbs = jnp.cumsum(lb_p, axis=0) - lb_p[0]

    n_rows_sorted = (-(-(ttot * TOP_K) // EXPERT_ROWS) + N_EXPERTS) * EXPERT_ROWS
    nb = n_rows_sorted // EXPERT_ROWS
    nbp = -(-nb // SUBLANES) * SUBLANES
    xs = jnp.zeros((n_rows_sorted, d), F32)

    ts_p = TOKEN_TILE
    steps_p = tp // ts_p
    even_out, odd_out = [], []
    y_final = None
    for l in range(depth):
        jl = l // 2
        mod_l = mod[l]
        if l % 2 == 0:
            w_in = w_in_even[jl]
            w_main = jnp.concatenate([w_in[:, :3 * da], w_in[:, 3 * da + 2 * he:]], axis=1).astype(BF16)
            w_gate = jnp.zeros((d, LANES), F32).at[:, :2 * he].set(w_in[:, 3 * da:3 * da + 2 * he])
            proj, gates = _inproj_call(x, mod_l, norm_mix_w[l][None], w_main, w_gate, npc)
            gbias = jnp.zeros((1, LANES), F32).at[0, :he].set(b_mlstm_i[jl]).at[0, he:2 * he].set(b_mlstm_f[jl])
            weights = [w_mlstm_conv[jl], b_mlstm_conv[jl][None],
                       w_mlstm_q[jl].astype(BF16), w_mlstm_k[jl].astype(BF16), gbias,
                       mlstm_skip[jl][None], mlstm_norm_w[jl][None], ret_norm_w[jl][None]]
            zeros_p = (jnp.zeros((1, he, LANES, LANES), F32), jnp.zeros((1, he, 1, LANES), F32),
                       jnp.zeros((1, 1, LANES), F32), jnp.zeros((1, SUBLANES, da), F32),
                       jnp.zeros((1, he, LANES, LANES), F32))
            st_s = (state_mlstm_C[jl], state_mlstm_n[jl][:, :, None, :],
                    jnp.zeros((bs, 1, LANES), F32).at[:, 0, :he].set(state_mlstm_m[jl]),
                    jnp.zeros((bs, SUBLANES, da), F32).at[:, SUBLANES - (CONV_W - 1):].set(state_mlstm_conv[jl]),
                    state_ret_S[jl])
            res_p = _even_scan_call(proj, gates, cos_t, sin_t, zeros_p, weights,
                                    ts=ts_p, n_seq=1, steps=steps_p, row_off=0)
            res_s = _even_scan_call(proj, gates, cos_t, sin_t, st_s, weights,
                                    ts=CHUNK, n_seq=bs, steps=1, row_off=npc)
            u_p, u_s = res_p[0], res_s[0]
            even_out.append((res_p[1:], res_s[1:]))
            w_out = w_out_even[jl].astype(BF16)
        else:
            proj = _inproj_call(x, mod_l, norm_mix_w[l][None], w_in_odd[jl].astype(BF16), None, npc)[0]
            lb = lbs[l][None]
            nw = hgrn_norm_w[jl][None]
            u_p, sp = _odd_scan_call(proj, jnp.zeros((1, ho, LANES, LANES), F32), lb, nw,
                                     ts=ts_p, n_seq=1, steps=steps_p, row_off=0)
            u_s, ss = _odd_scan_call(proj, state_hgrn_S[jl], lb, nw,
                                     ts=CHUNK, n_seq=bs, steps=1, row_off=npc)
            odd_out.append((sp, ss))
            w_out = w_out_odd[jl].astype(BF16)

        w_r = jnp.zeros((d, LANES), F32).at[:, :N_EXPERTS].set(moe_router_w[l])
        b_r = jnp.full((1, LANES), -jnp.inf, F32).at[0, :N_EXPERTS].set(moe_router_b[l])
        x, hf, eidx, gate, rank, cnt = _post_call(x, u_p, u_s, mod_l, w_out, norm_ffn_w[l][None], w_r, b_r, npc)
        pos, blk, ends = _plan_call(eidx, rank, cnt, nbp)
        pos_flat = pos[:, :TOP_K].reshape(-1)
        blk_expert = blk[:nb, 0]
        n_used = ends[0, N_EXPERTS - 1:N_EXPERTS]
        xs = _dispatch_call(pos_flat, hf, xs)
        y_sorted = _expert_call(blk_expert, n_used, xs, moe_w_gate_up[l],
                                moe_b_gate_up[l][:, None, :], moe_w_down[l], moe_b_down[l][:, None, :])
        if l == depth - 1:
            x, y_final = _combine_call(pos_flat, x, gate, mod_l, y_sorted, npc, final_norm_w[None])
        else:
            x = _combine_call(pos_flat, x, gate, mod_l, y_sorted, npc, None)[0]

    def even_states(which):
        cs = jnp.stack([e[which][0] for e in even_out])
        ns = jnp.stack([e[which][1][:, :, 0, :] for e in even_out])
        ms = jnp.stack([e[which][2][:, 0, :he] for e in even_out])
        cv = jnp.stack([e[which][3][:, SUBLANES - (CONV_W - 1):, :] for e in even_out])
        ss = jnp.stack([e[which][4] for e in even_out])
        return cs, ns, ms, cv, ss

    p_c, p_n, p_m, p_cv, p_s = even_states(0)
    s_c, s_n, s_m, s_cv, s_s = even_states(1)
    p_h = jnp.stack([o[0] for o in odd_out])
    s_h = jnp.stack([o[1] for o in odd_out])
    y_prompt = y_final[:tp].reshape(bp, seq, d)
    y_sample = y_final[tp:].reshape(bs, dseq, d)
    return (y_prompt, y_sample, p_c, p_n, p_m, p_cv, p_s, p_h, s_c, s_n, s_m, s_cv, s_s, s_h)
```

```python
import functools
import math

import jax
import jax.numpy as jnp
from jax import lax
from jax.experimental import pallas as pl
from jax.experimental.pallas import tpu as pltpu

F32 = jnp.float32
BF16 = jnp.bfloat16

CHUNK = 64
LANES = 128
SUBLANES = 8
D_MODEL = 1024
N_HEADS_EVEN = 4
N_HEADS_ODD = 8
CONV_W = 4
N_EXPERTS = 32
TOP_K = 4
SWIGLU_LIMIT = 7.0
SWIGLU_ALPHA = 1.702
EPS = 1e-6
ROPE_BASE = 10000.0

TOKEN_TILE = 512
INPROJ_TILE = 256
COMBINE_TILE = 256
EXPERT_ROWS = 512
VMEM_LIMIT = 56 * 1024 * 1024


def _cparams(sem, vmem=VMEM_LIMIT):
    return pltpu.CompilerParams(dimension_semantics=sem, vmem_limit_bytes=vmem)


def _dot(a, b):
    return jnp.dot(a, b, preferred_element_type=F32)


def _dot_nt(a, b):
    return lax.dot_general(a, b, (((1,), (1,)), ((), ())), preferred_element_type=F32)


def _dot_tn(a, b):
    return lax.dot_general(a, b, (((0,), (0,)), ((), ())), preferred_element_type=F32)


def _split3(x):
    p1 = x.astype(BF16)
    r1 = x - p1.astype(F32)
    p2 = r1.astype(BF16)
    p3 = (r1 - p2.astype(F32)).astype(BF16)
    return p1, p2, p3


def _dot3(a, b):
    ah = a.astype(BF16)
    al = (a - ah.astype(F32)).astype(BF16)
    bh = b.astype(BF16)
    bl = (b - bh.astype(F32)).astype(BF16)
    return _dot(ah, bh) + (_dot(ah, bl) + _dot(al, bh))


def _dot_sel_lhs(sel_bf16, x):
    p1, p2, p3 = _split3(x)
    return _dot(sel_bf16, p1) + (_dot(sel_bf16, p2) + _dot(sel_bf16, p3))


def _dot_sel_nt(sel_bf16, x):
    p1, p2, p3 = _split3(x)
    return _dot_nt(sel_bf16, p1) + (_dot_nt(sel_bf16, p2) + _dot_nt(sel_bf16, p3))


def _sigmoid(x):
    return 1.0 / (1.0 + jnp.exp(-x))


def _log_sigmoid(x):
    return jnp.minimum(x, 0.0) - jnp.log1p(jnp.exp(-jnp.abs(x)))


def _logaddexp(a, b):
    return jnp.maximum(a, b) + jnp.log1p(jnp.exp(-jnp.abs(a - b)))


def _rms(x):
    return x * lax.rsqrt(jnp.mean(x * x, axis=-1, keepdims=True) + EPS)


def _chunk_tri(n):
    r = lax.broadcasted_iota(jnp.int32, (n, n), 0)
    c = lax.broadcasted_iota(jnp.int32, (n, n), 1)
    return jnp.where((r // CHUNK == c // CHUNK) & (c <= r), 1.0, 0.0).astype(BF16)


def _seq_row(tile_idx, chunks_per_tile, c, n_prompt_chunks):
    return jnp.maximum(tile_idx * chunks_per_tile + c - (n_prompt_chunks - 1), 0)


def _ada_kernel(c_ref, w_ref, b_ref, o_ref):
    c = c_ref[...]
    o_ref[0] = _dot3(c * _sigmoid(c), w_ref[0]) + b_ref[0]


def _ada_call(c_all, w_ada, b_ada):
    depth = w_ada.shape[0]
    nrow = c_all.shape[0]
    ncol = w_ada.shape[2] // D_MODEL
    return pl.pallas_call(
        _ada_kernel,
        grid=(depth, ncol),
        in_specs=[pl.BlockSpec((nrow, D_MODEL), lambda l, j: (0, 0)),
                  pl.BlockSpec((1, D_MODEL, D_MODEL), lambda l, j: (l, 0, j)),
                  pl.BlockSpec((1, 1, D_MODEL), lambda l, j: (l, 0, j))],
        out_specs=pl.BlockSpec((1, nrow, D_MODEL), lambda l, j: (l, 0, j)),
        out_shape=jax.ShapeDtypeStruct((depth, nrow, ncol * D_MODEL), F32),
        compiler_params=_cparams(("arbitrary", "arbitrary")),
        name="ada",
    )(c_all, w_ada, b_ada.reshape(depth, 1, -1))


def _inproj_kernel(*refs, tm, ng, npc, has_gates):
    if has_gates:
        x_ref, mod_ref, nw_ref, w_ref, wg_ref, proj_ref, gates_ref, h_scr = refs
    else:
        x_ref, mod_ref, nw_ref, w_ref, proj_ref, h_scr = refs
    i = pl.program_id(0)
    nch = tm // CHUNK
    for c in range(nch):
        seq = _seq_row(i, nch, c, npc)
        sh = mod_ref[pl.ds(seq, 1), 0:D_MODEL]
        sc = mod_ref[pl.ds(seq, 1), D_MODEL:2 * D_MODEL]
        xc = x_ref[c * CHUNK:(c + 1) * CHUNK, :]
        h_scr[c * CHUNK:(c + 1) * CHUNK, :] = _rms(xc) * nw_ref[...] * (1.0 + sc) + sh
    h = h_scr[...]
    hb = h.astype(BF16)
    for g in range(0, ng, 4):
        res = _dot(hb, w_ref[:, g * LANES:(g + 4) * LANES])
        for jj in range(4):
            proj_ref[g + jj] = res[:, jj * LANES:(jj + 1) * LANES]
    if has_gates:
        gates_ref[...] = _dot3(h, wg_ref[...])


def _inproj_call(x, mod_l, nw, w_main, w_gate, npc):
    ttot = x.shape[0]
    tm = INPROJ_TILE
    ng = w_main.shape[1] // LANES
    has_gates = w_gate is not None
    in_specs = [pl.BlockSpec((tm, D_MODEL), lambda i: (i, 0)),
                pl.BlockSpec(mod_l.shape, lambda i: (0, 0)),
                pl.BlockSpec((1, D_MODEL), lambda i: (0, 0)),
                pl.BlockSpec(w_main.shape, lambda i: (0, 0))]
    out_specs = [pl.BlockSpec((ng, tm, LANES), lambda i: (0, i, 0))]
    out_shape = [jax.ShapeDtypeStruct((ng, ttot, LANES), F32)]
    args = [x, mod_l, nw, w_main]
    if has_gates:
        in_specs.append(pl.BlockSpec(w_gate.shape, lambda i: (0, 0)))
        out_specs.append(pl.BlockSpec((tm, LANES), lambda i: (i, 0)))
        out_shape.append(jax.ShapeDtypeStruct((ttot, LANES), F32))
        args.append(w_gate)
    return pl.pallas_call(
        functools.partial(_inproj_kernel, tm=tm, ng=ng, npc=npc, has_gates=has_gates),
        grid=(ttot // tm,),
        in_specs=in_specs, out_specs=out_specs, out_shape=out_shape,
        scratch_shapes=[pltpu.VMEM((tm, D_MODEL), F32)],
        compiler_params=_cparams(("arbitrary",)),
        name="inproj",
    )(*args)


_G_XM, _G_VA, _G_OA, _G_QB, _G_KB, _G_VB, _G_GB = 0, 4, 8, 12, 16, 20, 24


def _even_scan_kernel(proj_ref, gates_ref, cos_ref, sin_ref,
                      c0_ref, n0_ref, m0_ref, conv0_ref, s0_ref,
                      cw_ref, cb_ref, wq_ref, wk_ref, gbias_ref, skip_ref, nwa_ref, nwb_ref,
                      u_ref, co_ref, no_ref, mo_ref, convo_ref, so_ref,
                      c_scr, n_scr, m_scr, conv_scr, s_scr,
                      xbuf, xc_scr, q_scr, k_scr, qr_scr, kr_scr,
                      gl_scr, bc_scr, rowi_scr, rowb_scr, dec_scr, *, ts):
    H = N_HEADS_EVEN
    j = pl.program_id(1)
    nj = pl.num_programs(1)
    nc = ts // CHUNK

    @pl.when(j == 0)
    def _():
        c_scr[...] = c0_ref[0]
        n_scr[...] = n0_ref[0]
        m_scr[...] = m0_ref[0]
        conv_scr[...] = conv0_ref[0]
        s_scr[...] = s0_ref[0]

    for g in range(H):
        lo, hi = g * LANES, (g + 1) * LANES
        x_g = proj_ref[_G_XM + g]
        xbuf[0:SUBLANES, :] = conv_scr[:, lo:hi]
        xbuf[SUBLANES:SUBLANES + ts, :] = x_g
        acc = cb_ref[:, lo:hi] + cw_ref[CONV_W - 1:CONV_W, lo:hi] * x_g
        for t in range(CONV_W - 1):
            off = SUBLANES - (CONV_W - 1) + t
            acc = acc + cw_ref[t:t + 1, lo:hi] * xbuf[off:off + ts, :]
        conv_scr[:, lo:hi] = xbuf[ts:ts + SUBLANES, :]
        xc = acc * _sigmoid(acc)
        xc_scr[g] = xc
        xcb = xc.astype(BF16)
        q_scr[g] = _dot(xcb, wq_ref[g])
        k_scr[g] = _dot(xcb, wk_ref[g]) * (LANES ** -0.5)

    cosv = cos_ref[...]
    sinv = sin_ref[...]
    for g in range(H):
        qb = proj_ref[_G_QB + g]
        kb = proj_ref[_G_KB + g]
        qr_scr[g] = qb * cosv + pltpu.roll(qb, LANES // 2, 1) * sinv
        kr_scr[g] = (kb * cosv + pltpu.roll(kb, LANES // 2, 1) * sinv) * (LANES ** -0.5)

    gpre = gates_ref[...] + gbias_ref[...]
    lane = lax.broadcasted_iota(jnp.int32, (ts, LANES), 1)
    gl = jnp.where(lane < H, gpre, _log_sigmoid(gpre))
    gl_scr[...] = gl
    bc = _dot_sel_lhs(_chunk_tri(ts), gl)
    bc_scr[...] = bc
    eye8 = jnp.where(lax.broadcasted_iota(jnp.int32, (SUBLANES, LANES), 0)
                     == lax.broadcasted_iota(jnp.int32, (SUBLANES, LANES), 1), 1.0, 0.0).astype(BF16)
    for c in range(nc):
        rowi_scr[c] = _dot_sel_nt(eye8, gl[c * CHUNK:(c + 1) * CHUNK, :])
        rowb_scr[c] = _dot_sel_nt(eye8, bc[c * CHUNK:(c + 1) * CHUNK, :])

    ti = lax.broadcasted_iota(jnp.int32, (CHUNK, CHUNK), 0)
    si = lax.broadcasted_iota(jnp.int32, (CHUNK, CHUNK), 1)
    tril = ti >= si
    tcol = lax.broadcasted_iota(jnp.int32, (CHUNK, 1), 0).astype(F32)
    log_gamma = [math.log1p(-2.0 ** (-5 - h)) for h in range(H)]
    for h in range(H):
        dec_scr[h] = jnp.where(tril, jnp.exp((ti - si).astype(F32) * log_gamma[h]), 0.0)

    def chunk_body(c, carry):
        r0 = pl.multiple_of(c * CHUNK, CHUNK)
        rows = pl.ds(r0, CHUNK)
        rowi = rowi_scr[c]
        rowb = rowb_scr[c]
        glc = gl_scr[rows, :]
        bcc = bc_scr[rows, :]
        for h in range(H):
            lo, hi = h * LANES, (h + 1) * LANES
            q = q_scr[h, rows, :]
            k = k_scr[h, rows, :]
            vb = proj_ref[_G_VA + h, rows, :].astype(BF16)
            b_col = bcc[:, H + h:H + h + 1]
            i_col = glc[:, h:h + 1]
            b_row = rowb[H + h:H + h + 1, :]
            i_row = rowi[h:h + 1, :]
            dmat = jnp.where(tril, b_col - b_row + i_row, -jnp.inf)
            a = jnp.max(dmat, axis=-1, keepdims=True)
            m_prev = m_scr[:, h:h + 1]
            inter = b_col + m_prev
            m_t = jnp.maximum(inter, a)
            w_intra = jnp.exp(dmat - m_t)
            w_inter = jnp.exp(inter - m_t)
            qb = q.astype(BF16)
            amat = _dot_nt(qb, k.astype(BF16)) * w_intra
            cst = c_scr[h]
            n_row = n_scr[h]
            num = _dot(amat.astype(BF16), vb) + w_inter * _dot(qb, cst.astype(BF16))
            den = (jnp.sum(amat, axis=-1, keepdims=True)
                   + w_inter * jnp.sum(q * n_row, axis=-1, keepdims=True))
            hout = num / jnp.maximum(jnp.abs(den), jnp.exp(-m_t))
            m_new = m_t[CHUNK - 1:CHUNK, :]
            b_last = b_col[CHUNK - 1:CHUNK, :]
            w_s = jnp.exp(b_last - b_col + i_col - m_new)
            decay = jnp.exp(b_last + m_prev - m_new)
            kw = k * w_s
            c_scr[h] = decay * cst + _dot_tn(kw.astype(BF16), vb)
            n_scr[h] = decay * n_row + jnp.sum(kw, axis=0, keepdims=True)
            m_scr[:, h:h + 1] = m_new
            z = _sigmoid(proj_ref[_G_OA + h, rows, :]) * hout
            u_ref[rows, lo:hi] = (_rms(z) * nwa_ref[:, lo:hi]
                                  + skip_ref[:, lo:hi] * xc_scr[h, rows, :])
        for h in range(H):
            lo, hi = h * LANES, (h + 1) * LANES
            lg = log_gamma[h]
            qrb = qr_scr[h, rows, :].astype(BF16)
            kr = kr_scr[h, rows, :]
            vb = proj_ref[_G_VB + h, rows, :].astype(BF16)
            amat = _dot_nt(qrb, kr.astype(BF16)) * dec_scr[h]
            sst = s_scr[h]
            o = (_dot(amat.astype(BF16), vb)
                 + jnp.exp((tcol + 1.0) * lg) * _dot(qrb, sst.astype(BF16)))
            kws = kr * jnp.exp((CHUNK - 1.0 - tcol) * lg)
            s_scr[h] = math.exp(CHUNK * lg) * sst + _dot_tn(kws.astype(BF16), vb)
            gate = proj_ref[_G_GB + h, rows, :]
            u_ref[rows, D_MODEL // 2 + lo:D_MODEL // 2 + hi] = (
                _rms(o) * nwb_ref[:, lo:hi] * (gate * _sigmoid(gate)))
        return carry

    lax.fori_loop(0, nc, chunk_body, 0)

    @pl.when(j == nj - 1)
    def _():
        co_ref[0] = c_scr[...]
        no_ref[0] = n_scr[...]
        mo_ref[0] = m_scr[...]
        convo_ref[0] = conv_scr[...]
        so_ref[0] = s_scr[...]


def _even_scan_call(proj, gates, cos_t, sin_t, states, weights, *, ts, n_seq, steps, row_off):
    H = N_HEADS_EVEN
    c0, n0, m0, conv0, s0 = states
    ng = proj.shape[0]
    rows_idx = lambda b, j: (row_off + b * steps + j, 0)
    state_specs = [pl.BlockSpec((1, H, LANES, LANES), lambda b, j: (b, 0, 0, 0)),
                   pl.BlockSpec((1, H, 1, LANES), lambda b, j: (b, 0, 0, 0)),
                   pl.BlockSpec((1, 1, LANES), lambda b, j: (b, 0, 0)),
                   pl.BlockSpec((1, SUBLANES, H * LANES), lambda b, j: (b, 0, 0)),
                   pl.BlockSpec((1, H, LANES, LANES), lambda b, j: (b, 0, 0, 0))]
    in_specs = [pl.BlockSpec((ng, ts, LANES), lambda b, j: (0, row_off + b * steps + j, 0)),
                pl.BlockSpec((ts, LANES), rows_idx),
                pl.BlockSpec((ts, LANES), rows_idx),
                pl.BlockSpec((ts, LANES), rows_idx)] + state_specs
    for w in weights:
        in_specs.append(pl.BlockSpec(w.shape, functools.partial(lambda nd, b, j: (0,) * nd, w.ndim)))
    out_specs = [pl.BlockSpec((ts, D_MODEL), lambda b, j: (b * steps + j, 0))] + state_specs
    out_shape = [jax.ShapeDtypeStruct((n_seq * steps * ts, D_MODEL), F32),
                 jax.ShapeDtypeStruct(c0.shape, F32), jax.ShapeDtypeStruct(n0.shape, F32),
                 jax.ShapeDtypeStruct(m0.shape, F32), jax.ShapeDtypeStruct(conv0.shape, F32),
                 jax.ShapeDtypeStruct(s0.shape, F32)]
    nc = ts // CHUNK
    scratch = [pltpu.VMEM((H, LANES, LANES), F32), pltpu.VMEM((H, 1, LANES), F32),
               pltpu.VMEM((1, LANES), F32), pltpu.VMEM((SUBLANES, H * LANES), F32),
               pltpu.VMEM((H, LANES, LANES), F32),
               pltpu.VMEM((ts + 2 * SUBLANES, LANES), F32),
               pltpu.VMEM((H, ts, LANES), F32), pltpu.VMEM((H, ts, LANES), F32),
               pltpu.VMEM((H, ts, LANES), F32), pltpu.VMEM((H, ts, LANES), F32),
               pltpu.VMEM((H, ts, LANES), F32),
               pltpu.VMEM((ts, LANES), F32), pltpu.VMEM((ts, LANES), F32),
               pltpu.VMEM((nc, SUBLANES, CHUNK), F32), pltpu.VMEM((nc, SUBLANES, CHUNK), F32),
               pltpu.VMEM((H, CHUNK, CHUNK), F32)]
    return pl.pallas_call(
        functools.partial(_even_scan_kernel, ts=ts),
        grid=(n_seq, steps),
        in_specs=in_specs, out_specs=out_specs, out_shape=out_shape,
        scratch_shapes=scratch,
        compiler_params=_cparams(("arbitrary", "arbitrary")),
        name="even_scan",
    )(proj, gates, cos_t, sin_t, c0, n0, m0, conv0, s0, *weights)


_G_Q, _G_F, _G_I, _G_G = 0, 8, 16, 24


def _hgrn_intra(q, k, bcum, ti, si, tcol_i):
    amat = jnp.zeros((CHUNK, CHUNK), F32)
    for b in (32, 16, 8):
        nb2 = CHUNK // (2 * b)
        parts = [jnp.broadcast_to(bcum[m * 2 * b + b - 1:m * 2 * b + b, :], (2 * b, LANES))
                 for m in range(nb2)]
        ref = parts[0] if nb2 == 1 else jnp.concatenate(parts, axis=0)
        upper = ((tcol_i // b) % 2) == 1
        ql = jnp.where(upper, q * jnp.exp(jnp.minimum(bcum - ref, 0.0)), 0.0)
        kl = jnp.where(upper, 0.0, k * jnp.exp(jnp.minimum(ref - bcum, 0.0)))
        al = _dot_nt(ql.astype(BF16), kl.astype(BF16))
        amat = amat + jnp.where((ti // (2 * b)) == (si // (2 * b)), al, 0.0)
    nblk = CHUNK // SUBLANES
    b3 = bcum.reshape(nblk, SUBLANES, LANES)
    k3 = k.reshape(nblk, SUBLANES, LANES)
    for jj in range(SUBLANES):
        bj = jnp.broadcast_to(b3[:, jj:jj + 1, :], (nblk, SUBLANES, LANES)).reshape(CHUNK, LANES)
        kj = jnp.broadcast_to(k3[:, jj:jj + 1, :], (nblk, SUBLANES, LANES)).reshape(CHUNK, LANES)
        col = jnp.sum(q * kj * jnp.exp(jnp.minimum(bcum - bj, 0.0)), axis=-1, keepdims=True)
        sel = (si == (ti // SUBLANES) * SUBLANES + jj) & ((ti % SUBLANES) >= jj)
        amat = jnp.where(sel, col, amat)
    return amat


def _odd_scan_kernel(proj_ref, s0_ref, lb_ref, nw_ref, u_ref, so_ref,
                     st_scr, k_scr, bc_scr, *, ts):
    H = N_HEADS_ODD
    j = pl.program_id(1)
    nj = pl.num_programs(1)
    nc = ts // CHUNK

    @pl.when(j == 0)
    def _():
        for h in range(H):
            st_scr[h] = s0_ref[0, h].T

    tri = _chunk_tri(ts)
    for h in range(H):
        lo, hi = h * LANES, (h + 1) * LANES
        lbv = lb_ref[:, lo:hi]
        fpre = proj_ref[_G_F + h]
        logf = _logaddexp(jnp.log(lbv), jnp.log1p(-lbv) + _log_sigmoid(fpre))
        k_scr[h] = (1.0 - lbv) * _sigmoid(-fpre)
        bc_scr[h] = _dot_sel_lhs(tri, logf)

    ti = lax.broadcasted_iota(jnp.int32, (CHUNK, CHUNK), 0)
    si = lax.broadcasted_iota(jnp.int32, (CHUNK, CHUNK), 1)
    tcol_i = lax.broadcasted_iota(jnp.int32, (CHUNK, 1), 0)

    def chunk_body(c, carry):
        r0 = pl.multiple_of(c * CHUNK, CHUNK)
        rows = pl.ds(r0, CHUNK)
        for h in range(H):
            lo, hi = h * LANES, (h + 1) * LANES
            q = proj_ref[_G_Q + h, rows, :]
            k = k_scr[h, rows, :]
            vb = proj_ref[_G_I + h, rows, :].astype(BF16)
            bcum = bc_scr[h, rows, :]
            amat = _hgrn_intra(q, k, bcum, ti, si, tcol_i)
            st = st_scr[h]
            o = (_dot(amat.astype(BF16), vb)
                 + _dot_nt((q * jnp.exp(bcum)).astype(BF16), st.astype(BF16)))
            last = bcum[CHUNK - 1:CHUNK, :]
            kd = k * jnp.exp(last - bcum)
            st_scr[h] = st * jnp.exp(last) + _dot_tn(vb, kd.astype(BF16))
            u_ref[rows, lo:hi] = (_rms(o) * nw_ref[:, lo:hi]
                                  * _sigmoid(proj_ref[_G_G + h, rows, :]))
        return carry

    lax.fori_loop(0, nc, chunk_body, 0)

    @pl.when(j == nj - 1)
    def _():
        for h in range(H):
            so_ref[0, h] = st_scr[h].T


def _odd_scan_call(proj, s0, lb, nw, *, ts, n_seq, steps, row_off):
    H = N_HEADS_ODD
    ng = proj.shape[0]
    st_spec = pl.BlockSpec((1, H, LANES, LANES), lambda b, j: (b, 0, 0, 0))
    return pl.pallas_call(
        functools.partial(_odd_scan_kernel, ts=ts),
        grid=(n_seq, steps),
        in_specs=[pl.BlockSpec((ng, ts, LANES), lambda b, j: (0, row_off + b * steps + j, 0)),
                  st_spec,
                  pl.BlockSpec((1, D_MODEL), lambda b, j: (0, 0)),
                  pl.BlockSpec((1, D_MODEL), lambda b, j: (0, 0))],
        out_specs=[pl.BlockSpec((ts, D_MODEL), lambda b, j: (b * steps + j, 0)), st_spec],
        out_shape=[jax.ShapeDtypeStruct((n_seq * steps * ts, D_MODEL), F32),
                   jax.ShapeDtypeStruct(s0.shape, F32)],
        scratch_shapes=[pltpu.VMEM((H, LANES, LANES), F32),
                        pltpu.VMEM((H, ts, LANES), F32),
                        pltpu.VMEM((H, ts, LANES), F32)],
        compiler_params=_cparams(("arbitrary", "arbitrary")),
        name="odd_scan",
    )(proj, s0, lb, nw)


def _post_kernel(x_ref, up_ref, us_ref, mod_ref, wout_ref, nw_ref, wr_ref, br_ref,
                 xo_ref, hf_ref, eidx_ref, gate_ref, rank_ref, cnt_ref, run_scr,
                 *, tm, npc, n_prompt_tiles):
    i = pl.program_id(0)
    nch = tm // CHUNK

    @pl.when(i == 0)
    def _():
        run_scr[...] = jnp.zeros_like(run_scr)

    u = jnp.where(i < n_prompt_tiles, up_ref[...], us_ref[...])
    y = _dot(u.astype(BF16), wout_ref[...])
    for c in range(nch):
        seq = _seq_row(i, nch, c, npc)
        gm = mod_ref[pl.ds(seq, 1), 2 * D_MODEL:3 * D_MODEL]
        shf = mod_ref[pl.ds(seq, 1), 3 * D_MODEL:4 * D_MODEL]
        scf = mod_ref[pl.ds(seq, 1), 4 * D_MODEL:5 * D_MODEL]
        rs = slice(c * CHUNK, (c + 1) * CHUNK)
        xn = x_ref[rs, :] + gm * y[rs, :]
        xo_ref[rs, :] = xn
        hf_ref[rs, :] = _rms(xn) * nw_ref[...] * (1.0 + scf) + shf

    logits = _dot3(hf_ref[...], wr_ref[...]) + br_ref[...]
    lane_i = lax.broadcasted_iota(jnp.int32, (tm, LANES), 1)
    lane_f = lane_i.astype(F32)
    vals, idxs = [], []
    cur = logits
    for _ in range(TOP_K):
        m = jnp.max(cur, axis=-1, keepdims=True)
        idx = jnp.min(jnp.where(cur == m, lane_f, float(LANES)), axis=-1, keepdims=True)
        vals.append(m)
        idxs.append(idx)
        cur = jnp.where(lane_f == idx, -jnp.inf, cur)
    exps = [jnp.exp(v - vals[0]) for v in vals]
    denom = exps[0] + exps[1] + exps[2] + exps[3]
    onehot = jnp.zeros((tm, LANES), F32)
    for idx in idxs:
        onehot = onehot + jnp.where(lane_f == idx, 1.0, 0.0)
    r = lax.broadcasted_iota(jnp.int32, (tm, tm), 0)
    cidx = lax.broadcasted_iota(jnp.int32, (tm, tm), 1)
    strict = jnp.where(cidx < r, 1.0, 0.0).astype(BF16)
    before = _dot(strict, onehot.astype(BF16)) + run_scr[...]
    eidx_o = jnp.zeros((tm, LANES), F32)
    gate_o = jnp.zeros((tm, LANES), F32)
    rank_o = jnp.zeros((tm, LANES), F32)
    for kk in range(TOP_K):
        rk = jnp.sum(jnp.where(lane_f == idxs[kk], before, 0.0), axis=-1, keepdims=True)
        eidx_o = jnp.where(lane_i == kk, idxs[kk], eidx_o)
        gate_o = jnp.where(lane_i == kk, exps[kk] / denom, gate_o)
        rank_o = jnp.where(lane_i == kk, rk, rank_o)
    eidx_ref[...] = eidx_o
    gate_ref[...] = gate_o
    rank_ref[...] = rank_o
    run_scr[...] = run_scr[...] + jnp.sum(onehot, axis=0, keepdims=True)
    cnt_ref[...] = run_scr[...]


def _post_call(x, u_p, u_s, mod_l, w_out, nw, w_r, b_r, npc):
    ttot = x.shape[0]
    tm = TOKEN_TILE
    npt = u_p.shape[0] // tm
    tile = lambda i: (i, 0)
    const = lambda i: (0, 0)
    return pl.pallas_call(
        functools.partial(_post_kernel, tm=tm, npc=npc, n_prompt_tiles=npt),
        grid=(ttot // tm,),
        in_specs=[pl.BlockSpec((tm, D_MODEL), tile),
                  pl.BlockSpec((tm, D_MODEL), lambda i: (jnp.minimum(i, npt - 1), 0)),
                  pl.BlockSpec((tm, D_MODEL), lambda i: (jnp.maximum(i - npt, 0), 0)),
                  pl.BlockSpec(mod_l.shape, const),
                  pl.BlockSpec(w_out.shape, const),
                  pl.BlockSpec((1, D_MODEL), const),
                  pl.BlockSpec(w_r.shape, const),
                  pl.BlockSpec((1, LANES), const)],
        out_specs=[pl.BlockSpec((tm, D_MODEL), tile), pl.BlockSpec((tm, D_MODEL), tile),
                   pl.BlockSpec((tm, LANES), tile), pl.BlockSpec((tm, LANES), tile),
                   pl.BlockSpec((tm, LANES), tile), pl.BlockSpec((1, LANES), const)],
        out_shape=[jax.ShapeDtypeStruct((ttot, D_MODEL), F32), jax.ShapeDtypeStruct((ttot, D_MODEL), F32),
                   jax.ShapeDtypeStruct((ttot, LANES), F32), jax.ShapeDtypeStruct((ttot, LANES), F32),
                   jax.ShapeDtypeStruct((ttot, LANES), F32), jax.ShapeDtypeStruct((1, LANES), F32)],
        scratch_shapes=[pltpu.VMEM((1, LANES), F32)],
        compiler_params=_cparams(("arbitrary",)),
        name="post",
    )(x, u_p, u_s, mod_l, w_out, nw, w_r, b_r)


def _plan_kernel(eidx_ref, rank_ref, cnt_ref, pos_ref, blk_ref, ends_ref, *, tm, nbp):
    cnt = jnp.broadcast_to(cnt_ref[...], (SUBLANES, LANES))
    nblk = jnp.floor((cnt + (EXPERT_ROWS - 1.0)) * (1.0 / EXPERT_ROWS))
    r = lax.broadcasted_iota(jnp.int32, (LANES, LANES), 0)
    c = lax.broadcasted_iota(jnp.int32, (LANES, LANES), 1)
    upper = jnp.where(r <= c, 1.0, 0.0).astype(BF16)
    p1, p2, p3 = _split3(nblk)
    ends = _dot(p1, upper) + (_dot(p2, upper) + _dot(p3, upper))
    start_row = (ends[0:1, :] - nblk[0:1, :]) * float(EXPERT_ROWS)
    lane_i = lax.broadcasted_iota(jnp.int32, (tm, LANES), 1)
    lane_f = lane_i.astype(F32)
    eidx = eidx_ref[...]
    rank = rank_ref[...]
    pos = jnp.zeros((tm, LANES), F32)
    for kk in range(TOP_K):
        base = jnp.sum(jnp.where(lane_f == eidx[:, kk:kk + 1], start_row, 0.0), axis=-1, keepdims=True)
        pos = jnp.where(lane_i == kk, base + rank[:, kk:kk + 1], pos)
    pos_ref[...] = pos.astype(jnp.int32)

    @pl.when(pl.program_id(0) == 0)
    def _():
        bi = lax.broadcasted_iota(jnp.int32, (nbp, LANES), 0).astype(F32)
        li = lax.broadcasted_iota(jnp.int32, (nbp, LANES), 1)
        done = jnp.where((li < N_EXPERTS) & (ends[0:1, :] <= bi), 1.0, 0.0)
        be = jnp.minimum(jnp.sum(done, axis=-1, keepdims=True), N_EXPERTS - 1.0)
        blk_ref[...] = jnp.broadcast_to(be, (nbp, LANES)).astype(jnp.int32)
        ends_ref[...] = ends.astype(jnp.int32)


def _plan_call(eidx, rank, cnt, nbp):
    ttot = eidx.shape[0]
    tm = TOKEN_TILE
    tile = lambda i: (i, 0)
    const = lambda i: (0, 0)
    return pl.pallas_call(
        functools.partial(_plan_kernel, tm=tm, nbp=nbp),
        grid=(ttot // tm,),
        in_specs=[pl.BlockSpec((tm, LANES), tile), pl.BlockSpec((tm, LANES), tile),
                  pl.BlockSpec((1, LANES), const)],
        out_specs=[pl.BlockSpec((tm, LANES), tile), pl.BlockSpec((nbp, LANES), const),
                   pl.BlockSpec((SUBLANES, LANES), const)],
        out_shape=[jax.ShapeDtypeStruct((ttot, LANES), jnp.int32),
                   jax.ShapeDtypeStruct((nbp, LANES), jnp.int32),
                   jax.ShapeDtypeStruct((SUBLANES, LANES), jnp.int32)],
        compiler_params=_cparams(("arbitrary",)),
        name="plan",
    )(eidx, rank, cnt)


def _dispatch_kernel(pos_ref, hf_ref, xs_in_ref, xs_ref, sem, *, tm):
    del xs_in_ref

    def row_copy(t, p):
        return pltpu.make_async_copy(hf_ref.at[pl.ds(t, 1), :], xs_ref.at[pl.ds(p, 1), :], sem)

    def issue(t, carry):
        for kk in range(TOP_K):
            row_copy(t, pos_ref[t * TOP_K + kk]).start()
        return carry

    lax.fori_loop(0, tm, issue, 0)

    def drain(t, carry):
        for kk in range(TOP_K):
            row_copy(t, pos_ref[t * TOP_K + kk]).wait()
        return carry

    lax.fori_loop(0, tm, drain, 0)


def _dispatch_call(pos_flat, hf, xs):
    ttot = hf.shape[0]
    tm = TOKEN_TILE
    return pl.pallas_call(
        functools.partial(_dispatch_kernel, tm=tm),
        grid=(ttot // tm,),
        in_specs=[pl.BlockSpec((tm * TOP_K,), lambda i: (i,), memory_space=pltpu.SMEM),
                  pl.BlockSpec((tm, D_MODEL), lambda i: (i, 0)),
                  pl.BlockSpec(memory_space=pl.ANY)],
        out_specs=pl.BlockSpec(memory_space=pl.ANY),
        out_shape=jax.ShapeDtypeStruct(xs.shape, xs.dtype),
        scratch_shapes=[pltpu.SemaphoreType.DMA(())],
        input_output_aliases={2: 0},
        compiler_params=pltpu.CompilerParams(dimension_semantics=("arbitrary",),
                                             vmem_limit_bytes=VMEM_LIMIT, has_side_effects=True),
        name="dispatch",
    )(pos_flat, hf, xs)


def _expert_kernel(be_ref, nu_ref, xs_ref, wgu_ref, bgu_ref, wdn_ref, bdn_ref, y_ref,
                   wgu_bf, wdn_bf):
    b = pl.program_id(0)

    @pl.when(b < nu_ref[0])
    def _():
        prev = be_ref[jnp.maximum(b - 1, 0)]

        @pl.when((b == 0) | (be_ref[b] != prev))
        def _():
            wgu_bf[...] = wgu_ref[0, 0].astype(BF16)
            wdn_bf[...] = wdn_ref[0, 0].astype(BF16)

        gu = _dot(xs_ref[...].astype(BF16), wgu_bf[...]) + bgu_ref[0, 0]
        g = jnp.minimum(gu[:, :D_MODEL], SWIGLU_LIMIT)
        u = jnp.clip(gu[:, D_MODEL:], -SWIGLU_LIMIT, SWIGLU_LIMIT)
        act = (u + 1.0) * (g * _sigmoid(SWIGLU_ALPHA * g))
        y_ref[...] = _dot(act.astype(BF16), wdn_bf[...]) + bdn_ref[0, 0]

    @pl.when(b >= nu_ref[0])
    def _():
        y_ref[...] = jnp.zeros_like(y_ref)


def _expert_call(blk_expert, n_used, xs, w_gu, b_gu, w_dn, b_dn, layer):
    nb = xs.shape[0] // EXPERT_ROWS
    d_ff2 = w_gu.shape[3]
    blk = lambda b, be, nu: (jnp.minimum(b, nu[0] - 1), 0)
    blk_out = lambda b, be, nu: (b, 0)
    exp4 = lambda b, be, nu: (layer, be[jnp.minimum(b, nu[0] - 1)], 0, 0)
    grid_spec = pltpu.PrefetchScalarGridSpec(
        num_scalar_prefetch=2, grid=(nb,),
        in_specs=[pl.BlockSpec((EXPERT_ROWS, D_MODEL), blk),
                  pl.BlockSpec((1, 1, D_MODEL, d_ff2), exp4),
                  pl.BlockSpec((1, 1, 1, d_ff2), exp4),
                  pl.BlockSpec((1, 1, D_MODEL, D_MODEL), exp4),
                  pl.BlockSpec((1, 1, 1, D_MODEL), exp4)],
        out_specs=pl.BlockSpec((EXPERT_ROWS, D_MODEL), blk_out),
        scratch_shapes=[pltpu.VMEM((D_MODEL, d_ff2), BF16), pltpu.VMEM((D_MODEL, D_MODEL), BF16)])
    return pl.pallas_call(
        _expert_kernel,
        grid_spec=grid_spec,
        out_shape=jax.ShapeDtypeStruct(xs.shape, F32),
        compiler_params=_cparams(("arbitrary",)),
        name="experts",
    )(blk_expert, n_used, xs, w_gu, b_gu, w_dn, b_dn)


def _combine_kernel(*refs, tm, npc, final):
    if final:
        pos_ref, x_ref, gate_ref, mod_ref, fnw_ref, y_hbm, xo_ref, yf_ref, buf, sem = refs
    else:
        pos_ref, x_ref, gate_ref, mod_ref, y_hbm, xo_ref, buf, sem = refs
    i = pl.program_id(0)
    nch = tm // CHUNK

    def row_copy(t, kk, p):
        return pltpu.make_async_copy(y_hbm.at[pl.ds(p, 1), :], buf.at[kk, pl.ds(t, 1), :], sem)

    def issue(t, carry):
        for kk in range(TOP_K):
            row_copy(t, kk, pos_ref[t * TOP_K + kk]).start()
        return carry

    lax.fori_loop(0, tm, issue, 0)

    def drain(t, carry):
        for kk in range(TOP_K):
            row_copy(t, kk, pos_ref[t * TOP_K + kk]).wait()
        return carry

    lax.fori_loop(0, tm, drain, 0)

    gate = gate_ref[...]
    acc = gate[:, 0:1] * buf[0]
    for kk in range(1, TOP_K):
        acc = acc + gate[:, kk:kk + 1] * buf[kk]
    for c in range(nch):
        seq = _seq_row(i, nch, c, npc)
        gf = mod_ref[pl.ds(seq, 1), 5 * D_MODEL:6 * D_MODEL]
        rs = slice(c * CHUNK, (c + 1) * CHUNK)
        xn = x_ref[rs, :] + gf * acc[rs, :]
        xo_ref[rs, :] = xn
        if final:
            yf_ref[rs, :] = _rms(xn) * fnw_ref[...]


def _combine_call(pos_flat, x, gate, mod_l, y_sorted, npc, final_w):
    ttot = x.shape[0]
    tm = COMBINE_TILE
    final = final_w is not None
    tile = lambda i: (i, 0)
    const = lambda i: (0, 0)
    in_specs = [pl.BlockSpec((tm * TOP_K,), lambda i: (i,), memory_space=pltpu.SMEM),
                pl.BlockSpec((tm, D_MODEL), tile),
                pl.BlockSpec((tm, LANES), tile),
                pl.BlockSpec(mod_l.shape, const)]
    args = [pos_flat, x, gate, mod_l]
    if final:
        in_specs.append(pl.BlockSpec((1, D_MODEL), const))
        args.append(final_w)
    in_specs.append(pl.BlockSpec(memory_space=pl.ANY))
    args.append(y_sorted)
    out_specs = [pl.BlockSpec((tm, D_MODEL), tile)]
    out_shape = [jax.ShapeDtypeStruct((ttot, D_MODEL), F32)]
    if final:
        out_specs.append(pl.BlockSpec((tm, D_MODEL), tile))
        out_shape.append(jax.ShapeDtypeStruct((ttot, D_MODEL), F32))
    return pl.pallas_call(
        functools.partial(_combine_kernel, tm=tm, npc=npc, final=final),
        grid=(ttot // tm,),
        in_specs=in_specs, out_specs=out_specs, out_shape=out_shape,
        scratch_shapes=[pltpu.VMEM((TOP_K, tm, D_MODEL), F32), pltpu.SemaphoreType.DMA(())],
        compiler_params=_cparams(("arbitrary",)),
        name="combine",
    )(*args)


def kernel(x_prompt, x_sample, c_prompt, c_sample, state_mlstm_C, state_mlstm_n, state_mlstm_m, state_mlstm_conv, state_ret_S, state_hgrn_S, w_ada, b_ada, norm_mix_w, norm_ffn_w, final_norm_w, w_in_even, b_mlstm_i, b_mlstm_f, w_mlstm_conv, b_mlstm_conv, w_mlstm_q, w_mlstm_k, mlstm_skip, mlstm_norm_w, ret_norm_w, w_out_even, w_in_odd, hgrn_lb_logits, hgrn_norm_w, w_out_odd, moe_router_w, moe_router_b, moe_w_gate_up, moe_b_gate_up, moe_w_down, moe_b_down):
    bp, seq, d = x_prompt.shape
    bs, dseq, _ = x_sample.shape
    assert bp == 1 and d == D_MODEL and dseq == CHUNK
    assert seq % TOKEN_TILE == 0 and (bs * dseq) % TOKEN_TILE == 0
    depth = w_ada.shape[0]
    tp, tsmp = seq, bs * dseq
    ttot = tp + tsmp
    npc = tp // CHUNK
    he, ho = N_HEADS_EVEN, N_HEADS_ODD
    da = he * LANES
    past_len = 1024

    x = jnp.concatenate([x_prompt.reshape(tp, d), x_sample.reshape(tsmp, d)], axis=0)
    n_mod_rows = 2 * SUBLANES
    assert 1 + bs <= n_mod_rows
    c_all = jnp.zeros((n_mod_rows, d), F32).at[0:1].set(c_prompt).at[1:1 + bs].set(c_sample)
    mod = _ada_call(c_all, w_ada, b_ada)

    half = LANES // 2
    inv = ROPE_BASE ** (-jnp.arange(half, dtype=F32) / half)
    pos_all = jnp.concatenate([jnp.arange(tp, dtype=F32),
                               jnp.tile(past_len + jnp.arange(dseq, dtype=F32), bs)])
    ang = pos_all[:, None] * inv[None, :]
    cos_t = jnp.concatenate([jnp.cos(ang), jnp.cos(ang)], axis=-1)
    sin_t = jnp.concatenate([-jnp.sin(ang), jnp.sin(ang)], axis=-1)

    lb_p = jax.nn.softmax(hgrn_lb_logits.astype(F32), axis=0)
    lbs = jnp.cumsum(lb_p, axis=0) - lb_p[0]

    n_rows_sorted = (-(-(ttot * TOP_K) // EXPERT_ROWS) + N_EXPERTS) * EXPERT_ROWS
    nb = n_rows_sorted // EXPERT_ROWS
    nbp = -(-nb // SUBLANES) * SUBLANES
    xs = jnp.zeros((n_rows_sorted, d), F32)

    ts_p = TOKEN_TILE
    steps_p = tp // ts_p
    even_out, odd_out = [], []
    y_final = None
    for l in range(depth):
        jl = l // 2
        mod_l = mod[l]
        if l % 2 == 0:
            w_in = w_in_even[jl]
            w_main = jnp.concatenate([w_in[:, :3 * da], w_in[:, 3 * da + 2 * he:]], axis=1).astype(BF16)
            w_gate = jnp.zeros((d, LANES), F32).at[:, :2 * he].set(w_in[:, 3 * da:3 * da + 2 * he])
            proj, gates = _inproj_call(x, mod_l, norm_mix_w[l][None], w_main, w_gate, npc)
            gbias = jnp.zeros((1, LANES), F32).at[0, :he].set(b_mlstm_i[jl]).at[0, he:2 * he].set(b_mlstm_f[jl])
            weights = [w_mlstm_conv[jl], b_mlstm_conv[jl][None],
                       w_mlstm_q[jl].astype(BF16), w_mlstm_k[jl].astype(BF16), gbias,
                       mlstm_skip[jl][None], mlstm_norm_w[jl][None], ret_norm_w[jl][None]]
            zeros_p = (jnp.zeros((1, he, LANES, LANES), F32), jnp.zeros((1, he, 1, LANES), F32),
                       jnp.zeros((1, 1, LANES), F32), jnp.zeros((1, SUBLANES, da), F32),
                       jnp.zeros((1, he, LANES, LANES), F32))
            st_s = (state_mlstm_C[jl], state_mlstm_n[jl][:, :, None, :],
                    jnp.zeros((bs, 1, LANES), F32).at[:, 0, :he].set(state_mlstm_m[jl]),
                    jnp.zeros((bs, SUBLANES, da), F32).at[:, SUBLANES - (CONV_W - 1):].set(state_mlstm_conv[jl]),
                    state_ret_S[jl])
            res_p = _even_scan_call(proj, gates, cos_t, sin_t, zeros_p, weights,
                                    ts=ts_p, n_seq=1, steps=steps_p, row_off=0)
            res_s = _even_scan_call(proj, gates, cos_t, sin_t, st_s, weights,
                                    ts=CHUNK, n_seq=bs, steps=1, row_off=npc)
            u_p, u_s = res_p[0], res_s[0]
            even_out.append((res_p[1:], res_s[1:]))
            w_out = w_out_even[jl].astype(BF16)
        else:
            proj = _inproj_call(x, mod_l, norm_mix_w[l][None], w_in_odd[jl].astype(BF16), None, npc)[0]
            lb = lbs[l][None]
            nw = hgrn_norm_w[jl][None]
            u_p, sp = _odd_scan_call(proj, jnp.zeros((1, ho, LANES, LANES), F32), lb, nw,
                                     ts=ts_p, n_seq=1, steps=steps_p, row_off=0)
            u_s, ss = _odd_scan_call(proj, state_hgrn_S[jl], lb, nw,
                                     ts=CHUNK, n_seq=bs, steps=1, row_off=npc)
            odd_out.append((sp, ss))
            w_out = w_out_odd[jl].astype(BF16)

        w_r = jnp.zeros((d, LANES), F32).at[:, :N_EXPERTS].set(moe_router_w[l])
        b_r = jnp.full((1, LANES), -jnp.inf, F32).at[0, :N_EXPERTS].set(moe_router_b[l])
        x, hf, eidx, gate, rank, cnt = _post_call(x, u_p, u_s, mod_l, w_out, norm_ffn_w[l][None], w_r, b_r, npc)
        pos, blk, ends = _plan_call(eidx, rank, cnt, nbp)
        pos_flat = pos[:, :TOP_K].reshape(-1)
        blk_expert = blk[:nb, 0]
        n_used = ends[0, N_EXPERTS - 1:N_EXPERTS]
        xs = _dispatch_call(pos_flat, hf, xs)
        y_sorted = _expert_call(blk_expert, n_used, xs, moe_w_gate_up, moe_b_gate_up[:, :, None, :],
                                moe_w_down, moe_b_down[:, :, None, :], l)
        if l == depth - 1:
            x, y_final = _combine_call(pos_flat, x, gate, mod_l, y_sorted, npc, final_norm_w[None])
        else:
            x = _combine_call(pos_flat, x, gate, mod_l, y_sorted, npc, None)[0]

    def even_states(which):
        cs = jnp.stack([e[which][0] for e in even_out])
        ns = jnp.stack([e[which][1][:, :, 0, :] for e in even_out])
        ms = jnp.stack([e[which][2][:, 0, :he] for e in even_out])
        cv = jnp.stack([e[which][3][:, SUBLANES - (CONV_W - 1):, :] for e in even_out])
        ss = jnp.stack([e[which][4] for e in even_out])
        return cs, ns, ms, cv, ss

    p_c, p_n, p_m, p_cv, p_s = even_states(0)
    s_c, s_n, s_m, s_cv, s_s = even_states(1)
    p_h = jnp.stack([o[0] for o in odd_out])
    s_h = jnp.stack([o[1] for o in odd_out])
    y_prompt = y_final[:tp].reshape(bp, seq, d)
    y_sample = y_final[tp:].reshape(bs, dseq, d)
    return (y_prompt, y_sample, p_c, p_n, p_m, p_cv, p_s, p_h, s_c, s_n, s_m, s_cv, s_s, s_h)
```

```python
import functools
import math

import jax
import jax.numpy as jnp
from jax import lax
from jax.experimental import pallas as pl
from jax.experimental.pallas import tpu as pltpu

F32 = jnp.float32
BF16 = jnp.bfloat16

CHUNK = 64
LANES = 128
SUBLANES = 8
D_MODEL = 1024
N_HEADS_EVEN = 4
N_HEADS_ODD = 8
CONV_W = 4
N_EXPERTS = 32
TOP_K = 4
SWIGLU_LIMIT = 7.0
SWIGLU_ALPHA = 1.702
EPS = 1e-6
ROPE_BASE = 10000.0

TOKEN_TILE = 512
INPROJ_TILE = 256
DISPATCH_TILE = 256
SORT_ROWS = DISPATCH_TILE * TOP_K + LANES * 2
XS_WIDTH = D_MODEL + LANES
EXPERT_ROWS = 512
VMEM_LIMIT = 56 * 1024 * 1024


def _cparams(sem, vmem=VMEM_LIMIT):
    return pltpu.CompilerParams(dimension_semantics=sem, vmem_limit_bytes=vmem)


def _dot(a, b):
    return jnp.dot(a, b, preferred_element_type=F32)


def _dot_nt(a, b):
    return lax.dot_general(a, b, (((1,), (1,)), ((), ())), preferred_element_type=F32)


def _dot_tn(a, b):
    return lax.dot_general(a, b, (((0,), (0,)), ((), ())), preferred_element_type=F32)


def _split3(x):
    p1 = x.astype(BF16)
    r1 = x - p1.astype(F32)
    p2 = r1.astype(BF16)
    p3 = (r1 - p2.astype(F32)).astype(BF16)
    return p1, p2, p3


def _dot3(a, b):
    ah = a.astype(BF16)
    al = (a - ah.astype(F32)).astype(BF16)
    bh = b.astype(BF16)
    bl = (b - bh.astype(F32)).astype(BF16)
    return _dot(ah, bh) + (_dot(ah, bl) + _dot(al, bh))


def _dot_sel_lhs(sel_bf16, x):
    p1, p2, p3 = _split3(x)
    return _dot(sel_bf16, p1) + (_dot(sel_bf16, p2) + _dot(sel_bf16, p3))


def _dot_sel_nt(sel_bf16, x):
    p1, p2, p3 = _split3(x)
    return _dot_nt(sel_bf16, p1) + (_dot_nt(sel_bf16, p2) + _dot_nt(sel_bf16, p3))


def _dot_sel_rhs(x, sel_bf16):
    p1, p2, p3 = _split3(x)
    return _dot(p1, sel_bf16) + (_dot(p2, sel_bf16) + _dot(p3, sel_bf16))


def _dot_sel_lhs2(sel_bf16, x):
    hi = x.astype(BF16)
    lo = (x - hi.astype(F32)).astype(BF16)
    return _dot(sel_bf16, hi) + _dot(sel_bf16, lo)


def _sigmoid(x):
    return 1.0 / (1.0 + jnp.exp(-x))


def _log_sigmoid(x):
    return jnp.minimum(x, 0.0) - jnp.log1p(jnp.exp(-jnp.abs(x)))


def _logaddexp(a, b):
    return jnp.maximum(a, b) + jnp.log1p(jnp.exp(-jnp.abs(a - b)))


def _rms(x):
    return x * lax.rsqrt(jnp.mean(x * x, axis=-1, keepdims=True) + EPS)


def _chunk_tri(n):
    r = lax.broadcasted_iota(jnp.int32, (n, n), 0)
    c = lax.broadcasted_iota(jnp.int32, (n, n), 1)
    return jnp.where((r // CHUNK == c // CHUNK) & (c <= r), 1.0, 0.0).astype(BF16)


def _seq_row(tile_idx, chunks_per_tile, c, n_prompt_chunks):
    return jnp.maximum(tile_idx * chunks_per_tile + c - (n_prompt_chunks - 1), 0)


def _ada_kernel(c_ref, w_ref, b_ref, o_ref):
    c = c_ref[...]
    o_ref[0] = _dot3(c * _sigmoid(c), w_ref[0]) + b_ref[0]


def _ada_call(c_all, w_ada, b_ada):
    depth = w_ada.shape[0]
    nrow = c_all.shape[0]
    ncol = w_ada.shape[2] // D_MODEL
    return pl.pallas_call(
        _ada_kernel,
        grid=(depth, ncol),
        in_specs=[pl.BlockSpec((nrow, D_MODEL), lambda l, j: (0, 0)),
                  pl.BlockSpec((1, D_MODEL, D_MODEL), lambda l, j: (l, 0, j)),
                  pl.BlockSpec((1, 1, D_MODEL), lambda l, j: (l, 0, j))],
        out_specs=pl.BlockSpec((1, nrow, D_MODEL), lambda l, j: (l, 0, j)),
        out_shape=jax.ShapeDtypeStruct((depth, nrow, ncol * D_MODEL), F32),
        compiler_params=_cparams(("arbitrary", "arbitrary")),
        name="ada",
    )(c_all, w_ada, b_ada.reshape(depth, 1, -1))


def _inproj_kernel(*refs, tm, ng, npc, has_gates):
    if has_gates:
        x_ref, mod_ref, nw_ref, w_ref, wg_ref, proj_ref, gates_ref, h_scr = refs
    else:
        x_ref, mod_ref, nw_ref, w_ref, proj_ref, h_scr = refs
    i = pl.program_id(0)
    nch = tm // CHUNK
    for c in range(nch):
        seq = _seq_row(i, nch, c, npc)
        sh = mod_ref[pl.ds(seq, 1), 0:D_MODEL]
        sc = mod_ref[pl.ds(seq, 1), D_MODEL:2 * D_MODEL]
        xc = x_ref[c * CHUNK:(c + 1) * CHUNK, :]
        h_scr[c * CHUNK:(c + 1) * CHUNK, :] = _rms(xc) * nw_ref[...] * (1.0 + sc) + sh
    h = h_scr[...]
    hb = h.astype(BF16)
    for g in range(0, ng, 4):
        res = _dot(hb, w_ref[:, g * LANES:(g + 4) * LANES])
        for jj in range(4):
            proj_ref[g + jj] = res[:, jj * LANES:(jj + 1) * LANES]
    if has_gates:
        gates_ref[...] = _dot3(h, wg_ref[...])


def _inproj_call(x, mod_l, nw, w_main, w_gate, npc):
    ttot = x.shape[0]
    tm = INPROJ_TILE
    ng = w_main.shape[1] // LANES
    has_gates = w_gate is not None
    in_specs = [pl.BlockSpec((tm, D_MODEL), lambda i: (i, 0)),
                pl.BlockSpec(mod_l.shape, lambda i: (0, 0)),
                pl.BlockSpec((1, D_MODEL), lambda i: (0, 0)),
                pl.BlockSpec(w_main.shape, lambda i: (0, 0))]
    out_specs = [pl.BlockSpec((ng, tm, LANES), lambda i: (0, i, 0))]
    out_shape = [jax.ShapeDtypeStruct((ng, ttot, LANES), F32)]
    args = [x, mod_l, nw, w_main]
    if has_gates:
        in_specs.append(pl.BlockSpec(w_gate.shape, lambda i: (0, 0)))
        out_specs.append(pl.BlockSpec((tm, LANES), lambda i: (i, 0)))
        out_shape.append(jax.ShapeDtypeStruct((ttot, LANES), F32))
        args.append(w_gate)
    return pl.pallas_call(
        functools.partial(_inproj_kernel, tm=tm, ng=ng, npc=npc, has_gates=has_gates),
        grid=(ttot // tm,),
        in_specs=in_specs, out_specs=out_specs, out_shape=out_shape,
        scratch_shapes=[pltpu.VMEM((tm, D_MODEL), F32)],
        compiler_params=_cparams(("arbitrary",)),
        name="inproj",
    )(*args)


_G_XM, _G_VA, _G_OA, _G_QB, _G_KB, _G_VB, _G_GB = 0, 4, 8, 12, 16, 20, 24


def _even_scan_kernel(proj_ref, gates_ref, cos_ref, sin_ref,
                      c0_ref, n0_ref, m0_ref, conv0_ref, s0_ref,
                      cw_ref, cb_ref, wq_ref, wk_ref, gbias_ref, skip_ref, nwa_ref, nwb_ref,
                      u_ref, co_ref, no_ref, mo_ref, convo_ref, so_ref,
                      c_scr, n_scr, m_scr, conv_scr, s_scr,
                      xbuf, xc_scr, q_scr, k_scr, qr_scr, kr_scr,
                      gl_scr, bc_scr, rowi_scr, rowb_scr, dec_scr, *, ts):
    H = N_HEADS_EVEN
    j = pl.program_id(1)
    nj = pl.num_programs(1)
    nc = ts // CHUNK

    @pl.when(j == 0)
    def _():
        c_scr[...] = c0_ref[0]
        n_scr[...] = n0_ref[0]
        m_scr[...] = m0_ref[0]
        conv_scr[...] = conv0_ref[0]
        s_scr[...] = s0_ref[0]

    for g in range(H):
        lo, hi = g * LANES, (g + 1) * LANES
        x_g = proj_ref[_G_XM + g]
        xbuf[0:SUBLANES, :] = conv_scr[:, lo:hi]
        xbuf[SUBLANES:SUBLANES + ts, :] = x_g
        acc = cb_ref[:, lo:hi] + cw_ref[CONV_W - 1:CONV_W, lo:hi] * x_g
        for t in range(CONV_W - 1):
            off = SUBLANES - (CONV_W - 1) + t
            acc = acc + cw_ref[t:t + 1, lo:hi] * xbuf[off:off + ts, :]
        conv_scr[:, lo:hi] = xbuf[ts:ts + SUBLANES, :]
        xc = acc * _sigmoid(acc)
        xc_scr[g] = xc
        xcb = xc.astype(BF16)
        q_scr[g] = _dot(xcb, wq_ref[g])
        k_scr[g] = _dot(xcb, wk_ref[g]) * (LANES ** -0.5)

    cosv = cos_ref[...]
    sinv = sin_ref[...]
    for g in range(H):
        qb = proj_ref[_G_QB + g]
        kb = proj_ref[_G_KB + g]
        qr_scr[g] = qb * cosv + pltpu.roll(qb, LANES // 2, 1) * sinv
        kr_scr[g] = (kb * cosv + pltpu.roll(kb, LANES // 2, 1) * sinv) * (LANES ** -0.5)

    gpre = gates_ref[...] + gbias_ref[...]
    lane = lax.broadcasted_iota(jnp.int32, (ts, LANES), 1)
    gl = jnp.where(lane < H, gpre, _log_sigmoid(gpre))
    gl_scr[...] = gl
    bc = _dot_sel_lhs(_chunk_tri(ts), gl)
    bc_scr[...] = bc
    eye8 = jnp.where(lax.broadcasted_iota(jnp.int32, (SUBLANES, LANES), 0)
                     == lax.broadcasted_iota(jnp.int32, (SUBLANES, LANES), 1), 1.0, 0.0).astype(BF16)
    for c in range(nc):
        rowi_scr[c] = _dot_sel_nt(eye8, gl[c * CHUNK:(c + 1) * CHUNK, :])
        rowb_scr[c] = _dot_sel_nt(eye8, bc[c * CHUNK:(c + 1) * CHUNK, :])

    ti = lax.broadcasted_iota(jnp.int32, (CHUNK, CHUNK), 0)
    si = lax.broadcasted_iota(jnp.int32, (CHUNK, CHUNK), 1)
    tril = ti >= si
    tcol = lax.broadcasted_iota(jnp.int32, (CHUNK, 1), 0).astype(F32)
    log_gamma = [math.log1p(-2.0 ** (-5 - h)) for h in range(H)]
    for h in range(H):
        dec_scr[h] = jnp.where(tril, jnp.exp((ti - si).astype(F32) * log_gamma[h]), 0.0)

    def chunk_body(c, carry):
        r0 = pl.multiple_of(c * CHUNK, CHUNK)
        rows = pl.ds(r0, CHUNK)
        rowi = rowi_scr[c]
        rowb = rowb_scr[c]
        glc = gl_scr[rows, :]
        bcc = bc_scr[rows, :]
        for h in range(H):
            lo, hi = h * LANES, (h + 1) * LANES
            q = q_scr[h, rows, :]
            k = k_scr[h, rows, :]
            vb = proj_ref[_G_VA + h, rows, :].astype(BF16)
            b_col = bcc[:, H + h:H + h + 1]
            i_col = glc[:, h:h + 1]
            b_row = rowb[H + h:H + h + 1, :]
            i_row = rowi[h:h + 1, :]
            dmat = jnp.where(tril, b_col - b_row + i_row, -jnp.inf)
            a = jnp.max(dmat, axis=-1, keepdims=True)
            m_prev = m_scr[:, h:h + 1]
            inter = b_col + m_prev
            m_t = jnp.maximum(inter, a)
            w_intra = jnp.exp(dmat - m_t)
            w_inter = jnp.exp(inter - m_t)
            qb = q.astype(BF16)
            amat = _dot_nt(qb, k.astype(BF16)) * w_intra
            cst = c_scr[h]
            n_row = n_scr[h]
            num = _dot(amat.astype(BF16), vb) + w_inter * _dot(qb, cst.astype(BF16))
            den = (jnp.sum(amat, axis=-1, keepdims=True)
                   + w_inter * jnp.sum(q * n_row, axis=-1, keepdims=True))
            hout = num / jnp.maximum(jnp.abs(den), jnp.exp(-m_t))
            m_new = m_t[CHUNK - 1:CHUNK, :]
            b_last = b_col[CHUNK - 1:CHUNK, :]
            w_s = jnp.exp(b_last - b_col + i_col - m_new)
            decay = jnp.exp(b_last + m_prev - m_new)
            kw = k * w_s
            c_scr[h] = decay * cst + _dot_tn(kw.astype(BF16), vb)
            n_scr[h] = decay * n_row + jnp.sum(kw, axis=0, keepdims=True)
            m_scr[:, h:h + 1] = m_new
            z = _sigmoid(proj_ref[_G_OA + h, rows, :]) * hout
            u_ref[rows, lo:hi] = (_rms(z) * nwa_ref[:, lo:hi]
                                  + skip_ref[:, lo:hi] * xc_scr[h, rows, :])
        for h in range(H):
            lo, hi = h * LANES, (h + 1) * LANES
            lg = log_gamma[h]
            qrb = qr_scr[h, rows, :].astype(BF16)
            kr = kr_scr[h, rows, :]
            vb = proj_ref[_G_VB + h, rows, :].astype(BF16)
            amat = _dot_nt(qrb, kr.astype(BF16)) * dec_scr[h]
            sst = s_scr[h]
            o = (_dot(amat.astype(BF16), vb)
                 + jnp.exp((tcol + 1.0) * lg) * _dot(qrb, sst.astype(BF16)))
            kws = kr * jnp.exp((CHUNK - 1.0 - tcol) * lg)
            s_scr[h] = math.exp(CHUNK * lg) * sst + _dot_tn(kws.astype(BF16), vb)
            gate = proj_ref[_G_GB + h, rows, :]
            u_ref[rows, D_MODEL // 2 + lo:D_MODEL // 2 + hi] = (
                _rms(o) * nwb_ref[:, lo:hi] * (gate * _sigmoid(gate)))
        return carry

    lax.fori_loop(0, nc, chunk_body, 0)

    @pl.when(j == nj - 1)
    def _():
        co_ref[0] = c_scr[...]
        no_ref[0] = n_scr[...]
        mo_ref[0] = m_scr[...]
        convo_ref[0] = conv_scr[...]
        so_ref[0] = s_scr[...]


def _even_scan_call(proj, gates, cos_t, sin_t, states, weights, *, ts, n_seq, steps, row_off):
    H = N_HEADS_EVEN
    c0, n0, m0, conv0, s0 = states
    ng = proj.shape[0]
    rows_idx = lambda b, j: (row_off + b * steps + j, 0)
    state_specs = [pl.BlockSpec((1, H, LANES, LANES), lambda b, j: (b, 0, 0, 0)),
                   pl.BlockSpec((1, H, 1, LANES), lambda b, j: (b, 0, 0, 0)),
                   pl.BlockSpec((1, 1, LANES), lambda b, j: (b, 0, 0)),
                   pl.BlockSpec((1, SUBLANES, H * LANES), lambda b, j: (b, 0, 0)),
                   pl.BlockSpec((1, H, LANES, LANES), lambda b, j: (b, 0, 0, 0))]
    in_specs = [pl.BlockSpec((ng, ts, LANES), lambda b, j: (0, row_off + b * steps + j, 0)),
                pl.BlockSpec((ts, LANES), rows_idx),
                pl.BlockSpec((ts, LANES), rows_idx),
                pl.BlockSpec((ts, LANES), rows_idx)] + state_specs
    for w in weights:
        in_specs.append(pl.BlockSpec(w.shape, functools.partial(lambda nd, b, j: (0,) * nd, w.ndim)))
    out_specs = [pl.BlockSpec((ts, D_MODEL), lambda b, j: (b * steps + j, 0))] + state_specs
    out_shape = [jax.ShapeDtypeStruct((n_seq * steps * ts, D_MODEL), F32),
                 jax.ShapeDtypeStruct(c0.shape, F32), jax.ShapeDtypeStruct(n0.shape, F32),
                 jax.ShapeDtypeStruct(m0.shape, F32), jax.ShapeDtypeStruct(conv0.shape, F32),
                 jax.ShapeDtypeStruct(s0.shape, F32)]
    nc = ts // CHUNK
    scratch = [pltpu.VMEM((H, LANES, LANES), F32), pltpu.VMEM((H, 1, LANES), F32),
               pltpu.VMEM((1, LANES), F32), pltpu.VMEM((SUBLANES, H * LANES), F32),
               pltpu.VMEM((H, LANES, LANES), F32),
               pltpu.VMEM((ts + 2 * SUBLANES, LANES), F32),
               pltpu.VMEM((H, ts, LANES), F32), pltpu.VMEM((H, ts, LANES), F32),
               pltpu.VMEM((H, ts, LANES), F32), pltpu.VMEM((H, ts, LANES), F32),
               pltpu.VMEM((H, ts, LANES), F32),
               pltpu.VMEM((ts, LANES), F32), pltpu.VMEM((ts, LANES), F32),
               pltpu.VMEM((nc, SUBLANES, CHUNK), F32), pltpu.VMEM((nc, SUBLANES, CHUNK), F32),
               pltpu.VMEM((H, CHUNK, CHUNK), F32)]
    return pl.pallas_call(
        functools.partial(_even_scan_kernel, ts=ts),
        grid=(n_seq, steps),
        in_specs=in_specs, out_specs=out_specs, out_shape=out_shape,
        scratch_shapes=scratch,
        compiler_params=_cparams(("arbitrary", "arbitrary")),
        name="even_scan",
    )(proj, gates, cos_t, sin_t, c0, n0, m0, conv0, s0, *weights)


_G_Q, _G_F, _G_I, _G_G = 0, 8, 16, 24


def _hgrn_intra(q, k, bcum, ti, si, tcol_i):
    amat = jnp.zeros((CHUNK, CHUNK), F32)
    for b in (32, 16, 8):
        nb2 = CHUNK // (2 * b)
        parts = [jnp.broadcast_to(bcum[m * 2 * b + b - 1:m * 2 * b + b, :], (2 * b, LANES))
                 for m in range(nb2)]
        ref = parts[0] if nb2 == 1 else jnp.concatenate(parts, axis=0)
        upper = ((tcol_i // b) % 2) == 1
        ql = jnp.where(upper, q * jnp.exp(jnp.minimum(bcum - ref, 0.0)), 0.0)
        kl = jnp.where(upper, 0.0, k * jnp.exp(jnp.minimum(ref - bcum, 0.0)))
        al = _dot_nt(ql.astype(BF16), kl.astype(BF16))
        amat = amat + jnp.where((ti // (2 * b)) == (si // (2 * b)), al, 0.0)
    nblk = CHUNK // SUBLANES
    b3 = bcum.reshape(nblk, SUBLANES, LANES)
    k3 = k.reshape(nblk, SUBLANES, LANES)
    for jj in range(SUBLANES):
        bj = jnp.broadcast_to(b3[:, jj:jj + 1, :], (nblk, SUBLANES, LANES)).reshape(CHUNK, LANES)
        kj = jnp.broadcast_to(k3[:, jj:jj + 1, :], (nblk, SUBLANES, LANES)).reshape(CHUNK, LANES)
        col = jnp.sum(q * kj * jnp.exp(jnp.minimum(bcum - bj, 0.0)), axis=-1, keepdims=True)
        sel = (si == (ti // SUBLANES) * SUBLANES + jj) & ((ti % SUBLANES) >= jj)
        amat = jnp.where(sel, col, amat)
    return amat


def _odd_scan_kernel(proj_ref, s0_ref, lb_ref, nw_ref, u_ref, so_ref,
                     st_scr, k_scr, bc_scr, *, ts):
    H = N_HEADS_ODD
    j = pl.program_id(1)
    nj = pl.num_programs(1)
    nc = ts // CHUNK

    @pl.when(j == 0)
    def _():
        for h in range(H):
            st_scr[h] = s0_ref[0, h].T

    tri = _chunk_tri(ts)
    for h in range(H):
        lo, hi = h * LANES, (h + 1) * LANES
        lbv = lb_ref[:, lo:hi]
        fpre = proj_ref[_G_F + h]
        logf = _logaddexp(jnp.log(lbv), jnp.log1p(-lbv) + _log_sigmoid(fpre))
        k_scr[h] = (1.0 - lbv) * _sigmoid(-fpre)
        bc_scr[h] = _dot_sel_lhs(tri, logf)

    ti = lax.broadcasted_iota(jnp.int32, (CHUNK, CHUNK), 0)
    si = lax.broadcasted_iota(jnp.int32, (CHUNK, CHUNK), 1)
    tcol_i = lax.broadcasted_iota(jnp.int32, (CHUNK, 1), 0)

    def chunk_body(c, carry):
        r0 = pl.multiple_of(c * CHUNK, CHUNK)
        rows = pl.ds(r0, CHUNK)
        for h in range(H):
            lo, hi = h * LANES, (h + 1) * LANES
            q = proj_ref[_G_Q + h, rows, :]
            k = k_scr[h, rows, :]
            vb = proj_ref[_G_I + h, rows, :].astype(BF16)
            bcum = bc_scr[h, rows, :]
            amat = _hgrn_intra(q, k, bcum, ti, si, tcol_i)
            st = st_scr[h]
            o = (_dot(amat.astype(BF16), vb)
                 + _dot_nt((q * jnp.exp(bcum)).astype(BF16), st.astype(BF16)))
            last = bcum[CHUNK - 1:CHUNK, :]
            kd = k * jnp.exp(last - bcum)
            st_scr[h] = st * jnp.exp(last) + _dot_tn(vb, kd.astype(BF16))
            u_ref[rows, lo:hi] = (_rms(o) * nw_ref[:, lo:hi]
                                  * _sigmoid(proj_ref[_G_G + h, rows, :]))
        return carry

    lax.fori_loop(0, nc, chunk_body, 0)

    @pl.when(j == nj - 1)
    def _():
        for h in range(H):
            so_ref[0, h] = st_scr[h].T


def _odd_scan_call(proj, s0, lb, nw, *, ts, n_seq, steps, row_off):
    H = N_HEADS_ODD
    ng = proj.shape[0]
    st_spec = pl.BlockSpec((1, H, LANES, LANES), lambda b, j: (b, 0, 0, 0))
    return pl.pallas_call(
        functools.partial(_odd_scan_kernel, ts=ts),
        grid=(n_seq, steps),
        in_specs=[pl.BlockSpec((ng, ts, LANES), lambda b, j: (0, row_off + b * steps + j, 0)),
                  st_spec,
                  pl.BlockSpec((1, D_MODEL), lambda b, j: (0, 0)),
                  pl.BlockSpec((1, D_MODEL), lambda b, j: (0, 0))],
        out_specs=[pl.BlockSpec((ts, D_MODEL), lambda b, j: (b * steps + j, 0)), st_spec],
        out_shape=[jax.ShapeDtypeStruct((n_seq * steps * ts, D_MODEL), F32),
                   jax.ShapeDtypeStruct(s0.shape, F32)],
        scratch_shapes=[pltpu.VMEM((H, LANES, LANES), F32),
                        pltpu.VMEM((H, ts, LANES), F32),
                        pltpu.VMEM((H, ts, LANES), F32)],
        compiler_params=_cparams(("arbitrary", "arbitrary")),
        name="odd_scan",
    )(proj, s0, lb, nw)


def _post_kernel(x_ref, up_ref, us_ref, mod_ref, wout_ref, nw_ref, wr_ref, br_ref,
                 xo_ref, hf_ref, eidx_ref, gate_ref, rank_ref, cnt_ref,
                 *, tm, npc, n_prompt_tiles):
    i = pl.program_id(0)
    nch = tm // CHUNK

    u = jnp.where(i < n_prompt_tiles, up_ref[...], us_ref[...])
    y = _dot(u.astype(BF16), wout_ref[...])
    for c in range(nch):
        seq = _seq_row(i, nch, c, npc)
        gm = mod_ref[pl.ds(seq, 1), 2 * D_MODEL:3 * D_MODEL]
        shf = mod_ref[pl.ds(seq, 1), 3 * D_MODEL:4 * D_MODEL]
        scf = mod_ref[pl.ds(seq, 1), 4 * D_MODEL:5 * D_MODEL]
        rs = slice(c * CHUNK, (c + 1) * CHUNK)
        xn = x_ref[rs, :] + gm * y[rs, :]
        xo_ref[rs, :] = xn
        hf_ref[rs, :] = _rms(xn) * nw_ref[...] * (1.0 + scf) + shf

    logits = _dot3(hf_ref[...], wr_ref[...]) + br_ref[...]
    lane_i = lax.broadcasted_iota(jnp.int32, (tm, LANES), 1)
    lane_f = lane_i.astype(F32)
    vals, idxs = [], []
    cur = logits
    for _ in range(TOP_K):
        m = jnp.max(cur, axis=-1, keepdims=True)
        idx = jnp.min(jnp.where(cur == m, lane_f, float(LANES)), axis=-1, keepdims=True)
        vals.append(m)
        idxs.append(idx)
        cur = jnp.where(lane_f == idx, -jnp.inf, cur)
    exps = [jnp.exp(v - vals[0]) for v in vals]
    denom = exps[0] + exps[1] + exps[2] + exps[3]
    onehot = jnp.zeros((tm, LANES), F32)
    for idx in idxs:
        onehot = onehot + jnp.where(lane_f == idx, 1.0, 0.0)
    r = lax.broadcasted_iota(jnp.int32, (tm, tm), 0)
    cidx = lax.broadcasted_iota(jnp.int32, (tm, tm), 1)
    strict = jnp.where((cidx < r) & (cidx // DISPATCH_TILE == r // DISPATCH_TILE), 1.0, 0.0).astype(BF16)
    before = _dot(strict, onehot.astype(BF16))
    eidx_o = jnp.zeros((tm, LANES), F32)
    gate_o = jnp.zeros((tm, LANES), F32)
    rank_o = jnp.zeros((tm, LANES), F32)
    for kk in range(TOP_K):
        rk = jnp.sum(jnp.where(lane_f == idxs[kk], before, 0.0), axis=-1, keepdims=True)
        eidx_o = jnp.where(lane_i == kk, idxs[kk], eidx_o)
        gate_o = jnp.where(lane_i == kk, exps[kk] / denom, gate_o)
        rank_o = jnp.where(lane_i == kk, rk, rank_o)
    eidx_ref[...] = eidx_o
    gate_ref[...] = gate_o
    rank_ref[...] = rank_o
    for s in range(tm // DISPATCH_TILE):
        cnt_ref[s] = jnp.sum(onehot[s * DISPATCH_TILE:(s + 1) * DISPATCH_TILE, :], axis=0, keepdims=True)


def _post_call(x, u_p, u_s, mod_l, w_out, nw, w_r, b_r, npc):
    ttot = x.shape[0]
    tm = TOKEN_TILE
    npt = u_p.shape[0] // tm
    sub = tm // DISPATCH_TILE
    tile = lambda i: (i, 0)
    const = lambda i: (0, 0)
    return pl.pallas_call(
        functools.partial(_post_kernel, tm=tm, npc=npc, n_prompt_tiles=npt),
        grid=(ttot // tm,),
        in_specs=[pl.BlockSpec((tm, D_MODEL), tile),
                  pl.BlockSpec((tm, D_MODEL), lambda i: (jnp.minimum(i, npt - 1), 0)),
                  pl.BlockSpec((tm, D_MODEL), lambda i: (jnp.maximum(i - npt, 0), 0)),
                  pl.BlockSpec(mod_l.shape, const),
                  pl.BlockSpec(w_out.shape, const),
                  pl.BlockSpec((1, D_MODEL), const),
                  pl.BlockSpec(w_r.shape, const),
                  pl.BlockSpec((1, LANES), const)],
        out_specs=[pl.BlockSpec((tm, D_MODEL), tile), pl.BlockSpec((tm, D_MODEL), tile),
                   pl.BlockSpec((tm, LANES), tile), pl.BlockSpec((tm, LANES), tile),
                   pl.BlockSpec((tm, LANES), tile), pl.BlockSpec((sub, 1, LANES), lambda i: (i, 0, 0))],
        out_shape=[jax.ShapeDtypeStruct((ttot, D_MODEL), F32), jax.ShapeDtypeStruct((ttot, D_MODEL), F32),
                   jax.ShapeDtypeStruct((ttot, LANES), F32), jax.ShapeDtypeStruct((ttot, LANES), F32),
                   jax.ShapeDtypeStruct((ttot, LANES), F32),
                   jax.ShapeDtypeStruct((ttot // DISPATCH_TILE, 1, LANES), F32)],
        compiler_params=_cparams(("arbitrary",)),
        name="post",
    )(x, u_p, u_s, mod_l, w_out, nw, w_r, b_r)


def _plan_kernel(eidx_ref, rank_ref, cnt_ref, lpos_ref, n8_ref, loff_ref, gbase_ref, blk_ref, ends_ref,
                 loff_scr, *, tm, nbp, ntp):
    cnt = cnt_ref[...]
    n8 = jnp.floor((cnt + (SUBLANES - 1.0)) * (1.0 / SUBLANES)) * float(SUBLANES)
    r = lax.broadcasted_iota(jnp.int32, (LANES, LANES), 0)
    c = lax.broadcasted_iota(jnp.int32, (LANES, LANES), 1)
    p1, p2, p3 = _split3(n8)
    lt = jnp.where(r < c, 1.0, 0.0).astype(BF16)
    loff = _dot(p1, lt) + (_dot(p2, lt) + _dot(p3, lt))
    gtot = jnp.broadcast_to(jnp.sum(n8, axis=0, keepdims=True), (SUBLANES, LANES))
    nblk = jnp.floor((gtot + (EXPERT_ROWS - 1.0)) * (1.0 / EXPERT_ROWS))
    le = jnp.where(r <= c, 1.0, 0.0).astype(BF16)
    q1, q2, q3 = _split3(nblk)
    ends = _dot(q1, le) + (_dot(q2, le) + _dot(q3, le))
    start_row = (ends[0:1, :] - nblk[0:1, :]) * float(EXPERT_ROWS)
    tr = lax.broadcasted_iota(jnp.int32, (ntp, ntp), 0)
    tc = lax.broadcasted_iota(jnp.int32, (ntp, ntp), 1)
    gbase = start_row + _dot_sel_lhs(jnp.where(tc < tr, 1.0, 0.0).astype(BF16), n8)
    loff_scr[...] = loff

    i = pl.program_id(0)
    lane_i = lax.broadcasted_iota(jnp.int32, (DISPATCH_TILE, LANES), 1)
    lane_f = lane_i.astype(F32)
    for s in range(tm // DISPATCH_TILE):
        rs = slice(s * DISPATCH_TILE, (s + 1) * DISPATCH_TILE)
        off_row = loff_scr[pl.ds(i * (tm // DISPATCH_TILE) + s, 1), :]
        eidx = eidx_ref[rs, :]
        rank = rank_ref[rs, :]
        lpos = jnp.zeros((DISPATCH_TILE, LANES), F32)
        for kk in range(TOP_K):
            base = jnp.sum(jnp.where(lane_f == eidx[:, kk:kk + 1], off_row, 0.0), axis=-1, keepdims=True)
            lpos = jnp.where(lane_i == kk, base + rank[:, kk:kk + 1], lpos)
        lpos_ref[rs, :] = lpos

    @pl.when(i == 0)
    def _():
        n8_ref[...] = n8.astype(jnp.int32)
        loff_ref[...] = loff.astype(jnp.int32)
        gbase_ref[...] = gbase.astype(jnp.int32)
        bi = lax.broadcasted_iota(jnp.int32, (nbp, LANES), 0).astype(F32)
        li = lax.broadcasted_iota(jnp.int32, (nbp, LANES), 1)
        done = jnp.where((li < N_EXPERTS) & (ends[0:1, :] <= bi), 1.0, 0.0)
        be = jnp.minimum(jnp.sum(done, axis=-1, keepdims=True), N_EXPERTS - 1.0)
        blk_ref[...] = jnp.broadcast_to(be, (nbp, LANES)).astype(jnp.int32)
        ends_ref[...] = ends.astype(jnp.int32)


def _plan_call(eidx, rank, cnt_tiles, nbp):
    ttot = eidx.shape[0]
    tm = TOKEN_TILE
    ntp = cnt_tiles.shape[0]
    tile = lambda i: (i, 0)
    const = lambda i: (0, 0)
    tbl = jax.ShapeDtypeStruct((ntp, LANES), jnp.int32)
    return pl.pallas_call(
        functools.partial(_plan_kernel, tm=tm, nbp=nbp, ntp=ntp),
        grid=(ttot // tm,),
        in_specs=[pl.BlockSpec((tm, LANES), tile), pl.BlockSpec((tm, LANES), tile),
                  pl.BlockSpec((ntp, LANES), const)],
        out_specs=[pl.BlockSpec((tm, LANES), tile),
                   pl.BlockSpec((ntp, LANES), const), pl.BlockSpec((ntp, LANES), const),
                   pl.BlockSpec((ntp, LANES), const),
                   pl.BlockSpec((nbp, LANES), const), pl.BlockSpec((SUBLANES, LANES), const)],
        out_shape=[jax.ShapeDtypeStruct((ttot, LANES), F32), tbl, tbl, tbl,
                   jax.ShapeDtypeStruct((nbp, LANES), jnp.int32),
                   jax.ShapeDtypeStruct((SUBLANES, LANES), jnp.int32)],
        scratch_shapes=[pltpu.VMEM((ntp, LANES), F32)],
        compiler_params=_cparams(("arbitrary",)),
        name="plan",
    )(eidx, rank, cnt_tiles)


_GROUP_BITS = tuple(range(3, DISPATCH_TILE.bit_length()))


def _group_copies(n8_ref, loff_ref, gbase_ref, tile, make_copy, wait):
    def per_expert(e, carry):
        idx = tile * N_EXPERTS + e
        n = n8_ref[idx]
        off = loff_ref[idx]
        base = gbase_ref[idx]
        for bit in _GROUP_BITS:
            size = 1 << bit

            @pl.when((n & size) != 0)
            def _():
                done = n & ~(2 * size - 1)
                cp = make_copy(pl.multiple_of(off + done, SUBLANES), pl.multiple_of(base + done, SUBLANES), size)
                if wait:
                    cp.wait()
                else:
                    cp.start()
        return carry

    lax.fori_loop(0, N_EXPERTS, per_expert, 0)


def _dispatch_kernel(n8_ref, loff_ref, gbase_ref, hf_ref, lpos_ref, gate_ref, xs_in_ref, xs_ref,
                     sbuf, sem):
    del xs_in_ref
    i = pl.program_id(0)
    eye8 = jnp.where(lax.broadcasted_iota(jnp.int32, (SUBLANES, LANES), 0)
                     == lax.broadcasted_iota(jnp.int32, (SUBLANES, LANES), 1), 1.0, 0.0).astype(BF16)
    lpos_t = _dot_sel_nt(eye8, lpos_ref[...])
    gate_t = _dot_sel_nt(eye8, gate_ref[...])
    row = lax.broadcasted_iota(jnp.int32, (SORT_ROWS, DISPATCH_TILE), 0).astype(F32)
    perm = jnp.zeros((SORT_ROWS, DISPATCH_TILE), F32)
    wgate = jnp.zeros((SORT_ROWS, DISPATCH_TILE), F32)
    for kk in range(TOP_K):
        hit = row == lpos_t[kk:kk + 1, :]
        perm = jnp.where(hit, 1.0, perm)
        wgate = jnp.where(hit, gate_t[kk:kk + 1, :], wgate)
    sbuf[:, 0:D_MODEL] = _dot(perm.astype(BF16), hf_ref[...].astype(BF16))
    sbuf[:, D_MODEL:XS_WIDTH] = _dot_sel_rhs(wgate, jnp.ones((DISPATCH_TILE, LANES), BF16))

    def make_copy(src_row, dst_row, size):
        return pltpu.make_async_copy(sbuf.at[pl.ds(src_row, size), :], xs_ref.at[pl.ds(dst_row, size), :], sem)

    _group_copies(n8_ref, loff_ref, gbase_ref, i, make_copy, wait=False)
    _group_copies(n8_ref, loff_ref, gbase_ref, i, make_copy, wait=True)


def _dispatch_call(tables, hf, lpos, gate, xs):
    ttot = hf.shape[0]
    tm = DISPATCH_TILE
    tile = lambda i, *_: (i, 0)
    grid_spec = pltpu.PrefetchScalarGridSpec(
        num_scalar_prefetch=3, grid=(ttot // tm,),
        in_specs=[pl.BlockSpec((tm, D_MODEL), tile), pl.BlockSpec((tm, LANES), tile),
                  pl.BlockSpec((tm, LANES), tile), pl.BlockSpec(memory_space=pl.ANY)],
        out_specs=pl.BlockSpec(memory_space=pl.ANY),
        scratch_shapes=[pltpu.VMEM((SORT_ROWS, XS_WIDTH), F32), pltpu.SemaphoreType.DMA(())])
    return pl.pallas_call(
        _dispatch_kernel,
        grid_spec=grid_spec,
        out_shape=jax.ShapeDtypeStruct(xs.shape, xs.dtype),
        input_output_aliases={6: 0},
        compiler_params=pltpu.CompilerParams(dimension_semantics=("arbitrary",),
                                             vmem_limit_bytes=VMEM_LIMIT, has_side_effects=True),
        name="dispatch",
    )(*tables, hf, lpos, gate, xs)


def _expert_kernel(be_ref, nu_ref, xs_ref, wgu_ref, bgu_ref, wdn_ref, bdn_ref, y_ref,
                   wgu_bf, wdn_bf):
    b = pl.program_id(0)

    @pl.when(b < nu_ref[0])
    def _():
        prev = be_ref[jnp.maximum(b - 1, 0)]

        @pl.when((b == 0) | (be_ref[b] != prev))
        def _():
            wgu_bf[...] = wgu_ref[0, 0].astype(BF16)
            wdn_bf[...] = wdn_ref[0, 0].astype(BF16)

        gu = _dot(xs_ref[:, 0:D_MODEL].astype(BF16), wgu_bf[...]) + bgu_ref[0, 0]
        g = jnp.minimum(gu[:, :D_MODEL], SWIGLU_LIMIT)
        u = jnp.clip(gu[:, D_MODEL:], -SWIGLU_LIMIT, SWIGLU_LIMIT)
        act = (u + 1.0) * (g * _sigmoid(SWIGLU_ALPHA * g))
        gate = xs_ref[:, D_MODEL:D_MODEL + 1]
        y_ref[...] = (_dot(act.astype(BF16), wdn_bf[...]) + bdn_ref[0, 0]) * gate

    @pl.when(b >= nu_ref[0])
    def _():
        y_ref[...] = jnp.zeros_like(y_ref)


def _expert_call(blk_expert, n_used, xs, w_gu, b_gu, w_dn, b_dn, layer):
    nb = xs.shape[0] // EXPERT_ROWS
    d_ff2 = w_gu.shape[3]
    blk = lambda b, be, nu: (jnp.minimum(b, nu[0] - 1), 0)
    blk_out = lambda b, be, nu: (b, 0)
    exp4 = lambda b, be, nu: (layer, be[jnp.minimum(b, nu[0] - 1)], 0, 0)
    grid_spec = pltpu.PrefetchScalarGridSpec(
        num_scalar_prefetch=2, grid=(nb,),
        in_specs=[pl.BlockSpec((EXPERT_ROWS, XS_WIDTH), blk),
                  pl.BlockSpec((1, 1, D_MODEL, d_ff2), exp4),
                  pl.BlockSpec((1, 1, 1, d_ff2), exp4),
                  pl.BlockSpec((1, 1, D_MODEL, D_MODEL), exp4),
                  pl.BlockSpec((1, 1, 1, D_MODEL), exp4)],
        out_specs=pl.BlockSpec((EXPERT_ROWS, D_MODEL), blk_out),
        scratch_shapes=[pltpu.VMEM((D_MODEL, d_ff2), BF16), pltpu.VMEM((D_MODEL, D_MODEL), BF16)])
    return pl.pallas_call(
        _expert_kernel,
        grid_spec=grid_spec,
        out_shape=jax.ShapeDtypeStruct((xs.shape[0], D_MODEL), F32),
        compiler_params=_cparams(("arbitrary",)),
        name="experts",
    )(blk_expert, n_used, xs, w_gu, b_gu, w_dn, b_dn)


def _combine_kernel(*refs, tm, npc, final):
    if final:
        (n8_ref, loff_ref, gbase_ref, x_ref, lpos_ref, mod_ref, fnw_ref, y_hbm,
         xo_ref, yf_ref, ybuf, sem) = refs
    else:
        n8_ref, loff_ref, gbase_ref, x_ref, lpos_ref, mod_ref, y_hbm, xo_ref, ybuf, sem = refs
    i = pl.program_id(0)
    nch = tm // CHUNK

    @pl.when(i == 0)
    def _():
        ybuf[...] = jnp.zeros_like(ybuf)

    def make_copy(buf_row, src_row, size):
        return pltpu.make_async_copy(y_hbm.at[pl.ds(src_row, size), :], ybuf.at[pl.ds(buf_row, size), :], sem)

    _group_copies(n8_ref, loff_ref, gbase_ref, i, make_copy, wait=False)
    _group_copies(n8_ref, loff_ref, gbase_ref, i, make_copy, wait=True)

    lpos = lpos_ref[...]
    col = lax.broadcasted_iota(jnp.int32, (tm, SORT_ROWS), 1).astype(F32)
    unperm = jnp.zeros((tm, SORT_ROWS), F32)
    for kk in range(TOP_K):
        unperm = jnp.where(col == lpos[:, kk:kk + 1], 1.0, unperm)
    acc = _dot_sel_lhs2(unperm.astype(BF16), ybuf[...])
    for c in range(nch):
        seq = _seq_row(i, nch, c, npc)
        gf = mod_ref[pl.ds(seq, 1), 5 * D_MODEL:6 * D_MODEL]
        rs = slice(c * CHUNK, (c + 1) * CHUNK)
        xn = x_ref[rs, :] + gf * acc[rs, :]
        xo_ref[rs, :] = xn
        if final:
            yf_ref[rs, :] = _rms(xn) * fnw_ref[...]


def _combine_call(tables, x, lpos, mod_l, y_sorted, npc, final_w):
    ttot = x.shape[0]
    tm = DISPATCH_TILE
    final = final_w is not None
    tile = lambda i, *_: (i, 0)
    const = lambda i, *_: (0, 0)
    in_specs = [pl.BlockSpec((tm, D_MODEL), tile),
                pl.BlockSpec((tm, LANES), tile),
                pl.BlockSpec(mod_l.shape, const)]
    args = [x, lpos, mod_l]
    if final:
        in_specs.append(pl.BlockSpec((1, D_MODEL), const))
        args.append(final_w)
    in_specs.append(pl.BlockSpec(memory_space=pl.ANY))
    args.append(y_sorted)
    out_specs = [pl.BlockSpec((tm, D_MODEL), tile)]
    out_shape = [jax.ShapeDtypeStruct((ttot, D_MODEL), F32)]
    if final:
        out_specs.append(pl.BlockSpec((tm, D_MODEL), tile))
        out_shape.append(jax.ShapeDtypeStruct((ttot, D_MODEL), F32))
    grid_spec = pltpu.PrefetchScalarGridSpec(
        num_scalar_prefetch=3, grid=(ttot // tm,),
        in_specs=in_specs, out_specs=out_specs,
        scratch_shapes=[pltpu.VMEM((SORT_ROWS, D_MODEL), F32), pltpu.SemaphoreType.DMA(())])
    return pl.pallas_call(
        functools.partial(_combine_kernel, tm=tm, npc=npc, final=final),
        grid_spec=grid_spec, out_shape=out_shape,
        compiler_params=_cparams(("arbitrary",)),
        name="combine",
    )(*tables, *args)


def kernel(x_prompt, x_sample, c_prompt, c_sample, state_mlstm_C, state_mlstm_n, state_mlstm_m, state_mlstm_conv, state_ret_S, state_hgrn_S, w_ada, b_ada, norm_mix_w, norm_ffn_w, final_norm_w, w_in_even, b_mlstm_i, b_mlstm_f, w_mlstm_conv, b_mlstm_conv, w_mlstm_q, w_mlstm_k, mlstm_skip, mlstm_norm_w, ret_norm_w, w_out_even, w_in_odd, hgrn_lb_logits, hgrn_norm_w, w_out_odd, moe_router_w, moe_router_b, moe_w_gate_up, moe_b_gate_up, moe_w_down, moe_b_down):
    bp, seq, d = x_prompt.shape
    bs, dseq, _ = x_sample.shape
    assert bp == 1 and d == D_MODEL and dseq == CHUNK
    assert seq % TOKEN_TILE == 0 and (bs * dseq) % TOKEN_TILE == 0
    depth = w_ada.shape[0]
    tp, tsmp = seq, bs * dseq
    ttot = tp + tsmp
    npc = tp // CHUNK
    he, ho = N_HEADS_EVEN, N_HEADS_ODD
    da = he * LANES
    past_len = 1024

    x = jnp.concatenate([x_prompt.reshape(tp, d), x_sample.reshape(tsmp, d)], axis=0)
    n_mod_rows = 2 * SUBLANES
    assert 1 + bs <= n_mod_rows
    c_all = jnp.zeros((n_mod_rows, d), F32).at[0:1].set(c_prompt).at[1:1 + bs].set(c_sample)
    mod = _ada_call(c_all, w_ada, b_ada)

    half = LANES // 2
    inv = ROPE_BASE ** (-jnp.arange(half, dtype=F32) / half)
    pos_all = jnp.concatenate([jnp.arange(tp, dtype=F32),
                               jnp.tile(past_len + jnp.arange(dseq, dtype=F32), bs)])
    ang = pos_all[:, None] * inv[None, :]
    cos_t = jnp.concatenate([jnp.cos(ang), jnp.cos(ang)], axis=-1)
    sin_t = jnp.concatenate([-jnp.sin(ang), jnp.sin(ang)], axis=-1)

    lb_p = jax.nn.softmax(hgrn_lb_logits.astype(F32), axis=0)
    lbs = jnp.cumsum(lb_p, axis=0) - lb_p[0]

    n_tiles = ttot // DISPATCH_TILE
    ntp = -(-n_tiles // LANES) * LANES
    max_rows = ttot * TOP_K + n_tiles * N_EXPERTS * (SUBLANES - 1)
    nb = -(-max_rows // EXPERT_ROWS) + N_EXPERTS
    nbp = -(-nb // SUBLANES) * SUBLANES
    xs = jnp.zeros((nb * EXPERT_ROWS, XS_WIDTH), F32)

    ts_p = TOKEN_TILE
    steps_p = tp // ts_p
    even_out, odd_out = [], []
    y_final = None
    for l in range(depth):
        jl = l // 2
        mod_l = mod[l]
        if l % 2 == 0:
            w_in = w_in_even[jl]
            w_main = jnp.concatenate([w_in[:, :3 * da], w_in[:, 3 * da + 2 * he:]], axis=1).astype(BF16)
            w_gate = jnp.zeros((d, LANES), F32).at[:, :2 * he].set(w_in[:, 3 * da:3 * da + 2 * he])
            proj, gates = _inproj_call(x, mod_l, norm_mix_w[l][None], w_main, w_gate, npc)
            gbias = jnp.zeros((1, LANES), F32).at[0, :he].set(b_mlstm_i[jl]).at[0, he:2 * he].set(b_mlstm_f[jl])
            weights = [w_mlstm_conv[jl], b_mlstm_conv[jl][None],
                       w_mlstm_q[jl].astype(BF16), w_mlstm_k[jl].astype(BF16), gbias,
                       mlstm_skip[jl][None], mlstm_norm_w[jl][None], ret_norm_w[jl][None]]
            zeros_p = (jnp.zeros((1, he, LANES, LANES), F32), jnp.zeros((1, he, 1, LANES), F32),
                       jnp.zeros((1, 1, LANES), F32), jnp.zeros((1, SUBLANES, da), F32),
                       jnp.zeros((1, he, LANES, LANES), F32))
            st_s = (state_mlstm_C[jl], state_mlstm_n[jl][:, :, None, :],
                    jnp.zeros((bs, 1, LANES), F32).at[:, 0, :he].set(state_mlstm_m[jl]),
                    jnp.zeros((bs, SUBLANES, da), F32).at[:, SUBLANES - (CONV_W - 1):].set(state_mlstm_conv[jl]),
                    state_ret_S[jl])
            res_p = _even_scan_call(proj, gates, cos_t, sin_t, zeros_p, weights,
                                    ts=ts_p, n_seq=1, steps=steps_p, row_off=0)
            res_s = _even_scan_call(proj, gates, cos_t, sin_t, st_s, weights,
                                    ts=CHUNK, n_seq=bs, steps=1, row_off=npc)
            u_p, u_s = res_p[0], res_s[0]
            even_out.append((res_p[1:], res_s[1:]))
            w_out = w_out_even[jl].astype(BF16)
        else:
            proj = _inproj_call(x, mod_l, norm_mix_w[l][None], w_in_odd[jl].astype(BF16), None, npc)[0]
            lb = lbs[l][None]
            nw = hgrn_norm_w[jl][None]
            u_p, sp = _odd_scan_call(proj, jnp.zeros((1, ho, LANES, LANES), F32), lb, nw,
                                     ts=ts_p, n_seq=1, steps=steps_p, row_off=0)
            u_s, ss = _odd_scan_call(proj, state_hgrn_S[jl], lb, nw,
                                     ts=CHUNK, n_seq=bs, steps=1, row_off=npc)
            odd_out.append((sp, ss))
            w_out = w_out_odd[jl].astype(BF16)

        w_r = jnp.zeros((d, LANES), F32).at[:, :N_EXPERTS].set(moe_router_w[l])
        b_r = jnp.full((1, LANES), -jnp.inf, F32).at[0, :N_EXPERTS].set(moe_router_b[l])
        x, hf, eidx, gate, rank, cnt = _post_call(x, u_p, u_s, mod_l, w_out, norm_ffn_w[l][None], w_r, b_r, npc)
        cnt_tiles = jnp.zeros((ntp, LANES), F32).at[:n_tiles].set(cnt[:, 0, :])
        lpos, n8, loff, gbase, blk, ends = _plan_call(eidx, rank, cnt_tiles, nbp)
        tables = [t[:n_tiles, :N_EXPERTS].reshape(-1) for t in (n8, loff, gbase)]
        blk_expert = blk[:nb, 0]
        n_used = ends[0, N_EXPERTS - 1:N_EXPERTS]
        xs = _dispatch_call(tables, hf, lpos, gate, xs)
        y_sorted = _expert_call(blk_expert, n_used, xs, moe_w_gate_up, moe_b_gate_up[:, :, None, :],
                                moe_w_down, moe_b_down[:, :, None, :], l)
        if l == depth - 1:
            x, y_final = _combine_call(tables, x, lpos, mod_l, y_sorted, npc, final_norm_w[None])
        else:
            x = _combine_call(tables, x, lpos, mod_l, y_sorted, npc, None)[0]

    def even_states(which):
        cs = jnp.stack([e[which][0] for e in even_out])
        ns = jnp.stack([e[which][1][:, :, 0, :] for e in even_out])
        ms = jnp.stack([e[which][2][:, 0, :he] for e in even_out])
        cv = jnp.stack([e[which][3][:, SUBLANES - (CONV_W - 1):, :] for e in even_out])
        ss = jnp.stack([e[which][4] for e in even_out])
        return cs, ns, ms, cv, ss

    p_c, p_n, p_m, p_cv, p_s = even_states(0)
    s_c, s_n, s_m, s_cv, s_s = even_states(1)
    p_h = jnp.stack([o[0] for o in odd_out])
    s_h = jnp.stack([o[1] for o in odd_out])
    y_prompt = y_final[:tp].reshape(bp, seq, d)
    y_sample = y_final[tp:].reshape(bs, dseq, d)
    return (y_prompt, y_sample, p_c, p_n, p_m, p_cv, p_s, p_h, s_c, s_n, s_m, s_cv, s_s, s_h)
```

```python
import functools
import math

import jax
import jax.numpy as jnp
from jax import lax
from jax.experimental import pallas as pl
from jax.experimental.pallas import tpu as pltpu

F32 = jnp.float32
BF16 = jnp.bfloat16

CHUNK = 64
LANES = 128
SUBLANES = 8
D_MODEL = 1024
N_HEADS_EVEN = 4
N_HEADS_ODD = 8
CONV_W = 4
N_EXPERTS = 32
TOP_K = 4
SWIGLU_LIMIT = 7.0
SWIGLU_ALPHA = 1.702
EPS = 1e-6
ROPE_BASE = 10000.0

TOKEN_TILE = 512
INPROJ_TILE = 256
DISPATCH_TILE = 256
SORT_ROWS = DISPATCH_TILE * TOP_K + LANES * 2
XS_WIDTH = D_MODEL + LANES
EXPERT_ROWS = 512
VMEM_LIMIT = 56 * 1024 * 1024


def _cparams(sem, vmem=VMEM_LIMIT):
    return pltpu.CompilerParams(dimension_semantics=sem, vmem_limit_bytes=vmem)


def _dot(a, b):
    return jnp.dot(a, b, preferred_element_type=F32)


def _dot_nt(a, b):
    return lax.dot_general(a, b, (((1,), (1,)), ((), ())), preferred_element_type=F32)


def _dot_tn(a, b):
    return lax.dot_general(a, b, (((0,), (0,)), ((), ())), preferred_element_type=F32)


def _split3(x):
    p1 = x.astype(BF16)
    r1 = x - p1.astype(F32)
    p2 = r1.astype(BF16)
    p3 = (r1 - p2.astype(F32)).astype(BF16)
    return p1, p2, p3


def _dot3(a, b):
    ah = a.astype(BF16)
    al = (a - ah.astype(F32)).astype(BF16)
    bh = b.astype(BF16)
    bl = (b - bh.astype(F32)).astype(BF16)
    return _dot(ah, bh) + (_dot(ah, bl) + _dot(al, bh))


def _dot_sel_lhs(sel_bf16, x):
    p1, p2, p3 = _split3(x)
    return _dot(sel_bf16, p1) + (_dot(sel_bf16, p2) + _dot(sel_bf16, p3))


def _dot_sel_nt(sel_bf16, x):
    p1, p2, p3 = _split3(x)
    return _dot_nt(sel_bf16, p1) + (_dot_nt(sel_bf16, p2) + _dot_nt(sel_bf16, p3))


def _dot_sel_rhs(x, sel_bf16):
    p1, p2, p3 = _split3(x)
    return _dot(p1, sel_bf16) + (_dot(p2, sel_bf16) + _dot(p3, sel_bf16))


def _dot_sel_lhs2(sel_bf16, x):
    hi = x.astype(BF16)
    lo = (x - hi.astype(F32)).astype(BF16)
    return _dot(sel_bf16, hi) + _dot(sel_bf16, lo)


def _sigmoid(x):
    return 1.0 / (1.0 + jnp.exp(-x))


def _log_sigmoid(x):
    return jnp.minimum(x, 0.0) - jnp.log1p(jnp.exp(-jnp.abs(x)))


def _logaddexp(a, b):
    return jnp.maximum(a, b) + jnp.log1p(jnp.exp(-jnp.abs(a - b)))


def _rms(x):
    return x * lax.rsqrt(jnp.mean(x * x, axis=-1, keepdims=True) + EPS)


def _chunk_tri(n):
    r = lax.broadcasted_iota(jnp.int32, (n, n), 0)
    c = lax.broadcasted_iota(jnp.int32, (n, n), 1)
    return jnp.where((r // CHUNK == c // CHUNK) & (c <= r), 1.0, 0.0).astype(BF16)


def _chunk_cumsum(x):
    rows, n = x.shape
    vregs = CHUNK // SUBLANES
    x4 = x.reshape(rows // CHUNK, vregs, SUBLANES, n)
    sub = lax.broadcasted_iota(jnp.int32, x4.shape, 2)
    s = x4
    for shift in (1, 2, 4):
        s = s + jnp.where(sub >= shift, pltpu.roll(s, shift, 2), 0.0)
    outs, carry = [], None
    for v in range(vregs):
        cur = s[:, v] if carry is None else s[:, v] + carry
        outs.append(cur)
        carry = jnp.broadcast_to(cur[:, SUBLANES - 1:SUBLANES, :], cur.shape)
    return jnp.stack(outs, axis=1).reshape(rows, n)


def _seq_row(tile_idx, chunks_per_tile, c, n_prompt_chunks):
    return jnp.maximum(tile_idx * chunks_per_tile + c - (n_prompt_chunks - 1), 0)


def _ada_kernel(c_ref, w_ref, b_ref, o_ref):
    c = c_ref[...]
    o_ref[0] = _dot3(c * _sigmoid(c), w_ref[0]) + b_ref[0]


def _ada_call(c_all, w_ada, b_ada):
    depth = w_ada.shape[0]
    nrow = c_all.shape[0]
    ncol = w_ada.shape[2] // D_MODEL
    return pl.pallas_call(
        _ada_kernel,
        grid=(depth, ncol),
        in_specs=[pl.BlockSpec((nrow, D_MODEL), lambda l, j: (0, 0)),
                  pl.BlockSpec((1, D_MODEL, D_MODEL), lambda l, j: (l, 0, j)),
                  pl.BlockSpec((1, 1, D_MODEL), lambda l, j: (l, 0, j))],
        out_specs=pl.BlockSpec((1, nrow, D_MODEL), lambda l, j: (l, 0, j)),
        out_shape=jax.ShapeDtypeStruct((depth, nrow, ncol * D_MODEL), F32),
        compiler_params=_cparams(("arbitrary", "arbitrary")),
        name="ada",
    )(c_all, w_ada, b_ada.reshape(depth, 1, -1))


def _inproj_kernel(*refs, tm, ng, npc, has_gates):
    if has_gates:
        x_ref, mod_ref, nw_ref, w_ref, wg_ref, proj_ref, gates_ref, h_scr = refs
    else:
        x_ref, mod_ref, nw_ref, w_ref, proj_ref, h_scr = refs
    i = pl.program_id(0)
    nch = tm // CHUNK
    for c in range(nch):
        seq = _seq_row(i, nch, c, npc)
        sh = mod_ref[pl.ds(seq, 1), 0:D_MODEL]
        sc = mod_ref[pl.ds(seq, 1), D_MODEL:2 * D_MODEL]
        xc = x_ref[c * CHUNK:(c + 1) * CHUNK, :]
        h_scr[c * CHUNK:(c + 1) * CHUNK, :] = _rms(xc) * nw_ref[...] * (1.0 + sc) + sh
    h = h_scr[...]
    hb = h.astype(BF16)
    for g in range(0, ng, 4):
        res = _dot(hb, w_ref[:, g * LANES:(g + 4) * LANES])
        for jj in range(4):
            proj_ref[g + jj] = res[:, jj * LANES:(jj + 1) * LANES]
    if has_gates:
        gates_ref[...] = _dot3(h, wg_ref[...])


def _inproj_call(x, mod_l, nw, w_main, w_gate, npc):
    ttot = x.shape[0]
    tm = INPROJ_TILE
    ng = w_main.shape[1] // LANES
    has_gates = w_gate is not None
    in_specs = [pl.BlockSpec((tm, D_MODEL), lambda i: (i, 0)),
                pl.BlockSpec(mod_l.shape, lambda i: (0, 0)),
                pl.BlockSpec((1, D_MODEL), lambda i: (0, 0)),
                pl.BlockSpec(w_main.shape, lambda i: (0, 0))]
    out_specs = [pl.BlockSpec((ng, tm, LANES), lambda i: (0, i, 0))]
    out_shape = [jax.ShapeDtypeStruct((ng, ttot, LANES), F32)]
    args = [x, mod_l, nw, w_main]
    if has_gates:
        in_specs.append(pl.BlockSpec(w_gate.shape, lambda i: (0, 0)))
        out_specs.append(pl.BlockSpec((tm, LANES), lambda i: (i, 0)))
        out_shape.append(jax.ShapeDtypeStruct((ttot, LANES), F32))
        args.append(w_gate)
    return pl.pallas_call(
        functools.partial(_inproj_kernel, tm=tm, ng=ng, npc=npc, has_gates=has_gates),
        grid=(ttot // tm,),
        in_specs=in_specs, out_specs=out_specs, out_shape=out_shape,
        scratch_shapes=[pltpu.VMEM((tm, D_MODEL), F32)],
        compiler_params=_cparams(("arbitrary",)),
        name="inproj",
    )(*args)


_G_XM, _G_VA, _G_OA, _G_QB, _G_KB, _G_VB, _G_GB = 0, 4, 8, 12, 16, 20, 24


def _even_scan_kernel(proj_ref, gates_ref, cos_ref, sin_ref,
                      c0_ref, n0_ref, m0_ref, conv0_ref, s0_ref,
                      cw_ref, cb_ref, wq_ref, wk_ref, gbias_ref, skip_ref, nwa_ref, nwb_ref,
                      u_ref, co_ref, no_ref, mo_ref, convo_ref, so_ref,
                      c_scr, n_scr, m_scr, conv_scr, s_scr,
                      xbuf, xc_scr, q_scr, k_scr, qr_scr, kr_scr,
                      gl_scr, bc_scr, rowi_scr, rowb_scr, dec_scr, *, ts):
    H = N_HEADS_EVEN
    j = pl.program_id(1)
    nj = pl.num_programs(1)
    nc = ts // CHUNK

    @pl.when(j == 0)
    def _():
        c_scr[...] = c0_ref[0]
        n_scr[...] = n0_ref[0]
        m_scr[...] = m0_ref[0]
        conv_scr[...] = conv0_ref[0]
        s_scr[...] = s0_ref[0]

    for g in range(H):
        lo, hi = g * LANES, (g + 1) * LANES
        x_g = proj_ref[_G_XM + g]
        xbuf[0:SUBLANES, :] = conv_scr[:, lo:hi]
        xbuf[SUBLANES:SUBLANES + ts, :] = x_g
        acc = cb_ref[:, lo:hi] + cw_ref[CONV_W - 1:CONV_W, lo:hi] * x_g
        for t in range(CONV_W - 1):
            off = SUBLANES - (CONV_W - 1) + t
            acc = acc + cw_ref[t:t + 1, lo:hi] * xbuf[off:off + ts, :]
        conv_scr[:, lo:hi] = xbuf[ts:ts + SUBLANES, :]
        xc = acc * _sigmoid(acc)
        xc_scr[g] = xc
        xcb = xc.astype(BF16)
        q_scr[g] = _dot(xcb, wq_ref[g])
        k_scr[g] = _dot(xcb, wk_ref[g]) * (LANES ** -0.5)

    cosv = cos_ref[...]
    sinv = sin_ref[...]
    for g in range(H):
        qb = proj_ref[_G_QB + g]
        kb = proj_ref[_G_KB + g]
        qr_scr[g] = qb * cosv + pltpu.roll(qb, LANES // 2, 1) * sinv
        kr_scr[g] = (kb * cosv + pltpu.roll(kb, LANES // 2, 1) * sinv) * (LANES ** -0.5)

    gpre = gates_ref[...] + gbias_ref[...]
    lane = lax.broadcasted_iota(jnp.int32, (ts, LANES), 1)
    gl = jnp.where(lane < H, gpre, _log_sigmoid(gpre))
    gl_scr[...] = gl
    bc = _dot_sel_lhs(_chunk_tri(ts), gl)
    bc_scr[...] = bc
    eye8 = jnp.where(lax.broadcasted_iota(jnp.int32, (SUBLANES, LANES), 0)
                     == lax.broadcasted_iota(jnp.int32, (SUBLANES, LANES), 1), 1.0, 0.0).astype(BF16)
    for c in range(nc):
        rowi_scr[c] = _dot_sel_nt(eye8, gl[c * CHUNK:(c + 1) * CHUNK, :])
        rowb_scr[c] = _dot_sel_nt(eye8, bc[c * CHUNK:(c + 1) * CHUNK, :])

    ti = lax.broadcasted_iota(jnp.int32, (CHUNK, CHUNK), 0)
    si = lax.broadcasted_iota(jnp.int32, (CHUNK, CHUNK), 1)
    tril = ti >= si
    tcol = lax.broadcasted_iota(jnp.int32, (CHUNK, 1), 0).astype(F32)
    log_gamma = [math.log1p(-2.0 ** (-5 - h)) for h in range(H)]
    for h in range(H):
        dec_scr[h] = jnp.where(tril, jnp.exp((ti - si).astype(F32) * log_gamma[h]), 0.0)

    def chunk_body(c, carry):
        r0 = pl.multiple_of(c * CHUNK, CHUNK)
        rows = pl.ds(r0, CHUNK)
        rowi = rowi_scr[c]
        rowb = rowb_scr[c]
        glc = gl_scr[rows, :]
        bcc = bc_scr[rows, :]
        m_all = m_scr[...]
        c_old = [c_scr[h] for h in range(H)]
        n_old = [n_scr[h] for h in range(H)]
        s_old = [s_scr[h] for h in range(H)]
        c_new, n_new, s_new = [], [], []
        m_next = m_all
        lane_row = lax.broadcasted_iota(jnp.int32, (1, LANES), 1)
        for h in range(H):
            lo, hi = h * LANES, (h + 1) * LANES
            q = q_scr[h, rows, :]
            k = k_scr[h, rows, :]
            vb = proj_ref[_G_VA + h, rows, :].astype(BF16)
            b_col = bcc[:, H + h:H + h + 1]
            i_col = glc[:, h:h + 1]
            b_row = rowb[H + h:H + h + 1, :]
            i_row = rowi[h:h + 1, :]
            dmat = jnp.where(tril, b_col - b_row + i_row, -jnp.inf)
            a = jnp.max(dmat, axis=-1, keepdims=True)
            m_prev = m_all[:, h:h + 1]
            inter = b_col + m_prev
            m_t = jnp.maximum(inter, a)
            w_intra = jnp.exp(dmat - m_t)
            w_inter = jnp.exp(inter - m_t)
            qb = q.astype(BF16)
            amat = _dot_nt(qb, k.astype(BF16)) * w_intra
            cst = c_old[h]
            n_row = n_old[h]
            num =_dot(amat.astype(BF16), vb) + w_inter * _dot(qb, cst.astype(BF16))
            den = (jnp.sum(amat, axis=-1, keepdims=True)
                   + w_inter * jnp.sum(q * n_row, axis=-1, keepdims=True))
            hout = num / jnp.maximum(jnp.abs(den), jnp.exp(-m_t))
            m_new = m_t[CHUNK - 1:CHUNK, :]
            b_last = b_col[CHUNK - 1:CHUNK, :]
            w_s = jnp.exp(b_last - b_col + i_col - m_new)
            decay = jnp.exp(b_last + m_prev - m_new)
            kw = k * w_s
            c_new.append(decay * cst + _dot_tn(kw.astype(BF16), vb))
            n_new.append(decay * n_row + jnp.sum(kw, axis=0, keepdims=True))
            m_next = jnp.where(lane_row == h, m_new, m_next)
            z =_sigmoid(proj_ref[_G_OA + h, rows, :]) * hout
            u_ref[rows, lo:hi] = (_rms(z) * nwa_ref[:, lo:hi]
                                  + skip_ref[:, lo:hi] * xc_scr[h, rows, :])
        for h in range(H):
            lo, hi = h * LANES, (h + 1) * LANES
            lg = log_gamma[h]
            qrb = qr_scr[h, rows, :].astype(BF16)
            kr = kr_scr[h, rows, :]
            vb = proj_ref[_G_VB + h, rows, :].astype(BF16)
            amat = _dot_nt(qrb, kr.astype(BF16)) * dec_scr[h]
            sst = s_old[h]
            o = (_dot(amat.astype(BF16), vb)
                 + jnp.exp((tcol + 1.0) * lg) * _dot(qrb, sst.astype(BF16)))
            kws = kr * jnp.exp((CHUNK - 1.0 - tcol) * lg)
            s_new.append(math.exp(CHUNK * lg) * sst + _dot_tn(kws.astype(BF16), vb))
            gate = proj_ref[_G_GB + h, rows, :]
            u_ref[rows, D_MODEL // 2 + lo:D_MODEL // 2 + hi] = (
                _rms(o) * nwb_ref[:, lo:hi] * (gate * _sigmoid(gate)))
        for h in range(H):
            c_scr[h] = c_new[h]
            n_scr[h] = n_new[h]
            s_scr[h] = s_new[h]
        m_scr[...] = m_next
        return carry

    lax.fori_loop(0, nc, chunk_body, 0)

    @pl.when(j == nj - 1)
    def _():
        co_ref[0] = c_scr[...]
        no_ref[0] = n_scr[...]
        mo_ref[0] = m_scr[...]
        convo_ref[0] = conv_scr[...]
        so_ref[0] = s_scr[...]


def _even_scan_call(proj, gates, cos_t, sin_t, states, weights, *, ts, n_seq, steps, row_off):
    H = N_HEADS_EVEN
    c0, n0, m0, conv0, s0 = states
    ng = proj.shape[0]
    rows_idx = lambda b, j: (row_off + b * steps + j, 0)
    state_specs = [pl.BlockSpec((1, H, LANES, LANES), lambda b, j: (b, 0, 0, 0)),
                   pl.BlockSpec((1, H, 1, LANES), lambda b, j: (b, 0, 0, 0)),
                   pl.BlockSpec((1, 1, LANES), lambda b, j: (b, 0, 0)),
                   pl.BlockSpec((1, SUBLANES, H * LANES), lambda b, j: (b, 0, 0)),
                   pl.BlockSpec((1, H, LANES, LANES), lambda b, j: (b, 0, 0, 0))]
    in_specs = [pl.BlockSpec((ng, ts, LANES), lambda b, j: (0, row_off + b * steps + j, 0)),
                pl.BlockSpec((ts, LANES), rows_idx),
                pl.BlockSpec((ts, LANES), rows_idx),
                pl.BlockSpec((ts, LANES), rows_idx)] + state_specs
    for w in weights:
        in_specs.append(pl.BlockSpec(w.shape, functools.partial(lambda nd, b, j: (0,) * nd, w.ndim)))
    out_specs = [pl.BlockSpec((ts, D_MODEL), lambda b, j: (b * steps + j, 0))] + state_specs
    out_shape = [jax.ShapeDtypeStruct((n_seq * steps * ts, D_MODEL), F32),
                 jax.ShapeDtypeStruct(c0.shape, F32), jax.ShapeDtypeStruct(n0.shape, F32),
                 jax.ShapeDtypeStruct(m0.shape, F32), jax.ShapeDtypeStruct(conv0.shape, F32),
                 jax.ShapeDtypeStruct(s0.shape, F32)]
    nc = ts // CHUNK
    scratch = [pltpu.VMEM((H, LANES, LANES), F32), pltpu.VMEM((H, 1, LANES), F32),
               pltpu.VMEM((1, LANES), F32), pltpu.VMEM((SUBLANES, H * LANES), F32),
               pltpu.VMEM((H, LANES, LANES), F32),
               pltpu.VMEM((ts + 2 * SUBLANES, LANES), F32),
               pltpu.VMEM((H, ts, LANES), F32), pltpu.VMEM((H, ts, LANES), F32),
               pltpu.VMEM((H, ts, LANES), F32), pltpu.VMEM((H, ts, LANES), F32),
               pltpu.VMEM((H, ts, LANES), F32),
               pltpu.VMEM((ts, LANES), F32), pltpu.VMEM((ts, LANES), F32),
               pltpu.VMEM((nc, SUBLANES, CHUNK), F32), pltpu.VMEM((nc, SUBLANES, CHUNK), F32),
               pltpu.VMEM((H, CHUNK, CHUNK), F32)]
    return pl.pallas_call(
        functools.partial(_even_scan_kernel, ts=ts),
        grid=(n_seq, steps),
        in_specs=in_specs, out_specs=out_specs, out_shape=out_shape,
        scratch_shapes=scratch,
        compiler_params=_cparams(("arbitrary", "arbitrary")),
        name="even_scan",
    )(proj, gates, cos_t, sin_t, c0, n0, m0, conv0, s0, *weights)


_G_Q, _G_F, _G_I, _G_G = 0, 8, 16, 24


def _hgrn_intra(q, k, bcum, ti, si, tcol_i):
    amat = jnp.zeros((CHUNK, CHUNK), F32)
    for b in (32, 16, 8):
        nb2 = CHUNK // (2 * b)
        parts = [jnp.broadcast_to(bcum[m * 2 * b + b - 1:m * 2 * b + b, :], (2 * b, LANES))
                 for m in range(nb2)]
        ref = parts[0] if nb2 == 1 else jnp.concatenate(parts, axis=0)
        upper = ((tcol_i // b) % 2) == 1
        ql = jnp.where(upper, q * jnp.exp(bcum - ref), 0.0)
        kl = jnp.where(upper, 0.0, k * jnp.exp(ref - bcum))
        al = _dot_nt(ql.astype(BF16), kl.astype(BF16))
        amat = amat + jnp.where((ti // (2 * b)) == (si // (2 * b)), al, 0.0)
    nblk = CHUNK // SUBLANES
    b3 = bcum.reshape(nblk, SUBLANES, LANES)
    k3 = k.reshape(nblk, SUBLANES, LANES)
    for jj in range(SUBLANES):
        bj = jnp.broadcast_to(b3[:, jj:jj + 1, :], (nblk, SUBLANES, LANES)).reshape(CHUNK, LANES)
        kj = jnp.broadcast_to(k3[:, jj:jj + 1, :], (nblk, SUBLANES, LANES)).reshape(CHUNK, LANES)
        col = jnp.sum(q * kj * jnp.exp(bcum - bj), axis=-1, keepdims=True)
        sel = (si == (ti // SUBLANES) * SUBLANES + jj) & ((ti % SUBLANES) >= jj)
        amat = jnp.where(sel, col, amat)
    return amat


def _odd_scan_kernel(proj_ref, s0_ref, lb_ref, nw_ref, u_ref, so_ref,
                     st_scr, k_scr, bc_scr, *, ts):
    H = N_HEADS_ODD
    j = pl.program_id(1)
    nj = pl.num_programs(1)
    nc = ts // CHUNK

    @pl.when(j == 0)
    def _():
        for h in range(H):
            st_scr[h] = s0_ref[0, h].T

    for h in range(H):
        lo, hi = h * LANES, (h + 1) * LANES
        lbv = lb_ref[:, lo:hi]
        fpre = proj_ref[_G_F + h]
        logf = _logaddexp(jnp.log(lbv), jnp.log1p(-lbv) + _log_sigmoid(fpre))
        k_scr[h] = (1.0 - lbv) * _sigmoid(-fpre)
        bc_scr[h] = _chunk_cumsum(logf)

    ti = lax.broadcasted_iota(jnp.int32, (CHUNK, CHUNK), 0)
    si = lax.broadcasted_iota(jnp.int32, (CHUNK, CHUNK), 1)
    tcol_i = lax.broadcasted_iota(jnp.int32, (CHUNK, 1), 0)

    def chunk_body(c, carry):
        r0 = pl.multiple_of(c * CHUNK, CHUNK)
        rows = pl.ds(r0, CHUNK)
        st_old = [st_scr[h] for h in range(H)]
        st_new = []
        for h in range(H):
            lo, hi = h * LANES, (h + 1) * LANES
            q = proj_ref[_G_Q + h, rows, :]
            k = k_scr[h, rows, :]
            vb = proj_ref[_G_I + h, rows, :].astype(BF16)
            bcum = bc_scr[h, rows, :]
            amat = _hgrn_intra(q, k, bcum, ti, si, tcol_i)
            st = st_old[h]
            o = (_dot(amat.astype(BF16), vb)
                 + _dot_nt((q * jnp.exp(bcum)).astype(BF16), st.astype(BF16)))
            last = bcum[CHUNK - 1:CHUNK, :]
            kd = k * jnp.exp(last - bcum)
            st_new.append(st * jnp.exp(last) + _dot_tn(vb, kd.astype(BF16)))
            u_ref[rows, lo:hi] = (_rms(o) * nw_ref[:, lo:hi]
                                  * _sigmoid(proj_ref[_G_G + h, rows, :]))
        for h in range(H):
            st_scr[h] = st_new[h]
        return carry

    lax.fori_loop(0, nc, chunk_body, 0)

    @pl.when(j == nj - 1)
    def _():
        for h in range(H):
            so_ref[0, h] = st_scr[h].T


def _odd_scan_call(proj, s0, lb, nw, *, ts, n_seq, steps, row_off):
    H = N_HEADS_ODD
    ng = proj.shape[0]
    st_spec = pl.BlockSpec((1, H, LANES, LANES), lambda b, j: (b, 0, 0, 0))
    return pl.pallas_call(
        functools.partial(_odd_scan_kernel, ts=ts),
        grid=(n_seq, steps),
        in_specs=[pl.BlockSpec((ng, ts, LANES), lambda b, j: (0, row_off + b * steps + j, 0)),
                  st_spec,
                  pl.BlockSpec((1, D_MODEL), lambda b, j: (0, 0)),
                  pl.BlockSpec((1, D_MODEL), lambda b, j: (0, 0))],
        out_specs=[pl.BlockSpec((ts, D_MODEL), lambda b, j: (b * steps + j, 0)), st_spec],
        out_shape=[jax.ShapeDtypeStruct((n_seq * steps * ts, D_MODEL), F32),
                   jax.ShapeDtypeStruct(s0.shape, F32)],
        scratch_shapes=[pltpu.VMEM((H, LANES, LANES), F32),
                        pltpu.VMEM((H, ts, LANES), F32),
                        pltpu.VMEM((H, ts, LANES), F32)],
        compiler_params=_cparams(("arbitrary", "arbitrary")),
        name="odd_scan",
    )(proj, s0, lb, nw)


def _post_kernel(x_ref, up_ref, us_ref, mod_ref, wout_ref, nw_ref, wr_ref, br_ref,
                 xo_ref, hf_ref, eidx_ref, gate_ref, rank_ref, cnt_ref,
                 *, tm, npc, n_prompt_tiles):
    i = pl.program_id(0)
    nch = tm // CHUNK

    u = jnp.where(i < n_prompt_tiles, up_ref[...], us_ref[...])
    y = _dot(u.astype(BF16), wout_ref[...])
    for c in range(nch):
        seq = _seq_row(i, nch, c, npc)
        gm = mod_ref[pl.ds(seq, 1), 2 * D_MODEL:3 * D_MODEL]
        shf = mod_ref[pl.ds(seq, 1), 3 * D_MODEL:4 * D_MODEL]
        scf = mod_ref[pl.ds(seq, 1), 4 * D_MODEL:5 * D_MODEL]
        rs = slice(c * CHUNK, (c + 1) * CHUNK)
        xn = x_ref[rs, :] + gm * y[rs, :]
        xo_ref[rs, :] = xn
        hf_ref[rs, :] = _rms(xn) * nw_ref[...] * (1.0 + scf) + shf

    logits = _dot3(hf_ref[...], wr_ref[...]) + br_ref[...]
    lane_i = lax.broadcasted_iota(jnp.int32, (tm, LANES), 1)
    lane_f = lane_i.astype(F32)
    vals, idxs = [], []
    cur = logits
    for _ in range(TOP_K):
        m = jnp.max(cur, axis=-1, keepdims=True)
        idx = jnp.min(jnp.where(cur == m, lane_f, float(LANES)), axis=-1, keepdims=True)
        vals.append(m)
        idxs.append(idx)
        cur = jnp.where(lane_f == idx, -jnp.inf, cur)
    exps = [jnp.exp(v - vals[0]) for v in vals]
    denom = exps[0] + exps[1] + exps[2] + exps[3]
    onehot = jnp.zeros((tm, LANES), F32)
    for idx in idxs:
        onehot = onehot + jnp.where(lane_f == idx, 1.0, 0.0)
    r = lax.broadcasted_iota(jnp.int32, (tm, tm), 0)
    cidx = lax.broadcasted_iota(jnp.int32, (tm, tm), 1)
    strict = jnp.where((cidx < r) & (cidx // DISPATCH_TILE == r // DISPATCH_TILE), 1.0, 0.0).astype(BF16)
    before = _dot(strict, onehot.astype(BF16))
    eidx_o = jnp.zeros((tm, LANES), F32)
    gate_o = jnp.zeros((tm, LANES), F32)
    rank_o = jnp.zeros((tm, LANES), F32)
    for kk in range(TOP_K):
        rk = jnp.sum(jnp.where(lane_f == idxs[kk], before, 0.0), axis=-1, keepdims=True)
        eidx_o = jnp.where(lane_i == kk, idxs[kk], eidx_o)
        gate_o = jnp.where(lane_i == kk, exps[kk] / denom, gate_o)
        rank_o = jnp.where(lane_i == kk, rk, rank_o)
    eidx_ref[...] = eidx_o
    gate_ref[...] = gate_o
    rank_ref[...] = rank_o
    for s in range(tm // DISPATCH_TILE):
        cnt_ref[s] = jnp.sum(onehot[s * DISPATCH_TILE:(s + 1) * DISPATCH_TILE, :], axis=0, keepdims=True)


def _post_call(x, u_p, u_s, mod_l, w_out, nw, w_r, b_r, npc):
    ttot = x.shape[0]
    tm = TOKEN_TILE
    npt = u_p.shape[0] // tm
    sub = tm // DISPATCH_TILE
    tile = lambda i: (i, 0)
    const = lambda i: (0, 0)
    return pl.pallas_call(
        functools.partial(_post_kernel, tm=tm, npc=npc, n_prompt_tiles=npt),
        grid=(ttot // tm,),
        in_specs=[pl.BlockSpec((tm, D_MODEL), tile),
                  pl.BlockSpec((tm, D_MODEL), lambda i: (jnp.minimum(i, npt - 1), 0)),
                  pl.BlockSpec((tm, D_MODEL), lambda i: (jnp.maximum(i - npt, 0), 0)),
                  pl.BlockSpec(mod_l.shape, const),
                  pl.BlockSpec(w_out.shape, const),
                  pl.BlockSpec((1, D_MODEL), const),
                  pl.BlockSpec(w_r.shape, const),
                  pl.BlockSpec((1, LANES), const)],
        out_specs=[pl.BlockSpec((tm, D_MODEL), tile), pl.BlockSpec((tm, D_MODEL), tile),
                   pl.BlockSpec((tm, LANES), tile), pl.BlockSpec((tm, LANES), tile),
                   pl.BlockSpec((tm, LANES), tile), pl.BlockSpec((sub, 1, LANES), lambda i: (i, 0, 0))],
        out_shape=[jax.ShapeDtypeStruct((ttot, D_MODEL), F32), jax.ShapeDtypeStruct((ttot, D_MODEL), F32),
                   jax.ShapeDtypeStruct((ttot, LANES), F32), jax.ShapeDtypeStruct((ttot, LANES), F32),
                   jax.ShapeDtypeStruct((ttot, LANES), F32),
                   jax.ShapeDtypeStruct((ttot // DISPATCH_TILE, 1, LANES), F32)],
        compiler_params=_cparams(("arbitrary",)),
        name="post",
    )(x, u_p, u_s, mod_l, w_out, nw, w_r, b_r)


def _plan_kernel(eidx_ref, rank_ref, cnt_ref, lpos_ref, n8_ref, loff_ref, gbase_ref, blk_ref, ends_ref,
                 loff_scr, *, tm, nbp, ntp):
    cnt = cnt_ref[...]
    n8 = jnp.floor((cnt + (SUBLANES - 1.0)) * (1.0 / SUBLANES)) * float(SUBLANES)
    r = lax.broadcasted_iota(jnp.int32, (LANES, LANES), 0)
    c = lax.broadcasted_iota(jnp.int32, (LANES, LANES), 1)
    p1, p2, p3 = _split3(n8)
    lt = jnp.where(r < c, 1.0, 0.0).astype(BF16)
    loff = _dot(p1, lt) + (_dot(p2, lt) + _dot(p3, lt))
    gtot = jnp.broadcast_to(jnp.sum(n8, axis=0, keepdims=True), (SUBLANES, LANES))
    nblk = jnp.floor((gtot + (EXPERT_ROWS - 1.0)) * (1.0 / EXPERT_ROWS))
    le = jnp.where(r <= c, 1.0, 0.0).astype(BF16)
    q1, q2, q3 = _split3(nblk)
    ends = _dot(q1, le) + (_dot(q2, le) + _dot(q3, le))
    start_row = (ends[0:1, :] - nblk[0:1, :]) * float(EXPERT_ROWS)
    tr = lax.broadcasted_iota(jnp.int32, (ntp, ntp), 0)
    tc = lax.broadcasted_iota(jnp.int32, (ntp, ntp), 1)
    gbase = start_row + _dot_sel_lhs(jnp.where(tc < tr, 1.0, 0.0).astype(BF16), n8)
    loff_scr[...] = loff

    i = pl.program_id(0)
    lane_i = lax.broadcasted_iota(jnp.int32, (DISPATCH_TILE, LANES), 1)
    lane_f = lane_i.astype(F32)
    for s in range(tm // DISPATCH_TILE):
        rs = slice(s * DISPATCH_TILE, (s + 1) * DISPATCH_TILE)
        off_row = loff_scr[pl.ds(i * (tm // DISPATCH_TILE) + s, 1), :]
        eidx = eidx_ref[rs, :]
        rank = rank_ref[rs, :]
        lpos = jnp.zeros((DISPATCH_TILE, LANES), F32)
        for kk in range(TOP_K):
            base = jnp.sum(jnp.where(lane_f == eidx[:, kk:kk + 1], off_row, 0.0), axis=-1, keepdims=True)
            lpos = jnp.where(lane_i == kk, base + rank[:, kk:kk + 1], lpos)
        lpos_ref[rs, :] = lpos

    @pl.when(i == 0)
    def _():
        n8_ref[...] = n8.astype(jnp.int32)
        loff_ref[...] = loff.astype(jnp.int32)
        gbase_ref[...] = gbase.astype(jnp.int32)
        bi = lax.broadcasted_iota(jnp.int32, (nbp, LANES), 0).astype(F32)
        li = lax.broadcasted_iota(jnp.int32, (nbp, LANES), 1)
        done = jnp.where((li < N_EXPERTS) & (ends[0:1, :] <= bi), 1.0, 0.0)
        be = jnp.minimum(jnp.sum(done, axis=-1, keepdims=True), N_EXPERTS - 1.0)
        blk_ref[...] = jnp.broadcast_to(be, (nbp, LANES)).astype(jnp.int32)
        ends_ref[...] = ends.astype(jnp.int32)


def _plan_call(eidx, rank, cnt_tiles, nbp):
    ttot = eidx.shape[0]
    tm = TOKEN_TILE
    ntp = cnt_tiles.shape[0]
    tile = lambda i: (i, 0)
    const = lambda i: (0, 0)
    tbl = jax.ShapeDtypeStruct((ntp, LANES), jnp.int32)
    return pl.pallas_call(
        functools.partial(_plan_kernel, tm=tm, nbp=nbp, ntp=ntp),
        grid=(ttot // tm,),
        in_specs=[pl.BlockSpec((tm, LANES), tile), pl.BlockSpec((tm, LANES), tile),
                  pl.BlockSpec((ntp, LANES), const)],
        out_specs=[pl.BlockSpec((tm, LANES), tile),
                   pl.BlockSpec((ntp, LANES), const), pl.BlockSpec((ntp, LANES), const),
                   pl.BlockSpec((ntp, LANES), const),
                   pl.BlockSpec((nbp, LANES), const), pl.BlockSpec((SUBLANES, LANES), const)],
        out_shape=[jax.ShapeDtypeStruct((ttot, LANES), F32), tbl, tbl, tbl,
                   jax.ShapeDtypeStruct((nbp, LANES), jnp.int32),
                   jax.ShapeDtypeStruct((SUBLANES, LANES), jnp.int32)],
        scratch_shapes=[pltpu.VMEM((ntp, LANES), F32)],
        compiler_params=_cparams(("arbitrary",)),
        name="plan",
    )(eidx, rank, cnt_tiles)


_GROUP_BITS = tuple(range(3, DISPATCH_TILE.bit_length()))


def _group_copies(n8_ref, loff_ref, gbase_ref, tile, make_copy, wait):
    def per_expert(e, carry):
        idx = tile * N_EXPERTS + e
        n = n8_ref[idx]
        off = loff_ref[idx]
        base = gbase_ref[idx]
        for bit in _GROUP_BITS:
            size = 1 << bit

            @pl.when((n & size) != 0)
            def _():
                done = n & ~(2 * size - 1)
                cp = make_copy(pl.multiple_of(off + done, SUBLANES), pl.multiple_of(base + done, SUBLANES), size)
                if wait:
                    cp.wait()
                else:
                    cp.start()
        return carry

    lax.fori_loop(0, N_EXPERTS, per_expert, 0)


def _dispatch_kernel(n8_ref, loff_ref, gbase_ref, hf_ref, lpos_ref, gate_ref, xs_in_ref, xs_ref,
                     sbuf, sem):
    del xs_in_ref
    i = pl.program_id(0)
    eye8 = jnp.where(lax.broadcasted_iota(jnp.int32, (SUBLANES, LANES), 0)
                     == lax.broadcasted_iota(jnp.int32, (SUBLANES, LANES), 1), 1.0, 0.0).astype(BF16)
    lpos_t = _dot_sel_nt(eye8, lpos_ref[...])
    gate_t = _dot_sel_nt(eye8, gate_ref[...])
    row = lax.broadcasted_iota(jnp.int32, (SORT_ROWS, DISPATCH_TILE), 0).astype(F32)
    perm = jnp.zeros((SORT_ROWS, DISPATCH_TILE), F32)
    wgate = jnp.zeros((SORT_ROWS, DISPATCH_TILE), F32)
    for kk in range(TOP_K):
        hit = row == lpos_t[kk:kk + 1, :]
        perm = jnp.where(hit, 1.0, perm)
        wgate = jnp.where(hit, gate_t[kk:kk + 1, :], wgate)
    slot = i % 2
    sbuf[slot, :, 0:D_MODEL] = _dot(perm.astype(BF16), hf_ref[...].astype(BF16))
    sbuf[slot, :, D_MODEL:XS_WIDTH] = _dot_sel_rhs(wgate, jnp.ones((DISPATCH_TILE, LANES), BF16))

    def copies(tile, buf_slot, wait):
        def make_copy(src_row, dst_row, size):
            return pltpu.make_async_copy(sbuf.at[buf_slot, pl.ds(src_row, size), :],
                                         xs_ref.at[pl.ds(dst_row, size), :], sem.at[buf_slot])
        _group_copies(n8_ref, loff_ref, gbase_ref, tile, make_copy, wait=wait)

    copies(i, slot, wait=False)

    @pl.when(i > 0)
    def _():
        copies(i - 1, 1 - slot, wait=True)

    @pl.when(i == pl.num_programs(0) - 1)
    def _():
        copies(i, slot, wait=True)


def _dispatch_call(tables, hf, lpos, gate, xs):
    ttot = hf.shape[0]
    tm = DISPATCH_TILE
    tile = lambda i, *_: (i, 0)
    grid_spec = pltpu.PrefetchScalarGridSpec(
        num_scalar_prefetch=3, grid=(ttot // tm,),
        in_specs=[pl.BlockSpec((tm, D_MODEL), tile), pl.BlockSpec((tm, LANES), tile),
                  pl.BlockSpec((tm, LANES), tile), pl.BlockSpec(memory_space=pl.ANY)],
        out_specs=pl.BlockSpec(memory_space=pl.ANY),
        scratch_shapes=[pltpu.VMEM((2, SORT_ROWS, XS_WIDTH), F32), pltpu.SemaphoreType.DMA((2,))])
    return pl.pallas_call(
        _dispatch_kernel,
        grid_spec=grid_spec,
        out_shape=jax.ShapeDtypeStruct(xs.shape, xs.dtype),
        input_output_aliases={6: 0},
        compiler_params=pltpu.CompilerParams(dimension_semantics=("arbitrary",),
                                             vmem_limit_bytes=VMEM_LIMIT, has_side_effects=True),
        name="dispatch",
    )(*tables, hf, lpos, gate, xs)


def _expert_kernel(be_ref, nu_ref, xs_ref, wgu_ref, bgu_ref, wdn_ref, bdn_ref, y_ref,
                   wgu_bf, wdn_bf):
    b = pl.program_id(0)

    @pl.when(b < nu_ref[0])
    def _():
        prev = be_ref[jnp.maximum(b - 1, 0)]

        @pl.when((b == 0) | (be_ref[b] != prev))
        def _():
            wgu_bf[...] = wgu_ref[0, 0].astype(BF16)
            wdn_bf[...] = wdn_ref[0, 0].astype(BF16)

        gu = _dot(xs_ref[:, 0:D_MODEL].astype(BF16), wgu_bf[...]) + bgu_ref[0, 0]
        g = jnp.minimum(gu[:, :D_MODEL], SWIGLU_LIMIT)
        u = jnp.clip(gu[:, D_MODEL:], -SWIGLU_LIMIT, SWIGLU_LIMIT)
        act = (u + 1.0) * (g * _sigmoid(SWIGLU_ALPHA * g))
        gate = xs_ref[:, D_MODEL:D_MODEL + 1]
        y_ref[...] = (_dot(act.astype(BF16), wdn_bf[...]) + bdn_ref[0, 0]) * gate

    @pl.when(b >= nu_ref[0])
    def _():
        y_ref[...] = jnp.zeros_like(y_ref)


def _expert_call(blk_expert, n_used, xs, w_gu, b_gu, w_dn, b_dn, layer):
    nb = xs.shape[0] // EXPERT_ROWS
    d_ff2 = w_gu.shape[3]
    blk = lambda b, be, nu: (jnp.minimum(b, nu[0] - 1), 0)
    blk_out = lambda b, be, nu: (b, 0)
    exp4 = lambda b, be, nu: (layer, be[jnp.minimum(b, nu[0] - 1)], 0, 0)
    grid_spec = pltpu.PrefetchScalarGridSpec(
        num_scalar_prefetch=2, grid=(nb,),
        in_specs=[pl.BlockSpec((EXPERT_ROWS, XS_WIDTH), blk),
                  pl.BlockSpec((1, 1, D_MODEL, d_ff2), exp4),
                  pl.BlockSpec((1, 1, 1, d_ff2), exp4),
                  pl.BlockSpec((1, 1, D_MODEL, D_MODEL), exp4),
                  pl.BlockSpec((1, 1, 1, D_MODEL), exp4)],
        out_specs=pl.BlockSpec((EXPERT_ROWS, D_MODEL), blk_out),
        scratch_shapes=[pltpu.VMEM((D_MODEL, d_ff2), BF16), pltpu.VMEM((D_MODEL, D_MODEL), BF16)])
    return pl.pallas_call(
        _expert_kernel,
        grid_spec=grid_spec,
        out_shape=jax.ShapeDtypeStruct((xs.shape[0], D_MODEL), F32),
        compiler_params=_cparams(("arbitrary",)),
        name="experts",
    )(blk_expert, n_used, xs, w_gu, b_gu, w_dn, b_dn)


def _combine_kernel(*refs, tm, npc, final):
    if final:
        (n8_ref, loff_ref, gbase_ref, x_ref, lpos_ref, mod_ref, fnw_ref, y_hbm,
         xo_ref, yf_ref, ybuf, sem) = refs
    else:
        n8_ref, loff_ref, gbase_ref, x_ref, lpos_ref, mod_ref, y_hbm, xo_ref, ybuf, sem = refs
    i = pl.program_id(0)
    nch = tm // CHUNK

    slot = i % 2

    def copies(tile, buf_slot, wait):
        def make_copy(buf_row, src_row, size):
            return pltpu.make_async_copy(y_hbm.at[pl.ds(src_row, size), :],
                                         ybuf.at[buf_slot, pl.ds(buf_row, size), :], sem.at[buf_slot])
        _group_copies(n8_ref, loff_ref, gbase_ref, tile, make_copy, wait=wait)

    @pl.when(i == 0)
    def _():
        ybuf[...] = jnp.zeros_like(ybuf)
        copies(i, slot, wait=False)

    @pl.when(i + 1 < pl.num_programs(0))
    def _():
        copies(i + 1, 1 - slot, wait=False)

    copies(i, slot, wait=True)

    lpos = lpos_ref[...]
    col = lax.broadcasted_iota(jnp.int32, (tm, SORT_ROWS), 1).astype(F32)
    unperm = jnp.zeros((tm, SORT_ROWS), F32)
    for kk in range(TOP_K):
        unperm = jnp.where(col == lpos[:, kk:kk + 1], 1.0, unperm)
    acc = _dot_sel_lhs2(unperm.astype(BF16), ybuf[slot])
    for c in range(nch):
        seq = _seq_row(i, nch, c, npc)
        gf = mod_ref[pl.ds(seq, 1), 5 * D_MODEL:6 * D_MODEL]
        rs = slice(c * CHUNK, (c + 1) * CHUNK)
        xn = x_ref[rs, :] + gf * acc[rs, :]
        xo_ref[rs, :] = xn
        if final:
            yf_ref[rs, :] = _rms(xn) * fnw_ref[...]


def _combine_call(tables, x, lpos, mod_l, y_sorted, npc, final_w):
    ttot = x.shape[0]
    tm = DISPATCH_TILE
    final = final_w is not None
    tile = lambda i, *_: (i, 0)
    const = lambda i, *_: (0, 0)
    in_specs = [pl.BlockSpec((tm, D_MODEL), tile),
                pl.BlockSpec((tm, LANES), tile),
                pl.BlockSpec(mod_l.shape, const)]
    args = [x, lpos, mod_l]
    if final:
        in_specs.append(pl.BlockSpec((1, D_MODEL), const))
        args.append(final_w)
    in_specs.append(pl.BlockSpec(memory_space=pl.ANY))
    args.append(y_sorted)
    out_specs = [pl.BlockSpec((tm, D_MODEL), tile)]
    out_shape = [jax.ShapeDtypeStruct((ttot, D_MODEL), F32)]
    if final:
        out_specs.append(pl.BlockSpec((tm, D_MODEL), tile))
        out_shape.append(jax.ShapeDtypeStruct((ttot, D_MODEL), F32))
    grid_spec = pltpu.PrefetchScalarGridSpec(
        num_scalar_prefetch=3, grid=(ttot // tm,),
        in_specs=in_specs, out_specs=out_specs,
        scratch_shapes=[pltpu.VMEM((2, SORT_ROWS, D_MODEL), F32), pltpu.SemaphoreType.DMA((2,))])
    return pl.pallas_call(
        functools.partial(_combine_kernel, tm=tm, npc=npc, final=final),
        grid_spec=grid_spec, out_shape=out_shape,
        compiler_params=_cparams(("arbitrary",)),
        name="combine",
    )(*tables, *args)


def kernel(x_prompt, x_sample, c_prompt, c_sample, state_mlstm_C, state_mlstm_n, state_mlstm_m, state_mlstm_conv, state_ret_S, state_hgrn_S, w_ada, b_ada, norm_mix_w, norm_ffn_w, final_norm_w, w_in_even, b_mlstm_i, b_mlstm_f, w_mlstm_conv, b_mlstm_conv, w_mlstm_q, w_mlstm_k, mlstm_skip, mlstm_norm_w, ret_norm_w, w_out_even, w_in_odd, hgrn_lb_logits, hgrn_norm_w, w_out_odd, moe_router_w, moe_router_b, moe_w_gate_up, moe_b_gate_up, moe_w_down, moe_b_down):
    bp, seq, d = x_prompt.shape
    bs, dseq, _ = x_sample.shape
    assert bp == 1 and d == D_MODEL and dseq == CHUNK
    assert seq % TOKEN_TILE == 0 and (bs * dseq) % TOKEN_TILE == 0
    depth = w_ada.shape[0]
    tp, tsmp = seq, bs * dseq
    ttot = tp + tsmp
    npc = tp // CHUNK
    he, ho = N_HEADS_EVEN, N_HEADS_ODD
    da = he * LANES
    past_len = 1024

    x = jnp.concatenate([x_prompt.reshape(tp, d), x_sample.reshape(tsmp, d)], axis=0)
    n_mod_rows = 2 * SUBLANES
    assert 1 + bs <= n_mod_rows
    c_all = jnp.zeros((n_mod_rows, d), F32).at[0:1].set(c_prompt).at[1:1 + bs].set(c_sample)
    mod = _ada_call(c_all, w_ada, b_ada)

    half = LANES // 2
    inv = ROPE_BASE ** (-jnp.arange(half, dtype=F32) / half)
    pos_all = jnp.concatenate([jnp.arange(tp, dtype=F32),
                               jnp.tile(past_len + jnp.arange(dseq, dtype=F32), bs)])
    ang = pos_all[:, None] * inv[None, :]
    cos_t = jnp.concatenate([jnp.cos(ang), jnp.cos(ang)], axis=-1)
    sin_t = jnp.concatenate([-jnp.sin(ang), jnp.sin(ang)], axis=-1)

    lb_p = jax.nn.softmax(hgrn_lb_logits.astype(F32), axis=0)
    lbs = jnp.cumsum(lb_p, axis=0) - lb_p[0]

    n_tiles = ttot // DISPATCH_TILE
    ntp = -(-n_tiles // LANES) * LANES
    max_rows = ttot * TOP_K + n_tiles * N_EXPERTS * (SUBLANES - 1)
    nb = -(-max_rows // EXPERT_ROWS) + N_EXPERTS
    nbp = -(-nb // SUBLANES) * SUBLANES
    xs = jnp.zeros((nb * EXPERT_ROWS, XS_WIDTH), F32)

    ts_p = TOKEN_TILE
    steps_p = tp // ts_p
    even_out, odd_out = [], []
    y_final = None
    for l in range(depth):
        jl = l // 2
        mod_l = mod[l]
        if l % 2 == 0:
            w_in = w_in_even[jl]
            w_main = jnp.concatenate([w_in[:, :3 * da], w_in[:, 3 * da + 2 * he:]], axis=1).astype(BF16)
            w_gate = jnp.zeros((d, LANES), F32).at[:, :2 * he].set(w_in[:, 3 * da:3 * da + 2 * he])
            proj, gates = _inproj_call(x, mod_l, norm_mix_w[l][None], w_main, w_gate, npc)
            gbias = jnp.zeros((1, LANES), F32).at[0, :he].set(b_mlstm_i[jl]).at[0, he:2 * he].set(b_mlstm_f[jl])
            weights = [w_mlstm_conv[jl], b_mlstm_conv[jl][None],
                       w_mlstm_q[jl].astype(BF16), w_mlstm_k[jl].astype(BF16), gbias,
                       mlstm_skip[jl][None], mlstm_norm_w[jl][None], ret_norm_w[jl][None]]
            zeros_p = (jnp.zeros((1, he, LANES, LANES), F32), jnp.zeros((1, he, 1, LANES), F32),
                       jnp.zeros((1, 1, LANES), F32), jnp.zeros((1, SUBLANES, da), F32),
                       jnp.zeros((1, he, LANES, LANES), F32))
            st_s = (state_mlstm_C[jl], state_mlstm_n[jl][:, :, None, :],
                    jnp.zeros((bs, 1, LANES), F32).at[:, 0, :he].set(state_mlstm_m[jl]),
                    jnp.zeros((bs, SUBLANES, da), F32).at[:, SUBLANES - (CONV_W - 1):].set(state_mlstm_conv[jl]),
                    state_ret_S[jl])
            res_p = _even_scan_call(proj, gates, cos_t, sin_t, zeros_p, weights,
                                    ts=ts_p, n_seq=1, steps=steps_p, row_off=0)
            res_s = _even_scan_call(proj, gates, cos_t, sin_t, st_s, weights,
                                    ts=CHUNK, n_seq=bs, steps=1, row_off=npc)
            u_p, u_s = res_p[0], res_s[0]
            even_out.append((res_p[1:], res_s[1:]))
            w_out = w_out_even[jl].astype(BF16)
        else:
            proj = _inproj_call(x, mod_l, norm_mix_w[l][None], w_in_odd[jl].astype(BF16), None, npc)[0]
            lb = lbs[l][None]
            nw = hgrn_norm_w[jl][None]
            u_p, sp = _odd_scan_call(proj, jnp.zeros((1, ho, LANES, LANES), F32), lb, nw,
                                     ts=ts_p, n_seq=1, steps=steps_p, row_off=0)
            u_s, ss = _odd_scan_call(proj, state_hgrn_S[jl], lb, nw,
                                     ts=CHUNK, n_seq=bs, steps=1, row_off=npc)
            odd_out.append((sp, ss))
            w_out = w_out_odd[jl].astype(BF16)

        w_r = jnp.zeros((d, LANES), F32).at[:, :N_EXPERTS].set(moe_router_w[l])
        b_r = jnp.full((1, LANES), -jnp.inf, F32).at[0, :N_EXPERTS].set(moe_router_b[l])
        x, hf, eidx, gate, rank, cnt = _post_call(x, u_p, u_s, mod_l, w_out, norm_ffn_w[l][None], w_r, b_r, npc)
        cnt_tiles = jnp.zeros((ntp, LANES), F32).at[:n_tiles].set(cnt[:, 0, :])
        lpos, n8, loff, gbase, blk, ends = _plan_call(eidx, rank, cnt_tiles, nbp)
        tables = [t[:n_tiles, :N_EXPERTS].reshape(-1) for t in (n8, loff, gbase)]
        blk_expert = blk[:nb, 0]
        n_used = ends[0, N_EXPERTS - 1:N_EXPERTS]
        xs = _dispatch_call(tables, hf, lpos, gate, xs)
        y_sorted = _expert_call(blk_expert, n_used, xs, moe_w_gate_up, moe_b_gate_up[:, :, None, :],
                                moe_w_down, moe_b_down[:, :, None, :], l)
        if l == depth - 1:
            x, y_final = _combine_call(tables, x, lpos, mod_l, y_sorted, npc, final_norm_w[None])
        else:
            x = _combine_call(tables, x, lpos, mod_l, y_sorted, npc, None)[0]

    def even_states(which):
        cs = jnp.stack([e[which][0] for e in even_out])
        ns = jnp.stack([e[which][1][:, :, 0, :] for e in even_out])
        ms = jnp.stack([e[which][2][:, 0, :he] for e in even_out])
        cv = jnp.stack([e[which][3][:, SUBLANES - (CONV_W - 1):, :] for e in even_out])
        ss = jnp.stack([e[which][4] for e in even_out])
        return cs, ns, ms, cv, ss

    p_c, p_n, p_m, p_cv, p_s = even_states(0)
    s_c, s_n, s_m, s_cv, s_s = even_states(1)
    p_h = jnp.stack([o[0] for o in odd_out])
    s_h = jnp.stack([o[1] for o in odd_out])
    y_prompt = y_final[:tp].reshape(bp, seq, d)
    y_sample = y_final[tp:].reshape(bs, dseq, d)
    return (y_prompt, y_sample, p_c, p_n, p_m, p_cv, p_s, p_h, s_c, s_n, s_m, s_cv, s_s, s_h)
```

```python
import functools
import math

import jax
import jax.numpy as jnp
from jax import lax
from jax.experimental import pallas as pl
from jax.experimental.pallas import tpu as pltpu

F32 = jnp.float32
BF16 = jnp.bfloat16

CHUNK = 64
LANES = 128
SUBLANES = 8
D_MODEL = 1024
N_HEADS_EVEN = 4
N_HEADS_ODD = 8
CONV_W = 4
N_EXPERTS = 32
TOP_K = 4
SWIGLU_LIMIT = 7.0
SWIGLU_ALPHA = 1.702
EPS = 1e-6
ROPE_BASE = 10000.0

TOKEN_TILE = 512
INPROJ_TILE = 512
DISPATCH_TILE = 256
SORT_ROWS = DISPATCH_TILE * TOP_K + LANES * 2
XS_WIDTH = D_MODEL + LANES
EXPERT_ROWS = 512
VMEM_LIMIT = 56 * 1024 * 1024


def _cparams(sem, vmem=VMEM_LIMIT):
    return pltpu.CompilerParams(dimension_semantics=sem, vmem_limit_bytes=vmem)


def _dot(a, b):
    return jnp.dot(a, b, preferred_element_type=F32)


def _dot_nt(a, b):
    return lax.dot_general(a, b, (((1,), (1,)), ((), ())), preferred_element_type=F32)


def _dot_tn(a, b):
    return lax.dot_general(a, b, (((0,), (0,)), ((), ())), preferred_element_type=F32)


def _split3(x):
    p1 = x.astype(BF16)
    r1 = x - p1.astype(F32)
    p2 = r1.astype(BF16)
    p3 = (r1 - p2.astype(F32)).astype(BF16)
    return p1, p2, p3


def _dot3(a, b):
    ah = a.astype(BF16)
    al = (a - ah.astype(F32)).astype(BF16)
    bh = b.astype(BF16)
    bl = (b - bh.astype(F32)).astype(BF16)
    return _dot(ah, bh) + (_dot(ah, bl) + _dot(al, bh))


def _dot3_tn(a, b):
    ah = a.astype(BF16)
    al = (a - ah.astype(F32)).astype(BF16)
    bh = b.astype(BF16)
    bl = (b - bh.astype(F32)).astype(BF16)
    return _dot_tn(ah, bh) + (_dot_tn(ah, bl) + _dot_tn(al, bh))


def _dot_sel_lhs(sel_bf16, x):
    p1, p2, p3 = _split3(x)
    return _dot(sel_bf16, p1) + (_dot(sel_bf16, p2) + _dot(sel_bf16, p3))


def _dot_sel_nt(sel_bf16, x):
    p1, p2, p3 = _split3(x)
    return _dot_nt(sel_bf16, p1) + (_dot_nt(sel_bf16, p2) + _dot_nt(sel_bf16, p3))


def _dot_sel_rhs(x, sel_bf16):
    p1, p2, p3 = _split3(x)
    return _dot(p1, sel_bf16) + (_dot(p2, sel_bf16) + _dot(p3, sel_bf16))


def _dot_sel_lhs2(sel_bf16, x):
    hi = x.astype(BF16)
    lo = (x - hi.astype(F32)).astype(BF16)
    return _dot(sel_bf16, hi) + _dot(sel_bf16, lo)


def _sigmoid(x):
    return 1.0 / (1.0 + jnp.exp(-x))


def _log_sigmoid(x):
    return jnp.minimum(x, 0.0) - jnp.log1p(jnp.exp(-jnp.abs(x)))


def _logaddexp(a, b):
    return jnp.maximum(a, b) + jnp.log1p(jnp.exp(-jnp.abs(a - b)))


def _rms(x):
    return x * lax.rsqrt(jnp.mean(x * x, axis=-1, keepdims=True) + EPS)


def _chunk_tri(n):
    r = lax.broadcasted_iota(jnp.int32, (n, n), 0)
    c = lax.broadcasted_iota(jnp.int32, (n, n), 1)
    return jnp.where((r // CHUNK == c // CHUNK) & (c <= r), 1.0, 0.0).astype(BF16)


def _chunk_cumsum(x):
    rows, n = x.shape
    vregs = CHUNK // SUBLANES
    x4 = x.reshape(rows // CHUNK, vregs, SUBLANES, n)
    sub = lax.broadcasted_iota(jnp.int32, x4.shape, 2)
    s = x4
    for shift in (1, 2, 4):
        s = s + jnp.where(sub >= shift, pltpu.roll(s, shift, 2), 0.0)
    outs, carry = [], None
    for v in range(vregs):
        cur = s[:, v] if carry is None else s[:, v] + carry
        outs.append(cur)
        carry = jnp.broadcast_to(cur[:, SUBLANES - 1:SUBLANES, :], cur.shape)
    return jnp.stack(outs, axis=1).reshape(rows, n)


def _seq_row(tile_idx, chunks_per_tile, c, n_prompt_chunks):
    return jnp.maximum(tile_idx * chunks_per_tile + c - (n_prompt_chunks - 1), 0)


def _ada_kernel(c_ref, w_ref, b_ref, o_ref):
    c = c_ref[...]
    o_ref[0] = _dot3(c * _sigmoid(c), w_ref[0]) + b_ref[0]


def _ada_call(c_all, w_ada, b_ada):
    depth = w_ada.shape[0]
    nrow = c_all.shape[0]
    ncol = w_ada.shape[2] // D_MODEL
    return pl.pallas_call(
        _ada_kernel,
        grid=(depth, ncol),
        in_specs=[pl.BlockSpec((nrow, D_MODEL), lambda l, j: (0, 0)),
                  pl.BlockSpec((1, D_MODEL, D_MODEL), lambda l, j: (l, 0, j)),
                  pl.BlockSpec((1, 1, D_MODEL), lambda l, j: (l, 0, j))],
        out_specs=pl.BlockSpec((1, nrow, D_MODEL), lambda l, j: (l, 0, j)),
        out_shape=jax.ShapeDtypeStruct((depth, nrow, ncol * D_MODEL), F32),
        compiler_params=_cparams(("arbitrary", "arbitrary")),
        name="ada",
    )(c_all, w_ada, b_ada.reshape(depth, 1, -1))


def _inproj_kernel(*refs, tm, ng, npc, n_hp):
    if n_hp:
        x_ref, mod_ref, nw_ref, w_ref, whp_ref, proj_ref, gates_ref, h_scr = refs
    else:
        x_ref, mod_ref, nw_ref, w_ref, proj_ref, h_scr = refs
    i = pl.program_id(0)
    nch = tm // CHUNK
    for c in range(nch):
        seq = _seq_row(i, nch, c, npc)
        sh = mod_ref[pl.ds(seq, 1), 0:D_MODEL]
        sc = mod_ref[pl.ds(seq, 1), D_MODEL:2 * D_MODEL]
        xc = x_ref[c * CHUNK:(c + 1) * CHUNK, :]
        h_scr[c * CHUNK:(c + 1) * CHUNK, :] = _rms(xc) * nw_ref[...] * (1.0 + sc) + sh
    h = h_scr[...]
    hb = h.astype(BF16)
    for g in range(0, ng - n_hp, 4):
        res = _dot(hb, w_ref[:, g * LANES:(g + 4) * LANES])
        for jj in range(4):
            proj_ref[n_hp + g + jj] = res[:, jj * LANES:(jj + 1) * LANES]
    if n_hp:
        res = _dot3(h, whp_ref[...])
        for jj in range(n_hp):
            proj_ref[jj] = res[:, jj * LANES:(jj + 1) * LANES]
        gates_ref[...] = res[:, n_hp * LANES:]


def _inproj_call(x, mod_l, nw, w_main, w_gate, npc):
    ttot = x.shape[0]
    tm = INPROJ_TILE
    has_gates = w_gate is not None
    n_hp = w_gate.shape[1] // LANES - 1 if has_gates else 0
    ng = w_main.shape[1] // LANES + n_hp
    in_specs = [pl.BlockSpec((tm, D_MODEL), lambda i: (i, 0)),
                pl.BlockSpec(mod_l.shape, lambda i: (0, 0)),
                pl.BlockSpec((1, D_MODEL), lambda i: (0, 0)),
                pl.BlockSpec(w_main.shape, lambda i: (0, 0))]
    out_specs = [pl.BlockSpec((ng, tm, LANES), lambda i: (0, i, 0))]
    out_shape = [jax.ShapeDtypeStruct((ng, ttot, LANES), F32)]
    args = [x, mod_l, nw, w_main]
    if has_gates:
        in_specs.append(pl.BlockSpec(w_gate.shape, lambda i: (0, 0)))
        out_specs.append(pl.BlockSpec((tm, LANES), lambda i: (i, 0)))
        out_shape.append(jax.ShapeDtypeStruct((ttot, LANES), F32))
        args.append(w_gate)
    return pl.pallas_call(
        functools.partial(_inproj_kernel, tm=tm, ng=ng, npc=npc, n_hp=n_hp),
        grid=(ttot // tm,),
        in_specs=in_specs, out_specs=out_specs, out_shape=out_shape,
        scratch_shapes=[pltpu.VMEM((tm, D_MODEL), F32)],
        compiler_params=_cparams(("arbitrary",)),
        name="inproj",
    )(*args)


_G_XM, _G_VA, _G_OA, _G_QB, _G_KB, _G_VB, _G_GB = 0, 4, 8, 12, 16, 20, 24


def _even_scan_kernel(proj_ref, gates_ref, cos_ref, sin_ref,
                      c0_ref, n0_ref, m0_ref, conv0_ref, s0_ref,
                      cw_ref, cb_ref, wq_ref, wk_ref, gbias_ref, skip_ref, nwa_ref, nwb_ref,
                      u_ref, co_ref, no_ref, mo_ref, convo_ref, so_ref,
                      c_scr, n_scr, m_scr, conv_scr, s_scr,
                      xbuf, xc_scr, q_scr, k_scr, qr_scr, kr_scr,
                      gl_scr, bc_scr, rowi_scr, rowb_scr, dec_scr, *, ts):
    H = N_HEADS_EVEN
    j = pl.program_id(1)
    nj = pl.num_programs(1)
    nc = ts // CHUNK

    @pl.when(j == 0)
    def _():
        c_scr[...] = c0_ref[0]
        n_scr[...] = n0_ref[0]
        m_scr[...] = m0_ref[0]
        conv_scr[...] = conv0_ref[0]
        s_scr[...] = s0_ref[0]

    for g in range(H):
        lo, hi = g * LANES, (g + 1) * LANES
        x_g = proj_ref[_G_XM + g]
        xbuf[0:SUBLANES, :] = conv_scr[:, lo:hi]
        xbuf[SUBLANES:SUBLANES + ts, :] = x_g
        acc = cb_ref[:, lo:hi] + cw_ref[CONV_W - 1:CONV_W, lo:hi] * x_g
        for t in range(CONV_W - 1):
            off = SUBLANES - (CONV_W - 1) + t
            acc = acc + cw_ref[t:t + 1, lo:hi] * xbuf[off:off + ts, :]
        conv_scr[:, lo:hi] = xbuf[ts:ts + SUBLANES, :]
        xc = acc * _sigmoid(acc)
        xc_scr[g] = xc
        q_scr[g] = _dot3(xc, wq_ref[g])
        k_scr[g] = _dot3(xc, wk_ref[g]) * (LANES ** -0.5)

    cosv = cos_ref[...]
    sinv = sin_ref[...]
    for g in range(H):
        qb = proj_ref[_G_QB + g]
        kb = proj_ref[_G_KB + g]
        qr_scr[g] = qb * cosv + pltpu.roll(qb, LANES // 2, 1) * sinv
        kr_scr[g] = (kb * cosv + pltpu.roll(kb, LANES // 2, 1) * sinv) * (LANES ** -0.5)

    gpre = gates_ref[...] + gbias_ref[...]
    lane = lax.broadcasted_iota(jnp.int32, (ts, LANES), 1)
    gl = jnp.where(lane < H, gpre, _log_sigmoid(gpre))
    gl_scr[...] = gl
    bc = _dot_sel_lhs(_chunk_tri(ts), gl)
    bc_scr[...] = bc
    eye8 = jnp.where(lax.broadcasted_iota(jnp.int32, (SUBLANES, LANES), 0)
                     == lax.broadcasted_iota(jnp.int32, (SUBLANES, LANES), 1), 1.0, 0.0).astype(BF16)
    for c in range(nc):
        rowi_scr[c] = _dot_sel_nt(eye8, gl[c * CHUNK:(c + 1) * CHUNK, :])
        rowb_scr[c] = _dot_sel_nt(eye8, bc[c * CHUNK:(c + 1) * CHUNK, :])

    ti = lax.broadcasted_iota(jnp.int32, (CHUNK, CHUNK), 0)
    si = lax.broadcasted_iota(jnp.int32, (CHUNK, CHUNK), 1)
    tril = ti >= si
    tcol = lax.broadcasted_iota(jnp.int32, (CHUNK, 1), 0).astype(F32)
    log_gamma = [math.log1p(-2.0 ** (-5 - h)) for h in range(H)]
    for h in range(H):
        dec_scr[h] = jnp.where(tril, jnp.exp((ti - si).astype(F32) * log_gamma[h]), 0.0)

    def chunk_body(c, carry):
        r0 = pl.multiple_of(c * CHUNK, CHUNK)
        rows = pl.ds(r0, CHUNK)
        rowi = rowi_scr[c]
        rowb = rowb_scr[c]
        glc = gl_scr[rows, :]
        bcc = bc_scr[rows, :]
        m_all = m_scr[...]
        c_old = [c_scr[h] for h in range(H)]
        n_old = [n_scr[h] for h in range(H)]
        s_old = [s_scr[h] for h in range(H)]
        c_new, n_new, s_new = [], [], []
        m_next = m_all
        lane_row = lax.broadcasted_iota(jnp.int32, (1, LANES), 1)
        for h in range(H):
            lo, hi = h * LANES, (h + 1) * LANES
            q = q_scr[h, rows, :]
            k = k_scr[h, rows, :]
            vb = proj_ref[_G_VA + h, rows, :].astype(BF16)
            b_col = bcc[:, H + h:H + h + 1]
            i_col = glc[:, h:h + 1]
            b_row = rowb[H + h:H + h + 1, :]
            i_row = rowi[h:h + 1, :]
            dmat = jnp.where(tril, b_col - b_row + i_row, -jnp.inf)
            a = jnp.max(dmat, axis=-1, keepdims=True)
            m_prev = m_all[:, h:h + 1]
            inter = b_col + m_prev
            m_t = jnp.maximum(inter, a)
            w_intra = jnp.exp(dmat - m_t)
            w_inter = jnp.exp(inter - m_t)
            qb = q.astype(BF16)
            amat = _dot_nt(qb, k.astype(BF16)) * w_intra
            cst = c_old[h]
            n_row = n_old[h]
            num =_dot(amat.astype(BF16), vb) + w_inter * _dot(qb, cst.astype(BF16))
            den = (jnp.sum(amat, axis=-1, keepdims=True)
                   + w_inter * jnp.sum(q * n_row, axis=-1, keepdims=True))
            hout = num / jnp.maximum(jnp.abs(den), jnp.exp(-m_t))
            m_new = m_t[CHUNK - 1:CHUNK, :]
            b_last = b_col[CHUNK - 1:CHUNK, :]
            w_s = jnp.exp(b_last - b_col + i_col - m_new)
            decay = jnp.exp(b_last + m_prev - m_new)
            kw = k * w_s
            c_new.append(decay * cst + _dot3_tn(kw, proj_ref[_G_VA + h, rows, :]))
            n_new.append(decay * n_row + jnp.sum(kw, axis=0, keepdims=True))
            m_next = jnp.where(lane_row == h, m_new, m_next)
            z =_sigmoid(proj_ref[_G_OA + h, rows, :]) * hout
            u_ref[rows, lo:hi] = (_rms(z) * nwa_ref[:, lo:hi]
                                  + skip_ref[:, lo:hi] * xc_scr[h, rows, :])
        for h in range(H):
            lo, hi = h * LANES, (h + 1) * LANES
            lg = log_gamma[h]
            qrb = qr_scr[h, rows, :].astype(BF16)
            kr = kr_scr[h, rows, :]
            vb = proj_ref[_G_VB + h, rows, :].astype(BF16)
            amat = _dot_nt(qrb, kr.astype(BF16)) * dec_scr[h]
            sst = s_old[h]
            o = (_dot(amat.astype(BF16), vb)
                 + jnp.exp((tcol + 1.0) * lg) * _dot(qrb, sst.astype(BF16)))
            kws = kr * jnp.exp((CHUNK - 1.0 - tcol) * lg)
            s_new.append(math.exp(CHUNK * lg) * sst + _dot_tn(kws.astype(BF16), vb))
            gate = proj_ref[_G_GB + h, rows, :]
            u_ref[rows, D_MODEL // 2 + lo:D_MODEL // 2 + hi] = (
                _rms(o) * nwb_ref[:, lo:hi] * (gate * _sigmoid(gate)))
        for h in range(H):
            c_scr[h] = c_new[h]
            n_scr[h] = n_new[h]
            s_scr[h] = s_new[h]
        m_scr[...] = m_next
        return carry

    lax.fori_loop(0, nc, chunk_body, 0)

    @pl.when(j == nj - 1)
    def _():
        co_ref[0] = c_scr[...]
        no_ref[0] = n_scr[...]
        mo_ref[0] = m_scr[...]
        convo_ref[0] = conv_scr[...]
        so_ref[0] = s_scr[...]


def _even_scan_call(proj, gates, cos_t, sin_t, states, weights, *, ts, n_seq, steps, row_off):
    H = N_HEADS_EVEN
    c0, n0, m0, conv0, s0 = states
    ng = proj.shape[0]
    rows_idx = lambda b, j: (row_off + b * steps + j, 0)
    state_specs = [pl.BlockSpec((1, H, LANES, LANES), lambda b, j: (b, 0, 0, 0)),
                   pl.BlockSpec((1, H, 1, LANES), lambda b, j: (b, 0, 0, 0)),
                   pl.BlockSpec((1, 1, LANES), lambda b, j: (b, 0, 0)),
                   pl.BlockSpec((1, SUBLANES, H * LANES), lambda b, j: (b, 0, 0)),
                   pl.BlockSpec((1, H, LANES, LANES), lambda b, j: (b, 0, 0, 0))]
    in_specs = [pl.BlockSpec((ng, ts, LANES), lambda b, j: (0, row_off + b * steps + j, 0)),
                pl.BlockSpec((ts, LANES), rows_idx),
                pl.BlockSpec((ts, LANES), rows_idx),
                pl.BlockSpec((ts, LANES), rows_idx)] + state_specs
    for w in weights:
        in_specs.append(pl.BlockSpec(w.shape, functools.partial(lambda nd, b, j: (0,) * nd, w.ndim)))
    out_specs = [pl.BlockSpec((ts, D_MODEL), lambda b, j: (b * steps + j, 0))] + state_specs
    out_shape = [jax.ShapeDtypeStruct((n_seq * steps * ts, D_MODEL), F32),
                 jax.ShapeDtypeStruct(c0.shape, F32), jax.ShapeDtypeStruct(n0.shape, F32),
                 jax.ShapeDtypeStruct(m0.shape, F32), jax.ShapeDtypeStruct(conv0.shape, F32),
                 jax.ShapeDtypeStruct(s0.shape, F32)]
    nc = ts // CHUNK
    scratch = [pltpu.VMEM((H, LANES, LANES), F32), pltpu.VMEM((H, 1, LANES), F32),
               pltpu.VMEM((1, LANES), F32), pltpu.VMEM((SUBLANES, H * LANES), F32),
               pltpu.VMEM((H, LANES, LANES), F32),
               pltpu.VMEM((ts + 2 * SUBLANES, LANES), F32),
               pltpu.VMEM((H, ts, LANES), F32), pltpu.VMEM((H, ts, LANES), F32),
               pltpu.VMEM((H, ts, LANES), F32), pltpu.VMEM((H, ts, LANES), F32),
               pltpu.VMEM((H, ts, LANES), F32),
               pltpu.VMEM((ts, LANES), F32), pltpu.VMEM((ts, LANES), F32),
               pltpu.VMEM((nc, SUBLANES, CHUNK), F32), pltpu.VMEM((nc, SUBLANES, CHUNK), F32),
               pltpu.VMEM((H, CHUNK, CHUNK), F32)]
    return pl.pallas_call(
        functools.partial(_even_scan_kernel, ts=ts),
        grid=(n_seq, steps),
        in_specs=in_specs, out_specs=out_specs, out_shape=out_shape,
        scratch_shapes=scratch,
        compiler_params=_cparams(("arbitrary", "arbitrary")),
        name="even_scan",
    )(proj, gates, cos_t, sin_t, c0, n0, m0, conv0, s0, *weights)


_G_Q, _G_F, _G_I, _G_G = 0, 8, 16, 24


def _hgrn_intra(q, k, bcum, ti, si, tcol_i):
    amat = jnp.zeros((CHUNK, CHUNK), F32)
    for b in (32, 16, 8):
        nb2 = CHUNK // (2 * b)
        parts = [jnp.broadcast_to(bcum[m * 2 * b + b - 1:m * 2 * b + b, :], (2 * b, LANES))
                 for m in range(nb2)]
        ref = parts[0] if nb2 == 1 else jnp.concatenate(parts, axis=0)
        upper = ((tcol_i // b) % 2) == 1
        ql = jnp.where(upper, q * jnp.exp(bcum - ref), 0.0)
        kl = jnp.where(upper, 0.0, k * jnp.exp(ref - bcum))
        al = _dot_nt(ql.astype(BF16), kl.astype(BF16))
        amat = amat + jnp.where((ti // (2 * b)) == (si // (2 * b)), al, 0.0)
    nblk = CHUNK // SUBLANES
    b3 = bcum.reshape(nblk, SUBLANES, LANES)
    k3 = k.reshape(nblk, SUBLANES, LANES)
    for jj in range(SUBLANES):
        bj = jnp.broadcast_to(b3[:, jj:jj + 1, :], (nblk, SUBLANES, LANES)).reshape(CHUNK, LANES)
        kj = jnp.broadcast_to(k3[:, jj:jj + 1, :], (nblk, SUBLANES, LANES)).reshape(CHUNK, LANES)
        col = jnp.sum(q * kj * jnp.exp(bcum - bj), axis=-1, keepdims=True)
        sel = (si == (ti // SUBLANES) * SUBLANES + jj) & ((ti % SUBLANES) >= jj)
        amat = jnp.where(sel, col, amat)
    return amat


def _odd_scan_kernel(proj_ref, s0_ref, lb_ref, nw_ref, u_ref, so_ref,
                     st_scr, k_scr, bc_scr, *, ts):
    H = N_HEADS_ODD
    j = pl.program_id(1)
    nj = pl.num_programs(1)
    nc = ts // CHUNK

    @pl.when(j == 0)
    def _():
        for h in range(H):
            st_scr[h] = s0_ref[0, h].T

    for h in range(H):
        lo, hi = h * LANES, (h + 1) * LANES
        lbv = lb_ref[:, lo:hi]
        fpre = proj_ref[_G_F + h]
        logf = _logaddexp(jnp.log(lbv), jnp.log1p(-lbv) + _log_sigmoid(fpre))
        k_scr[h] = (1.0 - lbv) * _sigmoid(-fpre)
        bc_scr[h] = _chunk_cumsum(logf)

    ti = lax.broadcasted_iota(jnp.int32, (CHUNK, CHUNK), 0)
    si = lax.broadcasted_iota(jnp.int32, (CHUNK, CHUNK), 1)
    tcol_i = lax.broadcasted_iota(jnp.int32, (CHUNK, 1), 0)

    def chunk_body(c, carry):
        r0 = pl.multiple_of(c * CHUNK, CHUNK)
        rows = pl.ds(r0, CHUNK)
        st_old = [st_scr[h] for h in range(H)]
        st_new = []
        for h in range(H):
            lo, hi = h * LANES, (h + 1) * LANES
            q = proj_ref[_G_Q + h, rows, :]
            k = k_scr[h, rows, :]
            vb = proj_ref[_G_I + h, rows, :].astype(BF16)
            bcum = bc_scr[h, rows, :]
            amat = _hgrn_intra(q, k, bcum, ti, si, tcol_i)
            st = st_old[h]
            o = (_dot(amat.astype(BF16), vb)
                 + _dot_nt((q * jnp.exp(bcum)).astype(BF16), st.astype(BF16)))
            last = bcum[CHUNK - 1:CHUNK, :]
            kd = k * jnp.exp(last - bcum)
            st_new.append(st * jnp.exp(last) + _dot_tn(vb, kd.astype(BF16)))
            u_ref[rows, lo:hi] = (_rms(o) * nw_ref[:, lo:hi]
                                  * _sigmoid(proj_ref[_G_G + h, rows, :]))
        for h in range(H):
            st_scr[h] = st_new[h]
        return carry

    lax.fori_loop(0, nc, chunk_body, 0)

    @pl.when(j == nj - 1)
    def _():
        for h in range(H):
            so_ref[0, h] = st_scr[h].T


def _odd_scan_call(proj, s0, lb, nw, *, ts, n_seq, steps, row_off):
    H = N_HEADS_ODD
    ng = proj.shape[0]
    st_spec = pl.BlockSpec((1, H, LANES, LANES), lambda b, j: (b, 0, 0, 0))
    return pl.pallas_call(
        functools.partial(_odd_scan_kernel, ts=ts),
        grid=(n_seq, steps),
        in_specs=[pl.BlockSpec((ng, ts, LANES), lambda b, j: (0, row_off + b * steps + j, 0)),
                  st_spec,
                  pl.BlockSpec((1, D_MODEL), lambda b, j: (0, 0)),
                  pl.BlockSpec((1, D_MODEL), lambda b, j: (0, 0))],
        out_specs=[pl.BlockSpec((ts, D_MODEL), lambda b, j: (b * steps + j, 0)), st_spec],
        out_shape=[jax.ShapeDtypeStruct((n_seq * steps * ts, D_MODEL), F32),
                   jax.ShapeDtypeStruct(s0.shape, F32)],
        scratch_shapes=[pltpu.VMEM((H, LANES, LANES), F32),
                        pltpu.VMEM((H, ts, LANES), F32),
                        pltpu.VMEM((H, ts, LANES), F32)],
        compiler_params=_cparams(("arbitrary", "arbitrary")),
        name="odd_scan",
    )(proj, s0, lb, nw)


def _post_kernel(x_ref, up_ref, us_ref, mod_ref, wout_ref, nw_ref, wr_ref, br_ref,
                 xo_ref, hf_ref, gate_ref, lpos_ref, cnt_ref,
                 *, tm, npc, n_prompt_tiles):
    i = pl.program_id(0)
    nch = tm // CHUNK

    u = jnp.where(i < n_prompt_tiles, up_ref[...], us_ref[...])
    y = _dot(u.astype(BF16), wout_ref[...])
    for c in range(nch):
        seq = _seq_row(i, nch, c, npc)
        gm = mod_ref[pl.ds(seq, 1), 2 * D_MODEL:3 * D_MODEL]
        shf = mod_ref[pl.ds(seq, 1), 3 * D_MODEL:4 * D_MODEL]
        scf = mod_ref[pl.ds(seq, 1), 4 * D_MODEL:5 * D_MODEL]
        rs = slice(c * CHUNK, (c + 1) * CHUNK)
        xn = x_ref[rs, :] + gm * y[rs, :]
        xo_ref[rs, :] = xn
        hf_ref[rs, :] = _rms(xn) * nw_ref[...] * (1.0 + scf) + shf

    logits = _dot3(hf_ref[...], wr_ref[...]) + br_ref[...]
    lane_i = lax.broadcasted_iota(jnp.int32, (tm, LANES), 1)
    lane_f = lane_i.astype(F32)
    vals, idxs = [], []
    cur = logits
    for _ in range(TOP_K):
        m = jnp.max(cur, axis=-1, keepdims=True)
        idx = jnp.min(jnp.where(cur == m, lane_f, float(LANES)), axis=-1, keepdims=True)
        vals.append(m)
        idxs.append(idx)
        cur = jnp.where(lane_f == idx, -jnp.inf, cur)
    exps = [jnp.exp(v - vals[0]) for v in vals]
    denom = exps[0] + exps[1] + exps[2] + exps[3]
    onehot = jnp.zeros((tm, LANES), F32)
    for idx in idxs:
        onehot = onehot + jnp.where(lane_f == idx, 1.0, 0.0)
    r = lax.broadcasted_iota(jnp.int32, (tm, tm), 0)
    cidx = lax.broadcasted_iota(jnp.int32, (tm, tm), 1)
    strict = jnp.where((cidx < r) & (cidx // DISPATCH_TILE == r // DISPATCH_TILE), 1.0, 0.0).astype(BF16)
    before = _dot(strict, onehot.astype(BF16))
    er = lax.broadcasted_iota(jnp.int32, (LANES, LANES), 0)
    ec = lax.broadcasted_iota(jnp.int32, (LANES, LANES), 1)
    lower_experts = jnp.where(er < ec, 1.0, 0.0).astype(BF16)
    pos_parts = []
    for s in range(tm // DISPATCH_TILE):
        rs = slice(s * DISPATCH_TILE, (s + 1) * DISPATCH_TILE)
        cnt = jnp.sum(onehot[rs, :], axis=0, keepdims=True)
        cnt_ref[s] = cnt
        n8 = jnp.floor((cnt + (SUBLANES - 1.0)) * (1.0 / SUBLANES)) * float(SUBLANES)
        run_start = _dot(jnp.broadcast_to(n8, (SUBLANES, LANES)).astype(BF16), lower_experts)[0:1, :]
        pos_parts.append(before[rs, :] + run_start)
    posmat = jnp.concatenate(pos_parts, axis=0)
    gate_o = jnp.zeros((tm, LANES), F32)
    lpos_o = jnp.zeros((tm, LANES), F32)
    for kk in range(TOP_K):
        lp = jnp.sum(jnp.where(lane_f == idxs[kk], posmat, 0.0), axis=-1, keepdims=True)
        gate_o = jnp.where(lane_i == kk, exps[kk] / denom, gate_o)
        lpos_o = jnp.where(lane_i == kk, lp, lpos_o)
    gate_ref[...] = gate_o
    lpos_ref[...] = lpos_o


def _post_call(x, u_p, u_s, mod_l, w_out, nw, w_r, b_r, npc):
    ttot = x.shape[0]
    tm = TOKEN_TILE
    npt = u_p.shape[0] // tm
    sub = tm // DISPATCH_TILE
    tile = lambda i: (i, 0)
    const = lambda i: (0, 0)
    return pl.pallas_call(
        functools.partial(_post_kernel, tm=tm, npc=npc, n_prompt_tiles=npt),
        grid=(ttot // tm,),
        in_specs=[pl.BlockSpec((tm, D_MODEL), tile),
                  pl.BlockSpec((tm, D_MODEL), lambda i: (jnp.minimum(i, npt - 1), 0)),
                  pl.BlockSpec((tm, D_MODEL), lambda i: (jnp.maximum(i - npt, 0), 0)),
                  pl.BlockSpec(mod_l.shape, const),
                  pl.BlockSpec(w_out.shape, const),
                  pl.BlockSpec((1, D_MODEL), const),
                  pl.BlockSpec(w_r.shape, const),
                  pl.BlockSpec((1, LANES), const)],
        out_specs=[pl.BlockSpec((tm, D_MODEL), tile), pl.BlockSpec((tm, D_MODEL), tile),
                   pl.BlockSpec((tm, LANES), tile), pl.BlockSpec((tm, LANES), tile),
                   pl.BlockSpec((sub, 1, LANES), lambda i: (i, 0, 0))],
        out_shape=[jax.ShapeDtypeStruct((ttot, D_MODEL), F32), jax.ShapeDtypeStruct((ttot, D_MODEL), F32),
                   jax.ShapeDtypeStruct((ttot, LANES), F32), jax.ShapeDtypeStruct((ttot, LANES), F32),
                   jax.ShapeDtypeStruct((ttot // DISPATCH_TILE, 1, LANES), F32)],
        compiler_params=_cparams(("arbitrary",)),
        name="post",
    )(x, u_p, u_s, mod_l, w_out, nw, w_r, b_r)


def _plan_kernel(cnt_ref, n8_ref, loff_ref, gbase_ref, blk_ref, ends_ref, *, nbp, ntp):
    cnt = cnt_ref[...]
    n8 = jnp.floor((cnt + (SUBLANES - 1.0)) * (1.0 / SUBLANES)) * float(SUBLANES)
    r = lax.broadcasted_iota(jnp.int32, (LANES, LANES), 0)
    c = lax.broadcasted_iota(jnp.int32, (LANES, LANES), 1)
    loff = _dot_sel_rhs(n8, jnp.where(r < c, 1.0, 0.0).astype(BF16))
    gtot = jnp.broadcast_to(jnp.sum(n8, axis=0, keepdims=True), (SUBLANES, LANES))
    nblk = jnp.floor((gtot + (EXPERT_ROWS - 1.0)) * (1.0 / EXPERT_ROWS))
    ends = _dot_sel_rhs(nblk, jnp.where(r <= c, 1.0, 0.0).astype(BF16))
    start_row = (ends[0:1, :] - nblk[0:1, :]) * float(EXPERT_ROWS)
    tr = lax.broadcasted_iota(jnp.int32, (ntp, ntp), 0)
    tc = lax.broadcasted_iota(jnp.int32, (ntp, ntp), 1)
    gbase = start_row + _dot_sel_lhs(jnp.where(tc < tr, 1.0, 0.0).astype(BF16), n8)
    n8_ref[...] = n8.astype(jnp.int32)
    loff_ref[...] = loff.astype(jnp.int32)
    gbase_ref[...] = gbase.astype(jnp.int32)
    bi = lax.broadcasted_iota(jnp.int32, (nbp, LANES), 0).astype(F32)
    li = lax.broadcasted_iota(jnp.int32, (nbp, LANES), 1)
    done = jnp.where((li < N_EXPERTS) & (ends[0:1, :] <= bi), 1.0, 0.0)
    be = jnp.minimum(jnp.sum(done, axis=-1, keepdims=True), N_EXPERTS - 1.0)
    blk_ref[...] = jnp.broadcast_to(be, (nbp, LANES)).astype(jnp.int32)
    ends_ref[...] = ends.astype(jnp.int32)


def _plan_call(cnt_tiles, nbp):
    ntp = cnt_tiles.shape[0]
    const = lambda i: (0, 0)
    tbl = jax.ShapeDtypeStruct((ntp, LANES), jnp.int32)
    return pl.pallas_call(
        functools.partial(_plan_kernel, nbp=nbp, ntp=ntp),
        grid=(1,),
        in_specs=[pl.BlockSpec((ntp, LANES), const)],
        out_specs=[pl.BlockSpec((ntp, LANES), const), pl.BlockSpec((ntp, LANES), const),
                   pl.BlockSpec((ntp, LANES), const),
                   pl.BlockSpec((nbp, LANES), const), pl.BlockSpec((SUBLANES, LANES), const)],
        out_shape=[tbl, tbl, tbl,
                   jax.ShapeDtypeStruct((nbp, LANES), jnp.int32),
                   jax.ShapeDtypeStruct((SUBLANES, LANES), jnp.int32)],
        compiler_params=_cparams(("arbitrary",)),
        name="plan",
    )(cnt_tiles)


_GROUP_BITS = tuple(range(3, DISPATCH_TILE.bit_length()))


_TOTAL_BITS = tuple(range(3, SORT_ROWS.bit_length()))


def _group_copies(n8_ref, loff_ref, gbase_ref, tile, make_copy, wait):
    if wait:
        total = lax.fori_loop(0, N_EXPERTS, lambda e, acc: acc + n8_ref[tile * N_EXPERTS + e], 0)
        for bit in _TOTAL_BITS:
            size = 1 << bit

            @pl.when((total & size) != 0)
            def _():
                make_copy(0, 0, size).wait()
        return

    def per_expert(e, carry):
        idx = tile * N_EXPERTS + e
        n = n8_ref[idx]
        off = loff_ref[idx]
        base = gbase_ref[idx]
        for bit in _GROUP_BITS:
            size = 1 << bit

            @pl.when((n & size) != 0)
            def _():
                done = n & ~(2 * size - 1)
                make_copy(pl.multiple_of(off + done, SUBLANES), pl.multiple_of(base + done, SUBLANES), size).start()
        return carry

    lax.fori_loop(0, N_EXPERTS, per_expert, 0)


def _dispatch_kernel(n8_ref, loff_ref, gbase_ref, hf_ref, lpos_ref, gate_ref, xs_in_ref, xs_ref,
                     sbuf, sem):
    del xs_in_ref
    i = pl.program_id(0)
    eye8 = jnp.where(lax.broadcasted_iota(jnp.int32, (SUBLANES, LANES), 0)
                     == lax.broadcasted_iota(jnp.int32, (SUBLANES, LANES), 1), 1.0, 0.0).astype(BF16)
    lpos_t = _dot_sel_nt(eye8, lpos_ref[...])
    gate_t = _dot_sel_nt(eye8, gate_ref[...])
    row = lax.broadcasted_iota(jnp.int32, (SORT_ROWS, DISPATCH_TILE), 0).astype(F32)
    perm = jnp.zeros((SORT_ROWS, DISPATCH_TILE), F32)
    wgate = jnp.zeros((SORT_ROWS, DISPATCH_TILE), F32)
    for kk in range(TOP_K):
        hit = row == lpos_t[kk:kk + 1, :]
        perm = jnp.where(hit, 1.0, perm)
        wgate = jnp.where(hit, gate_t[kk:kk + 1, :], wgate)
    slot = i % 2
    sbuf[slot, :, 0:D_MODEL] = _dot(perm.astype(BF16), hf_ref[...].astype(BF16))
    sbuf[slot, :, D_MODEL:XS_WIDTH] = _dot_sel_rhs(wgate, jnp.ones((DISPATCH_TILE, LANES), BF16))

    def copies(tile, buf_slot, wait):
        def make_copy(src_row, dst_row, size):
            return pltpu.make_async_copy(sbuf.at[buf_slot, pl.ds(src_row, size), :],
                                         xs_ref.at[pl.ds(dst_row, size), :], sem.at[buf_slot])
        _group_copies(n8_ref, loff_ref, gbase_ref, tile, make_copy, wait=wait)

    copies(i, slot, wait=False)

    @pl.when(i > 0)
    def _():
        copies(i - 1, 1 - slot, wait=True)

    @pl.when(i == pl.num_programs(0) - 1)
    def _():
        copies(i, slot, wait=True)


def _dispatch_call(tables, hf, lpos, gate, xs):
    ttot = hf.shape[0]
    tm = DISPATCH_TILE
    tile = lambda i, *_: (i, 0)
    grid_spec = pltpu.PrefetchScalarGridSpec(
        num_scalar_prefetch=3, grid=(ttot // tm,),
        in_specs=[pl.BlockSpec((tm, D_MODEL), tile), pl.BlockSpec((tm, LANES), tile),
                  pl.BlockSpec((tm, LANES), tile), pl.BlockSpec(memory_space=pl.ANY)],
        out_specs=pl.BlockSpec(memory_space=pl.ANY),
        scratch_shapes=[pltpu.VMEM((2, SORT_ROWS, XS_WIDTH), F32), pltpu.SemaphoreType.DMA((2,))])
    return pl.pallas_call(
        _dispatch_kernel,
        grid_spec=grid_spec,
        out_shape=jax.ShapeDtypeStruct(xs.shape, xs.dtype),
        input_output_aliases={6: 0},
        compiler_params=pltpu.CompilerParams(dimension_semantics=("arbitrary",),
                                             vmem_limit_bytes=VMEM_LIMIT, has_side_effects=True),
        name="dispatch",
    )(*tables, hf, lpos, gate, xs)


def _expert_kernel(be_ref, nu_ref, xs_ref, wgu_ref, bgu_ref, wdn_ref, bdn_ref, y_ref,
                   wgu_bf, wdn_bf):
    b = pl.program_id(0)

    @pl.when(b < nu_ref[0])
    def _():
        prev = be_ref[jnp.maximum(b - 1, 0)]

        @pl.when((b == 0) | (be_ref[b] != prev))
        def _():
            wgu_bf[...] = wgu_ref[0, 0].astype(BF16)
            wdn_bf[...] = wdn_ref[0, 0].astype(BF16)

        gu = _dot(xs_ref[:, 0:D_MODEL].astype(BF16), wgu_bf[...]) + bgu_ref[0, 0]
        g = jnp.minimum(gu[:, :D_MODEL], SWIGLU_LIMIT)
        u = jnp.clip(gu[:, D_MODEL:], -SWIGLU_LIMIT, SWIGLU_LIMIT)
        act = (u + 1.0) * (g * _sigmoid(SWIGLU_ALPHA * g))
        gate = xs_ref[:, D_MODEL:D_MODEL + 1]
        y_ref[...] = (_dot(act.astype(BF16), wdn_bf[...]) + bdn_ref[0, 0]) * gate

    @pl.when(b >= nu_ref[0])
    def _():
        y_ref[...] = jnp.zeros_like(y_ref)


def _expert_call(blk_expert, n_used, xs, w_gu, b_gu, w_dn, b_dn, layer):
    nb = xs.shape[0] // EXPERT_ROWS
    d_ff2 = w_gu.shape[3]
    blk = lambda b, be, nu: (jnp.minimum(b, nu[0] - 1), 0)
    blk_out = lambda b, be, nu: (b, 0)
    exp4 = lambda b, be, nu: (layer, be[jnp.minimum(b, nu[0] - 1)], 0, 0)
    grid_spec = pltpu.PrefetchScalarGridSpec(
        num_scalar_prefetch=2, grid=(nb,),
        in_specs=[pl.BlockSpec((EXPERT_ROWS, XS_WIDTH), blk),
                  pl.BlockSpec((1, 1, D_MODEL, d_ff2), exp4),
                  pl.BlockSpec((1, 1, 1, d_ff2), exp4),
                  pl.BlockSpec((1, 1, D_MODEL, D_MODEL), exp4),
                  pl.BlockSpec((1, 1, 1, D_MODEL), exp4)],
        out_specs=pl.BlockSpec((EXPERT_ROWS, D_MODEL), blk_out),
        scratch_shapes=[pltpu.VMEM((D_MODEL, d_ff2), BF16), pltpu.VMEM((D_MODEL, D_MODEL), BF16)])
    return pl.pallas_call(
        _expert_kernel,
        grid_spec=grid_spec,
        out_shape=jax.ShapeDtypeStruct((xs.shape[0], D_MODEL), F32),
        compiler_params=_cparams(("arbitrary",)),
        name="experts",
    )(blk_expert, n_used, xs, w_gu, b_gu, w_dn, b_dn)


def _combine_kernel(*refs, tm, npc, final):
    if final:
        (n8_ref, loff_ref, gbase_ref, x_ref, lpos_ref, mod_ref, fnw_ref, y_hbm,
         yp_ref, ys_ref, ybuf, sem) = refs
    else:
        n8_ref, loff_ref, gbase_ref, x_ref, lpos_ref, mod_ref, y_hbm, xo_ref, ybuf, sem = refs
    i = pl.program_id(0)
    nch = tm // CHUNK

    slot = i % 2

    def copies(tile, buf_slot, wait):
        def make_copy(buf_row, src_row, size):
            return pltpu.make_async_copy(y_hbm.at[pl.ds(src_row, size), :],
                                         ybuf.at[buf_slot, pl.ds(buf_row, size), :], sem.at[buf_slot])
        _group_copies(n8_ref, loff_ref, gbase_ref, tile, make_copy, wait=wait)

    @pl.when(i == 0)
    def _():
        ybuf[...] = jnp.zeros_like(ybuf)
        copies(i, slot, wait=False)

    @pl.when(i + 1 < pl.num_programs(0))
    def _():
        copies(i + 1, 1 - slot, wait=False)

    copies(i, slot, wait=True)

    lpos = lpos_ref[...]
    col = lax.broadcasted_iota(jnp.int32, (tm, SORT_ROWS), 1).astype(F32)
    unperm = jnp.zeros((tm, SORT_ROWS), F32)
    for kk in range(TOP_K):
        unperm = jnp.where(col == lpos[:, kk:kk + 1], 1.0, unperm)
    acc = _dot_sel_lhs2(unperm.astype(BF16), ybuf[slot])
    for c in range(nch):
        seq = _seq_row(i, nch, c, npc)
        gf = mod_ref[pl.ds(seq, 1), 5 * D_MODEL:6 * D_MODEL]
        rs = slice(c * CHUNK, (c + 1) * CHUNK)
        xn = x_ref[rs, :] + gf * acc[rs, :]
        if final:
            yn = _rms(xn) * fnw_ref[...]
            is_prompt = i < (npc * CHUNK) // tm

            @pl.when(is_prompt)
            def _():
                yp_ref[rs, :] = yn

            @pl.when(jnp.logical_not(is_prompt))
            def _():
                ys_ref[rs, :] = yn
        else:
            xo_ref[rs, :] = xn


def _combine_call(tables, x, lpos, mod_l, y_sorted, npc, final_w):
    ttot = x.shape[0]
    tm = DISPATCH_TILE
    final = final_w is not None
    npt = (npc * CHUNK) // tm
    tile = lambda i, *_: (i, 0)
    const = lambda i, *_: (0, 0)
    in_specs = [pl.BlockSpec((tm, D_MODEL), tile),
                pl.BlockSpec((tm, LANES), tile),
                pl.BlockSpec(mod_l.shape, const)]
    args = [x, lpos, mod_l]
    if final:
        in_specs.append(pl.BlockSpec((1, D_MODEL), const))
        args.append(final_w)
    in_specs.append(pl.BlockSpec(memory_space=pl.ANY))
    args.append(y_sorted)
    if final:
        out_specs = [pl.BlockSpec((tm, D_MODEL), lambda i, *_: (jnp.minimum(i, npt - 1), 0)),
                     pl.BlockSpec((tm, D_MODEL), lambda i, *_: (jnp.maximum(i - npt, 0), 0))]
        out_shape = [jax.ShapeDtypeStruct((npt * tm, D_MODEL), F32),
                     jax.ShapeDtypeStruct((ttot - npt * tm, D_MODEL), F32)]
    else:
        out_specs = [pl.BlockSpec((tm, D_MODEL), tile)]
        out_shape = [jax.ShapeDtypeStruct((ttot, D_MODEL), F32)]
    grid_spec = pltpu.PrefetchScalarGridSpec(
        num_scalar_prefetch=3, grid=(ttot // tm,),
        in_specs=in_specs, out_specs=out_specs,
        scratch_shapes=[pltpu.VMEM((2, SORT_ROWS, D_MODEL), F32), pltpu.SemaphoreType.DMA((2,))])
    return pl.pallas_call(
        functools.partial(_combine_kernel, tm=tm, npc=npc, final=final),
        grid_spec=grid_spec, out_shape=out_shape,
        compiler_params=_cparams(("arbitrary",)),
        name="combine",
    )(*tables, *args)


def kernel(x_prompt, x_sample, c_prompt, c_sample, state_mlstm_C, state_mlstm_n, state_mlstm_m, state_mlstm_conv, state_ret_S, state_hgrn_S, w_ada, b_ada, norm_mix_w, norm_ffn_w, final_norm_w, w_in_even, b_mlstm_i, b_mlstm_f, w_mlstm_conv, b_mlstm_conv, w_mlstm_q, w_mlstm_k, mlstm_skip, mlstm_norm_w, ret_norm_w, w_out_even, w_in_odd, hgrn_lb_logits, hgrn_norm_w, w_out_odd, moe_router_w, moe_router_b, moe_w_gate_up, moe_b_gate_up, moe_w_down, moe_b_down):
    bp, seq, d = x_prompt.shape
    bs, dseq, _ = x_sample.shape
    assert bp == 1 and d == D_MODEL and dseq == CHUNK
    assert seq % TOKEN_TILE == 0 and (bs * dseq) % TOKEN_TILE == 0
    depth = w_ada.shape[0]
    tp, tsmp = seq, bs * dseq
    ttot = tp + tsmp
    npc = tp // CHUNK
    he, ho = N_HEADS_EVEN, N_HEADS_ODD
    da = he * LANES
    past_len = 1024

    x = jnp.concatenate([x_prompt.reshape(tp, d), x_sample.reshape(tsmp, d)], axis=0)
    n_mod_rows = 2 * SUBLANES
    assert 1 + bs <= n_mod_rows
    c_all = jnp.zeros((n_mod_rows, d), F32).at[0:1].set(c_prompt).at[1:1 + bs].set(c_sample)
    mod = _ada_call(c_all, w_ada, b_ada)

    half = LANES // 2
    inv = ROPE_BASE ** (-jnp.arange(half, dtype=F32) / half)
    pos_all = jnp.concatenate([jnp.arange(tp, dtype=F32),
                               jnp.tile(past_len + jnp.arange(dseq, dtype=F32), bs)])
    ang = pos_all[:, None] * inv[None, :]
    cos_t = jnp.concatenate([jnp.cos(ang), jnp.cos(ang)], axis=-1)
    sin_t = jnp.concatenate([-jnp.sin(ang), jnp.sin(ang)], axis=-1)

    lb_p = jax.nn.softmax(hgrn_lb_logits.astype(F32), axis=0)
    lbs = jnp.cumsum(lb_p, axis=0) - lb_p[0]

    n_tiles = ttot // DISPATCH_TILE
    ntp = -(-n_tiles // LANES) * LANES
    max_rows = ttot * TOP_K + n_tiles * N_EXPERTS * (SUBLANES - 1)
    nb = -(-max_rows // EXPERT_ROWS) + N_EXPERTS
    nbp = -(-nb // SUBLANES) * SUBLANES
    xs = jnp.zeros((nb * EXPERT_ROWS, XS_WIDTH), F32)

    ts_p = TOKEN_TILE
    steps_p = tp // ts_p
    even_out, odd_out = [], []
    y_final = None
    for l in range(depth):
        jl = l // 2
        mod_l = mod[l]
        if l % 2 == 0:
            w_in = w_in_even[jl]
            w_main = jnp.concatenate([w_in[:, da:3 * da], w_in[:, 3 * da + 2 * he:]], axis=1).astype(BF16)
            w_gate = jnp.zeros((d, da + LANES), F32).at[:, :da].set(w_in[:, :da])
            w_gate = w_gate.at[:, da:da + 2 * he].set(w_in[:, 3 * da:3 * da + 2 * he])
            proj, gates = _inproj_call(x, mod_l, norm_mix_w[l][None], w_main, w_gate, npc)
            gbias = jnp.zeros((1, LANES), F32).at[0, :he].set(b_mlstm_i[jl]).at[0, he:2 * he].set(b_mlstm_f[jl])
            weights = [w_mlstm_conv[jl], b_mlstm_conv[jl][None],
                       w_mlstm_q[jl], w_mlstm_k[jl], gbias,
                       mlstm_skip[jl][None], mlstm_norm_w[jl][None], ret_norm_w[jl][None]]
            zeros_p = (jnp.zeros((1, he, LANES, LANES), F32), jnp.zeros((1, he, 1, LANES), F32),
                       jnp.zeros((1, 1, LANES), F32), jnp.zeros((1, SUBLANES, da), F32),
                       jnp.zeros((1, he, LANES, LANES), F32))
            st_s = (state_mlstm_C[jl], state_mlstm_n[jl][:, :, None, :],
                    jnp.zeros((bs, 1, LANES), F32).at[:, 0, :he].set(state_mlstm_m[jl]),
                    jnp.zeros((bs, SUBLANES, da), F32).at[:, SUBLANES - (CONV_W - 1):].set(state_mlstm_conv[jl]),
                    state_ret_S[jl])
            res_p = _even_scan_call(proj, gates, cos_t, sin_t, zeros_p, weights,
                                    ts=ts_p, n_seq=1, steps=steps_p, row_off=0)
            res_s = _even_scan_call(proj, gates, cos_t, sin_t, st_s, weights,
                                    ts=CHUNK, n_seq=bs, steps=1, row_off=npc)
            u_p, u_s = res_p[0], res_s[0]
            even_out.append((res_p[1:], res_s[1:]))
            w_out = w_out_even[jl].astype(BF16)
        else:
            proj = _inproj_call(x, mod_l, norm_mix_w[l][None], w_in_odd[jl].astype(BF16), None, npc)[0]
            lb = lbs[l][None]
            nw = hgrn_norm_w[jl][None]
            u_p, sp = _odd_scan_call(proj, jnp.zeros((1, ho, LANES, LANES), F32), lb, nw,
                                     ts=ts_p, n_seq=1, steps=steps_p, row_off=0)
            u_s, ss = _odd_scan_call(proj, state_hgrn_S[jl], lb, nw,
                                     ts=CHUNK, n_seq=bs, steps=1, row_off=npc)
            odd_out.append((sp, ss))
            w_out = w_out_odd[jl].astype(BF16)

        w_r = jnp.zeros((d, LANES), F32).at[:, :N_EXPERTS].set(moe_router_w[l])
        b_r = jnp.full((1, LANES), -jnp.inf, F32).at[0, :N_EXPERTS].set(moe_router_b[l])
        x, hf, gate, lpos, cnt = _post_call(x, u_p, u_s, mod_l, w_out, norm_ffn_w[l][None], w_r, b_r, npc)
        cnt_tiles = jnp.zeros((ntp, LANES), F32).at[:n_tiles].set(cnt[:, 0, :])
        n8, loff, gbase, blk, ends = _plan_call(cnt_tiles, nbp)
        tables = [t[:n_tiles, :N_EXPERTS].reshape(-1) for t in (n8, loff, gbase)]
        blk_expert = blk[:nb, 0]
        n_used = ends[0, N_EXPERTS - 1:N_EXPERTS]
        xs = _dispatch_call(tables, hf, lpos, gate, xs)
        y_sorted = _expert_call(blk_expert, n_used, xs, moe_w_gate_up, moe_b_gate_up[:, :, None, :],
                                moe_w_down, moe_b_down[:, :, None, :], l)
        if l == depth - 1:
            y_final = _combine_call(tables, x, lpos, mod_l, y_sorted, npc, final_norm_w[None])
        else:
            x = _combine_call(tables, x, lpos, mod_l, y_sorted, npc, None)[0]

    def even_states(which):
        cs = jnp.stack([e[which][0] for e in even_out])
        ns = jnp.stack([e[which][1][:, :, 0, :] for e in even_out])
        ms = jnp.stack([e[which][2][:, 0, :he] for e in even_out])
        cv = jnp.stack([e[which][3][:, SUBLANES - (CONV_W - 1):, :] for e in even_out])
        ss = jnp.stack([e[which][4] for e in even_out])
        return cs, ns, ms, cv, ss

    p_c, p_n, p_m, p_cv, p_s = even_states(0)
    s_c, s_n, s_m, s_cv, s_s = even_states(1)
    p_h = jnp.stack([o[0] for o in odd_out])
    s_h = jnp.stack([o[1] for o in odd_out])
    y_prompt = y_final[0].reshape(bp, seq, d)
    y_sample = y_final[1].reshape(bs, dseq, d)
    return (y_prompt, y_sample, p_c, p_n, p_m, p_cv, p_s, p_h, s_c, s_n, s_m, s_cv, s_s, s_h)
```

```python
import functools
import math

import jax
import jax.numpy as jnp
from jax import lax
from jax.experimental import pallas as pl
from jax.experimental.pallas import tpu as pltpu

F32 = jnp.float32
BF16 = jnp.bfloat16

CHUNK = 64
LANES = 128
SUBLANES = 8
D_MODEL = 1024
N_HEADS_EVEN = 4
N_HEADS_ODD = 8
CONV_W = 4
N_EXPERTS = 32
TOP_K = 4
SWIGLU_LIMIT = 7.0
SWIGLU_ALPHA = 1.702
EPS = 1e-6
ROPE_BASE = 10000.0

TOKEN_TILE = 512
INPROJ_TILE = 512
DISPATCH_TILE = 256
SORT_ROWS = DISPATCH_TILE * TOP_K + LANES * 2
XS_WIDTH = D_MODEL + LANES
EXPERT_ROWS = 512
VMEM_LIMIT = 56 * 1024 * 1024


def _cparams(sem, vmem=VMEM_LIMIT):
    return pltpu.CompilerParams(dimension_semantics=sem, vmem_limit_bytes=vmem)


def _dot(a, b):
    return jnp.dot(a, b, preferred_element_type=F32)


def _dot_nt(a, b):
    return lax.dot_general(a, b, (((1,), (1,)), ((), ())), preferred_element_type=F32)


def _dot_tn(a, b):
    return lax.dot_general(a, b, (((0,), (0,)), ((), ())), preferred_element_type=F32)


def _split3(x):
    p1 = x.astype(BF16)
    r1 = x - p1.astype(F32)
    p2 = r1.astype(BF16)
    p3 = (r1 - p2.astype(F32)).astype(BF16)
    return p1, p2, p3


def _dot3(a, b):
    ah = a.astype(BF16)
    al = (a - ah.astype(F32)).astype(BF16)
    bh = b.astype(BF16)
    bl = (b - bh.astype(F32)).astype(BF16)
    return _dot(ah, bh) + (_dot(ah, bl) + _dot(al, bh))


def _dot3_tn_fused(a, b):
    n = a.shape[1]
    ah = a.astype(BF16)
    al = (a - ah.astype(F32)).astype(BF16)
    bh = b.astype(BF16)
    bl = (b - bh.astype(F32)).astype(BF16)
    x = _dot_tn(jnp.concatenate([ah, al], axis=1), jnp.concatenate([bh, bl], axis=1))
    return x[:n, :n] + (x[:n, n:] + x[n:, :n])


def _block_diag(a, b):
    z = jnp.zeros_like(a)
    return jnp.concatenate([jnp.concatenate([a, z], axis=1), jnp.concatenate([z, b], axis=1)], axis=0)


def _dot_sel_lhs(sel_bf16, x):
    p1, p2, p3 = _split3(x)
    return _dot(sel_bf16, p1) + (_dot(sel_bf16, p2) + _dot(sel_bf16, p3))


def _dot_sel_nt(sel_bf16, x):
    p1, p2, p3 = _split3(x)
    return _dot_nt(sel_bf16, p1) + (_dot_nt(sel_bf16, p2) + _dot_nt(sel_bf16, p3))


def _dot_sel_rhs(x, sel_bf16):
    p1, p2, p3 = _split3(x)
    return _dot(p1, sel_bf16) + (_dot(p2, sel_bf16) + _dot(p3, sel_bf16))


def _dot_sel_lhs2(sel_bf16, x):
    hi = x.astype(BF16)
    lo = (x - hi.astype(F32)).astype(BF16)
    return _dot(sel_bf16, hi) + _dot(sel_bf16, lo)


def _sigmoid(x):
    return 1.0 / (1.0 + jnp.exp(-x))


def _log_sigmoid(x):
    return jnp.minimum(x, 0.0) - jnp.log1p(jnp.exp(-jnp.abs(x)))


def _logaddexp(a, b):
    return jnp.maximum(a, b) + jnp.log1p(jnp.exp(-jnp.abs(a - b)))


def _rms(x):
    return x * lax.rsqrt(jnp.mean(x * x, axis=-1, keepdims=True) + EPS)


def _chunk_tri(n):
    r = lax.broadcasted_iota(jnp.int32, (n, n), 0)
    c = lax.broadcasted_iota(jnp.int32, (n, n), 1)
    return jnp.where((r // CHUNK == c // CHUNK) & (c <= r), 1.0, 0.0).astype(BF16)


def _chunk_cumsum(x):
    rows, n = x.shape
    vregs = CHUNK // SUBLANES
    x4 = x.reshape(rows // CHUNK, vregs, SUBLANES, n)
    sub = lax.broadcasted_iota(jnp.int32, x4.shape, 2)
    s = x4
    for shift in (1, 2, 4):
        s = s + jnp.where(sub >= shift, pltpu.roll(s, shift, 2), 0.0)
    outs, carry = [], None
    for v in range(vregs):
        cur = s[:, v] if carry is None else s[:, v] + carry
        outs.append(cur)
        carry = jnp.broadcast_to(cur[:, SUBLANES - 1:SUBLANES, :], cur.shape)
    return jnp.stack(outs, axis=1).reshape(rows, n)


def _seq_row(tile_idx, chunks_per_tile, c, n_prompt_chunks):
    return jnp.maximum(tile_idx * chunks_per_tile + c - (n_prompt_chunks - 1), 0)


def _ada_kernel(c_ref, w_ref, b_ref, o_ref):
    c = c_ref[...]
    o_ref[0] = _dot3(c * _sigmoid(c), w_ref[0]) + b_ref[0]


def _ada_call(c_all, w_ada, b_ada):
    depth = w_ada.shape[0]
    nrow = c_all.shape[0]
    ncol = w_ada.shape[2] // D_MODEL
    return pl.pallas_call(
        _ada_kernel,
        grid=(depth, ncol),
        in_specs=[pl.BlockSpec((nrow, D_MODEL), lambda l, j: (0, 0)),
                  pl.BlockSpec((1, D_MODEL, D_MODEL), lambda l, j: (l, 0, j)),
                  pl.BlockSpec((1, 1, D_MODEL), lambda l, j: (l, 0, j))],
        out_specs=pl.BlockSpec((1, nrow, D_MODEL), lambda l, j: (l, 0, j)),
        out_shape=jax.ShapeDtypeStruct((depth, nrow, ncol * D_MODEL), F32),
        compiler_params=_cparams(("arbitrary", "arbitrary")),
        name="ada",
    )(c_all, w_ada, b_ada.reshape(depth, 1, -1))


def _inproj_kernel(*refs, tm, ng, npc, n_hp):
    if n_hp:
        x_ref, mod_ref, nw_ref, w_ref, whp_ref, proj_ref, gates_ref, h_scr = refs
    else:
        x_ref, mod_ref, nw_ref, w_ref, proj_ref, h_scr = refs
    i = pl.program_id(0)
    nch = tm // CHUNK
    for c in range(nch):
        seq = _seq_row(i, nch, c, npc)
        sh = mod_ref[pl.ds(seq, 1), 0:D_MODEL]
        sc = mod_ref[pl.ds(seq, 1), D_MODEL:2 * D_MODEL]
        xc = x_ref[c * CHUNK:(c + 1) * CHUNK, :]
        h_scr[c * CHUNK:(c + 1) * CHUNK, :] = _rms(xc) * nw_ref[...] * (1.0 + sc) + sh
    h = h_scr[...]
    hb = h.astype(BF16)
    for g in range(0, ng - n_hp, 4):
        res = _dot(hb, w_ref[:, g * LANES:(g + 4) * LANES])
        for jj in range(4):
            proj_ref[n_hp + g + jj] = res[:, jj * LANES:(jj + 1) * LANES]
    if n_hp:
        res = _dot3(h, whp_ref[...])
        for jj in range(n_hp):
            proj_ref[jj] = res[:, jj * LANES:(jj + 1) * LANES]
        gates_ref[...] = res[:, n_hp * LANES:]


def _inproj_call(x, mod_l, nw, w_main, w_gate, npc):
    ttot = x.shape[0]
    tm = INPROJ_TILE
    has_gates = w_gate is not None
    n_hp = w_gate.shape[1] // LANES - 1 if has_gates else 0
    ng = w_main.shape[1] // LANES + n_hp
    in_specs = [pl.BlockSpec((tm, D_MODEL), lambda i: (i, 0)),
                pl.BlockSpec(mod_l.shape, lambda i: (0, 0)),
                pl.BlockSpec((1, D_MODEL), lambda i: (0, 0)),
                pl.BlockSpec(w_main.shape, lambda i: (0, 0))]
    out_specs = [pl.BlockSpec((ng, tm, LANES), lambda i: (0, i, 0))]
    out_shape = [jax.ShapeDtypeStruct((ng, ttot, LANES), F32)]
    args = [x, mod_l, nw, w_main]
    if has_gates:
        in_specs.append(pl.BlockSpec(w_gate.shape, lambda i: (0, 0)))
        out_specs.append(pl.BlockSpec((tm, LANES), lambda i: (i, 0)))
        out_shape.append(jax.ShapeDtypeStruct((ttot, LANES), F32))
        args.append(w_gate)
    return pl.pallas_call(
        functools.partial(_inproj_kernel, tm=tm, ng=ng, npc=npc, n_hp=n_hp),
        grid=(ttot // tm,),
        in_specs=in_specs, out_specs=out_specs, out_shape=out_shape,
        scratch_shapes=[pltpu.VMEM((tm, D_MODEL), F32)],
        compiler_params=_cparams(("arbitrary",)),
        name="inproj",
    )(*args)


_G_XM, _G_VA, _G_OA, _G_QB, _G_KB, _G_VB, _G_GB = 0, 4, 8, 12, 16, 20, 24


def _even_scan_kernel(proj_ref, gates_ref, cos_ref, sin_ref,
                      c0_ref, n0_ref, m0_ref, conv0_ref, s0_ref,
                      cw_ref, cb_ref, wq_ref, wk_ref, gbias_ref, skip_ref, nwa_ref, nwb_ref,
                      u_ref, co_ref, no_ref, mo_ref, convo_ref, so_ref,
                      c_scr, n_scr, m_scr, conv_scr, s_scr,
                      xbuf, xc_scr, q_scr, k_scr, qr_scr, kr_scr,
                      gl_scr, bc_scr, rows_scr, dec_scr, *, ts):
    H = N_HEADS_EVEN
    j = pl.program_id(1)
    nj = pl.num_programs(1)
    nc = ts // CHUNK

    @pl.when(j == 0)
    def _():
        for p in range(H // 2):
            c_scr[p] = _block_diag(c0_ref[0, 2 * p], c0_ref[0, 2 * p + 1])
            s_scr[p] = _block_diag(s0_ref[0, 2 * p], s0_ref[0, 2 * p + 1])
        n_scr[...] = n0_ref[0]
        m_scr[...] = m0_ref[0]
        conv_scr[...] = conv0_ref[0]

    for g in range(H):
        lo, hi = g * LANES, (g + 1) * LANES
        x_g = proj_ref[_G_XM + g]
        xbuf[0:SUBLANES, :] = conv_scr[:, lo:hi]
        xbuf[SUBLANES:SUBLANES + ts, :] = x_g
        acc = cb_ref[:, lo:hi] + cw_ref[CONV_W - 1:CONV_W, lo:hi] * x_g
        for t in range(CONV_W - 1):
            off = SUBLANES - (CONV_W - 1) + t
            acc = acc + cw_ref[t:t + 1, lo:hi] * xbuf[off:off + ts, :]
        conv_scr[:, lo:hi] = xbuf[ts:ts + SUBLANES, :]
        xc = acc * _sigmoid(acc)
        xc_scr[g] = xc
        q_scr[g] = _dot3(xc, wq_ref[g])
        k_scr[g] = _dot3(xc, wk_ref[g]) * (LANES ** -0.5)

    cosv = cos_ref[...]
    sinv = sin_ref[...]
    for g in range(H):
        qb = proj_ref[_G_QB + g]
        kb = proj_ref[_G_KB + g]
        qr_scr[g] = qb * cosv + pltpu.roll(qb, LANES // 2, 1) * sinv
        kr_scr[g] = (kb * cosv + pltpu.roll(kb, LANES // 2, 1) * sinv) * (LANES ** -0.5)

    gpre = gates_ref[...] + gbias_ref[...]
    lane = lax.broadcasted_iota(jnp.int32, (ts, LANES), 1)
    gl = jnp.where(lane < H, gpre, _log_sigmoid(gpre))
    gl_scr[...] = gl
    bc = _dot_sel_lhs(_chunk_tri(ts), gl)
    bc_scr[...] = bc
    comb = jnp.where(lane < H, gl, bc)
    even_head = (lane % 2) == 0
    comb_even = jnp.where(even_head, comb, 0.0)
    comb_odd = jnp.where(even_head, 0.0, comb)
    pr = lax.broadcasted_iota(jnp.int32, (SUBLANES, LANES), 0)
    pc = lax.broadcasted_iota(jnp.int32, (SUBLANES, LANES), 1)
    pair_sel = jnp.where((pc // 2 == pr) & (pc < 2 * H), 1.0, 0.0).astype(BF16)
    for c in range(nc):
        cs = slice(c * CHUNK, (c + 1) * CHUNK)
        rows_scr[c] = _dot_sel_nt(pair_sel, jnp.concatenate([comb_even[cs, :], comb_odd[cs, :]], axis=0))

    ti = lax.broadcasted_iota(jnp.int32, (CHUNK, LANES), 0)
    lane2 = lax.broadcasted_iota(jnp.int32, (CHUNK, LANES), 1)
    left = lane2 < CHUNK
    si = lane2 % CHUNK
    tril = ti >= si
    left_wide = lax.broadcasted_iota(jnp.int32, (CHUNK, 2 * LANES), 1) < LANES
    br = lax.broadcasted_iota(jnp.int32, (2 * LANES, 2 * LANES), 0) < LANES
    bcol_blk = lax.broadcasted_iota(jnp.int32, (2 * LANES, 2 * LANES), 1) < LANES
    tcol = lax.broadcasted_iota(jnp.int32, (CHUNK, 1), 0).astype(F32)
    log_gamma = [math.log1p(-2.0 ** (-5 - h)) for h in range(H)]
    for p in range(H // 2):
        lg2 = jnp.where(left, log_gamma[2 * p], log_gamma[2 * p + 1])
        dec_scr[p] = jnp.where(tril, jnp.exp((ti - si).astype(F32) * lg2), 0.0)

    def chunk_body(c, carry):
        r0 = pl.multiple_of(c * CHUNK, CHUNK)
        rows = pl.ds(r0, CHUNK)
        pair_rows = rows_scr[c]
        glc = gl_scr[rows, :]
        bcc = bc_scr[rows, :]
        m_all = m_scr[...]
        c_old = [c_scr[p] for p in range(H // 2)]
        s_old = [s_scr[p] for p in range(H // 2)]
        n_old = [n_scr[h] for h in range(H)]
        c_new, s_new, n_new = [], [], []
        m_next = m_all
        lane_row = lax.broadcasted_iota(jnp.int32, (1, LANES), 1)
        for p in range(H // 2):
            hs = (2 * p, 2 * p + 1)
            q = [q_scr[h, rows, :] for h in hs]
            k = [k_scr[h, rows, :] for h in hs]
            v = [proj_ref[_G_VA + h, rows, :] for h in hs]
            b_col = [bcc[:, H + h:H + h + 1] for h in hs]
            i_col = [glc[:, h:h + 1] for h in hs]
            m_prev = [m_all[:, h:h + 1] for h in hs]
            q2 = jnp.concatenate(q, axis=1).astype(BF16)
            smat = _dot_nt(q2, _block_diag(k[0].astype(BF16), k[1].astype(BF16)))
            dmat = jnp.where(tril, jnp.where(left, b_col[0], b_col[1])
                             - pair_rows[2 + p:3 + p, :] + pair_rows[p:p + 1, :], -jnp.inf)
            a = [jnp.max(jnp.where(left, dmat, -jnp.inf), axis=-1, keepdims=True),
                 jnp.max(jnp.where(left, -jnp.inf, dmat), axis=-1, keepdims=True)]
            inter = [b_col[e] + m_prev[e] for e in range(2)]
            m_t = [jnp.maximum(inter[e], a[e]) for e in range(2)]
            w_inter = [jnp.exp(inter[e] - m_t[e]) for e in range(2)]
            amat = smat * jnp.exp(dmat - jnp.where(left, m_t[0], m_t[1]))
            cst = c_old[p]
            num = (_dot(amat.astype(BF16), _block_diag(v[0].astype(BF16), v[1].astype(BF16)))
                   + jnp.where(left_wide, w_inter[0], w_inter[1]) * _dot(q2, cst.astype(BF16)))
            den = [jnp.sum(jnp.where(left, amat, 0.0), axis=-1, keepdims=True),
                   jnp.sum(jnp.where(left, 0.0, amat), axis=-1, keepdims=True)]
            den = [jnp.maximum(jnp.abs(den[e] + w_inter[e] * jnp.sum(q[e] * n_old[hs[e]], axis=-1, keepdims=True)),
                               jnp.exp(-m_t[e])) for e in range(2)]
            hout = num / jnp.where(left_wide, den[0], den[1])
            decay, upd = [], []
            for e in range(2):
                m_new = m_t[e][CHUNK - 1:CHUNK, :]
                b_last = b_col[e][CHUNK - 1:CHUNK, :]
                kw = k[e] * jnp.exp(b_last - b_col[e] + i_col[e] - m_new)
                decay.append(jnp.exp(b_last + m_prev[e] - m_new))
                upd.append(_dot3_tn_fused(kw, v[e]))
                n_new.append(decay[e] * n_old[hs[e]] + jnp.sum(kw, axis=0, keepdims=True))
                m_next = jnp.where(lane_row == hs[e], m_new, m_next)
            c_new.append(jnp.where(br, decay[0], decay[1]) * cst + _block_diag(upd[0], upd[1]))
            for e in range(2):
                h = hs[e]
                lo, hi = h * LANES, (h + 1) * LANES
                z = _sigmoid(proj_ref[_G_OA + h, rows, :]) * hout[:, e * LANES:(e + 1) * LANES]
                u_ref[rows, lo:hi] = (_rms(z) * nwa_ref[:, lo:hi]
                                      + skip_ref[:, lo:hi] * xc_scr[h, rows, :])
        for p in range(H // 2):
            hs = (2 * p, 2 * p + 1)
            lg = [log_gamma[h] for h in hs]
            kr = [kr_scr[h, rows, :] for h in hs]
            vb = [proj_ref[_G_VB + h, rows, :].astype(BF16) for h in hs]
            q2 = jnp.concatenate([qr_scr[h, rows, :] for h in hs], axis=1).astype(BF16)
            amat = _dot_nt(q2, _block_diag(kr[0].astype(BF16), kr[1].astype(BF16))) * dec_scr[p]
            sst = s_old[p]
            o = (_dot(amat.astype(BF16), _block_diag(vb[0], vb[1]))
                 + jnp.exp((tcol + 1.0) * jnp.where(left_wide, lg[0], lg[1])) * _dot(q2, sst.astype(BF16)))
            kws = jnp.concatenate([kr[e] * jnp.exp((CHUNK - 1.0 - tcol) * lg[e]) for e in range(2)], axis=1)
            cross = _dot_tn(kws.astype(BF16), jnp.concatenate(vb, axis=1))
            s_new.append(jnp.where(br, math.exp(CHUNK * lg[0]), math.exp(CHUNK * lg[1])) * sst
                         + jnp.where(br == bcol_blk, cross, 0.0))
            for e in range(2):
                h = hs[e]
                lo, hi = h * LANES, (h + 1) * LANES
                gate = proj_ref[_G_GB + h, rows, :]
                u_ref[rows, D_MODEL // 2 + lo:D_MODEL // 2 + hi] = (
                    _rms(o[:, e * LANES:(e + 1) * LANES]) * nwb_ref[:, lo:hi] * (gate * _sigmoid(gate)))
        for p in range(H // 2):
            c_scr[p] = c_new[p]
            s_scr[p] = s_new[p]
        for h in range(H):
            n_scr[h] = n_new[h]
        m_scr[...] = m_next
        return carry

    lax.fori_loop(0, nc, chunk_body, 0)

    @pl.when(j == nj - 1)
    def _():
        for p in range(H // 2):
            for e in range(2):
                blk = slice(e * LANES, (e + 1) * LANES)
                co_ref[0, 2 * p + e] = c_scr[p, blk, blk]
                so_ref[0, 2 * p + e] = s_scr[p, blk, blk]
        no_ref[0] = n_scr[...]
        mo_ref[0] = m_scr[...]
        convo_ref[0] = conv_scr[...]


def _even_scan_call(proj, gates, cos_t, sin_t, states, weights, *, ts, n_seq, steps, row_off):
    H = N_HEADS_EVEN
    c0, n0, m0, conv0, s0 = states
    ng = proj.shape[0]
    rows_idx = lambda b, j: (row_off + b * steps + j, 0)
    state_specs = [pl.BlockSpec((1, H, LANES, LANES), lambda b, j: (b, 0, 0, 0)),
                   pl.BlockSpec((1, H, 1, LANES), lambda b, j: (b, 0, 0, 0)),
                   pl.BlockSpec((1, 1, LANES), lambda b, j: (b, 0, 0)),
                   pl.BlockSpec((1, SUBLANES, H * LANES), lambda b, j: (b, 0, 0)),
                   pl.BlockSpec((1, H, LANES, LANES), lambda b, j: (b, 0, 0, 0))]
    in_specs = [pl.BlockSpec((ng, ts, LANES), lambda b, j: (0, row_off + b * steps + j, 0)),
                pl.BlockSpec((ts, LANES), rows_idx),
                pl.BlockSpec((ts, LANES), rows_idx),
                pl.BlockSpec((ts, LANES), rows_idx)] + state_specs
    for w in weights:
        in_specs.append(pl.BlockSpec(w.shape, functools.partial(lambda nd, b, j: (0,) * nd, w.ndim)))
    out_specs = [pl.BlockSpec((ts, D_MODEL), lambda b, j: (b * steps + j, 0))] + state_specs
    out_shape = [jax.ShapeDtypeStruct((n_seq * steps * ts, D_MODEL), F32),
                 jax.ShapeDtypeStruct(c0.shape, F32), jax.ShapeDtypeStruct(n0.shape, F32),
                 jax.ShapeDtypeStruct(m0.shape, F32), jax.ShapeDtypeStruct(conv0.shape, F32),
                 jax.ShapeDtypeStruct(s0.shape, F32)]
    nc = ts // CHUNK
    pair_state = pltpu.VMEM((H // 2, 2 * LANES, 2 * LANES), F32)
    scratch = [pair_state, pltpu.VMEM((H, 1, LANES), F32),
               pltpu.VMEM((1, LANES), F32), pltpu.VMEM((SUBLANES, H * LANES), F32),
               pair_state,
               pltpu.VMEM((ts + 2 * SUBLANES, LANES), F32),
               pltpu.VMEM((H, ts, LANES), F32), pltpu.VMEM((H, ts, LANES), F32),
               pltpu.VMEM((H, ts, LANES), F32), pltpu.VMEM((H, ts, LANES), F32),
               pltpu.VMEM((H, ts, LANES), F32),
               pltpu.VMEM((ts, LANES), F32), pltpu.VMEM((ts, LANES), F32),
               pltpu.VMEM((nc, SUBLANES, LANES), F32),
               pltpu.VMEM((H // 2, CHUNK, LANES), F32)]
    return pl.pallas_call(
        functools.partial(_even_scan_kernel, ts=ts),
        grid=(n_seq, steps),
        in_specs=in_specs, out_specs=out_specs, out_shape=out_shape,
        scratch_shapes=scratch,
        compiler_params=_cparams(("arbitrary", "arbitrary")),
        name="even_scan",
    )(proj, gates, cos_t, sin_t, c0, n0, m0, conv0, s0, *weights)


_G_Q, _G_F, _G_I, _G_G = 0, 8, 16, 24


def _hgrn_intra_pair(q, k, bcum, ti, si, left, tcol_i):
    amat = jnp.zeros((CHUNK, LANES), F32)
    for b in (32, 16, 8):
        nb2 = CHUNK // (2 * b)
        upper = ((tcol_i // b) % 2) == 1
        ql, kl = [], []
        for e in range(2):
            parts = [jnp.broadcast_to(bcum[e][m * 2 * b + b - 1:m * 2 * b + b, :], (2 * b, LANES))
                     for m in range(nb2)]
            ref = parts[0] if nb2 == 1 else jnp.concatenate(parts, axis=0)
            ql.append(jnp.where(upper, q[e] * jnp.exp(bcum[e] - ref), 0.0).astype(BF16))
            kl.append(jnp.where(upper, 0.0, k[e] * jnp.exp(ref - bcum[e])).astype(BF16))
        al = _dot_nt(jnp.concatenate(ql, axis=1), _block_diag(kl[0], kl[1]))
        amat = amat + jnp.where((ti // (2 * b)) == (si // (2 * b)), al, 0.0)
    nblk = CHUNK // SUBLANES
    b3 = [x.reshape(nblk, SUBLANES, LANES) for x in bcum]
    k3 = [x.reshape(nblk, SUBLANES, LANES) for x in k]
    for jj in range(SUBLANES):
        col = []
        for e in range(2):
            bj = jnp.broadcast_to(b3[e][:, jj:jj + 1, :], (nblk, SUBLANES, LANES)).reshape(CHUNK, LANES)
            kj = jnp.broadcast_to(k3[e][:, jj:jj + 1, :], (nblk, SUBLANES, LANES)).reshape(CHUNK, LANES)
            col.append(jnp.sum(q[e] * kj * jnp.exp(bcum[e] - bj), axis=-1, keepdims=True))
        sel = (si == (ti // SUBLANES) * SUBLANES + jj) & ((ti % SUBLANES) >= jj)
        amat = jnp.where(sel, jnp.where(left, col[0], col[1]), amat)
    return amat


def _odd_scan_kernel(proj_ref, s0_ref, lb_ref, nw_ref, u_ref, so_ref,
                     st_scr, k_scr, bc_scr, *, ts):
    H = N_HEADS_ODD
    j = pl.program_id(1)
    nj = pl.num_programs(1)
    nc = ts // CHUNK

    @pl.when(j == 0)
    def _():
        for p in range(H // 2):
            st_scr[p] = _block_diag(s0_ref[0, 2 * p].T, s0_ref[0, 2 * p + 1].T)

    for h in range(H):
        lo, hi = h * LANES, (h + 1) * LANES
        lbv = lb_ref[:, lo:hi]
        fpre = proj_ref[_G_F + h]
        logf = _logaddexp(jnp.log(lbv), jnp.log1p(-lbv) + _log_sigmoid(fpre))
        k_scr[h] = (1.0 - lbv) * _sigmoid(-fpre)
        bc_scr[h] = _chunk_cumsum(logf)

    ti = lax.broadcasted_iota(jnp.int32, (CHUNK, LANES), 0)
    lane2 = lax.broadcasted_iota(jnp.int32, (CHUNK, LANES), 1)
    left = lane2 < CHUNK
    si = lane2 % CHUNK
    tcol_i = lax.broadcasted_iota(jnp.int32, (CHUNK, 1), 0)
    same_head = ((lax.broadcasted_iota(jnp.int32, (2 * LANES, 2 * LANES), 0) < LANES)
                 == (lax.broadcasted_iota(jnp.int32, (2 * LANES, 2 * LANES), 1) < LANES))

    def chunk_body(c, carry):
        r0 = pl.multiple_of(c * CHUNK, CHUNK)
        rows = pl.ds(r0, CHUNK)
        st_old = [st_scr[p] for p in range(H // 2)]
        st_new = []
        for p in range(H // 2):
            hs = (2 * p, 2 * p + 1)
            q = [proj_ref[_G_Q + h, rows, :] for h in hs]
            k = [k_scr[h, rows, :] for h in hs]
            vb = [proj_ref[_G_I + h, rows, :].astype(BF16) for h in hs]
            bcum = [bc_scr[h, rows, :] for h in hs]
            amat = _hgrn_intra_pair(q, k, bcum, ti, si, left, tcol_i)
            st = st_old[p]
            qg = jnp.concatenate([q[e] * jnp.exp(bcum[e]) for e in range(2)], axis=1)
            o = (_dot(amat.astype(BF16), _block_diag(vb[0], vb[1]))
                 + _dot_nt(qg.astype(BF16), st.astype(BF16)))
            last = [bcum[e][CHUNK - 1:CHUNK, :] for e in range(2)]
            kd = jnp.concatenate([k[e] * jnp.exp(last[e] - bcum[e]) for e in range(2)], axis=1)
            cross = _dot_tn(jnp.concatenate(vb, axis=1), kd.astype(BF16))
            st_new.append(st * jnp.exp(jnp.concatenate(last, axis=1)) + jnp.where(same_head, cross, 0.0))
            for e in range(2):
                h = hs[e]
                lo, hi = h * LANES, (h + 1) * LANES
                u_ref[rows, lo:hi] = (_rms(o[:, e * LANES:(e + 1) * LANES]) * nw_ref[:, lo:hi]
                                      * _sigmoid(proj_ref[_G_G + h, rows, :]))
        for p in range(H // 2):
            st_scr[p] = st_new[p]
        return carry

    lax.fori_loop(0, nc, chunk_body, 0)

    @pl.when(j == nj - 1)
    def _():
        for p in range(H // 2):
            for e in range(2):
                blk = slice(e * LANES, (e + 1) * LANES)
                so_ref[0, 2 * p + e] = st_scr[p, blk, blk].T


def _odd_scan_call(proj, s0, lb, nw, *, ts, n_seq, steps, row_off):
    H = N_HEADS_ODD
    ng = proj.shape[0]
    st_spec = pl.BlockSpec((1, H, LANES, LANES), lambda b, j: (b, 0, 0, 0))
    return pl.pallas_call(
        functools.partial(_odd_scan_kernel, ts=ts),
        grid=(n_seq, steps),
        in_specs=[pl.BlockSpec((ng, ts, LANES), lambda b, j: (0, row_off + b * steps + j, 0)),
                  st_spec,
                  pl.BlockSpec((1, D_MODEL), lambda b, j: (0, 0)),
                  pl.BlockSpec((1, D_MODEL), lambda b, j: (0, 0))],
        out_specs=[pl.BlockSpec((ts, D_MODEL), lambda b, j: (b * steps + j, 0)), st_spec],
        out_shape=[jax.ShapeDtypeStruct((n_seq * steps * ts, D_MODEL), F32),
                   jax.ShapeDtypeStruct(s0.shape, F32)],
        scratch_shapes=[pltpu.VMEM((H // 2, 2 * LANES, 2 * LANES), F32),
                        pltpu.VMEM((H, ts, LANES), F32),
                        pltpu.VMEM((H, ts, LANES), F32)],
        compiler_params=_cparams(("arbitrary", "arbitrary")),
        name="odd_scan",
    )(proj, s0, lb, nw)


def _post_kernel(x_ref, up_ref, us_ref, mod_ref, wout_ref, nw_ref, wr_ref, br_ref,
                 xo_ref, hf_ref, gate_ref, lpos_ref, cnt_ref,
                 *, tm, npc, n_prompt_tiles):
    i = pl.program_id(0)
    nch = tm // CHUNK

    u = jnp.where(i < n_prompt_tiles, up_ref[...], us_ref[...])
    y = _dot(u.astype(BF16), wout_ref[...])
    for c in range(nch):
        seq = _seq_row(i, nch, c, npc)
        gm = mod_ref[pl.ds(seq, 1), 2 * D_MODEL:3 * D_MODEL]
        shf = mod_ref[pl.ds(seq, 1), 3 * D_MODEL:4 * D_MODEL]
        scf = mod_ref[pl.ds(seq, 1), 4 * D_MODEL:5 * D_MODEL]
        rs = slice(c * CHUNK, (c + 1) * CHUNK)
        xn = x_ref[rs, :] + gm * y[rs, :]
        xo_ref[rs, :] = xn
        hf_ref[rs, :] = _rms(xn) * nw_ref[...] * (1.0 + scf) + shf

    logits = _dot3(hf_ref[...], wr_ref[...]) + br_ref[...]
    lane_i = lax.broadcasted_iota(jnp.int32, (tm, LANES), 1)
    lane_f = lane_i.astype(F32)
    vals, idxs = [], []
    cur = logits
    for _ in range(TOP_K):
        m = jnp.max(cur, axis=-1, keepdims=True)
        idx = jnp.min(jnp.where(cur == m, lane_f, float(LANES)), axis=-1, keepdims=True)
        vals.append(m)
        idxs.append(idx)
        cur = jnp.where(lane_f == idx, -jnp.inf, cur)
    exps = [jnp.exp(v - vals[0]) for v in vals]
    denom = exps[0] + exps[1] + exps[2] + exps[3]
    onehot = jnp.zeros((tm, LANES), F32)
    for idx in idxs:
        onehot = onehot + jnp.where(lane_f == idx, 1.0, 0.0)
    r = lax.broadcasted_iota(jnp.int32, (tm, tm), 0)
    cidx = lax.broadcasted_iota(jnp.int32, (tm, tm), 1)
    strict = jnp.where((cidx < r) & (cidx // DISPATCH_TILE == r // DISPATCH_TILE), 1.0, 0.0).astype(BF16)
    before = _dot(strict, onehot.astype(BF16))
    er = lax.broadcasted_iota(jnp.int32, (LANES, LANES), 0)
    ec = lax.broadcasted_iota(jnp.int32, (LANES, LANES), 1)
    lower_experts = jnp.where(er < ec, 1.0, 0.0).astype(BF16)
    pos_parts = []
    for s in range(tm // DISPATCH_TILE):
        rs = slice(s * DISPATCH_TILE, (s + 1) * DISPATCH_TILE)
        cnt = jnp.sum(onehot[rs, :], axis=0, keepdims=True)
        cnt_ref[s] = cnt
        n8 = jnp.floor((cnt + (SUBLANES - 1.0)) * (1.0 / SUBLANES)) * float(SUBLANES)
        run_start = _dot(jnp.broadcast_to(n8, (SUBLANES, LANES)).astype(BF16), lower_experts)[0:1, :]
        pos_parts.append(before[rs, :] + run_start)
    posmat = jnp.concatenate(pos_parts, axis=0)
    gate_o = jnp.zeros((tm, LANES), F32)
    lpos_o = jnp.zeros((tm, LANES), F32)
    for kk in range(TOP_K):
        lp = jnp.sum(jnp.where(lane_f == idxs[kk], posmat, 0.0), axis=-1, keepdims=True)
        gate_o = jnp.where(lane_i == kk, exps[kk] / denom, gate_o)
        lpos_o = jnp.where(lane_i == kk, lp, lpos_o)
    gate_ref[...] = gate_o
    lpos_ref[...] = lpos_o


def _post_call(x, u_p, u_s, mod_l, w_out, nw, w_r, b_r, npc):
    ttot = x.shape[0]
    tm = TOKEN_TILE
    npt = u_p.shape[0] // tm
    sub = tm // DISPATCH_TILE
    tile = lambda i: (i, 0)
    const = lambda i: (0, 0)
    return pl.pallas_call(
        functools.partial(_post_kernel, tm=tm, npc=npc, n_prompt_tiles=npt),
        grid=(ttot // tm,),
        in_specs=[pl.BlockSpec((tm, D_MODEL), tile),
                  pl.BlockSpec((tm, D_MODEL), lambda i: (jnp.minimum(i, npt - 1), 0)),
                  pl.BlockSpec((tm, D_MODEL), lambda i: (jnp.maximum(i - npt, 0), 0)),
                  pl.BlockSpec(mod_l.shape, const),
                  pl.BlockSpec(w_out.shape, const),
                  pl.BlockSpec((1, D_MODEL), const),
                  pl.BlockSpec(w_r.shape, const),
                  pl.BlockSpec((1, LANES), const)],
        out_specs=[pl.BlockSpec((tm, D_MODEL), tile), pl.BlockSpec((tm, D_MODEL), tile),
                   pl.BlockSpec((tm, LANES), tile), pl.BlockSpec((tm, LANES), tile),
                   pl.BlockSpec((sub, 1, LANES), lambda i: (i, 0, 0))],
        out_shape=[jax.ShapeDtypeStruct((ttot, D_MODEL), F32), jax.ShapeDtypeStruct((ttot, D_MODEL), F32),
                   jax.ShapeDtypeStruct((ttot, LANES), F32), jax.ShapeDtypeStruct((ttot, LANES), F32),
                   jax.ShapeDtypeStruct((ttot // DISPATCH_TILE, 1, LANES), F32)],
        compiler_params=_cparams(("arbitrary",)),
        name="post",
    )(x, u_p, u_s, mod_l, w_out, nw, w_r, b_r)


def _plan_kernel(cnt_ref, n8_ref, loff_ref, gbase_ref, blk_ref, ends_ref, *, nbp, ntp):
    cnt = cnt_ref[...]
    n8 = jnp.floor((cnt + (SUBLANES - 1.0)) * (1.0 / SUBLANES)) * float(SUBLANES)
    r = lax.broadcasted_iota(jnp.int32, (LANES, LANES), 0)
    c = lax.broadcasted_iota(jnp.int32, (LANES, LANES), 1)
    loff = _dot_sel_rhs(n8, jnp.where(r < c, 1.0, 0.0).astype(BF16))
    gtot = jnp.broadcast_to(jnp.sum(n8, axis=0, keepdims=True), (SUBLANES, LANES))
    nblk = jnp.floor((gtot + (EXPERT_ROWS - 1.0)) * (1.0 / EXPERT_ROWS))
    ends = _dot_sel_rhs(nblk, jnp.where(r <= c, 1.0, 0.0).astype(BF16))
    start_row = (ends[0:1, :] - nblk[0:1, :]) * float(EXPERT_ROWS)
    tr = lax.broadcasted_iota(jnp.int32, (ntp, ntp), 0)
    tc = lax.broadcasted_iota(jnp.int32, (ntp, ntp), 1)
    gbase = start_row + _dot_sel_lhs(jnp.where(tc < tr, 1.0, 0.0).astype(BF16), n8)
    n8_ref[...] = n8.astype(jnp.int32)
    loff_ref[...] = loff.astype(jnp.int32)
    gbase_ref[...] = gbase.astype(jnp.int32)
    bi = lax.broadcasted_iota(jnp.int32, (nbp, LANES), 0).astype(F32)
    li = lax.broadcasted_iota(jnp.int32, (nbp, LANES), 1)
    done = jnp.where((li < N_EXPERTS) & (ends[0:1, :] <= bi), 1.0, 0.0)
    be = jnp.minimum(jnp.sum(done, axis=-1, keepdims=True), N_EXPERTS - 1.0)
    blk_ref[...] = jnp.broadcast_to(be, (nbp, LANES)).astype(jnp.int32)
    ends_ref[...] = ends.astype(jnp.int32)


def _plan_call(cnt_tiles, nbp):
    ntp = cnt_tiles.shape[0]
    const = lambda i: (0, 0)
    tbl = jax.ShapeDtypeStruct((ntp, LANES), jnp.int32)
    return pl.pallas_call(
        functools.partial(_plan_kernel, nbp=nbp, ntp=ntp),
        grid=(1,),
        in_specs=[pl.BlockSpec((ntp, LANES), const)],
        out_specs=[pl.BlockSpec((ntp, LANES), const), pl.BlockSpec((ntp, LANES), const),
                   pl.BlockSpec((ntp, LANES), const),
                   pl.BlockSpec((nbp, LANES), const), pl.BlockSpec((SUBLANES, LANES), const)],
        out_shape=[tbl, tbl, tbl,
                   jax.ShapeDtypeStruct((nbp, LANES), jnp.int32),
                   jax.ShapeDtypeStruct((SUBLANES, LANES), jnp.int32)],
        compiler_params=_cparams(("arbitrary",)),
        name="plan",
    )(cnt_tiles)


_GROUP_BITS = tuple(range(3, DISPATCH_TILE.bit_length()))


_TOTAL_BITS = tuple(range(3, SORT_ROWS.bit_length()))


def _group_copies(n8_ref, loff_ref, gbase_ref, tile, make_copy, wait):
    if wait:
        total = lax.fori_loop(0, N_EXPERTS, lambda e, acc: acc + n8_ref[tile * N_EXPERTS + e], 0)
        for bit in _TOTAL_BITS:
            size = 1 << bit

            @pl.when((total & size) != 0)
            def _():
                make_copy(0, 0, size).wait()
        return

    def per_expert(e, carry):
        idx = tile * N_EXPERTS + e
        n = n8_ref[idx]
        off = loff_ref[idx]
        base = gbase_ref[idx]
        for bit in _GROUP_BITS:
            size = 1 << bit

            @pl.when((n & size) != 0)
            def _():
                done = n & ~(2 * size - 1)
                make_copy(pl.multiple_of(off + done, SUBLANES), pl.multiple_of(base + done, SUBLANES), size).start()
        return carry

    lax.fori_loop(0, N_EXPERTS, per_expert, 0)


def _dispatch_kernel(n8_ref, loff_ref, gbase_ref, hf_ref, lpos_ref, gate_ref, xs_in_ref, xs_ref,
                     sbuf, sem):
    del xs_in_ref
    i = pl.program_id(0)
    eye8 = jnp.where(lax.broadcasted_iota(jnp.int32, (SUBLANES, LANES), 0)
                     == lax.broadcasted_iota(jnp.int32, (SUBLANES, LANES), 1), 1.0, 0.0).astype(BF16)
    lpos_t = _dot_sel_nt(eye8, lpos_ref[...])
    gate_t = _dot_sel_nt(eye8, gate_ref[...])
    row = lax.broadcasted_iota(jnp.int32, (SORT_ROWS, DISPATCH_TILE), 0).astype(F32)
    perm = jnp.zeros((SORT_ROWS, DISPATCH_TILE), F32)
    wgate = jnp.zeros((SORT_ROWS, DISPATCH_TILE), F32)
    for kk in range(TOP_K):
        hit = row == lpos_t[kk:kk + 1, :]
        perm = jnp.where(hit, 1.0, perm)
        wgate = jnp.where(hit, gate_t[kk:kk + 1, :], wgate)
    slot = i % 2
    sbuf[slot, :, 0:D_MODEL] = _dot(perm.astype(BF16), hf_ref[...].astype(BF16))
    sbuf[slot, :, D_MODEL:XS_WIDTH] = _dot_sel_rhs(wgate, jnp.ones((DISPATCH_TILE, LANES), BF16))

    def copies(tile, buf_slot, wait):
        def make_copy(src_row, dst_row, size):
            return pltpu.make_async_copy(sbuf.at[buf_slot, pl.ds(src_row, size), :],
                                         xs_ref.at[pl.ds(dst_row, size), :], sem.at[buf_slot])
        _group_copies(n8_ref, loff_ref, gbase_ref, tile, make_copy, wait=wait)

    copies(i, slot, wait=False)

    @pl.when(i > 0)
    def _():
        copies(i - 1, 1 - slot, wait=True)

    @pl.when(i == pl.num_programs(0) - 1)
    def _():
        copies(i, slot, wait=True)


def _dispatch_call(tables, hf, lpos, gate, xs):
    ttot = hf.shape[0]
    tm = DISPATCH_TILE
    tile = lambda i, *_: (i, 0)
    grid_spec = pltpu.PrefetchScalarGridSpec(
        num_scalar_prefetch=3, grid=(ttot // tm,),
        in_specs=[pl.BlockSpec((tm, D_MODEL), tile), pl.BlockSpec((tm, LANES), tile),
                  pl.BlockSpec((tm, LANES), tile), pl.BlockSpec(memory_space=pl.ANY)],
        out_specs=pl.BlockSpec(memory_space=pl.ANY),
        scratch_shapes=[pltpu.VMEM((2, SORT_ROWS, XS_WIDTH), F32), pltpu.SemaphoreType.DMA((2,))])
    return pl.pallas_call(
        _dispatch_kernel,
        grid_spec=grid_spec,
        out_shape=jax.ShapeDtypeStruct(xs.shape, xs.dtype),
        input_output_aliases={6: 0},
        compiler_params=pltpu.CompilerParams(dimension_semantics=("arbitrary",),
                                             vmem_limit_bytes=VMEM_LIMIT, has_side_effects=True),
        name="dispatch",
    )(*tables, hf, lpos, gate, xs)


def _expert_kernel(be_ref, nu_ref, xs_ref, wgu_ref, bgu_ref, wdn_ref, bdn_ref, y_ref,
                   wgu_bf, wdn_bf):
    b = pl.program_id(0)

    @pl.when(b < nu_ref[0])
    def _():
        prev = be_ref[jnp.maximum(b - 1, 0)]

        @pl.when((b == 0) | (be_ref[b] != prev))
        def _():
            wgu_bf[...] = wgu_ref[0, 0].astype(BF16)
            wdn_bf[...] = wdn_ref[0, 0].astype(BF16)

        gu = _dot(xs_ref[:, 0:D_MODEL].astype(BF16), wgu_bf[...]) + bgu_ref[0, 0]
        g = jnp.minimum(gu[:, :D_MODEL], SWIGLU_LIMIT)
        u = jnp.clip(gu[:, D_MODEL:], -SWIGLU_LIMIT, SWIGLU_LIMIT)
        act = (u + 1.0) * (g * _sigmoid(SWIGLU_ALPHA * g))
        gate = xs_ref[:, D_MODEL:D_MODEL + 1]
        y_ref[...] = (_dot(act.astype(BF16), wdn_bf[...]) + bdn_ref[0, 0]) * gate

    @pl.when(b >= nu_ref[0])
    def _():
        y_ref[...] = jnp.zeros_like(y_ref)


def _expert_call(blk_expert, n_used, xs, w_gu, b_gu, w_dn, b_dn, layer):
    nb = xs.shape[0] // EXPERT_ROWS
    d_ff2 = w_gu.shape[3]
    blk = lambda b, be, nu: (jnp.minimum(b, nu[0] - 1), 0)
    blk_out = lambda b, be, nu: (b, 0)
    exp4 = lambda b, be, nu: (layer, be[jnp.minimum(b, nu[0] - 1)], 0, 0)
    grid_spec = pltpu.PrefetchScalarGridSpec(
        num_scalar_prefetch=2, grid=(nb,),
        in_specs=[pl.BlockSpec((EXPERT_ROWS, XS_WIDTH), blk),
                  pl.BlockSpec((1, 1, D_MODEL, d_ff2), exp4),
                  pl.BlockSpec((1, 1, 1, d_ff2), exp4),
                  pl.BlockSpec((1, 1, D_MODEL, D_MODEL), exp4),
                  pl.BlockSpec((1, 1, 1, D_MODEL), exp4)],
        out_specs=pl.BlockSpec((EXPERT_ROWS, D_MODEL), blk_out),
        scratch_shapes=[pltpu.VMEM((D_MODEL, d_ff2), BF16), pltpu.VMEM((D_MODEL, D_MODEL), BF16)])
    return pl.pallas_call(
        _expert_kernel,
        grid_spec=grid_spec,
        out_shape=jax.ShapeDtypeStruct((xs.shape[0], D_MODEL), F32),
        compiler_params=_cparams(("arbitrary",)),
        name="experts",
    )(blk_expert, n_used, xs, w_gu, b_gu, w_dn, b_dn)


def _combine_kernel(*refs, tm, npc, final):
    if final:
        (n8_ref, loff_ref, gbase_ref, x_ref, lpos_ref, mod_ref, fnw_ref, y_hbm,
         yp_ref, ys_ref, ybuf, sem) = refs
    else:
        n8_ref, loff_ref, gbase_ref, x_ref, lpos_ref, mod_ref, y_hbm, xo_ref, ybuf, sem = refs
    i = pl.program_id(0)
    nch = tm // CHUNK

    slot = i % 2

    def copies(tile, buf_slot, wait):
        def make_copy(buf_row, src_row, size):
            return pltpu.make_async_copy(y_hbm.at[pl.ds(src_row, size), :],
                                         ybuf.at[buf_slot, pl.ds(buf_row, size), :], sem.at[buf_slot])
        _group_copies(n8_ref, loff_ref, gbase_ref, tile, make_copy, wait=wait)

    @pl.when(i == 0)
    def _():
        ybuf[...] = jnp.zeros_like(ybuf)
        copies(i, slot, wait=False)

    @pl.when(i + 1 < pl.num_programs(0))
    def _():
        copies(i + 1, 1 - slot, wait=False)

    copies(i, slot, wait=True)

    lpos = lpos_ref[...]
    col = lax.broadcasted_iota(jnp.int32, (tm, SORT_ROWS), 1).astype(F32)
    unperm = jnp.zeros((tm, SORT_ROWS), F32)
    for kk in range(TOP_K):
        unperm = jnp.where(col == lpos[:, kk:kk + 1], 1.0, unperm)
    acc = _dot_sel_lhs2(unperm.astype(BF16), ybuf[slot])
    for c in range(nch):
        seq = _seq_row(i, nch, c, npc)
        gf = mod_ref[pl.ds(seq, 1), 5 * D_MODEL:6 * D_MODEL]
        rs = slice(c * CHUNK, (c + 1) * CHUNK)
        xn = x_ref[rs, :] + gf * acc[rs, :]
        if final:
            yn = _rms(xn) * fnw_ref[...]
            is_prompt = i < (npc * CHUNK) // tm

            @pl.when(is_prompt)
            def _():
                yp_ref[rs, :] = yn

            @pl.when(jnp.logical_not(is_prompt))
            def _():
                ys_ref[rs, :] = yn
        else:
            xo_ref[rs, :] = xn


def _combine_call(tables, x, lpos, mod_l, y_sorted, npc, final_w):
    ttot = x.shape[0]
    tm = DISPATCH_TILE
    final = final_w is not None
    npt = (npc * CHUNK) // tm
    tile = lambda i, *_: (i, 0)
    const = lambda i, *_: (0, 0)
    in_specs = [pl.BlockSpec((tm, D_MODEL), tile),
                pl.BlockSpec((tm, LANES), tile),
                pl.BlockSpec(mod_l.shape, const)]
    args = [x, lpos, mod_l]
    if final:
        in_specs.append(pl.BlockSpec((1, D_MODEL), const))
        args.append(final_w)
    in_specs.append(pl.BlockSpec(memory_space=pl.ANY))
    args.append(y_sorted)
    if final:
        out_specs = [pl.BlockSpec((tm, D_MODEL), lambda i, *_: (jnp.minimum(i, npt - 1), 0)),
                     pl.BlockSpec((tm, D_MODEL), lambda i, *_: (jnp.maximum(i - npt, 0), 0))]
        out_shape = [jax.ShapeDtypeStruct((npt * tm, D_MODEL), F32),
                     jax.ShapeDtypeStruct((ttot - npt * tm, D_MODEL), F32)]
    else:
        out_specs = [pl.BlockSpec((tm, D_MODEL), tile)]
        out_shape = [jax.ShapeDtypeStruct((ttot, D_MODEL), F32)]
    grid_spec = pltpu.PrefetchScalarGridSpec(
        num_scalar_prefetch=3, grid=(ttot // tm,),
        in_specs=in_specs, out_specs=out_specs,
        scratch_shapes=[pltpu.VMEM((2, SORT_ROWS, D_MODEL), F32), pltpu.SemaphoreType.DMA((2,))])
    return pl.pallas_call(
        functools.partial(_combine_kernel, tm=tm, npc=npc, final=final),
        grid_spec=grid_spec, out_shape=out_shape,
        compiler_params=_cparams(("arbitrary",)),
        name="combine",
    )(*tables, *args)


def kernel(x_prompt, x_sample, c_prompt, c_sample, state_mlstm_C, state_mlstm_n, state_mlstm_m, state_mlstm_conv, state_ret_S, state_hgrn_S, w_ada, b_ada, norm_mix_w, norm_ffn_w, final_norm_w, w_in_even, b_mlstm_i, b_mlstm_f, w_mlstm_conv, b_mlstm_conv, w_mlstm_q, w_mlstm_k, mlstm_skip, mlstm_norm_w, ret_norm_w, w_out_even, w_in_odd, hgrn_lb_logits, hgrn_norm_w, w_out_odd, moe_router_w, moe_router_b, moe_w_gate_up, moe_b_gate_up, moe_w_down, moe_b_down):
    bp, seq, d = x_prompt.shape
    bs, dseq, _ = x_sample.shape
    assert bp == 1 and d == D_MODEL and dseq == CHUNK
    assert seq % TOKEN_TILE == 0 and (bs * dseq) % TOKEN_TILE == 0
    depth = w_ada.shape[0]
    tp, tsmp = seq, bs * dseq
    ttot = tp + tsmp
    npc = tp // CHUNK
    he, ho = N_HEADS_EVEN, N_HEADS_ODD
    da = he * LANES
    past_len = 1024

    x = jnp.concatenate([x_prompt.reshape(tp, d), x_sample.reshape(tsmp, d)], axis=0)
    n_mod_rows = 2 * SUBLANES
    assert 1 + bs <= n_mod_rows
    c_all = jnp.zeros((n_mod_rows, d), F32).at[0:1].set(c_prompt).at[1:1 + bs].set(c_sample)
    mod = _ada_call(c_all, w_ada, b_ada)

    half = LANES // 2
    inv = ROPE_BASE ** (-jnp.arange(half, dtype=F32) / half)
    pos_all = jnp.concatenate([jnp.arange(tp, dtype=F32),
                               jnp.tile(past_len + jnp.arange(dseq, dtype=F32), bs)])
    ang = pos_all[:, None] * inv[None, :]
    cos_t = jnp.concatenate([jnp.cos(ang), jnp.cos(ang)], axis=-1)
    sin_t = jnp.concatenate([-jnp.sin(ang), jnp.sin(ang)], axis=-1)

    lb_p = jax.nn.softmax(hgrn_lb_logits.astype(F32), axis=0)
    lbs = jnp.cumsum(lb_p, axis=0) - lb_p[0]

    n_tiles = ttot // DISPATCH_TILE
    ntp = -(-n_tiles // LANES) * LANES
    max_rows = ttot * TOP_K + n_tiles * N_EXPERTS * (SUBLANES - 1)
    nb = -(-max_rows // EXPERT_ROWS) + N_EXPERTS
    nbp = -(-nb // SUBLANES) * SUBLANES
    xs = jnp.zeros((nb * EXPERT_ROWS, XS_WIDTH), F32)

    ts_p = TOKEN_TILE
    steps_p = tp // ts_p
    even_out, odd_out = [], []
    y_final = None
    for l in range(depth):
        jl = l // 2
        mod_l = mod[l]
        if l % 2 == 0:
            w_in = w_in_even[jl]
            w_main = jnp.concatenate([w_in[:, da:3 * da], w_in[:, 3 * da + 2 * he:]], axis=1).astype(BF16)
            w_gate = jnp.zeros((d, da + LANES), F32).at[:, :da].set(w_in[:, :da])
            w_gate = w_gate.at[:, da:da + 2 * he].set(w_in[:, 3 * da:3 * da + 2 * he])
            proj, gates = _inproj_call(x, mod_l, norm_mix_w[l][None], w_main, w_gate, npc)
            gbias = jnp.zeros((1, LANES), F32).at[0, :he].set(b_mlstm_i[jl]).at[0, he:2 * he].set(b_mlstm_f[jl])
            weights = [w_mlstm_conv[jl], b_mlstm_conv[jl][None],
                       w_mlstm_q[jl], w_mlstm_k[jl], gbias,
                       mlstm_skip[jl][None], mlstm_norm_w[jl][None], ret_norm_w[jl][None]]
            zeros_p = (jnp.zeros((1, he, LANES, LANES), F32), jnp.zeros((1, he, 1, LANES), F32),
                       jnp.zeros((1, 1, LANES), F32), jnp.zeros((1, SUBLANES, da), F32),
                       jnp.zeros((1, he, LANES, LANES), F32))
            st_s = (state_mlstm_C[jl], state_mlstm_n[jl][:, :, None, :],
                    jnp.zeros((bs, 1, LANES), F32).at[:, 0, :he].set(state_mlstm_m[jl]),
                    jnp.zeros((bs, SUBLANES, da), F32).at[:, SUBLANES - (CONV_W - 1):].set(state_mlstm_conv[jl]),
                    state_ret_S[jl])
            res_p = _even_scan_call(proj, gates, cos_t, sin_t, zeros_p, weights,
                                    ts=ts_p, n_seq=1, steps=steps_p, row_off=0)
            res_s = _even_scan_call(proj, gates, cos_t, sin_t, st_s, weights,
                                    ts=CHUNK, n_seq=bs, steps=1, row_off=npc)
            u_p, u_s = res_p[0], res_s[0]
            even_out.append((res_p[1:], res_s[1:]))
            w_out = w_out_even[jl].astype(BF16)
        else:
            proj = _inproj_call(x, mod_l, norm_mix_w[l][None], w_in_odd[jl].astype(BF16), None, npc)[0]
            lb = lbs[l][None]
            nw = hgrn_norm_w[jl][None]
            u_p, sp = _odd_scan_call(proj, jnp.zeros((1, ho, LANES, LANES), F32), lb, nw,
                                     ts=ts_p, n_seq=1, steps=steps_p, row_off=0)
            u_s, ss = _odd_scan_call(proj, state_hgrn_S[jl], lb, nw,
                                     ts=CHUNK, n_seq=bs, steps=1, row_off=npc)
            odd_out.append((sp, ss))
            w_out = w_out_odd[jl].astype(BF16)

        w_r = jnp.zeros((d, LANES), F32).at[:, :N_EXPERTS].set(moe_router_w[l])
        b_r = jnp.full((1, LANES), -jnp.inf, F32).at[0, :N_EXPERTS].set(moe_router_b[l])
        x, hf, gate, lpos, cnt = _post_call(x, u_p, u_s, mod_l, w_out, norm_ffn_w[l][None], w_r, b_r, npc)
        cnt_tiles = jnp.zeros((ntp, LANES), F32).at[:n_tiles].set(cnt[:, 0, :])
        n8, loff, gbase, blk, ends = _plan_call(cnt_tiles, nbp)
        tables = [t[:n_tiles, :N_EXPERTS].reshape(-1) for t in (n8, loff, gbase)]
        blk_expert = blk[:nb, 0]
        n_used = ends[0, N_EXPERTS - 1:N_EXPERTS]
        xs = _dispatch_call(tables, hf, lpos, gate, xs)
        y_sorted = _expert_call(blk_expert, n_used, xs, moe_w_gate_up, moe_b_gate_up[:, :, None, :],
                                moe_w_down, moe_b_down[:, :, None, :], l)
        if l == depth - 1:
            y_final = _combine_call(tables, x, lpos, mod_l, y_sorted, npc, final_norm_w[None])
        else:
            x = _combine_call(tables, x, lpos, mod_l, y_sorted, npc, None)[0]

    def even_states(which):
        cs = jnp.stack([e[which][0] for e in even_out])
        ns = jnp.stack([e[which][1][:, :, 0, :] for e in even_out])
        ms = jnp.stack([e[which][2][:, 0, :he] for e in even_out])
        cv = jnp.stack([e[which][3][:, SUBLANES - (CONV_W - 1):, :] for e in even_out])
        ss = jnp.stack([e[which][4] for e in even_out])
        return cs, ns, ms, cv, ss

    p_c, p_n, p_m, p_cv, p_s = even_states(0)
    s_c, s_n, s_m, s_cv, s_s = even_states(1)
    p_h = jnp.stack([o[0] for o in odd_out])
    s_h = jnp.stack([o[1] for o in odd_out])
    y_prompt = y_final[0].reshape(bp, seq, d)
    y_sample = y_final[1].reshape(bs, dseq, d)
    return (y_prompt, y_sample, p_c, p_n, p_m, p_cv, p_s, p_h, s_c, s_n, s_m, s_cv, s_s, s_h)
```

```python
import functools
import math

import jax
import jax.numpy as jnp
from jax import lax
from jax.experimental import pallas as pl
from jax.experimental.pallas import tpu as pltpu

F32 = jnp.float32
BF16 = jnp.bfloat16

CHUNK = 64
LANES = 128
SUBLANES = 8
D_MODEL = 1024
N_HEADS_EVEN = 4
N_HEADS_ODD = 8
CONV_W = 4
N_EXPERTS = 32
TOP_K = 4
SWIGLU_LIMIT = 7.0
SWIGLU_ALPHA = 1.702
EPS = 1e-6
ROPE_BASE = 10000.0

TOKEN_TILE = 512
INPROJ_TILE = 512
DISPATCH_TILE = 256
SORT_ROWS = DISPATCH_TILE * TOP_K + LANES * 2
XS_WIDTH = D_MODEL + LANES
EXPERT_ROWS = 512
EXPERT_STRIP = 128
VMEM_LIMIT = 56 * 1024 * 1024


def _cparams(sem, vmem=VMEM_LIMIT):
    return pltpu.CompilerParams(dimension_semantics=sem, vmem_limit_bytes=vmem)


def _dot(a, b):
    return jnp.dot(a, b, preferred_element_type=F32)


def _dot_nt(a, b):
    return lax.dot_general(a, b, (((1,), (1,)), ((), ())), preferred_element_type=F32)


def _dot_tn(a, b):
    return lax.dot_general(a, b, (((0,), (0,)), ((), ())), preferred_element_type=F32)


def _split3(x):
    p1 = x.astype(BF16)
    r1 = x - p1.astype(F32)
    p2 = r1.astype(BF16)
    p3 = (r1 - p2.astype(F32)).astype(BF16)
    return p1, p2, p3


def _dot3(a, b):
    ah = a.astype(BF16)
    al = (a - ah.astype(F32)).astype(BF16)
    bh = b.astype(BF16)
    bl = (b - bh.astype(F32)).astype(BF16)
    return _dot(ah, bh) + (_dot(ah, bl) + _dot(al, bh))


def _dot3_tn_fused(a, b):
    n = a.shape[1]
    ah = a.astype(BF16)
    al = (a - ah.astype(F32)).astype(BF16)
    bh = b.astype(BF16)
    bl = (b - bh.astype(F32)).astype(BF16)
    x = _dot_tn(jnp.concatenate([ah, al], axis=1), jnp.concatenate([bh, bl], axis=1))
    return x[:n, :n] + (x[:n, n:] + x[n:, :n])


def _block_diag(a, b):
    z = jnp.zeros_like(a)
    return jnp.concatenate([jnp.concatenate([a, z], axis=1), jnp.concatenate([z, b], axis=1)], axis=0)


def _dot_sel_lhs(sel_bf16, x):
    p1, p2, p3 = _split3(x)
    return _dot(sel_bf16, p1) + (_dot(sel_bf16, p2) + _dot(sel_bf16, p3))


def _dot_sel_nt(sel_bf16, x):
    p1, p2, p3 = _split3(x)
    return _dot_nt(sel_bf16, p1) + (_dot_nt(sel_bf16, p2) + _dot_nt(sel_bf16, p3))


def _dot_sel_rhs(x, sel_bf16):
    p1, p2, p3 = _split3(x)
    return _dot(p1, sel_bf16) + (_dot(p2, sel_bf16) + _dot(p3, sel_bf16))


def _dot_sel_lhs2(sel_bf16, x):
    hi = x.astype(BF16)
    lo = (x - hi.astype(F32)).astype(BF16)
    return _dot(sel_bf16, hi) + _dot(sel_bf16, lo)


def _sigmoid(x):
    return 1.0 / (1.0 + jnp.exp(-x))


def _log_sigmoid(x):
    return jnp.minimum(x, 0.0) - jnp.log1p(jnp.exp(-jnp.abs(x)))


def _rms(x):
    return x * lax.rsqrt(jnp.mean(x * x, axis=-1, keepdims=True) + EPS)


def _chunk_tri(n):
    r = lax.broadcasted_iota(jnp.int32, (n, n), 0)
    c = lax.broadcasted_iota(jnp.int32, (n, n), 1)
    return jnp.where((r // CHUNK == c // CHUNK) & (c <= r), 1.0, 0.0).astype(BF16)


def _chunk_cumsum(x):
    rows, n = x.shape
    vregs = CHUNK // SUBLANES
    x4 = x.reshape(rows // CHUNK, vregs, SUBLANES, n)
    sub = lax.broadcasted_iota(jnp.int32, x4.shape, 2)
    s = x4
    for shift in (1, 2, 4):
        s = s + jnp.where(sub >= shift, pltpu.roll(s, shift, 2), 0.0)
    outs, carry = [], None
    for v in range(vregs):
        cur = s[:, v] if carry is None else s[:, v] + carry
        outs.append(cur)
        carry = jnp.broadcast_to(cur[:, SUBLANES - 1:SUBLANES, :], cur.shape)
    return jnp.stack(outs, axis=1).reshape(rows, n)


def _seq_row(tile_idx, chunks_per_tile, c, n_prompt_chunks):
    return jnp.maximum(tile_idx * chunks_per_tile + c - (n_prompt_chunks - 1), 0)


def _ada_kernel(c_ref, w_ref, b_ref, o_ref):
    c = c_ref[...]
    o_ref[0] = _dot3(c * _sigmoid(c), w_ref[0]) + b_ref[0]


def _ada_call(c_all, w_ada, b_ada):
    depth = w_ada.shape[0]
    nrow = c_all.shape[0]
    ncol = w_ada.shape[2] // D_MODEL
    return pl.pallas_call(
        _ada_kernel,
        grid=(depth, ncol),
        in_specs=[pl.BlockSpec((nrow, D_MODEL), lambda l, j: (0, 0)),
                  pl.BlockSpec((1, D_MODEL, D_MODEL), lambda l, j: (l, 0, j)),
                  pl.BlockSpec((1, 1, D_MODEL), lambda l, j: (l, 0, j))],
        out_specs=pl.BlockSpec((1, nrow, D_MODEL), lambda l, j: (l, 0, j)),
        out_shape=jax.ShapeDtypeStruct((depth, nrow, ncol * D_MODEL), F32),
        compiler_params=_cparams(("arbitrary", "arbitrary")),
        name="ada",
    )(c_all, w_ada, b_ada.reshape(depth, 1, -1))


def _inproj_kernel(*refs, tm, ng, npc, n_hp):
    if n_hp:
        x_ref, mod_ref, nw_ref, w_ref, whp_ref, proj_ref, gates_ref, h_scr = refs
    else:
        x_ref, mod_ref, nw_ref, w_ref, proj_ref, h_scr = refs
    i = pl.program_id(0)
    nch = tm // CHUNK
    for c in range(nch):
        seq = _seq_row(i, nch, c, npc)
        sh = mod_ref[pl.ds(seq, 1), 0:D_MODEL]
        sc = mod_ref[pl.ds(seq, 1), D_MODEL:2 * D_MODEL]
        xc = x_ref[c * CHUNK:(c + 1) * CHUNK, :]
        h_scr[c * CHUNK:(c + 1) * CHUNK, :] = _rms(xc) * nw_ref[...] * (1.0 + sc) + sh
    h = h_scr[...]
    hb = h.astype(BF16)
    for g in range(0, ng - n_hp, 4):
        res = _dot(hb, w_ref[:, g * LANES:(g + 4) * LANES])
        for jj in range(4):
            proj_ref[n_hp + g + jj] = res[:, jj * LANES:(jj + 1) * LANES]
    if n_hp:
        res = _dot3(h, whp_ref[...])
        for jj in range(n_hp):
            proj_ref[jj] = res[:, jj * LANES:(jj + 1) * LANES]
        gates_ref[...] = res[:, n_hp * LANES:]


def _inproj_call(x, mod_l, nw, w_main, w_gate, npc):
    ttot = x.shape[0]
    tm = INPROJ_TILE
    has_gates = w_gate is not None
    n_hp = w_gate.shape[1] // LANES - 1 if has_gates else 0
    ng = w_main.shape[1] // LANES + n_hp
    in_specs = [pl.BlockSpec((tm, D_MODEL), lambda i: (i, 0)),
                pl.BlockSpec(mod_l.shape, lambda i: (0, 0)),
                pl.BlockSpec((1, D_MODEL), lambda i: (0, 0)),
                pl.BlockSpec(w_main.shape, lambda i: (0, 0))]
    out_specs = [pl.BlockSpec((ng, tm, LANES), lambda i: (0, i, 0))]
    out_shape = [jax.ShapeDtypeStruct((ng, ttot, LANES), F32)]
    args = [x, mod_l, nw, w_main]
    if has_gates:
        in_specs.append(pl.BlockSpec(w_gate.shape, lambda i: (0, 0)))
        out_specs.append(pl.BlockSpec((tm, LANES), lambda i: (i, 0)))
        out_shape.append(jax.ShapeDtypeStruct((ttot, LANES), F32))
        args.append(w_gate)
    return pl.pallas_call(
        functools.partial(_inproj_kernel, tm=tm, ng=ng, npc=npc, n_hp=n_hp),
        grid=(ttot // tm,),
        in_specs=in_specs, out_specs=out_specs, out_shape=out_shape,
        scratch_shapes=[pltpu.VMEM((tm, D_MODEL), F32)],
        compiler_params=_cparams(("arbitrary",)),
        name="inproj",
    )(*args)


_G_XM, _G_VA, _G_OA, _G_QB, _G_KB, _G_VB, _G_GB = 0, 4, 8, 12, 16, 20, 24


def _even_scan_kernel(proj_ref, gates_ref, cos_ref, sin_ref,
                      c0_ref, n0_ref, m0_ref, conv0_ref, s0_ref,
                      cw_ref, cb_ref, wq_ref, wk_ref, gbias_ref, skip_ref, nwa_ref, nwb_ref,
                      u_ref, co_ref, no_ref, mo_ref, convo_ref, so_ref,
                      c_scr, n_scr, m_scr, conv_scr, s_scr,
                      xbuf, xc_scr, q_scr, k_scr, qr_scr, kr_scr,
                      gl_scr, bc_scr, rows_scr, dec_scr, *, ts):
    H = N_HEADS_EVEN
    j = pl.program_id(1)
    nj = pl.num_programs(1)
    nc = ts // CHUNK

    @pl.when(j == 0)
    def _():
        for p in range(H // 2):
            c_scr[p] = _block_diag(c0_ref[0, 2 * p], c0_ref[0, 2 * p + 1])
            s_scr[p] = _block_diag(s0_ref[0, 2 * p], s0_ref[0, 2 * p + 1])
        n_scr[...] = n0_ref[0]
        m_scr[...] = m0_ref[0]
        conv_scr[...] = conv0_ref[0]

    for g in range(H):
        lo, hi = g * LANES, (g + 1) * LANES
        x_g = proj_ref[_G_XM + g]
        xbuf[0:SUBLANES, :] = conv_scr[:, lo:hi]
        xbuf[SUBLANES:SUBLANES + ts, :] = x_g
        acc = cb_ref[:, lo:hi] + cw_ref[CONV_W - 1:CONV_W, lo:hi] * x_g
        for t in range(CONV_W - 1):
            off = SUBLANES - (CONV_W - 1) + t
            acc = acc + cw_ref[t:t + 1, lo:hi] * xbuf[off:off + ts, :]
        conv_scr[:, lo:hi] = xbuf[ts:ts + SUBLANES, :]
        xc = acc * _sigmoid(acc)
        xc_scr[g] = xc
        q_scr[g] = _dot3(xc, wq_ref[g])
        k_scr[g] = _dot3(xc, wk_ref[g]) * (LANES ** -0.5)

    cosv = cos_ref[...]
    sinv = sin_ref[...]
    for g in range(H):
        qb = proj_ref[_G_QB + g]
        kb = proj_ref[_G_KB + g]
        qr_scr[g] = qb * cosv + pltpu.roll(qb, LANES // 2, 1) * sinv
        kr_scr[g] = (kb * cosv + pltpu.roll(kb, LANES // 2, 1) * sinv) * (LANES ** -0.5)

    gpre = gates_ref[...] + gbias_ref[...]
    lane = lax.broadcasted_iota(jnp.int32, (ts, LANES), 1)
    gl = jnp.where(lane < H, gpre, _log_sigmoid(gpre))
    gl_scr[...] = gl
    bc = _dot_sel_lhs(_chunk_tri(ts), gl)
    bc_scr[...] = bc
    comb = jnp.where(lane < H, gl, bc)
    even_head = (lane % 2) == 0
    comb_even = jnp.where(even_head, comb, 0.0)
    comb_odd = jnp.where(even_head, 0.0, comb)
    pr = lax.broadcasted_iota(jnp.int32, (SUBLANES, LANES), 0)
    pc = lax.broadcasted_iota(jnp.int32, (SUBLANES, LANES), 1)
    pair_sel = jnp.where((pc // 2 == pr) & (pc < 2 * H), 1.0, 0.0).astype(BF16)
    for c in range(nc):
        cs = slice(c * CHUNK, (c + 1) * CHUNK)
        rows_scr[c] = _dot_sel_nt(pair_sel, jnp.concatenate([comb_even[cs, :], comb_odd[cs, :]], axis=0))

    ti = lax.broadcasted_iota(jnp.int32, (CHUNK, LANES), 0)
    lane2 = lax.broadcasted_iota(jnp.int32, (CHUNK, LANES), 1)
    left = lane2 < CHUNK
    si = lane2 % CHUNK
    tril = ti >= si
    left_wide = lax.broadcasted_iota(jnp.int32, (CHUNK, 2 * LANES), 1) < LANES
    br = lax.broadcasted_iota(jnp.int32, (2 * LANES, 2 * LANES), 0) < LANES
    bcol_blk = lax.broadcasted_iota(jnp.int32, (2 * LANES, 2 * LANES), 1) < LANES
    tcol = lax.broadcasted_iota(jnp.int32, (CHUNK, 1), 0).astype(F32)
    log_gamma = [math.log1p(-2.0 ** (-5 - h)) for h in range(H)]
    for p in range(H // 2):
        lg2 = jnp.where(left, log_gamma[2 * p], log_gamma[2 * p + 1])
        dec_scr[p] = jnp.where(tril, jnp.exp((ti - si).astype(F32) * lg2), 0.0)

    def chunk_body(c, carry):
        r0 = pl.multiple_of(c * CHUNK, CHUNK)
        rows = pl.ds(r0, CHUNK)
        pair_rows = rows_scr[c]
        glc = gl_scr[rows, :]
        bcc = bc_scr[rows, :]
        m_all = m_scr[...]
        c_old = [c_scr[p] for p in range(H // 2)]
        s_old = [s_scr[p] for p in range(H // 2)]
        n_old = [n_scr[h] for h in range(H)]
        c_new, s_new, n_new = [], [], []
        m_next = m_all
        lane_row = lax.broadcasted_iota(jnp.int32, (1, LANES), 1)
        for p in range(H // 2):
            hs = (2 * p, 2 * p + 1)
            q = [q_scr[h, rows, :] for h in hs]
            k = [k_scr[h, rows, :] for h in hs]
            v = [proj_ref[_G_VA + h, rows, :] for h in hs]
            b_col = [bcc[:, H + h:H + h + 1] for h in hs]
            i_col = [glc[:, h:h + 1] for h in hs]
            m_prev = [m_all[:, h:h + 1] for h in hs]
            q2 = jnp.concatenate(q, axis=1).astype(BF16)
            smat = _dot_nt(q2, _block_diag(k[0].astype(BF16), k[1].astype(BF16)))
            dmat = jnp.where(tril, jnp.where(left, b_col[0], b_col[1])
                             - pair_rows[2 + p:3 + p, :] + pair_rows[p:p + 1, :], -jnp.inf)
            a = [jnp.max(jnp.where(left, dmat, -jnp.inf), axis=-1, keepdims=True),
                 jnp.max(jnp.where(left, -jnp.inf, dmat), axis=-1, keepdims=True)]
            inter = [b_col[e] + m_prev[e] for e in range(2)]
            m_t = [jnp.maximum(inter[e], a[e]) for e in range(2)]
            w_inter = [jnp.exp(inter[e] - m_t[e]) for e in range(2)]
            amat = smat * jnp.exp(dmat - jnp.where(left, m_t[0], m_t[1]))
            cst = c_old[p]
            num = (_dot(amat.astype(BF16), _block_diag(v[0].astype(BF16), v[1].astype(BF16)))
                   + jnp.where(left_wide, w_inter[0], w_inter[1]) * _dot(q2, cst.astype(BF16)))
            den = [jnp.sum(jnp.where(left, amat, 0.0), axis=-1, keepdims=True),
                   jnp.sum(jnp.where(left, 0.0, amat), axis=-1, keepdims=True)]
            den = [jnp.maximum(jnp.abs(den[e] + w_inter[e] * jnp.sum(q[e] * n_old[hs[e]], axis=-1, keepdims=True)),
                               jnp.exp(-m_t[e])) for e in range(2)]
            hout = num / jnp.where(left_wide, den[0], den[1])
            decay, upd = [], []
            for e in range(2):
                m_new = m_t[e][CHUNK - 1:CHUNK, :]
                b_last = b_col[e][CHUNK - 1:CHUNK, :]
                kw = k[e] * jnp.exp(b_last - b_col[e] + i_col[e] - m_new)
                decay.append(jnp.exp(b_last + m_prev[e] - m_new))
                upd.append(_dot3_tn_fused(kw, v[e]))
                n_new.append(decay[e] * n_old[hs[e]] + jnp.sum(kw, axis=0, keepdims=True))
                m_next = jnp.where(lane_row == hs[e], m_new, m_next)
            c_new.append(jnp.where(br, decay[0], decay[1]) * cst + _block_diag(upd[0], upd[1]))
            for e in range(2):
                h = hs[e]
                lo, hi = h * LANES, (h + 1) * LANES
                z = _sigmoid(proj_ref[_G_OA + h, rows, :]) * hout[:, e * LANES:(e + 1) * LANES]
                u_ref[rows, lo:hi] = (_rms(z) * nwa_ref[:, lo:hi]
                                      + skip_ref[:, lo:hi] * xc_scr[h, rows, :])
        for p in range(H // 2):
            hs = (2 * p, 2 * p + 1)
            lg = [log_gamma[h] for h in hs]
            kr = [kr_scr[h, rows, :] for h in hs]
            vb = [proj_ref[_G_VB + h, rows, :].astype(BF16) for h in hs]
            q2 = jnp.concatenate([qr_scr[h, rows, :] for h in hs], axis=1).astype(BF16)
            amat = _dot_nt(q2, _block_diag(kr[0].astype(BF16), kr[1].astype(BF16))) * dec_scr[p]
            sst = s_old[p]
            o = (_dot(amat.astype(BF16), _block_diag(vb[0], vb[1]))
                 + jnp.exp((tcol + 1.0) * jnp.where(left_wide, lg[0], lg[1])) * _dot(q2, sst.astype(BF16)))
            kws = jnp.concatenate([kr[e] * jnp.exp((CHUNK - 1.0 - tcol) * lg[e]) for e in range(2)], axis=1)
            cross = _dot_tn(kws.astype(BF16), jnp.concatenate(vb, axis=1))
            s_new.append(jnp.where(br, math.exp(CHUNK * lg[0]), math.exp(CHUNK * lg[1])) * sst
                         + jnp.where(br == bcol_blk, cross, 0.0))
            for e in range(2):
                h = hs[e]
                lo, hi = h * LANES, (h + 1) * LANES
                gate = proj_ref[_G_GB + h, rows, :]
                u_ref[rows, D_MODEL // 2 + lo:D_MODEL // 2 + hi] = (
                    _rms(o[:, e * LANES:(e + 1) * LANES]) * nwb_ref[:, lo:hi] * (gate * _sigmoid(gate)))
        for p in range(H // 2):
            c_scr[p] = c_new[p]
            s_scr[p] = s_new[p]
        for h in range(H):
            n_scr[h] = n_new[h]
        m_scr[...] = m_next
        return carry

    lax.fori_loop(0, nc, chunk_body, 0)

    @pl.when(j == nj - 1)
    def _():
        for p in range(H // 2):
            for e in range(2):
                blk = slice(e * LANES, (e + 1) * LANES)
                co_ref[0, 2 * p + e] = c_scr[p, blk, blk]
                so_ref[0, 2 * p + e] = s_scr[p, blk, blk]
        no_ref[0] = n_scr[...]
        mo_ref[0] = m_scr[...]
        convo_ref[0] = conv_scr[...]


def _even_scan_call(proj, gates, cos_t, sin_t, states, weights, *, ts, n_seq, steps, row_off):
    H = N_HEADS_EVEN
    c0, n0, m0, conv0, s0 = states
    ng = proj.shape[0]
    rows_idx = lambda b, j: (row_off + b * steps + j, 0)
    state_specs = [pl.BlockSpec((1, H, LANES, LANES), lambda b, j: (b, 0, 0, 0)),
                   pl.BlockSpec((1, H, 1, LANES), lambda b, j: (b, 0, 0, 0)),
                   pl.BlockSpec((1, 1, LANES), lambda b, j: (b, 0, 0)),
                   pl.BlockSpec((1, SUBLANES, H * LANES), lambda b, j: (b, 0, 0)),
                   pl.BlockSpec((1, H, LANES, LANES), lambda b, j: (b, 0, 0, 0))]
    in_specs = [pl.BlockSpec((ng, ts, LANES), lambda b, j: (0, row_off + b * steps + j, 0)),
                pl.BlockSpec((ts, LANES), rows_idx),
                pl.BlockSpec((ts, LANES), rows_idx),
                pl.BlockSpec((ts, LANES), rows_idx)] + state_specs
    for w in weights:
        in_specs.append(pl.BlockSpec(w.shape, functools.partial(lambda nd, b, j: (0,) * nd, w.ndim)))
    out_specs = [pl.BlockSpec((ts, D_MODEL), lambda b, j: (b * steps + j, 0))] + state_specs
    out_shape = [jax.ShapeDtypeStruct((n_seq * steps * ts, D_MODEL), F32),
                 jax.ShapeDtypeStruct(c0.shape, F32), jax.ShapeDtypeStruct(n0.shape, F32),
                 jax.ShapeDtypeStruct(m0.shape, F32), jax.ShapeDtypeStruct(conv0.shape, F32),
                 jax.ShapeDtypeStruct(s0.shape, F32)]
    nc = ts // CHUNK
    pair_state = pltpu.VMEM((H // 2, 2 * LANES, 2 * LANES), F32)
    scratch = [pair_state, pltpu.VMEM((H, 1, LANES), F32),
               pltpu.VMEM((1, LANES), F32), pltpu.VMEM((SUBLANES, H * LANES), F32),
               pair_state,
               pltpu.VMEM((ts + 2 * SUBLANES, LANES), F32),
               pltpu.VMEM((H, ts, LANES), F32), pltpu.VMEM((H, ts, LANES), F32),
               pltpu.VMEM((H, ts, LANES), F32), pltpu.VMEM((H, ts, LANES), F32),
               pltpu.VMEM((H, ts, LANES), F32),
               pltpu.VMEM((ts, LANES), F32), pltpu.VMEM((ts, LANES), F32),
               pltpu.VMEM((nc, SUBLANES, LANES), F32),
               pltpu.VMEM((H // 2, CHUNK, LANES), F32)]
    return pl.pallas_call(
        functools.partial(_even_scan_kernel, ts=ts),
        grid=(n_seq, steps),
        in_specs=in_specs, out_specs=out_specs, out_shape=out_shape,
        scratch_shapes=scratch,
        compiler_params=_cparams(("arbitrary", "arbitrary")),
        name="even_scan",
    )(proj, gates, cos_t, sin_t, c0, n0, m0, conv0, s0, *weights)


_G_Q, _G_F, _G_I, _G_G = 0, 8, 16, 24


def _hgrn_intra_pair(q, k, bcum, ti, si, left, tcol_i):
    amat = jnp.zeros((CHUNK, LANES), F32)
    for b in (32, 16, 8):
        nb2 = CHUNK // (2 * b)
        upper = ((tcol_i // b) % 2) == 1
        ql, kl = [], []
        for e in range(2):
            parts = [jnp.broadcast_to(bcum[e][m * 2 * b + b - 1:m * 2 * b + b, :], (2 * b, LANES))
                     for m in range(nb2)]
            ref = parts[0] if nb2 == 1 else jnp.concatenate(parts, axis=0)
            ql.append(jnp.where(upper, q[e] * jnp.exp(bcum[e] - ref), 0.0).astype(BF16))
            kl.append(jnp.where(upper, 0.0, k[e] * jnp.exp(ref - bcum[e])).astype(BF16))
        al = _dot_nt(jnp.concatenate(ql, axis=1), _block_diag(kl[0], kl[1]))
        amat = amat + jnp.where((ti // (2 * b)) == (si // (2 * b)), al, 0.0)
    nblk = CHUNK // SUBLANES
    b3 = [x.reshape(nblk, SUBLANES, LANES) for x in bcum]
    k3 = [x.reshape(nblk, SUBLANES, LANES) for x in k]
    for jj in range(SUBLANES):
        col = []
        for e in range(2):
            bj = jnp.broadcast_to(b3[e][:, jj:jj + 1, :], (nblk, SUBLANES, LANES)).reshape(CHUNK, LANES)
            kj = jnp.broadcast_to(k3[e][:, jj:jj + 1, :], (nblk, SUBLANES, LANES)).reshape(CHUNK, LANES)
            col.append(jnp.sum(q[e] * kj * jnp.exp(bcum[e] - bj), axis=-1, keepdims=True))
        sel = (si == (ti // SUBLANES) * SUBLANES + jj) & ((ti % SUBLANES) >= jj)
        amat = jnp.where(sel, jnp.where(left, col[0], col[1]), amat)
    return amat


def _odd_scan_kernel(proj_ref, s0_ref, lb_ref, nw_ref, u_ref, so_ref,
                     st_scr, k_scr, bc_scr, *, ts):
    H = N_HEADS_ODD
    j = pl.program_id(1)
    nj = pl.num_programs(1)
    nc = ts // CHUNK

    @pl.when(j == 0)
    def _():
        for p in range(H // 2):
            st_scr[p] = _block_diag(s0_ref[0, 2 * p].T, s0_ref[0, 2 * p + 1].T)

    for h in range(H):
        lo, hi = h * LANES, (h + 1) * LANES
        lbv = lb_ref[:, lo:hi]
        fpre = proj_ref[_G_F + h]
        e = jnp.exp(-jnp.abs(fpre))
        one_e = 1.0 + e
        log_lb = jnp.log(lbv)
        log_rest = jnp.log1p(-lbv) + (jnp.minimum(fpre, 0.0) - jnp.log(one_e))
        logf = jnp.maximum(log_lb, log_rest) + jnp.log(1.0 + jnp.exp(-jnp.abs(log_lb - log_rest)))
        k_scr[h] = (1.0 - lbv) * (jnp.where(fpre >= 0.0, e, 1.0) / one_e)
        bc_scr[h] = _chunk_cumsum(logf)

    ti = lax.broadcasted_iota(jnp.int32, (CHUNK, LANES), 0)
    lane2 = lax.broadcasted_iota(jnp.int32, (CHUNK, LANES), 1)
    left = lane2 < CHUNK
    si = lane2 % CHUNK
    tcol_i = lax.broadcasted_iota(jnp.int32, (CHUNK, 1), 0)
    same_head = ((lax.broadcasted_iota(jnp.int32, (2 * LANES, 2 * LANES), 0) < LANES)
                 == (lax.broadcasted_iota(jnp.int32, (2 * LANES, 2 * LANES), 1) < LANES))

    def chunk_body(c, carry):
        r0 = pl.multiple_of(c * CHUNK, CHUNK)
        rows = pl.ds(r0, CHUNK)
        st_old = [st_scr[p] for p in range(H // 2)]
        st_new = []
        for p in range(H // 2):
            hs = (2 * p, 2 * p + 1)
            q = [proj_ref[_G_Q + h, rows, :] for h in hs]
            k = [k_scr[h, rows, :] for h in hs]
            vb = [proj_ref[_G_I + h, rows, :].astype(BF16) for h in hs]
            bcum = [bc_scr[h, rows, :] for h in hs]
            amat = _hgrn_intra_pair(q, k, bcum, ti, si, left, tcol_i)
            st = st_old[p]
            qg = jnp.concatenate([q[e] * jnp.exp(bcum[e]) for e in range(2)], axis=1)
            o = (_dot(amat.astype(BF16), _block_diag(vb[0], vb[1]))
                 + _dot_nt(qg.astype(BF16), st.astype(BF16)))
            last = [bcum[e][CHUNK - 1:CHUNK, :] for e in range(2)]
            kd = jnp.concatenate([k[e] * jnp.exp(last[e] - bcum[e]) for e in range(2)], axis=1)
            cross = _dot_tn(jnp.concatenate(vb, axis=1), kd.astype(BF16))
            st_new.append(st * jnp.exp(jnp.concatenate(last, axis=1)) + jnp.where(same_head, cross, 0.0))
            for e in range(2):
                h = hs[e]
                lo, hi = h * LANES, (h + 1) * LANES
                u_ref[rows, lo:hi] = (_rms(o[:, e * LANES:(e + 1) * LANES]) * nw_ref[:, lo:hi]
                                      * _sigmoid(proj_ref[_G_G + h, rows, :]))
        for p in range(H // 2):
            st_scr[p] = st_new[p]
        return carry

    lax.fori_loop(0, nc, chunk_body, 0)

    @pl.when(j == nj - 1)
    def _():
        for p in range(H // 2):
            for e in range(2):
                blk = slice(e * LANES, (e + 1) * LANES)
                so_ref[0, 2 * p + e] = st_scr[p, blk, blk].T


def _odd_scan_call(proj, s0, lb, nw, *, ts, n_seq, steps, row_off):
    H = N_HEADS_ODD
    ng = proj.shape[0]
    st_spec = pl.BlockSpec((1, H, LANES, LANES), lambda b, j: (b, 0, 0, 0))
    return pl.pallas_call(
        functools.partial(_odd_scan_kernel, ts=ts),
        grid=(n_seq, steps),
        in_specs=[pl.BlockSpec((ng, ts, LANES), lambda b, j: (0, row_off + b * steps + j, 0)),
                  st_spec,
                  pl.BlockSpec((1, D_MODEL), lambda b, j: (0, 0)),
                  pl.BlockSpec((1, D_MODEL), lambda b, j: (0, 0))],
        out_specs=[pl.BlockSpec((ts, D_MODEL), lambda b, j: (b * steps + j, 0)), st_spec],
        out_shape=[jax.ShapeDtypeStruct((n_seq * steps * ts, D_MODEL), F32),
                   jax.ShapeDtypeStruct(s0.shape, F32)],
        scratch_shapes=[pltpu.VMEM((H // 2, 2 * LANES, 2 * LANES), F32),
                        pltpu.VMEM((H, ts, LANES), F32),
                        pltpu.VMEM((H, ts, LANES), F32)],
        compiler_params=_cparams(("arbitrary", "arbitrary")),
        name="odd_scan",
    )(proj, s0, lb, nw)


def _post_kernel(x_ref, up_ref, us_ref, mod_ref, wout_ref, nw_ref, wr_ref, br_ref,
                 xo_ref, hf_ref, gate_ref, lpos_ref, cnt_ref,
                 *, tm, npc, n_prompt_tiles):
    i = pl.program_id(0)
    nch = tm // CHUNK

    u = jnp.where(i < n_prompt_tiles, up_ref[...], us_ref[...])
    y = _dot(u.astype(BF16), wout_ref[...])
    for c in range(nch):
        seq = _seq_row(i, nch, c, npc)
        gm = mod_ref[pl.ds(seq, 1), 2 * D_MODEL:3 * D_MODEL]
        shf = mod_ref[pl.ds(seq, 1), 3 * D_MODEL:4 * D_MODEL]
        scf = mod_ref[pl.ds(seq, 1), 4 * D_MODEL:5 * D_MODEL]
        rs = slice(c * CHUNK, (c + 1) * CHUNK)
        xn = x_ref[rs, :] + gm * y[rs, :]
        xo_ref[rs, :] = xn
        hf_ref[rs, :] = _rms(xn) * nw_ref[...] * (1.0 + scf) + shf

    logits = _dot3(hf_ref[...], wr_ref[...]) + br_ref[...]
    lane_i = lax.broadcasted_iota(jnp.int32, (tm, LANES), 1)
    lane_f = lane_i.astype(F32)
    vals, idxs = [], []
    cur = logits
    for _ in range(TOP_K):
        m = jnp.max(cur, axis=-1, keepdims=True)
        idx = jnp.min(jnp.where(cur == m, lane_f, float(LANES)), axis=-1, keepdims=True)
        vals.append(m)
        idxs.append(idx)
        cur = jnp.where(lane_f == idx, -jnp.inf, cur)
    exps = [jnp.exp(v - vals[0]) for v in vals]
    denom = exps[0] + exps[1] + exps[2] + exps[3]
    onehot = jnp.zeros((tm, LANES), F32)
    for idx in idxs:
        onehot = onehot + jnp.where(lane_f == idx, 1.0, 0.0)
    r = lax.broadcasted_iota(jnp.int32, (tm, tm), 0)
    cidx = lax.broadcasted_iota(jnp.int32, (tm, tm), 1)
    strict = jnp.where((cidx < r) & (cidx // DISPATCH_TILE == r // DISPATCH_TILE), 1.0, 0.0).astype(BF16)
    before = _dot(strict, onehot.astype(BF16))
    er = lax.broadcasted_iota(jnp.int32, (LANES, LANES), 0)
    ec = lax.broadcasted_iota(jnp.int32, (LANES, LANES), 1)
    lower_experts = jnp.where(er < ec, 1.0, 0.0).astype(BF16)
    pos_parts = []
    for s in range(tm // DISPATCH_TILE):
        rs = slice(s * DISPATCH_TILE, (s + 1) * DISPATCH_TILE)
        cnt = jnp.sum(onehot[rs, :], axis=0, keepdims=True)
        cnt_ref[s] = cnt
        n8 = jnp.floor((cnt + (SUBLANES - 1.0)) * (1.0 / SUBLANES)) * float(SUBLANES)
        run_start = _dot(jnp.broadcast_to(n8, (SUBLANES, LANES)).astype(BF16), lower_experts)[0:1, :]
        pos_parts.append(before[rs, :] + run_start)
    posmat = jnp.concatenate(pos_parts, axis=0)
    gate_o = jnp.zeros((tm, LANES), F32)
    lpos_o = jnp.zeros((tm, LANES), F32)
    for kk in range(TOP_K):
        lp = jnp.sum(jnp.where(lane_f == idxs[kk], posmat, 0.0), axis=-1, keepdims=True)
        gate_o = jnp.where(lane_i == kk, exps[kk] / denom, gate_o)
        lpos_o = jnp.where(lane_i == kk, lp, lpos_o)
    gate_ref[...] = gate_o
    lpos_ref[...] = lpos_o


def _post_call(x, u_p, u_s, mod_l, w_out, nw, w_r, b_r, npc):
    ttot = x.shape[0]
    tm = TOKEN_TILE
    npt = u_p.shape[0] // tm
    sub = tm // DISPATCH_TILE
    tile = lambda i: (i, 0)
    const = lambda i: (0, 0)
    return pl.pallas_call(
        functools.partial(_post_kernel, tm=tm, npc=npc, n_prompt_tiles=npt),
        grid=(ttot // tm,),
        in_specs=[pl.BlockSpec((tm, D_MODEL), tile),
                  pl.BlockSpec((tm, D_MODEL), lambda i: (jnp.minimum(i, npt - 1), 0)),
                  pl.BlockSpec((tm, D_MODEL), lambda i: (jnp.maximum(i - npt, 0), 0)),
                  pl.BlockSpec(mod_l.shape, const),
                  pl.BlockSpec(w_out.shape, const),
                  pl.BlockSpec((1, D_MODEL), const),
                  pl.BlockSpec(w_r.shape, const),
                  pl.BlockSpec((1, LANES), const)],
        out_specs=[pl.BlockSpec((tm, D_MODEL), tile), pl.BlockSpec((tm, D_MODEL), tile),
                   pl.BlockSpec((tm, LANES), tile), pl.BlockSpec((tm, LANES), tile),
                   pl.BlockSpec((sub, 1, LANES), lambda i: (i, 0, 0))],
        out_shape=[jax.ShapeDtypeStruct((ttot, D_MODEL), F32), jax.ShapeDtypeStruct((ttot, D_MODEL), F32),
                   jax.ShapeDtypeStruct((ttot, LANES), F32), jax.ShapeDtypeStruct((ttot, LANES), F32),
                   jax.ShapeDtypeStruct((ttot // DISPATCH_TILE, 1, LANES), F32)],
        compiler_params=_cparams(("arbitrary",)),
        name="post",
    )(x, u_p, u_s, mod_l, w_out, nw, w_r, b_r)


def _plan_kernel(cnt_ref, n8_ref, loff_ref, gbase_ref, blk_ref, ends_ref, strips_ref, *, nbp, ntp):
    cnt = cnt_ref[...]
    n8 = jnp.floor((cnt + (SUBLANES - 1.0)) * (1.0 / SUBLANES)) * float(SUBLANES)
    r = lax.broadcasted_iota(jnp.int32, (LANES, LANES), 0)
    c = lax.broadcasted_iota(jnp.int32, (LANES, LANES), 1)
    loff = _dot_sel_rhs(n8, jnp.where(r < c, 1.0, 0.0).astype(BF16))
    gtot = jnp.broadcast_to(jnp.sum(n8, axis=0, keepdims=True), (SUBLANES, LANES))
    nblk = jnp.floor((gtot + (EXPERT_ROWS - 1.0)) * (1.0 / EXPERT_ROWS))
    ends = _dot_sel_rhs(nblk, jnp.where(r <= c, 1.0, 0.0).astype(BF16))
    start_row = (ends[0:1, :] - nblk[0:1, :]) * float(EXPERT_ROWS)
    tr = lax.broadcasted_iota(jnp.int32, (ntp, ntp), 0)
    tc = lax.broadcasted_iota(jnp.int32, (ntp, ntp), 1)
    gbase = start_row + _dot_sel_lhs(jnp.where(tc < tr, 1.0, 0.0).astype(BF16), n8)
    n8_ref[...] = n8.astype(jnp.int32)
    loff_ref[...] = loff.astype(jnp.int32)
    gbase_ref[...] = gbase.astype(jnp.int32)
    bi = lax.broadcasted_iota(jnp.int32, (nbp, LANES), 0).astype(F32)
    li = lax.broadcasted_iota(jnp.int32, (nbp, LANES), 1)
    done = jnp.where((li < N_EXPERTS) & (ends[0:1, :] <= bi), 1.0, 0.0)
    be = jnp.minimum(jnp.sum(done, axis=-1, keepdims=True), N_EXPERTS - 1.0)
    blk_ref[...] = jnp.broadcast_to(be, (nbp, LANES)).astype(jnp.int32)
    ends_ref[...] = ends.astype(jnp.int32)
    rows_left = gtot[0:1, :] - (bi - (ends[0:1, :] - nblk[0:1, :])) * float(EXPERT_ROWS)
    valid = jnp.sum(jnp.where(li.astype(F32) == be, rows_left, 0.0), axis=-1, keepdims=True)
    valid = jnp.clip(valid, 0.0, float(EXPERT_ROWS))
    strips = jnp.floor((valid + (EXPERT_STRIP - 1.0)) * (1.0 / EXPERT_STRIP))
    strips_ref[...] = jnp.broadcast_to(strips, (nbp, LANES)).astype(jnp.int32)


def _plan_call(cnt_tiles, nbp):
    ntp = cnt_tiles.shape[0]
    const = lambda i: (0, 0)
    tbl = jax.ShapeDtypeStruct((ntp, LANES), jnp.int32)
    return pl.pallas_call(
        functools.partial(_plan_kernel, nbp=nbp, ntp=ntp),
        grid=(1,),
        in_specs=[pl.BlockSpec((ntp, LANES), const)],
        out_specs=[pl.BlockSpec((ntp, LANES), const), pl.BlockSpec((ntp, LANES), const),
                   pl.BlockSpec((ntp, LANES), const),
                   pl.BlockSpec((nbp, LANES), const), pl.BlockSpec((SUBLANES, LANES), const),
                   pl.BlockSpec((nbp, LANES), const)],
        out_shape=[tbl, tbl, tbl,
                   jax.ShapeDtypeStruct((nbp, LANES), jnp.int32),
                   jax.ShapeDtypeStruct((SUBLANES, LANES), jnp.int32),
                   jax.ShapeDtypeStruct((nbp, LANES), jnp.int32)],
        compiler_params=_cparams(("arbitrary",)),
        name="plan",
    )(cnt_tiles)


_GROUP_BITS = tuple(range(3, DISPATCH_TILE.bit_length()))


_TOTAL_BITS = tuple(range(3, SORT_ROWS.bit_length()))


def _group_copies(n8_ref, loff_ref, gbase_ref, tile, make_copy, wait):
    if wait:
        total = lax.fori_loop(0, N_EXPERTS, lambda e, acc: acc + n8_ref[tile * N_EXPERTS + e], 0)
        for bit in _TOTAL_BITS:
            size = 1 << bit

            @pl.when((total & size) != 0)
            def _():
                make_copy(0, 0, size).wait()
        return

    def per_expert(e, carry):
        idx = tile * N_EXPERTS + e
        n = n8_ref[idx]
        off = loff_ref[idx]
        base = gbase_ref[idx]
        for bit in _GROUP_BITS:
            size = 1 << bit

            @pl.when((n & size) != 0)
            def _():
                done = n & ~(2 * size - 1)
                make_copy(pl.multiple_of(off + done, SUBLANES), pl.multiple_of(base + done, SUBLANES), size).start()
        return carry

    lax.fori_loop(0, N_EXPERTS, per_expert, 0)


def _dispatch_kernel(n8_ref, loff_ref, gbase_ref, hf_ref, lpos_ref, gate_ref, xs_in_ref, xs_ref,
                     sbuf, sem):
    del xs_in_ref
    i = pl.program_id(0)
    eye8 = jnp.where(lax.broadcasted_iota(jnp.int32, (SUBLANES, LANES), 0)
                     == lax.broadcasted_iota(jnp.int32, (SUBLANES, LANES), 1), 1.0, 0.0).astype(BF16)
    lpos_t = _dot_sel_nt(eye8, lpos_ref[...])
    gate_t = _dot_sel_nt(eye8, gate_ref[...])
    row = lax.broadcasted_iota(jnp.int32, (SORT_ROWS, DISPATCH_TILE), 0).astype(F32)
    perm = jnp.zeros((SORT_ROWS, DISPATCH_TILE), F32)
    wgate = jnp.zeros((SORT_ROWS, DISPATCH_TILE), F32)
    for kk in range(TOP_K):
        hit = row == lpos_t[kk:kk + 1, :]
        perm = jnp.where(hit, 1.0, perm)
        wgate = jnp.where(hit, gate_t[kk:kk + 1, :], wgate)
    slot = i % 2
    sbuf[slot, :, 0:D_MODEL] = _dot(perm.astype(BF16), hf_ref[...].astype(BF16))
    sbuf[slot, :, D_MODEL:XS_WIDTH] = _dot_sel_rhs(wgate, jnp.ones((DISPATCH_TILE, LANES), BF16))

    def copies(tile, buf_slot, wait):
        def make_copy(src_row, dst_row, size):
            return pltpu.make_async_copy(sbuf.at[buf_slot, pl.ds(src_row, size), :],
                                         xs_ref.at[pl.ds(dst_row, size), :], sem.at[buf_slot])
        _group_copies(n8_ref, loff_ref, gbase_ref, tile, make_copy, wait=wait)

    copies(i, slot, wait=False)

    @pl.when(i > 0)
    def _():
        copies(i - 1, 1 - slot, wait=True)

    @pl.when(i == pl.num_programs(0) - 1)
    def _():
        copies(i, slot, wait=True)


def _dispatch_call(tables, hf, lpos, gate, xs):
    ttot = hf.shape[0]
    tm = DISPATCH_TILE
    tile = lambda i, *_: (i, 0)
    grid_spec = pltpu.PrefetchScalarGridSpec(
        num_scalar_prefetch=3, grid=(ttot // tm,),
        in_specs=[pl.BlockSpec((tm, D_MODEL), tile), pl.BlockSpec((tm, LANES), tile),
                  pl.BlockSpec((tm, LANES), tile), pl.BlockSpec(memory_space=pl.ANY)],
        out_specs=pl.BlockSpec(memory_space=pl.ANY),
        scratch_shapes=[pltpu.VMEM((2, SORT_ROWS, XS_WIDTH), F32), pltpu.SemaphoreType.DMA((2,))])
    return pl.pallas_call(
        _dispatch_kernel,
        grid_spec=grid_spec,
        out_shape=jax.ShapeDtypeStruct(xs.shape, xs.dtype),
        input_output_aliases={6: 0},
        compiler_params=pltpu.CompilerParams(dimension_semantics=("arbitrary",),
                                             vmem_limit_bytes=VMEM_LIMIT, has_side_effects=True),
        name="dispatch",
    )(*tables, hf, lpos, gate, xs)


def _expert_kernel(be_ref, nu_ref, strips_ref, xs_ref, wgu_ref, bgu_ref, wdn_ref, bdn_ref, y_ref,
                   wgu_bf, wdn_bf):
    b = pl.program_id(0)
    strips = strips_ref[b]

    @pl.when(strips > 0)
    def _():
        prev = be_ref[jnp.maximum(b - 1, 0)]

        @pl.when((b == 0) | (be_ref[b] != prev))
        def _():
            wgu_bf[...] = wgu_ref[0, 0].astype(BF16)
            wdn_bf[...] = wdn_ref[0, 0].astype(BF16)

    for n_strips in range(1, EXPERT_ROWS // EXPERT_STRIP + 1):
        rows = n_strips * EXPERT_STRIP

        @pl.when(strips == n_strips)
        def _():
            gu = _dot(xs_ref[0:rows, 0:D_MODEL].astype(BF16), wgu_bf[...]) + bgu_ref[0, 0]
            g = jnp.minimum(gu[:, :D_MODEL], SWIGLU_LIMIT)
            u = jnp.clip(gu[:, D_MODEL:], -SWIGLU_LIMIT, SWIGLU_LIMIT)
            act = (u + 1.0) * (g * _sigmoid(SWIGLU_ALPHA * g))
            gate = xs_ref[0:rows, D_MODEL:D_MODEL + 1]
            y_ref[0:rows, :] = (_dot(act.astype(BF16), wdn_bf[...]) + bdn_ref[0, 0]) * gate
            if rows < EXPERT_ROWS:
                y_ref[rows:EXPERT_ROWS, :] = jnp.zeros((EXPERT_ROWS - rows, D_MODEL), F32)

    @pl.when(strips == 0)
    def _():
        y_ref[...] = jnp.zeros_like(y_ref)


def _expert_call(blk_expert, n_used, strips, xs, w_gu, b_gu, w_dn, b_dn, layer):
    nb = xs.shape[0] // EXPERT_ROWS
    d_ff2 = w_gu.shape[3]
    blk = lambda b, be, nu, st: (jnp.minimum(b, nu[0] - 1), 0)
    blk_out = lambda b, be, nu, st: (b, 0)
    exp4 = lambda b, be, nu, st: (layer, be[jnp.minimum(b, nu[0] - 1)], 0, 0)
    grid_spec = pltpu.PrefetchScalarGridSpec(
        num_scalar_prefetch=3, grid=(nb,),
        in_specs=[pl.BlockSpec((EXPERT_ROWS, XS_WIDTH), blk),
                  pl.BlockSpec((1, 1, D_MODEL, d_ff2), exp4),
                  pl.BlockSpec((1, 1, 1, d_ff2), exp4),
                  pl.BlockSpec((1, 1, D_MODEL, D_MODEL), exp4),
                  pl.BlockSpec((1, 1, 1, D_MODEL), exp4)],
        out_specs=pl.BlockSpec((EXPERT_ROWS, D_MODEL), blk_out),
        scratch_shapes=[pltpu.VMEM((D_MODEL, d_ff2), BF16), pltpu.VMEM((D_MODEL, D_MODEL), BF16)])
    return pl.pallas_call(
        _expert_kernel,
        grid_spec=grid_spec,
        out_shape=jax.ShapeDtypeStruct((xs.shape[0], D_MODEL), F32),
        compiler_params=_cparams(("arbitrary",)),
        name="experts",
    )(blk_expert, n_used, strips, xs, w_gu, b_gu, w_dn, b_dn)


def _combine_kernel(*refs, tm, npc, final):
    if final:
        (n8_ref, loff_ref, gbase_ref, x_ref, lpos_ref, mod_ref, fnw_ref, y_hbm,
         yp_ref, ys_ref, ybuf, sem) = refs
    else:
        n8_ref, loff_ref, gbase_ref, x_ref, lpos_ref, mod_ref, y_hbm, xo_ref, ybuf, sem = refs
    i = pl.program_id(0)
    nch = tm // CHUNK

    slot = i % 2

    def copies(tile, buf_slot, wait):
        def make_copy(buf_row, src_row, size):
            return pltpu.make_async_copy(y_hbm.at[pl.ds(src_row, size), :],
                                         ybuf.at[buf_slot, pl.ds(buf_row, size), :], sem.at[buf_slot])
        _group_copies(n8_ref, loff_ref, gbase_ref, tile, make_copy, wait=wait)

    @pl.when(i == 0)
    def _():
        ybuf[...] = jnp.zeros_like(ybuf)
        copies(i, slot, wait=False)

    @pl.when(i + 1 < pl.num_programs(0))
    def _():
        copies(i + 1, 1 - slot, wait=False)

    copies(i, slot, wait=True)

    lpos = lpos_ref[...]
    col = lax.broadcasted_iota(jnp.int32, (tm, SORT_ROWS), 1).astype(F32)
    unperm = jnp.zeros((tm, SORT_ROWS), F32)
    for kk in range(TOP_K):
        unperm = jnp.where(col == lpos[:, kk:kk + 1], 1.0, unperm)
    acc = _dot_sel_lhs2(unperm.astype(BF16), ybuf[slot])
    for c in range(nch):
        seq = _seq_row(i, nch, c, npc)
        gf = mod_ref[pl.ds(seq, 1), 5 * D_MODEL:6 * D_MODEL]
        rs = slice(c * CHUNK, (c + 1) * CHUNK)
        xn = x_ref[rs, :] + gf * acc[rs, :]
        if final:
            yn = _rms(xn) * fnw_ref[...]
            is_prompt = i < (npc * CHUNK) // tm

            @pl.when(is_prompt)
            def _():
                yp_ref[rs, :] = yn

            @pl.when(jnp.logical_not(is_prompt))
            def _():
                ys_ref[rs, :] = yn
        else:
            xo_ref[rs, :] = xn


def _combine_call(tables, x, lpos, mod_l, y_sorted, npc, final_w):
    ttot = x.shape[0]
    tm = DISPATCH_TILE
    final = final_w is not None
    npt = (npc * CHUNK) // tm
    tile = lambda i, *_: (i, 0)
    const = lambda i, *_: (0, 0)
    in_specs = [pl.BlockSpec((tm, D_MODEL), tile),
                pl.BlockSpec((tm, LANES), tile),
                pl.BlockSpec(mod_l.shape, const)]
    args = [x, lpos, mod_l]
    if final:
        in_specs.append(pl.BlockSpec((1, D_MODEL), const))
        args.append(final_w)
    in_specs.append(pl.BlockSpec(memory_space=pl.ANY))
    args.append(y_sorted)
    if final:
        out_specs = [pl.BlockSpec((tm, D_MODEL), lambda i, *_: (jnp.minimum(i, npt - 1), 0)),
                     pl.BlockSpec((tm, D_MODEL), lambda i, *_: (jnp.maximum(i - npt, 0), 0))]
        out_shape = [jax.ShapeDtypeStruct((npt * tm, D_MODEL), F32),
                     jax.ShapeDtypeStruct((ttot - npt * tm, D_MODEL), F32)]
    else:
        out_specs = [pl.BlockSpec((tm, D_MODEL), tile)]
        out_shape = [jax.ShapeDtypeStruct((ttot, D_MODEL), F32)]
    grid_spec = pltpu.PrefetchScalarGridSpec(
        num_scalar_prefetch=3, grid=(ttot // tm,),
        in_specs=in_specs, out_specs=out_specs,
        scratch_shapes=[pltpu.VMEM((2, SORT_ROWS, D_MODEL), F32), pltpu.SemaphoreType.DMA((2,))])
    return pl.pallas_call(
        functools.partial(_combine_kernel, tm=tm, npc=npc, final=final),
        grid_spec=grid_spec, out_shape=out_shape,
        compiler_params=_cparams(("arbitrary",)),
        name="combine",
    )(*tables, *args)


def kernel(x_prompt, x_sample, c_prompt, c_sample, state_mlstm_C, state_mlstm_n, state_mlstm_m, state_mlstm_conv, state_ret_S, state_hgrn_S, w_ada, b_ada, norm_mix_w, norm_ffn_w, final_norm_w, w_in_even, b_mlstm_i, b_mlstm_f, w_mlstm_conv, b_mlstm_conv, w_mlstm_q, w_mlstm_k, mlstm_skip, mlstm_norm_w, ret_norm_w, w_out_even, w_in_odd, hgrn_lb_logits, hgrn_norm_w, w_out_odd, moe_router_w, moe_router_b, moe_w_gate_up, moe_b_gate_up, moe_w_down, moe_b_down):
    bp, seq, d = x_prompt.shape
    bs, dseq, _ = x_sample.shape
    assert bp == 1 and d == D_MODEL and dseq == CHUNK
    assert seq % TOKEN_TILE == 0 and (bs * dseq) % TOKEN_TILE == 0
    depth = w_ada.shape[0]
    tp, tsmp = seq, bs * dseq
    ttot = tp + tsmp
    npc = tp // CHUNK
    he, ho = N_HEADS_EVEN, N_HEADS_ODD
    da = he * LANES
    past_len = 1024

    x = jnp.concatenate([x_prompt.reshape(tp, d), x_sample.reshape(tsmp, d)], axis=0)
    n_mod_rows = 2 * SUBLANES
    assert 1 + bs <= n_mod_rows
    c_all = jnp.zeros((n_mod_rows, d), F32).at[0:1].set(c_prompt).at[1:1 + bs].set(c_sample)
    mod = _ada_call(c_all, w_ada, b_ada)

    half = LANES // 2
    inv = ROPE_BASE ** (-jnp.arange(half, dtype=F32) / half)
    pos_all = jnp.concatenate([jnp.arange(tp, dtype=F32),
                               jnp.tile(past_len + jnp.arange(dseq, dtype=F32), bs)])
    ang = pos_all[:, None] * inv[None, :]
    cos_t = jnp.concatenate([jnp.cos(ang), jnp.cos(ang)], axis=-1)
    sin_t = jnp.concatenate([-jnp.sin(ang), jnp.sin(ang)], axis=-1)

    lb_p = jax.nn.softmax(hgrn_lb_logits.astype(F32), axis=0)
    lbs = jnp.cumsum(lb_p, axis=0) - lb_p[0]

    n_tiles = ttot // DISPATCH_TILE
    ntp = -(-n_tiles // LANES) * LANES
    max_rows = ttot * TOP_K + n_tiles * N_EXPERTS * (SUBLANES - 1)
    nb = -(-max_rows // EXPERT_ROWS) + N_EXPERTS
    nbp = -(-nb // SUBLANES) * SUBLANES
    xs = jnp.zeros((nb * EXPERT_ROWS, XS_WIDTH), F32)

    ts_p = TOKEN_TILE
    steps_p = tp // ts_p
    even_out, odd_out = [], []
    y_final = None
    for l in range(depth):
        jl = l // 2
        mod_l = mod[l]
        if l % 2 == 0:
            w_in = w_in_even[jl]
            w_main = jnp.concatenate([w_in[:, da:3 * da], w_in[:, 3 * da + 2 * he:]], axis=1).astype(BF16)
            w_gate = jnp.zeros((d, da + LANES), F32).at[:, :da].set(w_in[:, :da])
            w_gate = w_gate.at[:, da:da + 2 * he].set(w_in[:, 3 * da:3 * da + 2 * he])
            proj, gates = _inproj_call(x, mod_l, norm_mix_w[l][None], w_main, w_gate, npc)
            gbias = jnp.zeros((1, LANES), F32).at[0, :he].set(b_mlstm_i[jl]).at[0, he:2 * he].set(b_mlstm_f[jl])
            weights = [w_mlstm_conv[jl], b_mlstm_conv[jl][None],
                       w_mlstm_q[jl], w_mlstm_k[jl], gbias,
                       mlstm_skip[jl][None], mlstm_norm_w[jl][None], ret_norm_w[jl][None]]
            zeros_p = (jnp.zeros((1, he, LANES, LANES), F32), jnp.zeros((1, he, 1, LANES), F32),
                       jnp.zeros((1, 1, LANES), F32), jnp.zeros((1, SUBLANES, da), F32),
                       jnp.zeros((1, he, LANES, LANES), F32))
            st_s = (state_mlstm_C[jl], state_mlstm_n[jl][:, :, None, :],
                    jnp.zeros((bs, 1, LANES), F32).at[:, 0, :he].set(state_mlstm_m[jl]),
                    jnp.zeros((bs, SUBLANES, da), F32).at[:, SUBLANES - (CONV_W - 1):].set(state_mlstm_conv[jl]),
                    state_ret_S[jl])
            res_p = _even_scan_call(proj, gates, cos_t, sin_t, zeros_p, weights,
                                    ts=ts_p, n_seq=1, steps=steps_p, row_off=0)
            res_s = _even_scan_call(proj, gates, cos_t, sin_t, st_s, weights,
                                    ts=CHUNK, n_seq=bs, steps=1, row_off=npc)
            u_p, u_s = res_p[0], res_s[0]
            even_out.append((res_p[1:], res_s[1:]))
            w_out = w_out_even[jl].astype(BF16)
        else:
            proj = _inproj_call(x, mod_l, norm_mix_w[l][None], w_in_odd[jl].astype(BF16), None, npc)[0]
            lb = lbs[l][None]
            nw = hgrn_norm_w[jl][None]
            u_p, sp = _odd_scan_call(proj, jnp.zeros((1, ho, LANES, LANES), F32), lb, nw,
                                     ts=ts_p, n_seq=1, steps=steps_p, row_off=0)
            u_s, ss = _odd_scan_call(proj, state_hgrn_S[jl], lb, nw,
                                     ts=CHUNK, n_seq=bs, steps=1, row_off=npc)
            odd_out.append((sp, ss))
            w_out = w_out_odd[jl].astype(BF16)

        w_r = jnp.zeros((d, LANES), F32).at[:, :N_EXPERTS].set(moe_router_w[l])
        b_r = jnp.full((1, LANES), -jnp.inf, F32).at[0, :N_EXPERTS].set(moe_router_b[l])
        x, hf, gate, lpos, cnt = _post_call(x, u_p, u_s, mod_l, w_out, norm_ffn_w[l][None], w_r, b_r, npc)
        cnt_tiles = jnp.zeros((ntp, LANES), F32).at[:n_tiles].set(cnt[:, 0, :])
        n8, loff, gbase, blk, ends, strips = _plan_call(cnt_tiles, nbp)
        tables = [t[:n_tiles, :N_EXPERTS].reshape(-1) for t in (n8, loff, gbase)]
        blk_expert = blk[:nb, 0]
        n_used = ends[0, N_EXPERTS - 1:N_EXPERTS]
        xs = _dispatch_call(tables, hf, lpos, gate, xs)
        y_sorted = _expert_call(blk_expert, n_used, strips[:nb, 0], xs, moe_w_gate_up, moe_b_gate_up[:, :, None, :],
                                moe_w_down, moe_b_down[:, :, None, :], l)
        if l == depth - 1:
            y_final = _combine_call(tables, x, lpos, mod_l, y_sorted, npc, final_norm_w[None])
        else:
            x = _combine_call(tables, x, lpos, mod_l, y_sorted, npc, None)[0]

    def even_states(which):
        cs = jnp.stack([e[which][0] for e in even_out])
        ns = jnp.stack([e[which][1][:, :, 0, :] for e in even_out])
        ms = jnp.stack([e[which][2][:, 0, :he] for e in even_out])
        cv = jnp.stack([e[which][3][:, SUBLANES - (CONV_W - 1):, :] for e in even_out])
        ss = jnp.stack([e[which][4] for e in even_out])
        return cs, ns, ms, cv, ss

    p_c, p_n, p_m, p_cv, p_s = even_states(0)
    s_c, s_n, s_m, s_cv, s_s = even_states(1)
    p_h = jnp.stack([o[0] for o in odd_out])
    s_h = jnp.stack([o[1] for o in odd_out])
    y_prompt = y_final[0].reshape(bp, seq, d)
    y_sample = y_final[1].reshape(bs, dseq, d)
    return (y_prompt, y_sample, p_c, p_n, p_m, p_cv, p_s, p_h, s_c, s_n, s_m, s_cv, s_s, s_h)
```

```python
import functools
import math

import jax
import jax.numpy as jnp
from jax import lax
from jax.experimental import pallas as pl
from jax.experimental.pallas import tpu as pltpu

F32 = jnp.float32
BF16 = jnp.bfloat16

CHUNK = 64
LANES = 128
SUBLANES = 8
D_MODEL = 1024
N_HEADS_EVEN = 4
N_HEADS_ODD = 8
CONV_W = 4
N_EXPERTS = 32
TOP_K = 4
SWIGLU_LIMIT = 7.0
SWIGLU_ALPHA = 1.702
EPS = 1e-6
ROPE_BASE = 10000.0

TOKEN_TILE = 512
INPROJ_TILE = 512
DISPATCH_TILE = 256
SORT_ROWS = DISPATCH_TILE * TOP_K + LANES * 2
XS_WIDTH = D_MODEL + LANES
EXPERT_ROWS = 512
VMEM_LIMIT = 56 * 1024 * 1024


def _cparams(sem, vmem=VMEM_LIMIT):
    return pltpu.CompilerParams(dimension_semantics=sem, vmem_limit_bytes=vmem)


def _dot(a, b):
    return jnp.dot(a, b, preferred_element_type=F32)


def _dot_nt(a, b):
    return lax.dot_general(a, b, (((1,), (1,)), ((), ())), preferred_element_type=F32)


def _dot_tn(a, b):
    return lax.dot_general(a, b, (((0,), (0,)), ((), ())), preferred_element_type=F32)


def _split3(x):
    p1 = x.astype(BF16)
    r1 = x - p1.astype(F32)
    p2 = r1.astype(BF16)
    p3 = (r1 - p2.astype(F32)).astype(BF16)
    return p1, p2, p3


def _dot3(a, b):
    ah = a.astype(BF16)
    al = (a - ah.astype(F32)).astype(BF16)
    bh = b.astype(BF16)
    bl = (b - bh.astype(F32)).astype(BF16)
    return _dot(ah, bh) + (_dot(ah, bl) + _dot(al, bh))


def _dot3_tn_fused(a, b):
    n = a.shape[1]
    ah = a.astype(BF16)
    al = (a - ah.astype(F32)).astype(BF16)
    bh = b.astype(BF16)
    bl = (b - bh.astype(F32)).astype(BF16)
    x = _dot_tn(jnp.concatenate([ah, al], axis=1), jnp.concatenate([bh, bl], axis=1))
    return x[:n, :n] + (x[:n, n:] + x[n:, :n])


def _block_diag(a, b):
    z = jnp.zeros_like(a)
    return jnp.concatenate([jnp.concatenate([a, z], axis=1), jnp.concatenate([z, b], axis=1)], axis=0)


def _dot_sel_lhs(sel_bf16, x):
    p1, p2, p3 = _split3(x)
    return _dot(sel_bf16, p1) + (_dot(sel_bf16, p2) + _dot(sel_bf16, p3))


def _dot_sel_nt(sel_bf16, x):
    p1, p2, p3 = _split3(x)
    return _dot_nt(sel_bf16, p1) + (_dot_nt(sel_bf16, p2) + _dot_nt(sel_bf16, p3))


def _dot_sel_rhs(x, sel_bf16):
    p1, p2, p3 = _split3(x)
    return _dot(p1, sel_bf16) + (_dot(p2, sel_bf16) + _dot(p3, sel_bf16))


def _dot_sel_lhs2(sel_bf16, x):
    hi = x.astype(BF16)
    lo = (x - hi.astype(F32)).astype(BF16)
    return _dot(sel_bf16, hi) + _dot(sel_bf16, lo)


def _sigmoid(x):
    return 1.0 / (1.0 + jnp.exp(-x))


def _log_sigmoid(x):
    return jnp.minimum(x, 0.0) - jnp.log1p(jnp.exp(-jnp.abs(x)))


def _rms(x):
    return x * lax.rsqrt(jnp.mean(x * x, axis=-1, keepdims=True) + EPS)


def _chunk_tri(n):
    r = lax.broadcasted_iota(jnp.int32, (n, n), 0)
    c = lax.broadcasted_iota(jnp.int32, (n, n), 1)
    return jnp.where((r // CHUNK == c // CHUNK) & (c <= r), 1.0, 0.0).astype(BF16)


def _chunk_cumsum(x):
    rows, n = x.shape
    vregs = CHUNK // SUBLANES
    x4 = x.reshape(rows // CHUNK, vregs, SUBLANES, n)
    sub = lax.broadcasted_iota(jnp.int32, x4.shape, 2)
    s = x4
    for shift in (1, 2, 4):
        s = s + jnp.where(sub >= shift, pltpu.roll(s, shift, 2), 0.0)
    outs, carry = [], None
    for v in range(vregs):
        cur = s[:, v] if carry is None else s[:, v] + carry
        outs.append(cur)
        carry = jnp.broadcast_to(cur[:, SUBLANES - 1:SUBLANES, :], cur.shape)
    return jnp.stack(outs, axis=1).reshape(rows, n)


def _seq_row(tile_idx, chunks_per_tile, c, n_prompt_chunks):
    return jnp.maximum(tile_idx * chunks_per_tile + c - (n_prompt_chunks - 1), 0)


def _ada_kernel(c_ref, w_ref, b_ref, o_ref):
    c = c_ref[...]
    o_ref[0] = _dot3(c * _sigmoid(c), w_ref[0]) + b_ref[0]


def _ada_call(c_all, w_ada, b_ada):
    depth = w_ada.shape[0]
    nrow = c_all.shape[0]
    ncol = w_ada.shape[2] // D_MODEL
    return pl.pallas_call(
        _ada_kernel,
        grid=(depth, ncol),
        in_specs=[pl.BlockSpec((nrow, D_MODEL), lambda l, j: (0, 0)),
                  pl.BlockSpec((1, D_MODEL, D_MODEL), lambda l, j: (l, 0, j)),
                  pl.BlockSpec((1, 1, D_MODEL), lambda l, j: (l, 0, j))],
        out_specs=pl.BlockSpec((1, nrow, D_MODEL), lambda l, j: (l, 0, j)),
        out_shape=jax.ShapeDtypeStruct((depth, nrow, ncol * D_MODEL), F32),
        compiler_params=_cparams(("arbitrary", "arbitrary")),
        name="ada",
    )(c_all, w_ada, b_ada.reshape(depth, 1, -1))


def _inproj_kernel(*refs, tm, ng, npc, n_hp):
    if n_hp:
        x_ref, mod_ref, nw_ref, w_ref, whp_ref, proj_ref, gates_ref, h_scr = refs
    else:
        x_ref, mod_ref, nw_ref, w_ref, proj_ref, h_scr = refs
    i = pl.program_id(0)
    nch = tm // CHUNK
    for c in range(nch):
        seq = _seq_row(i, nch, c, npc)
        sh = mod_ref[pl.ds(seq, 1), 0:D_MODEL]
        sc = mod_ref[pl.ds(seq, 1), D_MODEL:2 * D_MODEL]
        xc = x_ref[c * CHUNK:(c + 1) * CHUNK, :]
        h_scr[c * CHUNK:(c + 1) * CHUNK, :] = _rms(xc) * nw_ref[...] * (1.0 + sc) + sh
    h = h_scr[...]
    hb = h.astype(BF16)
    for g in range(0, ng - n_hp, 4):
        res = _dot(hb, w_ref[:, g * LANES:(g + 4) * LANES])
        for jj in range(4):
            proj_ref[n_hp + g + jj] = res[:, jj * LANES:(jj + 1) * LANES]
    if n_hp:
        res = _dot3(h, whp_ref[...])
        for jj in range(n_hp):
            proj_ref[jj] = res[:, jj * LANES:(jj + 1) * LANES]
        gates_ref[...] = res[:, n_hp * LANES:]


def _inproj_call(x, mod_l, nw, w_main, w_gate, npc):
    ttot = x.shape[0]
    tm = INPROJ_TILE
    has_gates = w_gate is not None
    n_hp = w_gate.shape[1] // LANES - 1 if has_gates else 0
    ng = w_main.shape[1] // LANES + n_hp
    in_specs = [pl.BlockSpec((tm, D_MODEL), lambda i: (i, 0)),
                pl.BlockSpec(mod_l.shape, lambda i: (0, 0)),
                pl.BlockSpec((1, D_MODEL), lambda i: (0, 0)),
                pl.BlockSpec(w_main.shape, lambda i: (0, 0))]
    out_specs = [pl.BlockSpec((ng, tm, LANES), lambda i: (0, i, 0))]
    out_shape = [jax.ShapeDtypeStruct((ng, ttot, LANES), F32)]
    args = [x, mod_l, nw, w_main]
    if has_gates:
        in_specs.append(pl.BlockSpec(w_gate.shape, lambda i: (0, 0)))
        out_specs.append(pl.BlockSpec((tm, LANES), lambda i: (i, 0)))
        out_shape.append(jax.ShapeDtypeStruct((ttot, LANES), F32))
        args.append(w_gate)
    return pl.pallas_call(
        functools.partial(_inproj_kernel, tm=tm, ng=ng, npc=npc, n_hp=n_hp),
        grid=(ttot // tm,),
        in_specs=in_specs, out_specs=out_specs, out_shape=out_shape,
        scratch_shapes=[pltpu.VMEM((tm, D_MODEL), F32)],
        compiler_params=_cparams(("arbitrary",)),
        name="inproj",
    )(*args)


_G_XM, _G_VA, _G_OA, _G_QB, _G_KB, _G_VB, _G_GB = 0, 4, 8, 12, 16, 20, 24


def _even_scan_kernel(proj_ref, gates_ref, cos_ref, sin_ref,
                      c0_ref, n0_ref, m0_ref, conv0_ref, s0_ref,
                      cw_ref, cb_ref, wq_ref, wk_ref, gbias_ref, skip_ref, nwa_ref, nwb_ref,
                      u_ref, co_ref, no_ref, mo_ref, convo_ref, so_ref,
                      c_scr, n_scr, m_scr, conv_scr, s_scr,
                      xbuf, xc_scr, q_scr, k_scr, qr_scr, kr_scr,
                      gl_scr, bc_scr, rows_scr, dec_scr, *, ts):
    H = N_HEADS_EVEN
    j = pl.program_id(1)
    nj = pl.num_programs(1)
    nc = ts // CHUNK

    @pl.when(j == 0)
    def _():
        for p in range(H // 2):
            c_scr[p] = _block_diag(c0_ref[0, 2 * p], c0_ref[0, 2 * p + 1])
            s_scr[p] = _block_diag(s0_ref[0, 2 * p], s0_ref[0, 2 * p + 1])
        n_scr[...] = n0_ref[0]
        m_scr[...] = m0_ref[0]
        conv_scr[...] = conv0_ref[0]

    for g in range(H):
        lo, hi = g * LANES, (g + 1) * LANES
        x_g = proj_ref[_G_XM + g]
        xbuf[0:SUBLANES, :] = conv_scr[:, lo:hi]
        xbuf[SUBLANES:SUBLANES + ts, :] = x_g
        acc = cb_ref[:, lo:hi] + cw_ref[CONV_W - 1:CONV_W, lo:hi] * x_g
        for t in range(CONV_W - 1):
            off = SUBLANES - (CONV_W - 1) + t
            acc = acc + cw_ref[t:t + 1, lo:hi] * xbuf[off:off + ts, :]
        conv_scr[:, lo:hi] = xbuf[ts:ts + SUBLANES, :]
        xc = acc * _sigmoid(acc)
        xc_scr[g] = xc
        q_scr[g] = _dot3(xc, wq_ref[g])
        k_scr[g] = _dot3(xc, wk_ref[g]) * (LANES ** -0.5)

    cosv = cos_ref[...]
    sinv = sin_ref[...]
    for g in range(H):
        qb = proj_ref[_G_QB + g]
        kb = proj_ref[_G_KB + g]
        qr_scr[g] = qb * cosv + pltpu.roll(qb, LANES // 2, 1) * sinv
        kr_scr[g] = (kb * cosv + pltpu.roll(kb, LANES // 2, 1) * sinv) * (LANES ** -0.5)

    gpre = gates_ref[...] + gbias_ref[...]
    lane = lax.broadcasted_iota(jnp.int32, (ts, LANES), 1)
    gl = jnp.where(lane < H, gpre, _log_sigmoid(gpre))
    gl_scr[...] = gl
    bc = _dot_sel_lhs(_chunk_tri(ts), gl)
    bc_scr[...] = bc
    comb = jnp.where(lane < H, gl, bc)
    even_head = (lane % 2) == 0
    comb_even = jnp.where(even_head, comb, 0.0)
    comb_odd = jnp.where(even_head, 0.0, comb)
    pr = lax.broadcasted_iota(jnp.int32, (SUBLANES, LANES), 0)
    pc = lax.broadcasted_iota(jnp.int32, (SUBLANES, LANES), 1)
    pair_sel = jnp.where((pc // 2 == pr) & (pc < 2 * H), 1.0, 0.0).astype(BF16)
    for c in range(nc):
        cs = slice(c * CHUNK, (c + 1) * CHUNK)
        rows_scr[c] = _dot_sel_nt(pair_sel, jnp.concatenate([comb_even[cs, :], comb_odd[cs, :]], axis=0))

    ti = lax.broadcasted_iota(jnp.int32, (CHUNK, LANES), 0)
    lane2 = lax.broadcasted_iota(jnp.int32, (CHUNK, LANES), 1)
    left = lane2 < CHUNK
    si = lane2 % CHUNK
    tril = ti >= si
    left_wide = lax.broadcasted_iota(jnp.int32, (CHUNK, 2 * LANES), 1) < LANES
    br = lax.broadcasted_iota(jnp.int32, (2 * LANES, 2 * LANES), 0) < LANES
    bcol_blk = lax.broadcasted_iota(jnp.int32, (2 * LANES, 2 * LANES), 1) < LANES
    tcol = lax.broadcasted_iota(jnp.int32, (CHUNK, 1), 0).astype(F32)
    log_gamma = [math.log1p(-2.0 ** (-5 - h)) for h in range(H)]
    for p in range(H // 2):
        lg2 = jnp.where(left, log_gamma[2 * p], log_gamma[2 * p + 1])
        dec_scr[p] = jnp.where(tril, jnp.exp((ti - si).astype(F32) * lg2), 0.0)

    def chunk_body(c, carry):
        r0 = pl.multiple_of(c * CHUNK, CHUNK)
        rows = pl.ds(r0, CHUNK)
        pair_rows = rows_scr[c]
        glc = gl_scr[rows, :]
        bcc = bc_scr[rows, :]
        m_all = m_scr[...]
        c_old = [c_scr[p] for p in range(H // 2)]
        s_old = [s_scr[p] for p in range(H // 2)]
        n_old = [n_scr[h] for h in range(H)]
        c_new, s_new, n_new = [], [], []
        m_next = m_all
        lane_row = lax.broadcasted_iota(jnp.int32, (1, LANES), 1)
        for p in range(H // 2):
            hs = (2 * p, 2 * p + 1)
            q = [q_scr[h, rows, :] for h in hs]
            k = [k_scr[h, rows, :] for h in hs]
            v = [proj_ref[_G_VA + h, rows, :] for h in hs]
            b_col = [bcc[:, H + h:H + h + 1] for h in hs]
            i_col = [glc[:, h:h + 1] for h in hs]
            m_prev = [m_all[:, h:h + 1] for h in hs]
            q2 = jnp.concatenate(q, axis=1).astype(BF16)
            smat = _dot_nt(q2, _block_diag(k[0].astype(BF16), k[1].astype(BF16)))
            dmat = jnp.where(tril, jnp.where(left, b_col[0], b_col[1])
                             - pair_rows[2 + p:3 + p, :] + pair_rows[p:p + 1, :], -jnp.inf)
            a = [jnp.max(jnp.where(left, dmat, -jnp.inf), axis=-1, keepdims=True),
                 jnp.max(jnp.where(left, -jnp.inf, dmat), axis=-1, keepdims=True)]
            inter = [b_col[e] + m_prev[e] for e in range(2)]
            m_t = [jnp.maximum(inter[e], a[e]) for e in range(2)]
            w_inter = [jnp.exp(inter[e] - m_t[e]) for e in range(2)]
            amat = smat * jnp.exp(dmat - jnp.where(left, m_t[0], m_t[1]))
            cst = c_old[p]
            num = (_dot(amat.astype(BF16), _block_diag(v[0].astype(BF16), v[1].astype(BF16)))
                   + jnp.where(left_wide, w_inter[0], w_inter[1]) * _dot(q2, cst.astype(BF16)))
            den = [jnp.sum(jnp.where(left, amat, 0.0), axis=-1, keepdims=True),
                   jnp.sum(jnp.where(left, 0.0, amat), axis=-1, keepdims=True)]
            den = [jnp.maximum(jnp.abs(den[e] + w_inter[e] * jnp.sum(q[e] * n_old[hs[e]], axis=-1, keepdims=True)),
                               jnp.exp(-m_t[e])) for e in range(2)]
            hout = num / jnp.where(left_wide, den[0], den[1])
            decay, upd = [], []
            for e in range(2):
                m_new = m_t[e][CHUNK - 1:CHUNK, :]
                b_last = b_col[e][CHUNK - 1:CHUNK, :]
                kw = k[e] * jnp.exp(b_last - b_col[e] + i_col[e] - m_new)
                decay.append(jnp.exp(b_last + m_prev[e] - m_new))
                upd.append(_dot3_tn_fused(kw, v[e]))
                n_new.append(decay[e] * n_old[hs[e]] + jnp.sum(kw, axis=0, keepdims=True))
                m_next = jnp.where(lane_row == hs[e], m_new, m_next)
            c_new.append(jnp.where(br, decay[0], decay[1]) * cst + _block_diag(upd[0], upd[1]))
            for e in range(2):
                h = hs[e]
                lo, hi = h * LANES, (h + 1) * LANES
                z = _sigmoid(proj_ref[_G_OA + h, rows, :]) * hout[:, e * LANES:(e + 1) * LANES]
                u_ref[rows, lo:hi] = (_rms(z) * nwa_ref[:, lo:hi]
                                      + skip_ref[:, lo:hi] * xc_scr[h, rows, :])
        for p in range(H // 2):
            hs = (2 * p, 2 * p + 1)
            lg = [log_gamma[h] for h in hs]
            kr = [kr_scr[h, rows, :] for h in hs]
            vb = [proj_ref[_G_VB + h, rows, :].astype(BF16) for h in hs]
            q2 = jnp.concatenate([qr_scr[h, rows, :] for h in hs], axis=1).astype(BF16)
            amat = _dot_nt(q2, _block_diag(kr[0].astype(BF16), kr[1].astype(BF16))) * dec_scr[p]
            sst = s_old[p]
            o = (_dot(amat.astype(BF16), _block_diag(vb[0], vb[1]))
                 + jnp.exp((tcol + 1.0) * jnp.where(left_wide, lg[0], lg[1])) * _dot(q2, sst.astype(BF16)))
            kws = jnp.concatenate([kr[e] * jnp.exp((CHUNK - 1.0 - tcol) * lg[e]) for e in range(2)], axis=1)
            cross = _dot_tn(kws.astype(BF16), jnp.concatenate(vb, axis=1))
            s_new.append(jnp.where(br, math.exp(CHUNK * lg[0]), math.exp(CHUNK * lg[1])) * sst
                         + jnp.where(br == bcol_blk, cross, 0.0))
            for e in range(2):
                h = hs[e]
                lo, hi = h * LANES, (h + 1) * LANES
                gate = proj_ref[_G_GB + h, rows, :]
                u_ref[rows, D_MODEL // 2 + lo:D_MODEL // 2 + hi] = (
                    _rms(o[:, e * LANES:(e + 1) * LANES]) * nwb_ref[:, lo:hi] * (gate * _sigmoid(gate)))
        for p in range(H // 2):
            c_scr[p] = c_new[p]
            s_scr[p] = s_new[p]
        for h in range(H):
            n_scr[h] = n_new[h]
        m_scr[...] = m_next
        return carry

    lax.fori_loop(0, nc, chunk_body, 0)

    @pl.when(j == nj - 1)
    def _():
        for p in range(H // 2):
            for e in range(2):
                blk = slice(e * LANES, (e + 1) * LANES)
                co_ref[0, 2 * p + e] = c_scr[p, blk, blk]
                so_ref[0, 2 * p + e] = s_scr[p, blk, blk]
        no_ref[0] = n_scr[...]
        mo_ref[0] = m_scr[...]
        convo_ref[0] = conv_scr[...]


def _even_scan_call(proj, gates, cos_t, sin_t, states, weights, *, ts, n_seq, steps, row_off):
    H = N_HEADS_EVEN
    c0, n0, m0, conv0, s0 = states
    ng = proj.shape[0]
    rows_idx = lambda b, j: (row_off + b * steps + j, 0)
    state_specs = [pl.BlockSpec((1, H, LANES, LANES), lambda b, j: (b, 0, 0, 0)),
                   pl.BlockSpec((1, H, 1, LANES), lambda b, j: (b, 0, 0, 0)),
                   pl.BlockSpec((1, 1, LANES), lambda b, j: (b, 0, 0)),
                   pl.BlockSpec((1, SUBLANES, H * LANES), lambda b, j: (b, 0, 0)),
                   pl.BlockSpec((1, H, LANES, LANES), lambda b, j: (b, 0, 0, 0))]
    in_specs = [pl.BlockSpec((ng, ts, LANES), lambda b, j: (0, row_off + b * steps + j, 0)),
                pl.BlockSpec((ts, LANES), rows_idx),
                pl.BlockSpec((ts, LANES), rows_idx),
                pl.BlockSpec((ts, LANES), rows_idx)] + state_specs
    for w in weights:
        in_specs.append(pl.BlockSpec(w.shape, functools.partial(lambda nd, b, j: (0,) * nd, w.ndim)))
    out_specs = [pl.BlockSpec((ts, D_MODEL), lambda b, j: (b * steps + j, 0))] + state_specs
    out_shape = [jax.ShapeDtypeStruct((n_seq * steps * ts, D_MODEL), F32),
                 jax.ShapeDtypeStruct(c0.shape, F32), jax.ShapeDtypeStruct(n0.shape, F32),
                 jax.ShapeDtypeStruct(m0.shape, F32), jax.ShapeDtypeStruct(conv0.shape, F32),
                 jax.ShapeDtypeStruct(s0.shape, F32)]
    nc = ts // CHUNK
    pair_state = pltpu.VMEM((H // 2, 2 * LANES, 2 * LANES), F32)
    scratch = [pair_state, pltpu.VMEM((H, 1, LANES), F32),
               pltpu.VMEM((1, LANES), F32), pltpu.VMEM((SUBLANES, H * LANES), F32),
               pair_state,
               pltpu.VMEM((ts + 2 * SUBLANES, LANES), F32),
               pltpu.VMEM((H, ts, LANES), F32), pltpu.VMEM((H, ts, LANES), F32),
               pltpu.VMEM((H, ts, LANES), F32), pltpu.VMEM((H, ts, LANES), F32),
               pltpu.VMEM((H, ts, LANES), F32),
               pltpu.VMEM((ts, LANES), F32), pltpu.VMEM((ts, LANES), F32),
               pltpu.VMEM((nc, SUBLANES, LANES), F32),
               pltpu.VMEM((H // 2, CHUNK, LANES), F32)]
    return pl.pallas_call(
        functools.partial(_even_scan_kernel, ts=ts),
        grid=(n_seq, steps),
        in_specs=in_specs, out_specs=out_specs, out_shape=out_shape,
        scratch_shapes=scratch,
        compiler_params=_cparams(("arbitrary", "arbitrary")),
        name="even_scan",
    )(proj, gates, cos_t, sin_t, c0, n0, m0, conv0, s0, *weights)


_G_Q, _G_F, _G_I, _G_G = 0, 8, 16, 24


def _hgrn_intra_pair(q, k, bcum, ti, si, left, tcol_i):
    amat = jnp.zeros((CHUNK, LANES), F32)
    for b in (32, 16, 8):
        nb2 = CHUNK // (2 * b)
        upper = ((tcol_i // b) % 2) == 1
        ql, kl = [], []
        for e in range(2):
            parts = [jnp.broadcast_to(bcum[e][m * 2 * b + b - 1:m * 2 * b + b, :], (2 * b, LANES))
                     for m in range(nb2)]
            ref = parts[0] if nb2 == 1 else jnp.concatenate(parts, axis=0)
            ql.append(jnp.where(upper, q[e] * jnp.exp(bcum[e] - ref), 0.0).astype(BF16))
            kl.append(jnp.where(upper, 0.0, k[e] * jnp.exp(ref - bcum[e])).astype(BF16))
        al = _dot_nt(jnp.concatenate(ql, axis=1), _block_diag(kl[0], kl[1]))
        amat = amat + jnp.where((ti // (2 * b)) == (si // (2 * b)), al, 0.0)
    nblk = CHUNK // SUBLANES
    b3 = [x.reshape(nblk, SUBLANES, LANES) for x in bcum]
    k3 = [x.reshape(nblk, SUBLANES, LANES) for x in k]
    for jj in range(SUBLANES):
        col = []
        for e in range(2):
            bj = jnp.broadcast_to(b3[e][:, jj:jj + 1, :], (nblk, SUBLANES, LANES)).reshape(CHUNK, LANES)
            kj = jnp.broadcast_to(k3[e][:, jj:jj + 1, :], (nblk, SUBLANES, LANES)).reshape(CHUNK, LANES)
            col.append(jnp.sum(q[e] * kj * jnp.exp(bcum[e] - bj), axis=-1, keepdims=True))
        sel = (si == (ti // SUBLANES) * SUBLANES + jj) & ((ti % SUBLANES) >= jj)
        amat = jnp.where(sel, jnp.where(left, col[0], col[1]), amat)
    return amat


def _odd_scan_kernel(proj_ref, s0_ref, lb_ref, nw_ref, u_ref, so_ref,
                     st_scr, k_scr, bc_scr, *, ts):
    H = N_HEADS_ODD
    j = pl.program_id(1)
    nj = pl.num_programs(1)
    nc = ts // CHUNK

    @pl.when(j == 0)
    def _():
        for p in range(H // 2):
            st_scr[p] = _block_diag(s0_ref[0, 2 * p].T, s0_ref[0, 2 * p + 1].T)

    for h in range(H):
        lo, hi = h * LANES, (h + 1) * LANES
        lbv = lb_ref[:, lo:hi]
        fpre = proj_ref[_G_F + h]
        e = jnp.exp(-jnp.abs(fpre))
        one_e = 1.0 + e
        log_lb = jnp.log(lbv)
        log_rest = jnp.log1p(-lbv) + (jnp.minimum(fpre, 0.0) - jnp.log(one_e))
        logf = jnp.maximum(log_lb, log_rest) + jnp.log(1.0 + jnp.exp(-jnp.abs(log_lb - log_rest)))
        k_scr[h] = (1.0 - lbv) * (jnp.where(fpre >= 0.0, e, 1.0) / one_e)
        bc_scr[h] = _chunk_cumsum(logf)

    ti = lax.broadcasted_iota(jnp.int32, (CHUNK, LANES), 0)
    lane2 = lax.broadcasted_iota(jnp.int32, (CHUNK, LANES), 1)
    left = lane2 < CHUNK
    si = lane2 % CHUNK
    tcol_i = lax.broadcasted_iota(jnp.int32, (CHUNK, 1), 0)
    same_head = ((lax.broadcasted_iota(jnp.int32, (2 * LANES, 2 * LANES), 0) < LANES)
                 == (lax.broadcasted_iota(jnp.int32, (2 * LANES, 2 * LANES), 1) < LANES))

    def chunk_body(c, carry):
        r0 = pl.multiple_of(c * CHUNK, CHUNK)
        rows = pl.ds(r0, CHUNK)
        st_old = [st_scr[p] for p in range(H // 2)]
        st_new = []
        for p in range(H // 2):
            hs = (2 * p, 2 * p + 1)
            q = [proj_ref[_G_Q + h, rows, :] for h in hs]
            k = [k_scr[h, rows, :] for h in hs]
            vb = [proj_ref[_G_I + h, rows, :].astype(BF16) for h in hs]
            bcum = [bc_scr[h, rows, :] for h in hs]
            amat = _hgrn_intra_pair(q, k, bcum, ti, si, left, tcol_i)
            st = st_old[p]
            qg = jnp.concatenate([q[e] * jnp.exp(bcum[e]) for e in range(2)], axis=1)
            o = (_dot(amat.astype(BF16), _block_diag(vb[0], vb[1]))
                 + _dot_nt(qg.astype(BF16), st.astype(BF16)))
            last = [bcum[e][CHUNK - 1:CHUNK, :] for e in range(2)]
            kd = jnp.concatenate([k[e] * jnp.exp(last[e] - bcum[e]) for e in range(2)], axis=1)
            cross = _dot_tn(jnp.concatenate(vb, axis=1), kd.astype(BF16))
            st_new.append(st * jnp.exp(jnp.concatenate(last, axis=1)) + jnp.where(same_head, cross, 0.0))
            for e in range(2):
                h = hs[e]
                lo, hi = h * LANES, (h + 1) * LANES
                u_ref[rows, lo:hi] = (_rms(o[:, e * LANES:(e + 1) * LANES]) * nw_ref[:, lo:hi]
                                      * _sigmoid(proj_ref[_G_G + h, rows, :]))
        for p in range(H // 2):
            st_scr[p] = st_new[p]
        return carry

    lax.fori_loop(0, nc, chunk_body, 0)

    @pl.when(j == nj - 1)
    def _():
        for p in range(H // 2):
            for e in range(2):
                blk = slice(e * LANES, (e + 1) * LANES)
                so_ref[0, 2 * p + e] = st_scr[p, blk, blk].T


def _odd_scan_call(proj, s0, lb, nw, *, ts, n_seq, steps, row_off):
    H = N_HEADS_ODD
    ng = proj.shape[0]
    st_spec = pl.BlockSpec((1, H, LANES, LANES), lambda b, j: (b, 0, 0, 0))
    return pl.pallas_call(
        functools.partial(_odd_scan_kernel, ts=ts),
        grid=(n_seq, steps),
        in_specs=[pl.BlockSpec((ng, ts, LANES), lambda b, j: (0, row_off + b * steps + j, 0)),
                  st_spec,
                  pl.BlockSpec((1, D_MODEL), lambda b, j: (0, 0)),
                  pl.BlockSpec((1, D_MODEL), lambda b, j: (0, 0))],
        out_specs=[pl.BlockSpec((ts, D_MODEL), lambda b, j: (b * steps + j, 0)), st_spec],
        out_shape=[jax.ShapeDtypeStruct((n_seq * steps * ts, D_MODEL), F32),
                   jax.ShapeDtypeStruct(s0.shape, F32)],
        scratch_shapes=[pltpu.VMEM((H // 2, 2 * LANES, 2 * LANES), F32),
                        pltpu.VMEM((H, ts, LANES), F32),
                        pltpu.VMEM((H, ts, LANES), F32)],
        compiler_params=_cparams(("arbitrary", "arbitrary")),
        name="odd_scan",
    )(proj, s0, lb, nw)


def _post_kernel(x_ref, up_ref, us_ref, mod_ref, wout_ref, nw_ref, wr_ref, br_ref,
                 xo_ref, hf_ref, gate_ref, lpos_ref, cnt_ref,
                 *, tm, npc, n_prompt_tiles):
    i = pl.program_id(0)
    nch = tm // CHUNK

    u = jnp.where(i < n_prompt_tiles, up_ref[...], us_ref[...])
    y = _dot(u.astype(BF16), wout_ref[...])
    for c in range(nch):
        seq = _seq_row(i, nch, c, npc)
        gm = mod_ref[pl.ds(seq, 1), 2 * D_MODEL:3 * D_MODEL]
        shf = mod_ref[pl.ds(seq, 1), 3 * D_MODEL:4 * D_MODEL]
        scf = mod_ref[pl.ds(seq, 1), 4 * D_MODEL:5 * D_MODEL]
        rs = slice(c * CHUNK, (c + 1) * CHUNK)
        xn = x_ref[rs, :] + gm * y[rs, :]
        xo_ref[rs, :] = xn
        hf_ref[rs, :] = _rms(xn) * nw_ref[...] * (1.0 + scf) + shf

    logits = _dot3(hf_ref[...], wr_ref[...]) + br_ref[...]
    lane_i = lax.broadcasted_iota(jnp.int32, (tm, LANES), 1)
    lane_f = lane_i.astype(F32)
    vals, idxs = [], []
    cur = logits
    for _ in range(TOP_K):
        m = jnp.max(cur, axis=-1, keepdims=True)
        idx = jnp.min(jnp.where(cur == m, lane_f, float(LANES)), axis=-1, keepdims=True)
        vals.append(m)
        idxs.append(idx)
        cur = jnp.where(lane_f == idx, -jnp.inf, cur)
    exps = [jnp.exp(v - vals[0]) for v in vals]
    denom = exps[0] + exps[1] + exps[2] + exps[3]
    onehot = jnp.zeros((tm, LANES), F32)
    for idx in idxs:
        onehot = onehot + jnp.where(lane_f == idx, 1.0, 0.0)
    r = lax.broadcasted_iota(jnp.int32, (tm, tm), 0)
    cidx = lax.broadcasted_iota(jnp.int32, (tm, tm), 1)
    strict = jnp.where((cidx < r) & (cidx // DISPATCH_TILE == r // DISPATCH_TILE), 1.0, 0.0).astype(BF16)
    before = _dot(strict, onehot.astype(BF16))
    er = lax.broadcasted_iota(jnp.int32, (LANES, LANES), 0)
    ec = lax.broadcasted_iota(jnp.int32, (LANES, LANES), 1)
    lower_experts = jnp.where(er < ec, 1.0, 0.0).astype(BF16)
    pos_parts = []
    for s in range(tm // DISPATCH_TILE):
        rs = slice(s * DISPATCH_TILE, (s + 1) * DISPATCH_TILE)
        cnt = jnp.sum(onehot[rs, :], axis=0, keepdims=True)
        cnt_ref[s] = cnt
        n8 = jnp.floor((cnt + (SUBLANES - 1.0)) * (1.0 / SUBLANES)) * float(SUBLANES)
        run_start = _dot(jnp.broadcast_to(n8, (SUBLANES, LANES)).astype(BF16), lower_experts)[0:1, :]
        pos_parts.append(before[rs, :] + run_start)
    posmat = jnp.concatenate(pos_parts, axis=0)
    gate_o = jnp.zeros((tm, LANES), F32)
    lpos_o = jnp.zeros((tm, LANES), F32)
    for kk in range(TOP_K):
        lp = jnp.sum(jnp.where(lane_f == idxs[kk], posmat, 0.0), axis=-1, keepdims=True)
        gate_o = jnp.where(lane_i == kk, exps[kk] / denom, gate_o)
        lpos_o = jnp.where(lane_i == kk, lp, lpos_o)
    gate_ref[...] = gate_o
    lpos_ref[...] = lpos_o


def _post_call(x, u_p, u_s, mod_l, w_out, nw, w_r, b_r, npc):
    ttot = x.shape[0]
    tm = TOKEN_TILE
    npt = u_p.shape[0] // tm
    sub = tm // DISPATCH_TILE
    tile = lambda i: (i, 0)
    const = lambda i: (0, 0)
    return pl.pallas_call(
        functools.partial(_post_kernel, tm=tm, npc=npc, n_prompt_tiles=npt),
        grid=(ttot // tm,),
        in_specs=[pl.BlockSpec((tm, D_MODEL), tile),
                  pl.BlockSpec((tm, D_MODEL), lambda i: (jnp.minimum(i, npt - 1), 0)),
                  pl.BlockSpec((tm, D_MODEL), lambda i: (jnp.maximum(i - npt, 0), 0)),
                  pl.BlockSpec(mod_l.shape, const),
                  pl.BlockSpec(w_out.shape, const),
                  pl.BlockSpec((1, D_MODEL), const),
                  pl.BlockSpec(w_r.shape, const),
                  pl.BlockSpec((1, LANES), const)],
        out_specs=[pl.BlockSpec((tm, D_MODEL), tile), pl.BlockSpec((tm, D_MODEL), tile),
                   pl.BlockSpec((tm, LANES), tile), pl.BlockSpec((tm, LANES), tile),
                   pl.BlockSpec((sub, 1, LANES), lambda i: (i, 0, 0))],
        out_shape=[jax.ShapeDtypeStruct((ttot, D_MODEL), F32), jax.ShapeDtypeStruct((ttot, D_MODEL), F32),
                   jax.ShapeDtypeStruct((ttot, LANES), F32), jax.ShapeDtypeStruct((ttot, LANES), F32),
                   jax.ShapeDtypeStruct((ttot // DISPATCH_TILE, 1, LANES), F32)],
        compiler_params=_cparams(("arbitrary",)),
        name="post",
    )(x, u_p, u_s, mod_l, w_out, nw, w_r, b_r)


def _plan_kernel(cnt_ref, n8_ref, loff_ref, gbase_ref, blk_ref, ends_ref, *, nbp, ntp):
    cnt = cnt_ref[...]
    n8 = jnp.floor((cnt + (SUBLANES - 1.0)) * (1.0 / SUBLANES)) * float(SUBLANES)
    r = lax.broadcasted_iota(jnp.int32, (LANES, LANES), 0)
    c = lax.broadcasted_iota(jnp.int32, (LANES, LANES), 1)
    loff = _dot_sel_rhs(n8, jnp.where(r < c, 1.0, 0.0).astype(BF16))
    gtot = jnp.broadcast_to(jnp.sum(n8, axis=0, keepdims=True), (SUBLANES, LANES))
    nblk = jnp.floor((gtot + (EXPERT_ROWS - 1.0)) * (1.0 / EXPERT_ROWS))
    ends = _dot_sel_rhs(nblk, jnp.where(r <= c, 1.0, 0.0).astype(BF16))
    start_row = (ends[0:1, :] - nblk[0:1, :]) * float(EXPERT_ROWS)
    tr = lax.broadcasted_iota(jnp.int32, (ntp, ntp), 0)
    tc = lax.broadcasted_iota(jnp.int32, (ntp, ntp), 1)
    gbase = start_row + _dot_sel_lhs(jnp.where(tc < tr, 1.0, 0.0).astype(BF16), n8)
    n8_ref[...] = n8.astype(jnp.int32)
    loff_ref[...] = loff.astype(jnp.int32)
    gbase_ref[...] = gbase.astype(jnp.int32)
    bi = lax.broadcasted_iota(jnp.int32, (nbp, LANES), 0).astype(F32)
    li = lax.broadcasted_iota(jnp.int32, (nbp, LANES), 1)
    done = jnp.where((li < N_EXPERTS) & (ends[0:1, :] <= bi), 1.0, 0.0)
    be = jnp.minimum(jnp.sum(done, axis=-1, keepdims=True), N_EXPERTS - 1.0)
    blk_ref[...] = jnp.broadcast_to(be, (nbp, LANES)).astype(jnp.int32)
    r8 = lax.broadcasted_iota(jnp.int32, (SUBLANES, LANES), 0)
    tail_start = start_row + gtot[0:1, :]
    tail_len = nblk[0:1, :] * float(EXPERT_ROWS) - gtot[0:1, :]
    info = jnp.where(r8 == 0, ends, jnp.where(r8 == 1, tail_start, jnp.where(r8 == 2, tail_len, 0.0)))
    ends_ref[...] = info.astype(jnp.int32)


def _plan_call(cnt_tiles, nbp):
    ntp = cnt_tiles.shape[0]
    const = lambda i: (0, 0)
    tbl = jax.ShapeDtypeStruct((ntp, LANES), jnp.int32)
    return pl.pallas_call(
        functools.partial(_plan_kernel, nbp=nbp, ntp=ntp),
        grid=(1,),
        in_specs=[pl.BlockSpec((ntp, LANES), const)],
        out_specs=[pl.BlockSpec((ntp, LANES), const), pl.BlockSpec((ntp, LANES), const),
                   pl.BlockSpec((ntp, LANES), const),
                   pl.BlockSpec((nbp, LANES), const), pl.BlockSpec((SUBLANES, LANES), const)],
        out_shape=[tbl, tbl, tbl,
                   jax.ShapeDtypeStruct((nbp, LANES), jnp.int32),
                   jax.ShapeDtypeStruct((SUBLANES, LANES), jnp.int32)],
        compiler_params=_cparams(("arbitrary",)),
        name="plan",
    )(cnt_tiles)


_GROUP_BITS = tuple(range(3, DISPATCH_TILE.bit_length()))


_TOTAL_BITS = tuple(range(3, SORT_ROWS.bit_length()))


def _group_copies(n8_ref, loff_ref, gbase_ref, tile, make_copy, wait):
    if wait:
        total = lax.fori_loop(0, N_EXPERTS, lambda e, acc: acc + n8_ref[tile * N_EXPERTS + e], 0)
        for bit in _TOTAL_BITS:
            size = 1 << bit

            @pl.when((total & size) != 0)
            def _():
                make_copy(0, 0, size).wait()
        return

    def per_expert(e, carry):
        idx = tile * N_EXPERTS + e
        n = n8_ref[idx]
        off = loff_ref[idx]
        base = gbase_ref[idx]
        for bit in _GROUP_BITS:
            size = 1 << bit

            @pl.when((n & size) != 0)
            def _():
                done = n & ~(2 * size - 1)
                make_copy(pl.multiple_of(off + done, SUBLANES), pl.multiple_of(base + done, SUBLANES), size).start()
        return carry

    lax.fori_loop(0, N_EXPERTS, per_expert, 0)


def _zero_fill(tails_ref, nu_ref, n_blocks, make_zero_copy, wait):
    def finish(cp):
        cp.wait() if wait else cp.start()

    def per_expert(e, carry):
        base = tails_ref[e]
        n = tails_ref[N_EXPERTS + e]
        for bit in range(3, EXPERT_ROWS.bit_length() - 1):
            size = 1 << bit

            @pl.when((n & size) != 0)
            def _():
                done = n & ~(2 * size - 1)
                finish(make_zero_copy(pl.multiple_of(base + done, SUBLANES), size))
        return carry

    lax.fori_loop(0, N_EXPERTS, per_expert, 0)

    def per_block(blk, carry):
        finish(make_zero_copy(pl.multiple_of(blk * EXPERT_ROWS, EXPERT_ROWS), EXPERT_ROWS))
        return carry

    lax.fori_loop(nu_ref[0], n_blocks, per_block, 0)


def _dispatch_kernel(n8_ref, loff_ref, gbase_ref, tails_ref, nu_ref, hf_ref, lpos_ref, gate_ref, xs_ref,
                     sbuf, sem, *, n_blocks):
    i = pl.program_id(0)

    @pl.when(i == 0)
    def _():
        sbuf[1] = jnp.zeros((SORT_ROWS, XS_WIDTH), F32)

        def make_zero_copy(dst_row, size):
            return pltpu.make_async_copy(sbuf.at[1, pl.ds(0, size), :], xs_ref.at[pl.ds(dst_row, size), :], sem.at[1])

        _zero_fill(tails_ref, nu_ref, n_blocks, make_zero_copy, wait=False)
        _zero_fill(tails_ref, nu_ref, n_blocks, make_zero_copy, wait=True)

    eye8 = jnp.where(lax.broadcasted_iota(jnp.int32, (SUBLANES, LANES), 0)
                     == lax.broadcasted_iota(jnp.int32, (SUBLANES, LANES), 1), 1.0, 0.0).astype(BF16)
    lpos_t = _dot_sel_nt(eye8, lpos_ref[...])
    gate_t = _dot_sel_nt(eye8, gate_ref[...])
    row = lax.broadcasted_iota(jnp.int32, (SORT_ROWS, DISPATCH_TILE), 0).astype(F32)
    perm = jnp.zeros((SORT_ROWS, DISPATCH_TILE), F32)
    wgate = jnp.zeros((SORT_ROWS, DISPATCH_TILE), F32)
    for kk in range(TOP_K):
        hit = row == lpos_t[kk:kk + 1, :]
        perm = jnp.where(hit, 1.0, perm)
        wgate = jnp.where(hit, gate_t[kk:kk + 1, :], wgate)
    slot = i % 2
    sbuf[slot, :, 0:D_MODEL] = _dot(perm.astype(BF16), hf_ref[...].astype(BF16))
    sbuf[slot, :, D_MODEL:XS_WIDTH] = _dot_sel_rhs(wgate, jnp.ones((DISPATCH_TILE, LANES), BF16))

    def copies(tile, buf_slot, wait):
        def make_copy(src_row, dst_row, size):
            return pltpu.make_async_copy(sbuf.at[buf_slot, pl.ds(src_row, size), :],
                                         xs_ref.at[pl.ds(dst_row, size), :], sem.at[buf_slot])
        _group_copies(n8_ref, loff_ref, gbase_ref, tile, make_copy, wait=wait)

    copies(i, slot, wait=False)

    @pl.when(i > 0)
    def _():
        copies(i - 1, 1 - slot, wait=True)

    @pl.when(i == pl.num_programs(0) - 1)
    def _():
        copies(i, slot, wait=True)


def _dispatch_call(tables, tails, n_used, hf, lpos, gate, n_blocks):
    ttot = hf.shape[0]
    tm = DISPATCH_TILE
    tile = lambda i, *_: (i, 0)
    grid_spec = pltpu.PrefetchScalarGridSpec(
        num_scalar_prefetch=5, grid=(ttot // tm,),
        in_specs=[pl.BlockSpec((tm, D_MODEL), tile), pl.BlockSpec((tm, LANES), tile),
                  pl.BlockSpec((tm, LANES), tile)],
        out_specs=pl.BlockSpec(memory_space=pl.ANY),
        scratch_shapes=[pltpu.VMEM((2, SORT_ROWS, XS_WIDTH), F32), pltpu.SemaphoreType.DMA((2,))])
    return pl.pallas_call(
        functools.partial(_dispatch_kernel, n_blocks=n_blocks),
        grid_spec=grid_spec,
        out_shape=jax.ShapeDtypeStruct((n_blocks * EXPERT_ROWS, XS_WIDTH), F32),
        compiler_params=pltpu.CompilerParams(dimension_semantics=("arbitrary",),
                                             vmem_limit_bytes=VMEM_LIMIT, has_side_effects=True),
        name="dispatch",
    )(*tables, tails, n_used, hf, lpos, gate)


def _expert_kernel(be_ref, nu_ref, xs_ref, wgu_ref, bgu_ref, wdn_ref, bdn_ref, y_ref,
                   wgu_bf, wdn_bf):
    b = pl.program_id(0)

    @pl.when(b < nu_ref[0])
    def _():
        prev = be_ref[jnp.maximum(b - 1, 0)]

        @pl.when((b == 0) | (be_ref[b] != prev))
        def _():
            wgu_bf[...] = wgu_ref[0, 0].astype(BF16)
            wdn_bf[...] = wdn_ref[0, 0].astype(BF16)

        gu = _dot(xs_ref[:, 0:D_MODEL].astype(BF16), wgu_bf[...]) + bgu_ref[0, 0]
        g = jnp.minimum(gu[:, :D_MODEL], SWIGLU_LIMIT)
        u = jnp.clip(gu[:, D_MODEL:], -SWIGLU_LIMIT, SWIGLU_LIMIT)
        act = (u + 1.0) * (g * _sigmoid(SWIGLU_ALPHA * g))
        gate = xs_ref[:, D_MODEL:D_MODEL + 1]
        y_ref[...] = (_dot(act.astype(BF16), wdn_bf[...]) + bdn_ref[0, 0]) * gate

    @pl.when(b >= nu_ref[0])
    def _():
        y_ref[...] = jnp.zeros_like(y_ref)


def _expert_call(blk_expert, n_used, xs, w_gu, b_gu, w_dn, b_dn, layer):
    nb = xs.shape[0] // EXPERT_ROWS
    d_ff2 = w_gu.shape[3]
    blk = lambda b, be, nu: (jnp.minimum(b, nu[0] - 1), 0)
    blk_out = lambda b, be, nu: (b, 0)
    exp4 = lambda b, be, nu: (layer, be[jnp.minimum(b, nu[0] - 1)], 0, 0)
    grid_spec = pltpu.PrefetchScalarGridSpec(
        num_scalar_prefetch=2, grid=(nb,),
        in_specs=[pl.BlockSpec((EXPERT_ROWS, XS_WIDTH), blk),
                  pl.BlockSpec((1, 1, D_MODEL, d_ff2), exp4),
                  pl.BlockSpec((1, 1, 1, d_ff2), exp4),
                  pl.BlockSpec((1, 1, D_MODEL, D_MODEL), exp4),
                  pl.BlockSpec((1, 1, 1, D_MODEL), exp4)],
        out_specs=pl.BlockSpec((EXPERT_ROWS, D_MODEL), blk_out),
        scratch_shapes=[pltpu.VMEM((D_MODEL, d_ff2), BF16), pltpu.VMEM((D_MODEL, D_MODEL), BF16)])
    return pl.pallas_call(
        _expert_kernel,
        grid_spec=grid_spec,
        out_shape=jax.ShapeDtypeStruct((xs.shape[0], D_MODEL), F32),
        compiler_params=_cparams(("arbitrary",)),
        name="experts",
    )(blk_expert, n_used, xs, w_gu, b_gu, w_dn, b_dn)


def _combine_kernel(*refs, tm, npc, final):
    if final:
        (n8_ref, loff_ref, gbase_ref, x_ref, lpos_ref, mod_ref, fnw_ref, y_hbm,
         yp_ref, ys_ref, ybuf, sem, ynorm) = refs
    else:
        n8_ref, loff_ref, gbase_ref, x_ref, lpos_ref, mod_ref, y_hbm, xo_ref, ybuf, sem = refs
    i = pl.program_id(0)
    nch = tm // CHUNK

    slot = i % 2

    def copies(tile, buf_slot, wait):
        def make_copy(buf_row, src_row, size):
            return pltpu.make_async_copy(y_hbm.at[pl.ds(src_row, size), :],
                                         ybuf.at[buf_slot, pl.ds(buf_row, size), :], sem.at[buf_slot])
        _group_copies(n8_ref, loff_ref, gbase_ref, tile, make_copy, wait=wait)

    @pl.when(i == 0)
    def _():
        ybuf[...] = jnp.zeros_like(ybuf)
        copies(i, slot, wait=False)

    @pl.when(i + 1 < pl.num_programs(0))
    def _():
        copies(i + 1, 1 - slot, wait=False)

    copies(i, slot, wait=True)

    lpos = lpos_ref[...]
    col = lax.broadcasted_iota(jnp.int32, (tm, SORT_ROWS), 1).astype(F32)
    unperm = jnp.zeros((tm, SORT_ROWS), F32)
    for kk in range(TOP_K):
        unperm = jnp.where(col == lpos[:, kk:kk + 1], 1.0, unperm)
    acc = _dot_sel_lhs2(unperm.astype(BF16), ybuf[slot])
    for c in range(nch):
        seq = _seq_row(i, nch, c, npc)
        gf = mod_ref[pl.ds(seq, 1), 5 * D_MODEL:6 * D_MODEL]
        rs = slice(c * CHUNK, (c + 1) * CHUNK)
        xn = x_ref[rs, :] + gf * acc[rs, :]
        if final:
            ynorm[rs, :] = _rms(xn) * fnw_ref[...]
        else:
            xo_ref[rs, :] = xn
    if final:
        is_prompt = i < (npc * CHUNK) // tm

        @pl.when(is_prompt)
        def _():
            yp_ref[...] = ynorm[...]

        @pl.when(jnp.logical_not(is_prompt))
        def _():
            ys_ref[...] = ynorm[...]


def _combine_call(tables, x, lpos, mod_l, y_sorted, npc, final_w):
    ttot = x.shape[0]
    tm = DISPATCH_TILE
    final = final_w is not None
    npt = (npc * CHUNK) // tm
    tile = lambda i, *_: (i, 0)
    const = lambda i, *_: (0, 0)
    in_specs = [pl.BlockSpec((tm, D_MODEL), tile),
                pl.BlockSpec((tm, LANES), tile),
                pl.BlockSpec(mod_l.shape, const)]
    args = [x, lpos, mod_l]
    if final:
        in_specs.append(pl.BlockSpec((1, D_MODEL), const))
        args.append(final_w)
    in_specs.append(pl.BlockSpec(memory_space=pl.ANY))
    args.append(y_sorted)
    if final:
        out_specs = [pl.BlockSpec((tm, D_MODEL), lambda i, *_: (jnp.minimum(i, npt - 1), 0)),
                     pl.BlockSpec((tm, D_MODEL), lambda i, *_: (jnp.maximum(i - npt, 0), 0))]
        out_shape = [jax.ShapeDtypeStruct((npt * tm, D_MODEL), F32),
                     jax.ShapeDtypeStruct((ttot - npt * tm, D_MODEL), F32)]
    else:
        out_specs = [pl.BlockSpec((tm, D_MODEL), tile)]
        out_shape = [jax.ShapeDtypeStruct((ttot, D_MODEL), F32)]
    scratch = [pltpu.VMEM((2, SORT_ROWS, D_MODEL), F32), pltpu.SemaphoreType.DMA((2,))]
    if final:
        scratch.append(pltpu.VMEM((tm, D_MODEL), F32))
    grid_spec = pltpu.PrefetchScalarGridSpec(
        num_scalar_prefetch=3, grid=(ttot // tm,),
        in_specs=in_specs, out_specs=out_specs, scratch_shapes=scratch)
    return pl.pallas_call(
        functools.partial(_combine_kernel, tm=tm, npc=npc, final=final),
        grid_spec=grid_spec, out_shape=out_shape,
        compiler_params=_cparams(("arbitrary",)),
        name="combine",
    )(*tables, *args)


def kernel(x_prompt, x_sample, c_prompt, c_sample, state_mlstm_C, state_mlstm_n, state_mlstm_m, state_mlstm_conv, state_ret_S, state_hgrn_S, w_ada, b_ada, norm_mix_w, norm_ffn_w, final_norm_w, w_in_even, b_mlstm_i, b_mlstm_f, w_mlstm_conv, b_mlstm_conv, w_mlstm_q, w_mlstm_k, mlstm_skip, mlstm_norm_w, ret_norm_w, w_out_even, w_in_odd, hgrn_lb_logits, hgrn_norm_w, w_out_odd, moe_router_w, moe_router_b, moe_w_gate_up, moe_b_gate_up, moe_w_down, moe_b_down):
    bp, seq, d = x_prompt.shape
    bs, dseq, _ = x_sample.shape
    assert bp == 1 and d == D_MODEL and dseq == CHUNK
    assert seq % TOKEN_TILE == 0 and (bs * dseq) % TOKEN_TILE == 0
    depth = w_ada.shape[0]
    tp, tsmp = seq, bs * dseq
    ttot = tp + tsmp
    npc = tp // CHUNK
    he, ho = N_HEADS_EVEN, N_HEADS_ODD
    da = he * LANES
    past_len = 1024

    x = jnp.concatenate([x_prompt.reshape(tp, d), x_sample.reshape(tsmp, d)], axis=0)
    n_mod_rows = 2 * SUBLANES
    assert 1 + bs <= n_mod_rows
    c_all = jnp.zeros((n_mod_rows, d), F32).at[0:1].set(c_prompt).at[1:1 + bs].set(c_sample)
    mod = _ada_call(c_all, w_ada, b_ada)

    half = LANES // 2
    inv = ROPE_BASE ** (-jnp.arange(half, dtype=F32) / half)
    pos_all = jnp.concatenate([jnp.arange(tp, dtype=F32),
                               jnp.tile(past_len + jnp.arange(dseq, dtype=F32), bs)])
    ang = pos_all[:, None] * inv[None, :]
    cos_t = jnp.concatenate([jnp.cos(ang), jnp.cos(ang)], axis=-1)
    sin_t = jnp.concatenate([-jnp.sin(ang), jnp.sin(ang)], axis=-1)

    lb_p = jax.nn.softmax(hgrn_lb_logits.astype(F32), axis=0)
    lbs = jnp.cumsum(lb_p, axis=0) - lb_p[0]

    n_tiles = ttot // DISPATCH_TILE
    ntp = -(-n_tiles // LANES) * LANES
    max_rows = ttot * TOP_K + n_tiles * N_EXPERTS * (SUBLANES - 1)
    nb = -(-max_rows // EXPERT_ROWS) + N_EXPERTS
    nbp = -(-nb // SUBLANES) * SUBLANES

    ts_p = TOKEN_TILE
    steps_p = tp // ts_p
    even_out, odd_out = [], []
    y_final = None
    for l in range(depth):
        jl = l // 2
        mod_l = mod[l]
        if l % 2 == 0:
            w_in = w_in_even[jl]
            w_main = jnp.concatenate([w_in[:, da:3 * da], w_in[:, 3 * da + 2 * he:]], axis=1).astype(BF16)
            w_gate = jnp.zeros((d, da + LANES), F32).at[:, :da].set(w_in[:, :da])
            w_gate = w_gate.at[:, da:da + 2 * he].set(w_in[:, 3 * da:3 * da + 2 * he])
            proj, gates = _inproj_call(x, mod_l, norm_mix_w[l][None], w_main, w_gate, npc)
            gbias = jnp.zeros((1, LANES), F32).at[0, :he].set(b_mlstm_i[jl]).at[0, he:2 * he].set(b_mlstm_f[jl])
            weights = [w_mlstm_conv[jl], b_mlstm_conv[jl][None],
                       w_mlstm_q[jl], w_mlstm_k[jl], gbias,
                       mlstm_skip[jl][None], mlstm_norm_w[jl][None], ret_norm_w[jl][None]]
            zeros_p = (jnp.zeros((1, he, LANES, LANES), F32), jnp.zeros((1, he, 1, LANES), F32),
                       jnp.zeros((1, 1, LANES), F32), jnp.zeros((1, SUBLANES, da), F32),
                       jnp.zeros((1, he, LANES, LANES), F32))
            st_s = (state_mlstm_C[jl], state_mlstm_n[jl][:, :, None, :],
                    jnp.zeros((bs, 1, LANES), F32).at[:, 0, :he].set(state_mlstm_m[jl]),
                    jnp.zeros((bs, SUBLANES, da), F32).at[:, SUBLANES - (CONV_W - 1):].set(state_mlstm_conv[jl]),
                    state_ret_S[jl])
            res_p = _even_scan_call(proj, gates, cos_t, sin_t, zeros_p, weights,
                                    ts=ts_p, n_seq=1, steps=steps_p, row_off=0)
            res_s = _even_scan_call(proj, gates, cos_t, sin_t, st_s, weights,
                                    ts=CHUNK, n_seq=bs, steps=1, row_off=npc)
            u_p, u_s = res_p[0], res_s[0]
            even_out.append((res_p[1:], res_s[1:]))
            w_out = w_out_even[jl].astype(BF16)
        else:
            proj = _inproj_call(x, mod_l, norm_mix_w[l][None], w_in_odd[jl].astype(BF16), None, npc)[0]
            lb = lbs[l][None]
            nw = hgrn_norm_w[jl][None]
            u_p, sp = _odd_scan_call(proj, jnp.zeros((1, ho, LANES, LANES), F32), lb, nw,
                                     ts=ts_p, n_seq=1, steps=steps_p, row_off=0)
            u_s, ss = _odd_scan_call(proj, state_hgrn_S[jl], lb, nw,
                                     ts=CHUNK, n_seq=bs, steps=1, row_off=npc)
            odd_out.append((sp, ss))
            w_out = w_out_odd[jl].astype(BF16)

        w_r = jnp.zeros((d, LANES), F32).at[:, :N_EXPERTS].set(moe_router_w[l])
        b_r = jnp.full((1, LANES), -jnp.inf, F32).at[0, :N_EXPERTS].set(moe_router_b[l])
        x, hf, gate, lpos, cnt = _post_call(x, u_p, u_s, mod_l, w_out, norm_ffn_w[l][None], w_r, b_r, npc)
        cnt_tiles = jnp.zeros((ntp, LANES), F32).at[:n_tiles].set(cnt[:, 0, :])
        n8, loff, gbase, blk, ends = _plan_call(cnt_tiles, nbp)
        tables = [t[:n_tiles, :N_EXPERTS].reshape(-1) for t in (n8, loff, gbase)]
        blk_expert = blk[:nb, 0]
        n_used = ends[0, N_EXPERTS - 1:N_EXPERTS]
        tails = ends[1:3, :N_EXPERTS].reshape(-1)
        xs = _dispatch_call(tables, tails, n_used, hf, lpos, gate, nb)
        y_sorted = _expert_call(blk_expert, n_used, xs, moe_w_gate_up, moe_b_gate_up[:, :, None, :],
                                moe_w_down, moe_b_down[:, :, None, :], l)
        if l == depth - 1:
            y_final = _combine_call(tables, x, lpos, mod_l, y_sorted, npc, final_norm_w[None])
        else:
            x = _combine_call(tables, x, lpos, mod_l, y_sorted, npc, None)[0]

    def even_states(which):
        cs = jnp.stack([e[which][0] for e in even_out])
        ns = jnp.stack([e[which][1][:, :, 0, :] for e in even_out])
        ms = jnp.stack([e[which][2][:, 0, :he] for e in even_out])
        cv = jnp.stack([e[which][3][:, SUBLANES - (CONV_W - 1):, :] for e in even_out])
        ss = jnp.stack([e[which][4] for e in even_out])
        return cs, ns, ms, cv, ss

    p_c, p_n, p_m, p_cv, p_s = even_states(0)
    s_c, s_n, s_m, s_cv, s_s = even_states(1)
    p_h = jnp.stack([o[0] for o in odd_out])
    s_h = jnp.stack([o[1] for o in odd_out])
    y_prompt = y_final[0].reshape(bp, seq, d)
    y_sample = y_final[1].reshape(bs, dseq, d)
    return (y_prompt, y_sample, p_c, p_n, p_m, p_cv, p_s, p_h, s_c, s_n, s_m, s_cv, s_s, s_h)
```

```python
import functools
import math

import jax
import jax.numpy as jnp
from jax import lax
from jax.experimental import pallas as pl
from jax.experimental.pallas import tpu as pltpu

F32 = jnp.float32
BF16 = jnp.bfloat16

CHUNK = 64
LANES = 128
SUBLANES = 8
D_MODEL = 1024
N_HEADS_EVEN = 4
N_HEADS_ODD = 8
CONV_W = 4
N_EXPERTS = 32
TOP_K = 4
SWIGLU_LIMIT = 7.0
SWIGLU_ALPHA = 1.702
EPS = 1e-6
ROPE_BASE = 10000.0

TOKEN_TILE = 512
INPROJ_TILE = 512
DISPATCH_TILE = 256
SORT_ROWS = DISPATCH_TILE * TOP_K + LANES * 2
XS_WIDTH = D_MODEL + LANES
EXPERT_ROWS = 512
VMEM_LIMIT = 56 * 1024 * 1024


def _cparams(sem, vmem=VMEM_LIMIT):
    return pltpu.CompilerParams(dimension_semantics=sem, vmem_limit_bytes=vmem)


def _dot(a, b):
    return jnp.dot(a, b, preferred_element_type=F32)


def _dot_nt(a, b):
    return lax.dot_general(a, b, (((1,), (1,)), ((), ())), preferred_element_type=F32)


def _dot_tn(a, b):
    return lax.dot_general(a, b, (((0,), (0,)), ((), ())), preferred_element_type=F32)


def _split3(x):
    p1 = x.astype(BF16)
    r1 = x - p1.astype(F32)
    p2 = r1.astype(BF16)
    p3 = (r1 - p2.astype(F32)).astype(BF16)
    return p1, p2, p3


def _dot3(a, b):
    ah = a.astype(BF16)
    al = (a - ah.astype(F32)).astype(BF16)
    bh = b.astype(BF16)
    bl = (b - bh.astype(F32)).astype(BF16)
    return _dot(ah, bh) + (_dot(ah, bl) + _dot(al, bh))


def _dot3_tn_fused(a, b):
    n = a.shape[1]
    ah = a.astype(BF16)
    al = (a - ah.astype(F32)).astype(BF16)
    bh = b.astype(BF16)
    bl = (b - bh.astype(F32)).astype(BF16)
    x = _dot_tn(jnp.concatenate([ah, al], axis=1), jnp.concatenate([bh, bl], axis=1))
    return x[:n, :n] + (x[:n, n:] + x[n:, :n])


def _block_diag(a, b):
    z = jnp.zeros_like(a)
    return jnp.concatenate([jnp.concatenate([a, z], axis=1), jnp.concatenate([z, b], axis=1)], axis=0)


def _dot_sel_lhs(sel_bf16, x):
    p1, p2, p3 = _split3(x)
    return _dot(sel_bf16, p1) + (_dot(sel_bf16, p2) + _dot(sel_bf16, p3))


def _dot_sel_nt(sel_bf16, x):
    p1, p2, p3 = _split3(x)
    return _dot_nt(sel_bf16, p1) + (_dot_nt(sel_bf16, p2) + _dot_nt(sel_bf16, p3))


def _dot_sel_rhs(x, sel_bf16):
    p1, p2, p3 = _split3(x)
    return _dot(p1, sel_bf16) + (_dot(p2, sel_bf16) + _dot(p3, sel_bf16))


def _dot_sel_lhs2(sel_bf16, x):
    hi = x.astype(BF16)
    lo = (x - hi.astype(F32)).astype(BF16)
    return _dot(sel_bf16, hi) + _dot(sel_bf16, lo)


def _sigmoid(x):
    return 1.0 / (1.0 + jnp.exp(-x))


def _log_sigmoid(x):
    return jnp.minimum(x, 0.0) - jnp.log1p(jnp.exp(-jnp.abs(x)))


def _rms(x):
    return x * lax.rsqrt(jnp.mean(x * x, axis=-1, keepdims=True) + EPS)


def _chunk_tri(n):
    r = lax.broadcasted_iota(jnp.int32, (n, n), 0)
    c = lax.broadcasted_iota(jnp.int32, (n, n), 1)
    return jnp.where((r // CHUNK == c // CHUNK) & (c <= r), 1.0, 0.0).astype(BF16)


def _chunk_cumsum(x):
    rows, n = x.shape
    vregs = CHUNK // SUBLANES
    x4 = x.reshape(rows // CHUNK, vregs, SUBLANES, n)
    sub = lax.broadcasted_iota(jnp.int32, x4.shape, 2)
    s = x4
    for shift in (1, 2, 4):
        s = s + jnp.where(sub >= shift, pltpu.roll(s, shift, 2), 0.0)
    outs, carry = [], None
    for v in range(vregs):
        cur = s[:, v] if carry is None else s[:, v] + carry
        outs.append(cur)
        carry = jnp.broadcast_to(cur[:, SUBLANES - 1:SUBLANES, :], cur.shape)
    return jnp.stack(outs, axis=1).reshape(rows, n)


def _seq_row(tile_idx, chunks_per_tile, c, n_prompt_chunks):
    return jnp.maximum(tile_idx * chunks_per_tile + c - (n_prompt_chunks - 1), 0)


def _ada_kernel(c_ref, w_ref, b_ref, o_ref):
    c = c_ref[...]
    o_ref[0] = _dot3(c * _sigmoid(c), w_ref[0]) + b_ref[0]


def _ada_call(c_all, w_ada, b_ada):
    depth = w_ada.shape[0]
    nrow = c_all.shape[0]
    ncol = w_ada.shape[2] // D_MODEL
    return pl.pallas_call(
        _ada_kernel,
        grid=(depth, ncol),
        in_specs=[pl.BlockSpec((nrow, D_MODEL), lambda l, j: (0, 0)),
                  pl.BlockSpec((1, D_MODEL, D_MODEL), lambda l, j: (l, 0, j)),
                  pl.BlockSpec((1, 1, D_MODEL), lambda l, j: (l, 0, j))],
        out_specs=pl.BlockSpec((1, nrow, D_MODEL), lambda l, j: (l, 0, j)),
        out_shape=jax.ShapeDtypeStruct((depth, nrow, ncol * D_MODEL), F32),
        compiler_params=_cparams(("arbitrary", "arbitrary")),
        name="ada",
    )(c_all, w_ada, b_ada.reshape(depth, 1, -1))


def _inproj_kernel(*refs, tm, ng, npc, n_hp):
    if n_hp:
        x_ref, mod_ref, nw_ref, w_ref, whp_ref, proj_ref, gates_ref, h_scr = refs
    else:
        x_ref, mod_ref, nw_ref, w_ref, proj_ref, h_scr = refs
    i = pl.program_id(0)
    nch = tm // CHUNK
    for c in range(nch):
        seq = _seq_row(i, nch, c, npc)
        sh = mod_ref[pl.ds(seq, 1), 0:D_MODEL]
        sc = mod_ref[pl.ds(seq, 1), D_MODEL:2 * D_MODEL]
        xc = x_ref[c * CHUNK:(c + 1) * CHUNK, :]
        h_scr[c * CHUNK:(c + 1) * CHUNK, :] = _rms(xc) * nw_ref[...] * (1.0 + sc) + sh
    h = h_scr[...]
    hb = h.astype(BF16)
    for g in range(0, ng - n_hp, 4):
        res = _dot(hb, w_ref[:, g * LANES:(g + 4) * LANES])
        for jj in range(4):
            proj_ref[n_hp + g + jj] = res[:, jj * LANES:(jj + 1) * LANES]
    if n_hp:
        res = _dot3(h, whp_ref[...])
        for jj in range(n_hp):
            proj_ref[jj] = res[:, jj * LANES:(jj + 1) * LANES]
        gates_ref[...] = res[:, n_hp * LANES:]


def _inproj_call(x, mod_l, nw, w_main, w_gate, npc):
    ttot = x.shape[0]
    tm = INPROJ_TILE
    has_gates = w_gate is not None
    n_hp = w_gate.shape[1] // LANES - 1 if has_gates else 0
    ng = w_main.shape[1] // LANES + n_hp
    in_specs = [pl.BlockSpec((tm, D_MODEL), lambda i: (i, 0)),
                pl.BlockSpec(mod_l.shape, lambda i: (0, 0)),
                pl.BlockSpec((1, D_MODEL), lambda i: (0, 0)),
                pl.BlockSpec(w_main.shape, lambda i: (0, 0))]
    out_specs = [pl.BlockSpec((ng, tm, LANES), lambda i: (0, i, 0))]
    out_shape = [jax.ShapeDtypeStruct((ng, ttot, LANES), F32)]
    args = [x, mod_l, nw, w_main]
    if has_gates:
        in_specs.append(pl.BlockSpec(w_gate.shape, lambda i: (0, 0)))
        out_specs.append(pl.BlockSpec((tm, LANES), lambda i: (i, 0)))
        out_shape.append(jax.ShapeDtypeStruct((ttot, LANES), F32))
        args.append(w_gate)
    return pl.pallas_call(
        functools.partial(_inproj_kernel, tm=tm, ng=ng, npc=npc, n_hp=n_hp),
        grid=(ttot // tm,),
        in_specs=in_specs, out_specs=out_specs, out_shape=out_shape,
        scratch_shapes=[pltpu.VMEM((tm, D_MODEL), F32)],
        compiler_params=_cparams(("arbitrary",)),
        name="inproj",
    )(*args)


_G_XM, _G_VA, _G_OA, _G_QB, _G_KB, _G_VB, _G_GB = 0, 4, 8, 12, 16, 20, 24


def _even_scan_kernel(proj_ref, gates_ref, cos_ref, sin_ref,
                      c0_ref, n0_ref, m0_ref, conv0_ref, s0_ref,
                      cw_ref, cb_ref, wqk_ref, gbias_ref, skip_ref, nwa_ref, nwb_ref,
                      u_ref, co_ref, no_ref, mo_ref, convo_ref, so_ref,
                      c_scr, n_scr, m_scr, conv_scr, s_scr,
                      xbuf, xc_scr, q_scr, k_scr, qr_scr, kr_scr,
                      gl_scr, bc_scr, rows_scr, dec_scr, *, ts):
    H = N_HEADS_EVEN
    j = pl.program_id(1)
    nj = pl.num_programs(1)
    nc = ts // CHUNK

    @pl.when(j == 0)
    def _():
        for p in range(H // 2):
            c_scr[p] = _block_diag(c0_ref[0, 2 * p], c0_ref[0, 2 * p + 1])
            s_scr[p] = _block_diag(s0_ref[0, 2 * p], s0_ref[0, 2 * p + 1])
        n_scr[...] = n0_ref[0]
        m_scr[...] = m0_ref[0]
        conv_scr[...] = conv0_ref[0]

    for g in range(H):
        lo, hi = g * LANES, (g + 1) * LANES
        x_g = proj_ref[_G_XM + g]
        xbuf[0:SUBLANES, :] = conv_scr[:, lo:hi]
        xbuf[SUBLANES:SUBLANES + ts, :] = x_g
        acc = cb_ref[:, lo:hi] + cw_ref[CONV_W - 1:CONV_W, lo:hi] * x_g
        for t in range(CONV_W - 1):
            off = SUBLANES - (CONV_W - 1) + t
            acc = acc + cw_ref[t:t + 1, lo:hi] * xbuf[off:off + ts, :]
        conv_scr[:, lo:hi] = xbuf[ts:ts + SUBLANES, :]
        xc = acc * _sigmoid(acc)
        xc_scr[g] = xc
        qk = _dot3(xc, wqk_ref[g])
        q_scr[g] = qk[:, 0:LANES]
        k_scr[g] = qk[:, LANES:2 * LANES] * (LANES ** -0.5)

    cosv = cos_ref[...]
    sinv = sin_ref[...]
    for g in range(H):
        qb = proj_ref[_G_QB + g]
        kb = proj_ref[_G_KB + g]
        qr_scr[g] = qb * cosv + pltpu.roll(qb, LANES // 2, 1) * sinv
        kr_scr[g] = (kb * cosv + pltpu.roll(kb, LANES // 2, 1) * sinv) * (LANES ** -0.5)

    gpre = gates_ref[...] + gbias_ref[...]
    lane = lax.broadcasted_iota(jnp.int32, (ts, LANES), 1)
    gl = jnp.where(lane < H, gpre, _log_sigmoid(gpre))
    gl_scr[...] = gl
    bc = _dot_sel_lhs(_chunk_tri(ts), gl)
    bc_scr[...] = bc
    comb = jnp.where(lane < H, gl, bc)
    even_head = (lane % 2) == 0
    comb_even = jnp.where(even_head, comb, 0.0)
    comb_odd = jnp.where(even_head, 0.0, comb)
    pr = lax.broadcasted_iota(jnp.int32, (SUBLANES, LANES), 0)
    pc = lax.broadcasted_iota(jnp.int32, (SUBLANES, LANES), 1)
    pair_sel = jnp.where((pc // 2 == pr) & (pc < 2 * H), 1.0, 0.0).astype(BF16)
    for c in range(nc):
        cs = slice(c * CHUNK, (c + 1) * CHUNK)
        rows_scr[c] = _dot_sel_nt(pair_sel, jnp.concatenate([comb_even[cs, :], comb_odd[cs, :]], axis=0))

    ti = lax.broadcasted_iota(jnp.int32, (CHUNK, LANES), 0)
    lane2 = lax.broadcasted_iota(jnp.int32, (CHUNK, LANES), 1)
    left = lane2 < CHUNK
    si = lane2 % CHUNK
    tril = ti >= si
    left_wide = lax.broadcasted_iota(jnp.int32, (CHUNK, 2 * LANES), 1) < LANES
    br = lax.broadcasted_iota(jnp.int32, (2 * LANES, 2 * LANES), 0) < LANES
    bcol_blk = lax.broadcasted_iota(jnp.int32, (2 * LANES, 2 * LANES), 1) < LANES
    tcol = lax.broadcasted_iota(jnp.int32, (CHUNK, 1), 0).astype(F32)
    log_gamma = [math.log1p(-2.0 ** (-5 - h)) for h in range(H)]
    for p in range(H // 2):
        lg2 = jnp.where(left, log_gamma[2 * p], log_gamma[2 * p + 1])
        dec_scr[p] = jnp.where(tril, jnp.exp((ti - si).astype(F32) * lg2), 0.0)

    def chunk_body(c, carry):
        r0 = pl.multiple_of(c * CHUNK, CHUNK)
        rows = pl.ds(r0, CHUNK)
        pair_rows = rows_scr[c]
        glc = gl_scr[rows, :]
        bcc = bc_scr[rows, :]
        m_all = m_scr[...]
        c_old = [c_scr[p] for p in range(H // 2)]
        s_old = [s_scr[p] for p in range(H // 2)]
        n_old = [n_scr[h] for h in range(H)]
        c_new, s_new, n_new = [], [], []
        m_next = m_all
        lane_row = lax.broadcasted_iota(jnp.int32, (1, LANES), 1)
        for p in range(H // 2):
            hs = (2 * p, 2 * p + 1)
            q = [q_scr[h, rows, :] for h in hs]
            k = [k_scr[h, rows, :] for h in hs]
            v = [proj_ref[_G_VA + h, rows, :] for h in hs]
            b_col = [bcc[:, H + h:H + h + 1] for h in hs]
            i_col = [glc[:, h:h + 1] for h in hs]
            m_prev = [m_all[:, h:h + 1] for h in hs]
            q2 = jnp.concatenate(q, axis=1).astype(BF16)
            smat = _dot_nt(q2, _block_diag(k[0].astype(BF16), k[1].astype(BF16)))
            dmat = jnp.where(tril, jnp.where(left, b_col[0], b_col[1])
                             - pair_rows[2 + p:3 + p, :] + pair_rows[p:p + 1, :], -jnp.inf)
            a = [jnp.max(jnp.where(left, dmat, -jnp.inf), axis=-1, keepdims=True),
                 jnp.max(jnp.where(left, -jnp.inf, dmat), axis=-1, keepdims=True)]
            inter = [b_col[e] + m_prev[e] for e in range(2)]
            m_t = [jnp.maximum(inter[e], a[e]) for e in range(2)]
            w_inter = [jnp.exp(inter[e] - m_t[e]) for e in range(2)]
            amat = smat * jnp.exp(dmat - jnp.where(left, m_t[0], m_t[1]))
            cst = c_old[p]
            num = (_dot(amat.astype(BF16), _block_diag(v[0].astype(BF16), v[1].astype(BF16)))
                   + jnp.where(left_wide, w_inter[0], w_inter[1]) * _dot(q2, cst.astype(BF16)))
            den = [jnp.sum(jnp.where(left, amat, 0.0), axis=-1, keepdims=True),
                   jnp.sum(jnp.where(left, 0.0, amat), axis=-1, keepdims=True)]
            den = [jnp.maximum(jnp.abs(den[e] + w_inter[e] * jnp.sum(q[e] * n_old[hs[e]], axis=-1, keepdims=True)),
                               jnp.exp(-m_t[e])) for e in range(2)]
            hout = num / jnp.where(left_wide, den[0], den[1])
            decay, upd = [], []
            for e in range(2):
                m_new = m_t[e][CHUNK - 1:CHUNK, :]
                b_last = b_col[e][CHUNK - 1:CHUNK, :]
                kw = k[e] * jnp.exp(b_last - b_col[e] + i_col[e] - m_new)
                decay.append(jnp.exp(b_last + m_prev[e] - m_new))
                upd.append(_dot3_tn_fused(kw, v[e]))
                n_new.append(decay[e] * n_old[hs[e]] + jnp.sum(kw, axis=0, keepdims=True))
                m_next = jnp.where(lane_row == hs[e], m_new, m_next)
            c_new.append(jnp.where(br, decay[0], decay[1]) * cst + _block_diag(upd[0], upd[1]))
            for e in range(2):
                h = hs[e]
                lo, hi = h * LANES, (h + 1) * LANES
                z = _sigmoid(proj_ref[_G_OA + h, rows, :]) * hout[:, e * LANES:(e + 1) * LANES]
                u_ref[rows, lo:hi] = (_rms(z) * nwa_ref[:, lo:hi]
                                      + skip_ref[:, lo:hi] * xc_scr[h, rows, :])
        for p in range(H // 2):
            hs = (2 * p, 2 * p + 1)
            lg = [log_gamma[h] for h in hs]
            kr = [kr_scr[h, rows, :] for h in hs]
            vb = [proj_ref[_G_VB + h, rows, :].astype(BF16) for h in hs]
            q2 = jnp.concatenate([qr_scr[h, rows, :] for h in hs], axis=1).astype(BF16)
            amat = _dot_nt(q2, _block_diag(kr[0].astype(BF16), kr[1].astype(BF16))) * dec_scr[p]
            sst = s_old[p]
            o = (_dot(amat.astype(BF16), _block_diag(vb[0], vb[1]))
                 + jnp.exp((tcol + 1.0) * jnp.where(left_wide, lg[0], lg[1])) * _dot(q2, sst.astype(BF16)))
            kws = jnp.concatenate([kr[e] * jnp.exp((CHUNK - 1.0 - tcol) * lg[e]) for e in range(2)], axis=1)
            cross = _dot_tn(kws.astype(BF16), jnp.concatenate(vb, axis=1))
            s_new.append(jnp.where(br, math.exp(CHUNK * lg[0]), math.exp(CHUNK * lg[1])) * sst
                         + jnp.where(br == bcol_blk, cross, 0.0))
            for e in range(2):
                h = hs[e]
                lo, hi = h * LANES, (h + 1) * LANES
                gate = proj_ref[_G_GB + h, rows, :]
                u_ref[rows, D_MODEL // 2 + lo:D_MODEL // 2 + hi] = (
                    _rms(o[:, e * LANES:(e + 1) * LANES]) * nwb_ref[:, lo:hi] * (gate * _sigmoid(gate)))
        for p in range(H // 2):
            c_scr[p] = c_new[p]
            s_scr[p] = s_new[p]
        for h in range(H):
            n_scr[h] = n_new[h]
        m_scr[...] = m_next
        return carry

    lax.fori_loop(0, nc, chunk_body, 0)

    @pl.when(j == nj - 1)
    def _():
        for p in range(H // 2):
            for e in range(2):
                blk = slice(e * LANES, (e + 1) * LANES)
                co_ref[0, 2 * p + e] = c_scr[p, blk, blk]
                so_ref[0, 2 * p + e] = s_scr[p, blk, blk]
        no_ref[0] = n_scr[...]
        mo_ref[0] = m_scr[...]
        convo_ref[0] = conv_scr[...]


def _even_scan_call(proj, gates, cos_t, sin_t, states, weights, *, ts, n_seq, steps, row_off):
    H = N_HEADS_EVEN
    c0, n0, m0, conv0, s0 = states
    ng = proj.shape[0]
    rows_idx = lambda b, j: (row_off + b * steps + j, 0)
    state_specs = [pl.BlockSpec((1, H, LANES, LANES), lambda b, j: (b, 0, 0, 0)),
                   pl.BlockSpec((1, H, 1, LANES), lambda b, j: (b, 0, 0, 0)),
                   pl.BlockSpec((1, 1, LANES), lambda b, j: (b, 0, 0)),
                   pl.BlockSpec((1, SUBLANES, H * LANES), lambda b, j: (b, 0, 0)),
                   pl.BlockSpec((1, H, LANES, LANES), lambda b, j: (b, 0, 0, 0))]
    in_specs = [pl.BlockSpec((ng, ts, LANES), lambda b, j: (0, row_off + b * steps + j, 0)),
                pl.BlockSpec((ts, LANES), rows_idx),
                pl.BlockSpec((ts, LANES), rows_idx),
                pl.BlockSpec((ts, LANES), rows_idx)] + state_specs
    for w in weights:
        in_specs.append(pl.BlockSpec(w.shape, functools.partial(lambda nd, b, j: (0,) * nd, w.ndim)))
    out_specs = [pl.BlockSpec((ts, D_MODEL), lambda b, j: (b * steps + j, 0))] + state_specs
    out_shape = [jax.ShapeDtypeStruct((n_seq * steps * ts, D_MODEL), F32),
                 jax.ShapeDtypeStruct(c0.shape, F32), jax.ShapeDtypeStruct(n0.shape, F32),
                 jax.ShapeDtypeStruct(m0.shape, F32), jax.ShapeDtypeStruct(conv0.shape, F32),
                 jax.ShapeDtypeStruct(s0.shape, F32)]
    nc = ts // CHUNK
    pair_state = pltpu.VMEM((H // 2, 2 * LANES, 2 * LANES), F32)
    scratch = [pair_state, pltpu.VMEM((H, 1, LANES), F32),
               pltpu.VMEM((1, LANES), F32), pltpu.VMEM((SUBLANES, H * LANES), F32),
               pair_state,
               pltpu.VMEM((ts + 2 * SUBLANES, LANES), F32),
               pltpu.VMEM((H, ts, LANES), F32), pltpu.VMEM((H, ts, LANES), F32),
               pltpu.VMEM((H, ts, LANES), F32), pltpu.VMEM((H, ts, LANES), F32),
               pltpu.VMEM((H, ts, LANES), F32),
               pltpu.VMEM((ts, LANES), F32), pltpu.VMEM((ts, LANES), F32),
               pltpu.VMEM((nc, SUBLANES, LANES), F32),
               pltpu.VMEM((H // 2, CHUNK, LANES), F32)]
    return pl.pallas_call(
        functools.partial(_even_scan_kernel, ts=ts),
        grid=(n_seq, steps),
        in_specs=in_specs, out_specs=out_specs, out_shape=out_shape,
        scratch_shapes=scratch,
        compiler_params=_cparams(("arbitrary", "arbitrary")),
        name="even_scan",
    )(proj, gates, cos_t, sin_t, c0, n0, m0, conv0, s0, *weights)


_G_Q, _G_F, _G_I, _G_G = 0, 8, 16, 24


def _hgrn_intra_pair(q, k, bcum, ti, si, left, tcol_i):
    amat = jnp.zeros((CHUNK, LANES), F32)
    for b in (32, 16, 8):
        nb2 = CHUNK // (2 * b)
        upper = ((tcol_i // b) % 2) == 1
        ql, kl = [], []
        for e in range(2):
            parts = [jnp.broadcast_to(bcum[e][m * 2 * b + b - 1:m * 2 * b + b, :], (2 * b, LANES))
                     for m in range(nb2)]
            ref = parts[0] if nb2 == 1 else jnp.concatenate(parts, axis=0)
            ql.append(jnp.where(upper, q[e] * jnp.exp(bcum[e] - ref), 0.0).astype(BF16))
            kl.append(jnp.where(upper, 0.0, k[e] * jnp.exp(ref - bcum[e])).astype(BF16))
        al = _dot_nt(jnp.concatenate(ql, axis=1), _block_diag(kl[0], kl[1]))
        amat = amat + jnp.where((ti // (2 * b)) == (si // (2 * b)), al, 0.0)
    nblk = CHUNK // SUBLANES
    b3 = [x.reshape(nblk, SUBLANES, LANES) for x in bcum]
    k3 = [x.reshape(nblk, SUBLANES, LANES) for x in k]
    for jj in range(SUBLANES):
        col = []
        for e in range(2):
            bj = jnp.broadcast_to(b3[e][:, jj:jj + 1, :], (nblk, SUBLANES, LANES)).reshape(CHUNK, LANES)
            kj = jnp.broadcast_to(k3[e][:, jj:jj + 1, :], (nblk, SUBLANES, LANES)).reshape(CHUNK, LANES)
            col.append(jnp.sum(q[e] * kj * jnp.exp(bcum[e] - bj), axis=-1, keepdims=True))
        sel = (si == (ti // SUBLANES) * SUBLANES + jj) & ((ti % SUBLANES) >= jj)
        amat = jnp.where(sel, jnp.where(left, col[0], col[1]), amat)
    return amat


def _odd_scan_kernel(proj_ref, s0_ref, lb_ref, nw_ref, u_ref, so_ref,
                     st_scr, k_scr, bc_scr, *, ts):
    H = N_HEADS_ODD
    j = pl.program_id(1)
    nj = pl.num_programs(1)
    nc = ts // CHUNK

    @pl.when(j == 0)
    def _():
        for p in range(H // 2):
            st_scr[p] = _block_diag(s0_ref[0, 2 * p].T, s0_ref[0, 2 * p + 1].T)

    for h in range(H):
        lo, hi = h * LANES, (h + 1) * LANES
        lbv = lb_ref[:, lo:hi]
        fpre = proj_ref[_G_F + h]
        e = jnp.exp(-jnp.abs(fpre))
        one_e = 1.0 + e
        log_lb = jnp.log(lbv)
        log_rest = jnp.log1p(-lbv) + (jnp.minimum(fpre, 0.0) - jnp.log(one_e))
        logf = jnp.maximum(log_lb, log_rest) + jnp.log(1.0 + jnp.exp(-jnp.abs(log_lb - log_rest)))
        k_scr[h] = (1.0 - lbv) * (jnp.where(fpre >= 0.0, e, 1.0) / one_e)
        bc_scr[h] = _chunk_cumsum(logf)

    ti = lax.broadcasted_iota(jnp.int32, (CHUNK, LANES), 0)
    lane2 = lax.broadcasted_iota(jnp.int32, (CHUNK, LANES), 1)
    left = lane2 < CHUNK
    si = lane2 % CHUNK
    tcol_i = lax.broadcasted_iota(jnp.int32, (CHUNK, 1), 0)
    same_head = ((lax.broadcasted_iota(jnp.int32, (2 * LANES, 2 * LANES), 0) < LANES)
                 == (lax.broadcasted_iota(jnp.int32, (2 * LANES, 2 * LANES), 1) < LANES))

    def chunk_body(c, carry):
        r0 = pl.multiple_of(c * CHUNK, CHUNK)
        rows = pl.ds(r0, CHUNK)
        st_old = [st_scr[p] for p in range(H // 2)]
        st_new = []
        for p in range(H // 2):
            hs = (2 * p, 2 * p + 1)
            q = [proj_ref[_G_Q + h, rows, :] for h in hs]
            k = [k_scr[h, rows, :] for h in hs]
            vb = [proj_ref[_G_I + h, rows, :].astype(BF16) for h in hs]
            bcum = [bc_scr[h, rows, :] for h in hs]
            amat = _hgrn_intra_pair(q, k, bcum, ti, si, left, tcol_i)
            st = st_old[p]
            qg = jnp.concatenate([q[e] * jnp.exp(bcum[e]) for e in range(2)], axis=1)
            o = (_dot(amat.astype(BF16), _block_diag(vb[0], vb[1]))
                 + _dot_nt(qg.astype(BF16), st.astype(BF16)))
            last = [bcum[e][CHUNK - 1:CHUNK, :] for e in range(2)]
            kd = jnp.concatenate([k[e] * jnp.exp(last[e] - bcum[e]) for e in range(2)], axis=1)
            cross = _dot_tn(jnp.concatenate(vb, axis=1), kd.astype(BF16))
            st_new.append(st * jnp.exp(jnp.concatenate(last, axis=1)) + jnp.where(same_head, cross, 0.0))
            for e in range(2):
                h = hs[e]
                lo, hi = h * LANES, (h + 1) * LANES
                u_ref[rows, lo:hi] = (_rms(o[:, e * LANES:(e + 1) * LANES]) * nw_ref[:, lo:hi]
                                      * _sigmoid(proj_ref[_G_G + h, rows, :]))
        for p in range(H // 2):
            st_scr[p] = st_new[p]
        return carry

    lax.fori_loop(0, nc, chunk_body, 0)

    @pl.when(j == nj - 1)
    def _():
        for p in range(H // 2):
            for e in range(2):
                blk = slice(e * LANES, (e + 1) * LANES)
                so_ref[0, 2 * p + e] = st_scr[p, blk, blk].T


def _odd_scan_call(proj, s0, lb, nw, *, ts, n_seq, steps, row_off):
    H = N_HEADS_ODD
    ng = proj.shape[0]
    st_spec = pl.BlockSpec((1, H, LANES, LANES), lambda b, j: (b, 0, 0, 0))
    return pl.pallas_call(
        functools.partial(_odd_scan_kernel, ts=ts),
        grid=(n_seq, steps),
        in_specs=[pl.BlockSpec((ng, ts, LANES), lambda b, j: (0, row_off + b * steps + j, 0)),
                  st_spec,
                  pl.BlockSpec((1, D_MODEL), lambda b, j: (0, 0)),
                  pl.BlockSpec((1, D_MODEL), lambda b, j: (0, 0))],
        out_specs=[pl.BlockSpec((ts, D_MODEL), lambda b, j: (b * steps + j, 0)), st_spec],
        out_shape=[jax.ShapeDtypeStruct((n_seq * steps * ts, D_MODEL), F32),
                   jax.ShapeDtypeStruct(s0.shape, F32)],
        scratch_shapes=[pltpu.VMEM((H // 2, 2 * LANES, 2 * LANES), F32),
                        pltpu.VMEM((H, ts, LANES), F32),
                        pltpu.VMEM((H, ts, LANES), F32)],
        compiler_params=_cparams(("arbitrary", "arbitrary")),
        name="odd_scan",
    )(proj, s0, lb, nw)


def _post_kernel(x_ref, up_ref, us_ref, mod_ref, wout_ref, nw_ref, wr_ref, br_ref,
                 xo_ref, hf_ref, gate_ref, lpos_ref, cnt_ref,
                 *, tm, npc, n_prompt_tiles):
    i = pl.program_id(0)
    nch = tm // CHUNK

    u = jnp.where(i < n_prompt_tiles, up_ref[...], us_ref[...])
    y = _dot(u.astype(BF16), wout_ref[...])
    for c in range(nch):
        seq = _seq_row(i, nch, c, npc)
        gm = mod_ref[pl.ds(seq, 1), 2 * D_MODEL:3 * D_MODEL]
        shf = mod_ref[pl.ds(seq, 1), 3 * D_MODEL:4 * D_MODEL]
        scf = mod_ref[pl.ds(seq, 1), 4 * D_MODEL:5 * D_MODEL]
        rs = slice(c * CHUNK, (c + 1) * CHUNK)
        xn = x_ref[rs, :] + gm * y[rs, :]
        xo_ref[rs, :] = xn
        hf_ref[rs, :] = _rms(xn) * nw_ref[...] * (1.0 + scf) + shf

    logits = _dot3(hf_ref[...], wr_ref[...]) + br_ref[...]
    lane_i = lax.broadcasted_iota(jnp.int32, (tm, LANES), 1)
    lane_f = lane_i.astype(F32)
    vals, idxs = [], []
    cur = logits
    for _ in range(TOP_K):
        m = jnp.max(cur, axis=-1, keepdims=True)
        idx = jnp.min(jnp.where(cur == m, lane_f, float(LANES)), axis=-1, keepdims=True)
        vals.append(m)
        idxs.append(idx)
        cur = jnp.where(lane_f == idx, -jnp.inf, cur)
    exps = [jnp.exp(v - vals[0]) for v in vals]
    denom = exps[0] + exps[1] + exps[2] + exps[3]
    onehot = jnp.zeros((tm, LANES), F32)
    for idx in idxs:
        onehot = onehot + jnp.where(lane_f == idx, 1.0, 0.0)
    r = lax.broadcasted_iota(jnp.int32, (tm, tm), 0)
    cidx = lax.broadcasted_iota(jnp.int32, (tm, tm), 1)
    strict = jnp.where((cidx < r) & (cidx // DISPATCH_TILE == r // DISPATCH_TILE), 1.0, 0.0).astype(BF16)
    before = _dot(strict, onehot.astype(BF16))
    er = lax.broadcasted_iota(jnp.int32, (LANES, LANES), 0)
    ec = lax.broadcasted_iota(jnp.int32, (LANES, LANES), 1)
    lower_experts = jnp.where(er < ec, 1.0, 0.0).astype(BF16)
    pos_parts = []
    for s in range(tm // DISPATCH_TILE):
        rs = slice(s * DISPATCH_TILE, (s + 1) * DISPATCH_TILE)
        cnt = jnp.sum(onehot[rs, :], axis=0, keepdims=True)
        cnt_ref[s] = cnt
        n8 = jnp.floor((cnt + (SUBLANES - 1.0)) * (1.0 / SUBLANES)) * float(SUBLANES)
        run_start = _dot(jnp.broadcast_to(n8, (SUBLANES, LANES)).astype(BF16), lower_experts)[0:1, :]
        pos_parts.append(before[rs, :] + run_start)
    posmat = jnp.concatenate(pos_parts, axis=0)
    gate_o = jnp.zeros((tm, LANES), F32)
    lpos_o = jnp.zeros((tm, LANES), F32)
    for kk in range(TOP_K):
        lp = jnp.sum(jnp.where(lane_f == idxs[kk], posmat, 0.0), axis=-1, keepdims=True)
        gate_o = jnp.where(lane_i == kk, exps[kk] / denom, gate_o)
        lpos_o = jnp.where(lane_i == kk, lp, lpos_o)
    gate_ref[...] = gate_o
    lpos_ref[...] = lpos_o


def _post_call(x, u_p, u_s, mod_l, w_out, nw, w_r, b_r, npc):
    ttot = x.shape[0]
    tm = TOKEN_TILE
    npt = u_p.shape[0] // tm
    sub = tm // DISPATCH_TILE
    tile = lambda i: (i, 0)
    const = lambda i: (0, 0)
    return pl.pallas_call(
        functools.partial(_post_kernel, tm=tm, npc=npc, n_prompt_tiles=npt),
        grid=(ttot // tm,),
        in_specs=[pl.BlockSpec((tm, D_MODEL), tile),
                  pl.BlockSpec((tm, D_MODEL), lambda i: (jnp.minimum(i, npt - 1), 0)),
                  pl.BlockSpec((tm, D_MODEL), lambda i: (jnp.maximum(i - npt, 0), 0)),
                  pl.BlockSpec(mod_l.shape, const),
                  pl.BlockSpec(w_out.shape, const),
                  pl.BlockSpec((1, D_MODEL), const),
                  pl.BlockSpec(w_r.shape, const),
                  pl.BlockSpec((1, LANES), const)],
        out_specs=[pl.BlockSpec((tm, D_MODEL), tile), pl.BlockSpec((tm, D_MODEL), tile),
                   pl.BlockSpec((tm, LANES), tile), pl.BlockSpec((tm, LANES), tile),
                   pl.BlockSpec((sub, 1, LANES), lambda i: (i, 0, 0))],
        out_shape=[jax.ShapeDtypeStruct((ttot, D_MODEL), F32), jax.ShapeDtypeStruct((ttot, D_MODEL), F32),
                   jax.ShapeDtypeStruct((ttot, LANES), F32), jax.ShapeDtypeStruct((ttot, LANES), F32),
                   jax.ShapeDtypeStruct((ttot // DISPATCH_TILE, 1, LANES), F32)],
        compiler_params=_cparams(("arbitrary",)),
        name="post",
    )(x, u_p, u_s, mod_l, w_out, nw, w_r, b_r)


def _plan_kernel(cnt_ref, n8_ref, loff_ref, gbase_ref, blk_ref, ends_ref, *, nbp, ntp):
    cnt = cnt_ref[...]
    n8 = jnp.floor((cnt + (SUBLANES - 1.0)) * (1.0 / SUBLANES)) * float(SUBLANES)
    r = lax.broadcasted_iota(jnp.int32, (LANES, LANES), 0)
    c = lax.broadcasted_iota(jnp.int32, (LANES, LANES), 1)
    loff = _dot_sel_rhs(n8, jnp.where(r < c, 1.0, 0.0).astype(BF16))
    gtot = jnp.broadcast_to(jnp.sum(n8, axis=0, keepdims=True), (SUBLANES, LANES))
    nblk = jnp.floor((gtot + (EXPERT_ROWS - 1.0)) * (1.0 / EXPERT_ROWS))
    ends = _dot_sel_rhs(nblk, jnp.where(r <= c, 1.0, 0.0).astype(BF16))
    start_row = (ends[0:1, :] - nblk[0:1, :]) * float(EXPERT_ROWS)
    tr = lax.broadcasted_iota(jnp.int32, (ntp, ntp), 0)
    tc = lax.broadcasted_iota(jnp.int32, (ntp, ntp), 1)
    gbase = start_row + _dot_sel_lhs(jnp.where(tc < tr, 1.0, 0.0).astype(BF16), n8)
    n8_ref[...] = n8.astype(jnp.int32)
    loff_ref[...] = loff.astype(jnp.int32)
    gbase_ref[...] = gbase.astype(jnp.int32)
    bi = lax.broadcasted_iota(jnp.int32, (nbp, LANES), 0).astype(F32)
    li = lax.broadcasted_iota(jnp.int32, (nbp, LANES), 1)
    done = jnp.where((li < N_EXPERTS) & (ends[0:1, :] <= bi), 1.0, 0.0)
    be = jnp.minimum(jnp.sum(done, axis=-1, keepdims=True), N_EXPERTS - 1.0)
    blk_ref[...] = jnp.broadcast_to(be, (nbp, LANES)).astype(jnp.int32)
    r8 = lax.broadcasted_iota(jnp.int32, (SUBLANES, LANES), 0)
    tail_start = start_row + gtot[0:1, :]
    tail_len = nblk[0:1, :] * float(EXPERT_ROWS) - gtot[0:1, :]
    info = jnp.where(r8 == 0, ends, jnp.where(r8 == 1, tail_start, jnp.where(r8 == 2, tail_len, 0.0)))
    ends_ref[...] = info.astype(jnp.int32)


def _plan_call(cnt_tiles, nbp):
    ntp = cnt_tiles.shape[0]
    const = lambda i: (0, 0)
    tbl = jax.ShapeDtypeStruct((ntp, LANES), jnp.int32)
    return pl.pallas_call(
        functools.partial(_plan_kernel, nbp=nbp, ntp=ntp),
        grid=(1,),
        in_specs=[pl.BlockSpec((ntp, LANES), const)],
        out_specs=[pl.BlockSpec((ntp, LANES), const), pl.BlockSpec((ntp, LANES), const),
                   pl.BlockSpec((ntp, LANES), const),
                   pl.BlockSpec((nbp, LANES), const), pl.BlockSpec((SUBLANES, LANES), const)],
        out_shape=[tbl, tbl, tbl,
                   jax.ShapeDtypeStruct((nbp, LANES), jnp.int32),
                   jax.ShapeDtypeStruct((SUBLANES, LANES), jnp.int32)],
        compiler_params=_cparams(("arbitrary",)),
        name="plan",
    )(cnt_tiles)


_GROUP_BITS = tuple(range(3, DISPATCH_TILE.bit_length()))


_TOTAL_BITS = tuple(range(3, SORT_ROWS.bit_length()))


def _group_copies(n8_ref, loff_ref, gbase_ref, tile, make_copy, wait):
    if wait:
        total = lax.fori_loop(0, N_EXPERTS, lambda e, acc: acc + n8_ref[tile * N_EXPERTS + e], 0)
        for bit in _TOTAL_BITS:
            size = 1 << bit

            @pl.when((total & size) != 0)
            def _():
                make_copy(0, 0, size).wait()
        return

    def per_expert(e, carry):
        idx = tile * N_EXPERTS + e
        n = n8_ref[idx]
        off = loff_ref[idx]
        base = gbase_ref[idx]
        for bit in _GROUP_BITS:
            size = 1 << bit

            @pl.when((n & size) != 0)
            def _():
                done = n & ~(2 * size - 1)
                make_copy(pl.multiple_of(off + done, SUBLANES), pl.multiple_of(base + done, SUBLANES), size).start()
        return carry

    lax.fori_loop(0, N_EXPERTS, per_expert, 0)


def _zero_fill(tails_ref, nu_ref, n_blocks, make_zero_copy, wait):
    def finish(cp):
        cp.wait() if wait else cp.start()

    def per_expert(e, carry):
        base = tails_ref[e]
        n = tails_ref[N_EXPERTS + e]
        for bit in range(3, EXPERT_ROWS.bit_length() - 1):
            size = 1 << bit

            @pl.when((n & size) != 0)
            def _():
                done = n & ~(2 * size - 1)
                finish(make_zero_copy(pl.multiple_of(base + done, SUBLANES), size))
        return carry

    lax.fori_loop(0, N_EXPERTS, per_expert, 0)

    def per_block(blk, carry):
        finish(make_zero_copy(pl.multiple_of(blk * EXPERT_ROWS, EXPERT_ROWS), EXPERT_ROWS))
        return carry

    lax.fori_loop(nu_ref[0], n_blocks, per_block, 0)


def _dispatch_kernel(*refs, n_blocks, fresh):
    if fresh:
        n8_ref, loff_ref, gbase_ref, tails_ref, nu_ref, hf_ref, lpos_ref, gate_ref, xs_ref, sbuf, sem = refs
    else:
        (n8_ref, loff_ref, gbase_ref, tails_ref, nu_ref, hf_ref, lpos_ref, gate_ref, _, xs_ref,
         sbuf, sem) = refs
    i = pl.program_id(0)

    if fresh:
        @pl.when(i == 0)
        def _():
            sbuf[1] = jnp.zeros((SORT_ROWS, XS_WIDTH), F32)

            def make_zero_copy(dst_row, size):
                return pltpu.make_async_copy(sbuf.at[1, pl.ds(0, size), :],
                                             xs_ref.at[pl.ds(dst_row, size), :], sem.at[1])

            _zero_fill(tails_ref, nu_ref, n_blocks, make_zero_copy, wait=False)
            _zero_fill(tails_ref, nu_ref, n_blocks, make_zero_copy, wait=True)

    eye8 = jnp.where(lax.broadcasted_iota(jnp.int32, (SUBLANES, LANES), 0)
                     == lax.broadcasted_iota(jnp.int32, (SUBLANES, LANES), 1), 1.0, 0.0).astype(BF16)
    lpos_t = _dot_sel_nt(eye8, lpos_ref[...])
    gate_t = _dot_sel_nt(eye8, gate_ref[...])
    row = lax.broadcasted_iota(jnp.int32, (SORT_ROWS, DISPATCH_TILE), 0).astype(F32)
    perm = jnp.zeros((SORT_ROWS, DISPATCH_TILE), F32)
    wgate = jnp.zeros((SORT_ROWS, DISPATCH_TILE), F32)
    for kk in range(TOP_K):
        hit = row == lpos_t[kk:kk + 1, :]
        perm = jnp.where(hit, 1.0, perm)
        wgate = jnp.where(hit, gate_t[kk:kk + 1, :], wgate)
    slot = i % 2
    sbuf[slot, :, 0:D_MODEL] = _dot(perm.astype(BF16), hf_ref[...].astype(BF16))
    sbuf[slot, :, D_MODEL:XS_WIDTH] = _dot_sel_rhs(wgate, jnp.ones((DISPATCH_TILE, LANES), BF16))

    def copies(tile, buf_slot, wait):
        def make_copy(src_row, dst_row, size):
            return pltpu.make_async_copy(sbuf.at[buf_slot, pl.ds(src_row, size), :],
                                         xs_ref.at[pl.ds(dst_row, size), :], sem.at[buf_slot])
        _group_copies(n8_ref, loff_ref, gbase_ref, tile, make_copy, wait=wait)

    copies(i, slot, wait=False)

    @pl.when(i > 0)
    def _():
        copies(i - 1, 1 - slot, wait=True)

    @pl.when(i == pl.num_programs(0) - 1)
    def _():
        copies(i, slot, wait=True)


def _dispatch_call(tables, tails, n_used, hf, lpos, gate, n_blocks, xs_prev):
    ttot = hf.shape[0]
    tm = DISPATCH_TILE
    tile = lambda i, *_: (i, 0)
    fresh = xs_prev is None
    in_specs = [pl.BlockSpec((tm, D_MODEL), tile), pl.BlockSpec((tm, LANES), tile),
                pl.BlockSpec((tm, LANES), tile)]
    args = [*tables, tails, n_used, hf, lpos, gate]
    if not fresh:
        in_specs.append(pl.BlockSpec(memory_space=pl.ANY))
        args.append(xs_prev)
    grid_spec = pltpu.PrefetchScalarGridSpec(
        num_scalar_prefetch=5, grid=(ttot // tm,),
        in_specs=in_specs,
        out_specs=pl.BlockSpec(memory_space=pl.ANY),
        scratch_shapes=[pltpu.VMEM((2, SORT_ROWS, XS_WIDTH), F32), pltpu.SemaphoreType.DMA((2,))])
    return pl.pallas_call(
        functools.partial(_dispatch_kernel, n_blocks=n_blocks, fresh=fresh),
        grid_spec=grid_spec,
        out_shape=jax.ShapeDtypeStruct((n_blocks * EXPERT_ROWS, XS_WIDTH), F32),
        input_output_aliases={} if fresh else {len(args) - 1: 0},
        compiler_params=pltpu.CompilerParams(dimension_semantics=("arbitrary",),
                                             vmem_limit_bytes=VMEM_LIMIT, has_side_effects=True),
        name="dispatch",
    )(*args)


def _expert_kernel(be_ref, nu_ref, xs_ref, wgu_ref, bgu_ref, wdn_ref, bdn_ref, y_ref,
                   wgu_bf, wdn_bf):
    b = pl.program_id(0)

    @pl.when(b < nu_ref[0])
    def _():
        prev = be_ref[jnp.maximum(b - 1, 0)]

        @pl.when((b == 0) | (be_ref[b] != prev))
        def _():
            wgu_bf[...] = wgu_ref[0, 0].astype(BF16)
            wdn_bf[...] = wdn_ref[0, 0].astype(BF16)

        gu = _dot(xs_ref[:, 0:D_MODEL].astype(BF16), wgu_bf[...]) + bgu_ref[0, 0]
        g = jnp.minimum(gu[:, :D_MODEL], SWIGLU_LIMIT)
        u = jnp.clip(gu[:, D_MODEL:], -SWIGLU_LIMIT, SWIGLU_LIMIT)
        act = (u + 1.0) * (g * _sigmoid(SWIGLU_ALPHA * g))
        gate = xs_ref[:, D_MODEL:D_MODEL + 1]
        y_ref[...] = (_dot(act.astype(BF16), wdn_bf[...]) + bdn_ref[0, 0]) * gate

    @pl.when(b >= nu_ref[0])
    def _():
        y_ref[...] = jnp.zeros_like(y_ref)


def _expert_call(blk_expert, n_used, xs, w_gu, b_gu, w_dn, b_dn, layer):
    nb = xs.shape[0] // EXPERT_ROWS
    d_ff2 = w_gu.shape[3]
    blk = lambda b, be, nu: (jnp.minimum(b, nu[0] - 1), 0)
    blk_out = lambda b, be, nu: (b, 0)
    exp4 = lambda b, be, nu: (layer, be[jnp.minimum(b, nu[0] - 1)], 0, 0)
    grid_spec = pltpu.PrefetchScalarGridSpec(
        num_scalar_prefetch=2, grid=(nb,),
        in_specs=[pl.BlockSpec((EXPERT_ROWS, XS_WIDTH), blk),
                  pl.BlockSpec((1, 1, D_MODEL, d_ff2), exp4),
                  pl.BlockSpec((1, 1, 1, d_ff2), exp4),
                  pl.BlockSpec((1, 1, D_MODEL, D_MODEL), exp4),
                  pl.BlockSpec((1, 1, 1, D_MODEL), exp4)],
        out_specs=pl.BlockSpec((EXPERT_ROWS, D_MODEL), blk_out),
        scratch_shapes=[pltpu.VMEM((D_MODEL, d_ff2), BF16), pltpu.VMEM((D_MODEL, D_MODEL), BF16)])
    return pl.pallas_call(
        _expert_kernel,
        grid_spec=grid_spec,
        out_shape=jax.ShapeDtypeStruct((xs.shape[0], D_MODEL), F32),
        compiler_params=_cparams(("arbitrary",)),
        name="experts",
    )(blk_expert, n_used, xs, w_gu, b_gu, w_dn, b_dn)


def _combine_kernel(*refs, tm, npc, final):
    if final:
        (n8_ref, loff_ref, gbase_ref, x_ref, lpos_ref, mod_ref, fnw_ref, y_hbm,
         yp_ref, ys_ref, ybuf, sem, ynorm) = refs
    else:
        n8_ref, loff_ref, gbase_ref, x_ref, lpos_ref, mod_ref, y_hbm, xo_ref, ybuf, sem = refs
    i = pl.program_id(0)
    nch = tm // CHUNK

    slot = i % 2

    def copies(tile, buf_slot, wait):
        def make_copy(buf_row, src_row, size):
            return pltpu.make_async_copy(y_hbm.at[pl.ds(src_row, size), :],
                                         ybuf.at[buf_slot, pl.ds(buf_row, size), :], sem.at[buf_slot])
        _group_copies(n8_ref, loff_ref, gbase_ref, tile, make_copy, wait=wait)

    @pl.when(i == 0)
    def _():
        ybuf[...] = jnp.zeros_like(ybuf)
        copies(i, slot, wait=False)

    @pl.when(i + 1 < pl.num_programs(0))
    def _():
        copies(i + 1, 1 - slot, wait=False)

    copies(i, slot, wait=True)

    lpos = lpos_ref[...]
    col = lax.broadcasted_iota(jnp.int32, (tm, SORT_ROWS), 1).astype(F32)
    unperm = jnp.zeros((tm, SORT_ROWS), F32)
    for kk in range(TOP_K):
        unperm = jnp.where(col == lpos[:, kk:kk + 1], 1.0, unperm)
    acc = _dot_sel_lhs2(unperm.astype(BF16), ybuf[slot])
    for c in range(nch):
        seq = _seq_row(i, nch, c, npc)
        gf = mod_ref[pl.ds(seq, 1), 5 * D_MODEL:6 * D_MODEL]
        rs = slice(c * CHUNK, (c + 1) * CHUNK)
        xn = x_ref[rs, :] + gf * acc[rs, :]
        if final:
            ynorm[rs, :] = _rms(xn) * fnw_ref[...]
        else:
            xo_ref[rs, :] = xn
    if final:
        is_prompt = i < (npc * CHUNK) // tm

        @pl.when(is_prompt)
        def _():
            yp_ref[...] = ynorm[...]

        @pl.when(jnp.logical_not(is_prompt))
        def _():
            ys_ref[...] = ynorm[...]


def _combine_call(tables, x, lpos, mod_l, y_sorted, npc, final_w):
    ttot = x.shape[0]
    tm = DISPATCH_TILE
    final = final_w is not None
    npt = (npc * CHUNK) // tm
    tile = lambda i, *_: (i, 0)
    const = lambda i, *_: (0, 0)
    in_specs = [pl.BlockSpec((tm, D_MODEL), tile),
                pl.BlockSpec((tm, LANES), tile),
                pl.BlockSpec(mod_l.shape, const)]
    args = [x, lpos, mod_l]
    if final:
        in_specs.append(pl.BlockSpec((1, D_MODEL), const))
        args.append(final_w)
    in_specs.append(pl.BlockSpec(memory_space=pl.ANY))
    args.append(y_sorted)
    if final:
        out_specs = [pl.BlockSpec((tm, D_MODEL), lambda i, *_: (jnp.minimum(i, npt - 1), 0)),
                     pl.BlockSpec((tm, D_MODEL), lambda i, *_: (jnp.maximum(i - npt, 0), 0))]
        out_shape = [jax.ShapeDtypeStruct((npt * tm, D_MODEL), F32),
                     jax.ShapeDtypeStruct((ttot - npt * tm, D_MODEL), F32)]
    else:
        out_specs = [pl.BlockSpec((tm, D_MODEL), tile)]
        out_shape = [jax.ShapeDtypeStruct((ttot, D_MODEL), F32)]
    scratch = [pltpu.VMEM((2, SORT_ROWS, D_MODEL), F32), pltpu.SemaphoreType.DMA((2,))]
    if final:
        scratch.append(pltpu.VMEM((tm, D_MODEL), F32))
    grid_spec = pltpu.PrefetchScalarGridSpec(
        num_scalar_prefetch=3, grid=(ttot // tm,),
        in_specs=in_specs, out_specs=out_specs, scratch_shapes=scratch)
    return pl.pallas_call(
        functools.partial(_combine_kernel, tm=tm, npc=npc, final=final),
        grid_spec=grid_spec, out_shape=out_shape,
        compiler_params=_cparams(("arbitrary",)),
        name="combine",
    )(*tables, *args)


def kernel(x_prompt, x_sample, c_prompt, c_sample, state_mlstm_C, state_mlstm_n, state_mlstm_m, state_mlstm_conv, state_ret_S, state_hgrn_S, w_ada, b_ada, norm_mix_w, norm_ffn_w, final_norm_w, w_in_even, b_mlstm_i, b_mlstm_f, w_mlstm_conv, b_mlstm_conv, w_mlstm_q, w_mlstm_k, mlstm_skip, mlstm_norm_w, ret_norm_w, w_out_even, w_in_odd, hgrn_lb_logits, hgrn_norm_w, w_out_odd, moe_router_w, moe_router_b, moe_w_gate_up, moe_b_gate_up, moe_w_down, moe_b_down):
    bp, seq, d = x_prompt.shape
    bs, dseq, _ = x_sample.shape
    assert bp == 1 and d == D_MODEL and dseq == CHUNK
    assert seq % TOKEN_TILE == 0 and (bs * dseq) % TOKEN_TILE == 0
    depth = w_ada.shape[0]
    tp, tsmp = seq, bs * dseq
    ttot = tp + tsmp
    npc = tp // CHUNK
    he, ho = N_HEADS_EVEN, N_HEADS_ODD
    da = he * LANES
    past_len = 1024

    x = jnp.concatenate([x_prompt.reshape(tp, d), x_sample.reshape(tsmp, d)], axis=0)
    n_mod_rows = 2 * SUBLANES
    assert 1 + bs <= n_mod_rows
    c_all = jnp.zeros((n_mod_rows, d), F32).at[0:1].set(c_prompt).at[1:1 + bs].set(c_sample)
    mod = _ada_call(c_all, w_ada, b_ada)

    half = LANES // 2
    inv = ROPE_BASE ** (-jnp.arange(half, dtype=F32) / half)
    pos_all = jnp.concatenate([jnp.arange(tp, dtype=F32),
                               jnp.tile(past_len + jnp.arange(dseq, dtype=F32), bs)])
    ang = pos_all[:, None] * inv[None, :]
    cos_t = jnp.concatenate([jnp.cos(ang), jnp.cos(ang)], axis=-1)
    sin_t = jnp.concatenate([-jnp.sin(ang), jnp.sin(ang)], axis=-1)

    lb_p = jax.nn.softmax(hgrn_lb_logits.astype(F32), axis=0)
    lbs = jnp.cumsum(lb_p, axis=0) - lb_p[0]

    n_tiles = ttot // DISPATCH_TILE
    ntp = -(-n_tiles // LANES) * LANES
    max_rows = ttot * TOP_K + n_tiles * N_EXPERTS * (SUBLANES - 1)
    nb = -(-max_rows // EXPERT_ROWS) + N_EXPERTS
    nbp = -(-nb // SUBLANES) * SUBLANES
    xs = None

    ts_p = TOKEN_TILE
    steps_p = tp // ts_p
    even_out, odd_out = [], []
    y_final = None
    for l in range(depth):
        jl = l // 2
        mod_l = mod[l]
        if l % 2 == 0:
            w_in = w_in_even[jl]
            w_main = jnp.concatenate([w_in[:, da:3 * da], w_in[:, 3 * da + 2 * he:]], axis=1).astype(BF16)
            w_gate = jnp.zeros((d, da + LANES), F32).at[:, :da].set(w_in[:, :da])
            w_gate = w_gate.at[:, da:da + 2 * he].set(w_in[:, 3 * da:3 * da + 2 * he])
            proj, gates = _inproj_call(x, mod_l, norm_mix_w[l][None], w_main, w_gate, npc)
            gbias = jnp.zeros((1, LANES), F32).at[0, :he].set(b_mlstm_i[jl]).at[0, he:2 * he].set(b_mlstm_f[jl])
            weights = [w_mlstm_conv[jl], b_mlstm_conv[jl][None],
                       jnp.concatenate([w_mlstm_q[jl], w_mlstm_k[jl]], axis=-1), gbias,
                       mlstm_skip[jl][None], mlstm_norm_w[jl][None], ret_norm_w[jl][None]]
            zeros_p = (jnp.zeros((1, he, LANES, LANES), F32), jnp.zeros((1, he, 1, LANES), F32),
                       jnp.zeros((1, 1, LANES), F32), jnp.zeros((1, SUBLANES, da), F32),
                       jnp.zeros((1, he, LANES, LANES), F32))
            st_s = (state_mlstm_C[jl], state_mlstm_n[jl][:, :, None, :],
                    jnp.zeros((bs, 1, LANES), F32).at[:, 0, :he].set(state_mlstm_m[jl]),
                    jnp.zeros((bs, SUBLANES, da), F32).at[:, SUBLANES - (CONV_W - 1):].set(state_mlstm_conv[jl]),
                    state_ret_S[jl])
            res_p = _even_scan_call(proj, gates, cos_t, sin_t, zeros_p, weights,
                                    ts=ts_p, n_seq=1, steps=steps_p, row_off=0)
            res_s = _even_scan_call(proj, gates, cos_t, sin_t, st_s, weights,
                                    ts=CHUNK, n_seq=bs, steps=1, row_off=npc)
            u_p, u_s = res_p[0], res_s[0]
            even_out.append((res_p[1:], res_s[1:]))
            w_out = w_out_even[jl].astype(BF16)
        else:
            proj = _inproj_call(x, mod_l, norm_mix_w[l][None], w_in_odd[jl].astype(BF16), None, npc)[0]
            lb = lbs[l][None]
            nw = hgrn_norm_w[jl][None]
            u_p, sp = _odd_scan_call(proj, jnp.zeros((1, ho, LANES, LANES), F32), lb, nw,
                                     ts=ts_p, n_seq=1, steps=steps_p, row_off=0)
            u_s, ss = _odd_scan_call(proj, state_hgrn_S[jl], lb, nw,
                                     ts=CHUNK, n_seq=bs, steps=1, row_off=npc)
            odd_out.append((sp, ss))
            w_out = w_out_odd[jl].astype(BF16)

        w_r = jnp.zeros((d, LANES), F32).at[:, :N_EXPERTS].set(moe_router_w[l])
        b_r = jnp.full((1, LANES), -jnp.inf, F32).at[0, :N_EXPERTS].set(moe_router_b[l])
        x, hf, gate, lpos, cnt = _post_call(x, u_p, u_s, mod_l, w_out, norm_ffn_w[l][None], w_r, b_r, npc)
        cnt_tiles = jnp.zeros((ntp, LANES), F32).at[:n_tiles].set(cnt[:, 0, :])
        n8, loff, gbase, blk, ends = _plan_call(cnt_tiles, nbp)
        tables = [t[:n_tiles, :N_EXPERTS].reshape(-1) for t in (n8, loff, gbase)]
        blk_expert = blk[:nb, 0]
        n_used = ends[0, N_EXPERTS - 1:N_EXPERTS]
        tails = ends[1:3, :N_EXPERTS].reshape(-1)
        xs = _dispatch_call(tables, tails, n_used, hf, lpos, gate, nb, xs)
        y_sorted = _expert_call(blk_expert, n_used, xs, moe_w_gate_up, moe_b_gate_up[:, :, None, :],
                                moe_w_down, moe_b_down[:, :, None, :], l)
        if l == depth - 1:
            y_final = _combine_call(tables, x, lpos, mod_l, y_sorted, npc, final_norm_w[None])
        else:
            x = _combine_call(tables, x, lpos, mod_l, y_sorted, npc, None)[0]

    def even_states(which):
        cs = jnp.stack([e[which][0] for e in even_out])
        ns = jnp.stack([e[which][1][:, :, 0, :] for e in even_out])
        ms = jnp.stack([e[which][2][:, 0, :he] for e in even_out])
        cv = jnp.stack([e[which][3][:, SUBLANES - (CONV_W - 1):, :] for e in even_out])
        ss = jnp.stack([e[which][4] for e in even_out])
        return cs, ns, ms, cv, ss

    p_c, p_n, p_m, p_cv, p_s = even_states(0)
    s_c, s_n, s_m, s_cv, s_s = even_states(1)
    p_h = jnp.stack([o[0] for o in odd_out])
    s_h = jnp.stack([o[1] for o in odd_out])
    y_prompt = y_final[0].reshape(bp, seq, d)
    y_sample = y_final[1].reshape(bs, dseq, d)
    return (y_prompt, y_sample, p_c, p_n, p_m, p_cv, p_s, p_h, s_c, s_n, s_m, s_cv, s_s, s_h)
```

```python
import functools
import math

import jax
import jax.numpy as jnp
from jax import lax
from jax.experimental import pallas as pl
from jax.experimental.pallas import tpu as pltpu

F32 = jnp.float32
BF16 = jnp.bfloat16

CHUNK = 64
LANES = 128
SUBLANES = 8
D_MODEL = 1024
N_HEADS_EVEN = 4
N_HEADS_ODD = 8
CONV_W = 4
N_EXPERTS = 32
TOP_K = 4
SWIGLU_LIMIT = 7.0
SWIGLU_ALPHA = 1.702
EPS = 1e-6
ROPE_BASE = 10000.0

TOKEN_TILE = 512
INPROJ_TILE = 512
DISPATCH_TILE = 256
SORT_ROWS = DISPATCH_TILE * TOP_K + LANES * 2
XS_WIDTH = D_MODEL + LANES
EXPERT_ROWS = 512
VMEM_LIMIT = 56 * 1024 * 1024


def _cparams(sem, vmem=VMEM_LIMIT):
    return pltpu.CompilerParams(dimension_semantics=sem, vmem_limit_bytes=vmem)


def _dot(a, b):
    return jnp.dot(a, b, preferred_element_type=F32)


def _dot_nt(a, b):
    return lax.dot_general(a, b, (((1,), (1,)), ((), ())), preferred_element_type=F32)


def _dot_tn(a, b):
    return lax.dot_general(a, b, (((0,), (0,)), ((), ())), preferred_element_type=F32)


def _split3(x):
    p1 = x.astype(BF16)
    r1 = x - p1.astype(F32)
    p2 = r1.astype(BF16)
    p3 = (r1 - p2.astype(F32)).astype(BF16)
    return p1, p2, p3


def _dot3(a, b):
    ah = a.astype(BF16)
    al = (a - ah.astype(F32)).astype(BF16)
    bh = b.astype(BF16)
    bl = (b - bh.astype(F32)).astype(BF16)
    return _dot(ah, bh) + (_dot(ah, bl) + _dot(al, bh))


def _dot3_tn_fused(a, b):
    n = a.shape[1]
    ah = a.astype(BF16)
    al = (a - ah.astype(F32)).astype(BF16)
    bh = b.astype(BF16)
    bl = (b - bh.astype(F32)).astype(BF16)
    x = _dot_tn(jnp.concatenate([ah, al], axis=1), jnp.concatenate([bh, bl], axis=1))
    return x[:n, :n] + (x[:n, n:] + x[n:, :n])


def _block_diag(a, b):
    z = jnp.zeros_like(a)
    return jnp.concatenate([jnp.concatenate([a, z], axis=1), jnp.concatenate([z, b], axis=1)], axis=0)


def _dot_sel_lhs(sel_bf16, x):
    p1, p2, p3 = _split3(x)
    return _dot(sel_bf16, p1) + (_dot(sel_bf16, p2) + _dot(sel_bf16, p3))


def _dot_sel_nt(sel_bf16, x):
    p1, p2, p3 = _split3(x)
    return _dot_nt(sel_bf16, p1) + (_dot_nt(sel_bf16, p2) + _dot_nt(sel_bf16, p3))


def _dot_sel_rhs(x, sel_bf16):
    p1, p2, p3 = _split3(x)
    return _dot(p1, sel_bf16) + (_dot(p2, sel_bf16) + _dot(p3, sel_bf16))


def _dot_sel_lhs2(sel_bf16, x):
    hi = x.astype(BF16)
    lo = (x - hi.astype(F32)).astype(BF16)
    return _dot(sel_bf16, hi) + _dot(sel_bf16, lo)


def _sigmoid(x):
    return 1.0 / (1.0 + jnp.exp(-x))


def _log_sigmoid(x):
    return jnp.minimum(x, 0.0) - jnp.log1p(jnp.exp(-jnp.abs(x)))


def _rms(x):
    return x * lax.rsqrt(jnp.mean(x * x, axis=-1, keepdims=True) + EPS)


def _chunk_tri(n):
    r = lax.broadcasted_iota(jnp.int32, (n, n), 0)
    c = lax.broadcasted_iota(jnp.int32, (n, n), 1)
    return jnp.where((r // CHUNK == c // CHUNK) & (c <= r), 1.0, 0.0).astype(BF16)


def _chunk_cumsum(x):
    rows, n = x.shape
    vregs = CHUNK // SUBLANES
    x4 = x.reshape(rows // CHUNK, vregs, SUBLANES, n)
    sub = lax.broadcasted_iota(jnp.int32, x4.shape, 2)
    s = x4
    for shift in (1, 2, 4):
        s = s + jnp.where(sub >= shift, pltpu.roll(s, shift, 2), 0.0)
    outs, carry = [], None
    for v in range(vregs):
        cur = s[:, v] if carry is None else s[:, v] + carry
        outs.append(cur)
        carry = jnp.broadcast_to(cur[:, SUBLANES - 1:SUBLANES, :], cur.shape)
    return jnp.stack(outs, axis=1).reshape(rows, n)


def _seq_row(tile_idx, chunks_per_tile, c, n_prompt_chunks):
    return jnp.maximum(tile_idx * chunks_per_tile + c - (n_prompt_chunks - 1), 0)


def _ada_kernel(c_ref, w_ref, b_ref, o_ref):
    c = c_ref[...]
    o_ref[0] = _dot3(c * _sigmoid(c), w_ref[0]) + b_ref[0]


def _ada_call(c_all, w_ada, b_ada):
    depth = w_ada.shape[0]
    nrow = c_all.shape[0]
    ncol = w_ada.shape[2] // D_MODEL
    return pl.pallas_call(
        _ada_kernel,
        grid=(depth, ncol),
        in_specs=[pl.BlockSpec((nrow, D_MODEL), lambda l, j: (0, 0)),
                  pl.BlockSpec((1, D_MODEL, D_MODEL), lambda l, j: (l, 0, j)),
                  pl.BlockSpec((1, 1, D_MODEL), lambda l, j: (l, 0, j))],
        out_specs=pl.BlockSpec((1, nrow, D_MODEL), lambda l, j: (l, 0, j)),
        out_shape=jax.ShapeDtypeStruct((depth, nrow, ncol * D_MODEL), F32),
        compiler_params=_cparams(("arbitrary", "arbitrary")),
        name="ada",
    )(c_all, w_ada, b_ada.reshape(depth, 1, -1))


def _inproj_kernel(*refs, tm, ng, npc, n_hp):
    if n_hp:
        x_ref, mod_ref, nw_ref, w_ref, whp_ref, proj_ref, gates_ref, h_scr = refs
    else:
        x_ref, mod_ref, nw_ref, w_ref, proj_ref, h_scr = refs
    i = pl.program_id(0)
    nch = tm // CHUNK
    for c in range(nch):
        seq = _seq_row(i, nch, c, npc)
        sh = mod_ref[pl.ds(seq, 1), 0:D_MODEL]
        sc = mod_ref[pl.ds(seq, 1), D_MODEL:2 * D_MODEL]
        xc = x_ref[c * CHUNK:(c + 1) * CHUNK, :]
        h_scr[c * CHUNK:(c + 1) * CHUNK, :] = _rms(xc) * nw_ref[...] * (1.0 + sc) + sh
    h = h_scr[...]
    hb = h.astype(BF16)
    for g in range(0, ng - n_hp, 4):
        res = _dot(hb, w_ref[:, g * LANES:(g + 4) * LANES])
        for jj in range(4):
            proj_ref[n_hp + g + jj] = res[:, jj * LANES:(jj + 1) * LANES]
    if n_hp:
        res = _dot3(h, whp_ref[...])
        for jj in range(n_hp):
            proj_ref[jj] = res[:, jj * LANES:(jj + 1) * LANES]
        gates_ref[...] = res[:, n_hp * LANES:]


def _inproj_call(x, mod_l, nw, w_main, w_gate, npc):
    ttot = x.shape[0]
    tm = INPROJ_TILE
    has_gates = w_gate is not None
    n_hp = w_gate.shape[1] // LANES - 1 if has_gates else 0
    ng = w_main.shape[1] // LANES + n_hp
    in_specs = [pl.BlockSpec((tm, D_MODEL), lambda i: (i, 0)),
                pl.BlockSpec(mod_l.shape, lambda i: (0, 0)),
                pl.BlockSpec((1, D_MODEL), lambda i: (0, 0)),
                pl.BlockSpec(w_main.shape, lambda i: (0, 0))]
    out_specs = [pl.BlockSpec((ng, tm, LANES), lambda i: (0, i, 0))]
    out_shape = [jax.ShapeDtypeStruct((ng, ttot, LANES), F32)]
    args = [x, mod_l, nw, w_main]
    if has_gates:
        in_specs.append(pl.BlockSpec(w_gate.shape, lambda i: (0, 0)))
        out_specs.append(pl.BlockSpec((tm, LANES), lambda i: (i, 0)))
        out_shape.append(jax.ShapeDtypeStruct((ttot, LANES), F32))
        args.append(w_gate)
    return pl.pallas_call(
        functools.partial(_inproj_kernel, tm=tm, ng=ng, npc=npc, n_hp=n_hp),
        grid=(ttot // tm,),
        in_specs=in_specs, out_specs=out_specs, out_shape=out_shape,
        scratch_shapes=[pltpu.VMEM((tm, D_MODEL), F32)],
        compiler_params=_cparams(("arbitrary",)),
        name="inproj",
    )(*args)


_G_XM, _G_VA, _G_OA, _G_QB, _G_KB, _G_VB, _G_GB = 0, 4, 8, 12, 16, 20, 24


def _even_scan_kernel(proj_ref, gates_ref, cos_ref, sin_ref,
                      c0_ref, n0_ref, m0_ref, conv0_ref, s0_ref,
                      cw_ref, cb_ref, wqk_ref, gbias_ref, skip_ref, nwa_ref, nwb_ref,
                      u_ref, co_ref, no_ref, mo_ref, convo_ref, so_ref,
                      c_scr, n_scr, m_scr, conv_scr, s_scr,
                      xbuf, xc_scr, q_scr, k_scr, qr_scr, kr_scr,
                      gl_scr, bc_scr, rows_scr, dec_scr, *, ts):
    H = N_HEADS_EVEN
    j = pl.program_id(1)
    nj = pl.num_programs(1)
    nc = ts // CHUNK

    @pl.when(j == 0)
    def _():
        for p in range(H // 2):
            c_scr[p] = _block_diag(c0_ref[0, 2 * p], c0_ref[0, 2 * p + 1])
            s_scr[p] = _block_diag(s0_ref[0, 2 * p], s0_ref[0, 2 * p + 1])
        n_scr[...] = n0_ref[0]
        m_scr[...] = m0_ref[0]
        conv_scr[...] = conv0_ref[0]

    for g in range(H):
        lo, hi = g * LANES, (g + 1) * LANES
        x_g = proj_ref[_G_XM + g]
        xbuf[0:SUBLANES, :] = conv_scr[:, lo:hi]
        xbuf[SUBLANES:SUBLANES + ts, :] = x_g
        acc = cb_ref[:, lo:hi] + cw_ref[CONV_W - 1:CONV_W, lo:hi] * x_g
        for t in range(CONV_W - 1):
            off = SUBLANES - (CONV_W - 1) + t
            acc = acc + cw_ref[t:t + 1, lo:hi] * xbuf[off:off + ts, :]
        conv_scr[:, lo:hi] = xbuf[ts:ts + SUBLANES, :]
        xc = acc * _sigmoid(acc)
        xc_scr[g] = xc
        qk = _dot3(xc, wqk_ref[g])
        q_scr[g] = qk[:, 0:LANES]
        k_scr[g] = qk[:, LANES:2 * LANES] * (LANES ** -0.5)

    cosv = cos_ref[...]
    sinv = sin_ref[...]
    for g in range(H):
        qb = proj_ref[_G_QB + g]
        kb = proj_ref[_G_KB + g]
        qr_scr[g] = qb * cosv + pltpu.roll(qb, LANES // 2, 1) * sinv
        kr_scr[g] = (kb * cosv + pltpu.roll(kb, LANES // 2, 1) * sinv) * (LANES ** -0.5)

    gpre = gates_ref[...] + gbias_ref[...]
    lane = lax.broadcasted_iota(jnp.int32, (ts, LANES), 1)
    gl = jnp.where(lane < H, gpre, _log_sigmoid(gpre))
    gl_scr[...] = gl
    bc = _dot_sel_lhs(_chunk_tri(ts), gl)
    bc_scr[...] = bc
    comb = jnp.where(lane < H, gl, bc)
    even_head = (lane % 2) == 0
    comb_even = jnp.where(even_head, comb, 0.0)
    comb_odd = jnp.where(even_head, 0.0, comb)
    pr = lax.broadcasted_iota(jnp.int32, (SUBLANES, LANES), 0)
    pc = lax.broadcasted_iota(jnp.int32, (SUBLANES, LANES), 1)
    pair_sel = jnp.where((pc // 2 == pr) & (pc < 2 * H), 1.0, 0.0).astype(BF16)
    for c in range(nc):
        cs = slice(c * CHUNK, (c + 1) * CHUNK)
        rows_scr[c] = _dot_sel_nt(pair_sel, jnp.concatenate([comb_even[cs, :], comb_odd[cs, :]], axis=0))

    ti = lax.broadcasted_iota(jnp.int32, (CHUNK, LANES), 0)
    lane2 = lax.broadcasted_iota(jnp.int32, (CHUNK, LANES), 1)
    left = lane2 < CHUNK
    si = lane2 % CHUNK
    tril = ti >= si
    left_wide = lax.broadcasted_iota(jnp.int32, (CHUNK, 2 * LANES), 1) < LANES
    br = lax.broadcasted_iota(jnp.int32, (2 * LANES, 2 * LANES), 0) < LANES
    bcol_blk = lax.broadcasted_iota(jnp.int32, (2 * LANES, 2 * LANES), 1) < LANES
    tcol = lax.broadcasted_iota(jnp.int32, (CHUNK, 1), 0).astype(F32)
    log_gamma = [math.log1p(-2.0 ** (-5 - h)) for h in range(H)]
    for p in range(H // 2):
        lg2 = jnp.where(left, log_gamma[2 * p], log_gamma[2 * p + 1])
        dec_scr[p] = jnp.where(tril, jnp.exp((ti - si).astype(F32) * lg2), 0.0)

    def chunk_body(c, carry):
        r0 = pl.multiple_of(c * CHUNK, CHUNK)
        rows = pl.ds(r0, CHUNK)
        pair_rows = rows_scr[c]
        glc = gl_scr[rows, :]
        bcc = bc_scr[rows, :]
        m_all = m_scr[...]
        c_old = [c_scr[p] for p in range(H // 2)]
        s_old = [s_scr[p] for p in range(H // 2)]
        n_old = [n_scr[h] for h in range(H)]
        c_new, s_new, n_new = [], [], []
        m_next = m_all
        lane_row = lax.broadcasted_iota(jnp.int32, (1, LANES), 1)
        for p in range(H // 2):
            hs = (2 * p, 2 * p + 1)
            q = [q_scr[h, rows, :] for h in hs]
            k = [k_scr[h, rows, :] for h in hs]
            v = [proj_ref[_G_VA + h, rows, :] for h in hs]
            b_col = [bcc[:, H + h:H + h + 1] for h in hs]
            i_col = [glc[:, h:h + 1] for h in hs]
            m_prev = [m_all[:, h:h + 1] for h in hs]
            q2 = jnp.concatenate(q, axis=1).astype(BF16)
            smat = _dot_nt(q2, _block_diag(k[0].astype(BF16), k[1].astype(BF16)))
            dmat = jnp.where(tril, jnp.where(left, b_col[0], b_col[1])
                             - pair_rows[2 + p:3 + p, :] + pair_rows[p:p + 1, :], -jnp.inf)
            a = [jnp.max(jnp.where(left, dmat, -jnp.inf), axis=-1, keepdims=True),
                 jnp.max(jnp.where(left, -jnp.inf, dmat), axis=-1, keepdims=True)]
            inter = [b_col[e] + m_prev[e] for e in range(2)]
            m_t = [jnp.maximum(inter[e], a[e]) for e in range(2)]
            w_inter = [jnp.exp(inter[e] - m_t[e]) for e in range(2)]
            amat = smat * jnp.exp(dmat - jnp.where(left, m_t[0], m_t[1]))
            cst = c_old[p]
            num = (_dot(amat.astype(BF16), _block_diag(v[0].astype(BF16), v[1].astype(BF16)))
                   + jnp.where(left_wide, w_inter[0], w_inter[1]) * _dot(q2, cst.astype(BF16)))
            den = [jnp.sum(jnp.where(left, amat, 0.0), axis=-1, keepdims=True),
                   jnp.sum(jnp.where(left, 0.0, amat), axis=-1, keepdims=True)]
            den = [jnp.maximum(jnp.abs(den[e] + w_inter[e] * jnp.sum(q[e] * n_old[hs[e]], axis=-1, keepdims=True)),
                               jnp.exp(-m_t[e])) for e in range(2)]
            hout = num / jnp.where(left_wide, den[0], den[1])
            decay, upd = [], []
            for e in range(2):
                m_new = m_t[e][CHUNK - 1:CHUNK, :]
                b_last = b_col[e][CHUNK - 1:CHUNK, :]
                kw = k[e] * jnp.exp(b_last - b_col[e] + i_col[e] - m_new)
                decay.append(jnp.exp(b_last + m_prev[e] - m_new))
                upd.append(_dot3_tn_fused(kw, v[e]))
                n_new.append(decay[e] * n_old[hs[e]] + jnp.sum(kw, axis=0, keepdims=True))
                m_next = jnp.where(lane_row == hs[e], m_new, m_next)
            c_new.append(jnp.where(br, decay[0], decay[1]) * cst + _block_diag(upd[0], upd[1]))
            for e in range(2):
                h = hs[e]
                lo, hi = h * LANES, (h + 1) * LANES
                z = _sigmoid(proj_ref[_G_OA + h, rows, :]) * hout[:, e * LANES:(e + 1) * LANES]
                u_ref[rows, lo:hi] = (_rms(z) * nwa_ref[:, lo:hi]
                                      + skip_ref[:, lo:hi] * xc_scr[h, rows, :]).astype(BF16)
        for p in range(H // 2):
            hs = (2 * p, 2 * p + 1)
            lg = [log_gamma[h] for h in hs]
            kr = [kr_scr[h, rows, :] for h in hs]
            vb = [proj_ref[_G_VB + h, rows, :].astype(BF16) for h in hs]
            q2 = jnp.concatenate([qr_scr[h, rows, :] for h in hs], axis=1).astype(BF16)
            amat = _dot_nt(q2, _block_diag(kr[0].astype(BF16), kr[1].astype(BF16))) * dec_scr[p]
            sst = s_old[p]
            o = (_dot(amat.astype(BF16), _block_diag(vb[0], vb[1]))
                 + jnp.exp((tcol + 1.0) * jnp.where(left_wide, lg[0], lg[1])) * _dot(q2, sst.astype(BF16)))
            kws = jnp.concatenate([kr[e] * jnp.exp((CHUNK - 1.0 - tcol) * lg[e]) for e in range(2)], axis=1)
            cross = _dot_tn(kws.astype(BF16), jnp.concatenate(vb, axis=1))
            s_new.append(jnp.where(br, math.exp(CHUNK * lg[0]), math.exp(CHUNK * lg[1])) * sst
                         + jnp.where(br == bcol_blk, cross, 0.0))
            for e in range(2):
                h = hs[e]
                lo, hi = h * LANES, (h + 1) * LANES
                gate = proj_ref[_G_GB + h, rows, :]
                u_ref[rows, D_MODEL // 2 + lo:D_MODEL // 2 + hi] = (
                    _rms(o[:, e * LANES:(e + 1) * LANES]) * nwb_ref[:, lo:hi]
                    * (gate * _sigmoid(gate))).astype(BF16)
        for p in range(H // 2):
            c_scr[p] = c_new[p]
            s_scr[p] = s_new[p]
        for h in range(H):
            n_scr[h] = n_new[h]
        m_scr[...] = m_next
        return carry

    lax.fori_loop(0, nc, chunk_body, 0)

    @pl.when(j == nj - 1)
    def _():
        for p in range(H // 2):
            for e in range(2):
                blk = slice(e * LANES, (e + 1) * LANES)
                co_ref[0, 2 * p + e] = c_scr[p, blk, blk]
                so_ref[0, 2 * p + e] = s_scr[p, blk, blk]
        no_ref[0] = n_scr[...]
        mo_ref[0] = m_scr[...]
        convo_ref[0] = conv_scr[...]


def _even_scan_call(proj, gates, cos_t, sin_t, states, weights, *, ts, n_seq, steps, row_off):
    H = N_HEADS_EVEN
    c0, n0, m0, conv0, s0 = states
    ng = proj.shape[0]
    rows_idx = lambda b, j: (row_off + b * steps + j, 0)
    state_specs = [pl.BlockSpec((1, H, LANES, LANES), lambda b, j: (b, 0, 0, 0)),
                   pl.BlockSpec((1, H, 1, LANES), lambda b, j: (b, 0, 0, 0)),
                   pl.BlockSpec((1, 1, LANES), lambda b, j: (b, 0, 0)),
                   pl.BlockSpec((1, SUBLANES, H * LANES), lambda b, j: (b, 0, 0)),
                   pl.BlockSpec((1, H, LANES, LANES), lambda b, j: (b, 0, 0, 0))]
    in_specs = [pl.BlockSpec((ng, ts, LANES), lambda b, j: (0, row_off + b * steps + j, 0)),
                pl.BlockSpec((ts, LANES), rows_idx),
                pl.BlockSpec((ts, LANES), rows_idx),
                pl.BlockSpec((ts, LANES), rows_idx)] + state_specs
    for w in weights:
        in_specs.append(pl.BlockSpec(w.shape, functools.partial(lambda nd, b, j: (0,) * nd, w.ndim)))
    out_specs = [pl.BlockSpec((ts, D_MODEL), lambda b, j: (b * steps + j, 0))] + state_specs
    out_shape = [jax.ShapeDtypeStruct((n_seq * steps * ts, D_MODEL), BF16),
                 jax.ShapeDtypeStruct(c0.shape, F32), jax.ShapeDtypeStruct(n0.shape, F32),
                 jax.ShapeDtypeStruct(m0.shape, F32), jax.ShapeDtypeStruct(conv0.shape, F32),
                 jax.ShapeDtypeStruct(s0.shape, F32)]
    nc = ts // CHUNK
    pair_state = pltpu.VMEM((H // 2, 2 * LANES, 2 * LANES), F32)
    scratch = [pair_state, pltpu.VMEM((H, 1, LANES), F32),
               pltpu.VMEM((1, LANES), F32), pltpu.VMEM((SUBLANES, H * LANES), F32),
               pair_state,
               pltpu.VMEM((ts + 2 * SUBLANES, LANES), F32),
               pltpu.VMEM((H, ts, LANES), F32), pltpu.VMEM((H, ts, LANES), F32),
               pltpu.VMEM((H, ts, LANES), F32), pltpu.VMEM((H, ts, LANES), F32),
               pltpu.VMEM((H, ts, LANES), F32),
               pltpu.VMEM((ts, LANES), F32), pltpu.VMEM((ts, LANES), F32),
               pltpu.VMEM((nc, SUBLANES, LANES), F32),
               pltpu.VMEM((H // 2, CHUNK, LANES), F32)]
    return pl.pallas_call(
        functools.partial(_even_scan_kernel, ts=ts),
        grid=(n_seq, steps),
        in_specs=in_specs, out_specs=out_specs, out_shape=out_shape,
        scratch_shapes=scratch,
        compiler_params=_cparams(("arbitrary", "arbitrary")),
        name="even_scan",
    )(proj, gates, cos_t, sin_t, c0, n0, m0, conv0, s0, *weights)


_G_Q, _G_F, _G_I, _G_G = 0, 8, 16, 24


def _hgrn_intra_pair(q, k, bcum, ti, si, left, tcol_i):
    amat = jnp.zeros((CHUNK, LANES), F32)
    for b in (32, 16, 8):
        nb2 = CHUNK // (2 * b)
        upper = ((tcol_i // b) % 2) == 1
        ql, kl = [], []
        for e in range(2):
            parts = [jnp.broadcast_to(bcum[e][m * 2 * b + b - 1:m * 2 * b + b, :], (2 * b, LANES))
                     for m in range(nb2)]
            ref = parts[0] if nb2 == 1 else jnp.concatenate(parts, axis=0)
            ql.append(jnp.where(upper, q[e] * jnp.exp(bcum[e] - ref), 0.0).astype(BF16))
            kl.append(jnp.where(upper, 0.0, k[e] * jnp.exp(ref - bcum[e])).astype(BF16))
        al = _dot_nt(jnp.concatenate(ql, axis=1), _block_diag(kl[0], kl[1]))
        amat = amat + jnp.where((ti // (2 * b)) == (si // (2 * b)), al, 0.0)
    nblk = CHUNK // SUBLANES
    b3 = [x.reshape(nblk, SUBLANES, LANES) for x in bcum]
    k3 = [x.reshape(nblk, SUBLANES, LANES) for x in k]
    for jj in range(SUBLANES):
        col = []
        for e in range(2):
            bj = jnp.broadcast_to(b3[e][:, jj:jj + 1, :], (nblk, SUBLANES, LANES)).reshape(CHUNK, LANES)
            kj = jnp.broadcast_to(k3[e][:, jj:jj + 1, :], (nblk, SUBLANES, LANES)).reshape(CHUNK, LANES)
            col.append(jnp.sum(q[e] * kj * jnp.exp(bcum[e] - bj), axis=-1, keepdims=True))
        sel = (si == (ti // SUBLANES) * SUBLANES + jj) & ((ti % SUBLANES) >= jj)
        amat = jnp.where(sel, jnp.where(left, col[0], col[1]), amat)
    return amat


def _odd_scan_kernel(proj_ref, s0_ref, lb_ref, nw_ref, u_ref, so_ref,
                     st_scr, k_scr, bc_scr, *, ts):
    H = N_HEADS_ODD
    j = pl.program_id(1)
    nj = pl.num_programs(1)
    nc = ts // CHUNK

    @pl.when(j == 0)
    def _():
        for p in range(H // 2):
            st_scr[p] = _block_diag(s0_ref[0, 2 * p].T, s0_ref[0, 2 * p + 1].T)

    for h in range(H):
        lo, hi = h * LANES, (h + 1) * LANES
        lbv = lb_ref[:, lo:hi]
        fpre = proj_ref[_G_F + h]
        e = jnp.exp(-jnp.abs(fpre))
        one_e = 1.0 + e
        log_lb = jnp.log(lbv)
        log_rest = jnp.log1p(-lbv) + (jnp.minimum(fpre, 0.0) - jnp.log(one_e))
        logf = jnp.maximum(log_lb, log_rest) + jnp.log(1.0 + jnp.exp(-jnp.abs(log_lb - log_rest)))
        k_scr[h] = (1.0 - lbv) * (jnp.where(fpre >= 0.0, e, 1.0) / one_e)
        bc_scr[h] = _chunk_cumsum(logf)

    ti = lax.broadcasted_iota(jnp.int32, (CHUNK, LANES), 0)
    lane2 = lax.broadcasted_iota(jnp.int32, (CHUNK, LANES), 1)
    left = lane2 < CHUNK
    si = lane2 % CHUNK
    tcol_i = lax.broadcasted_iota(jnp.int32, (CHUNK, 1), 0)
    same_head = ((lax.broadcasted_iota(jnp.int32, (2 * LANES, 2 * LANES), 0) < LANES)
                 == (lax.broadcasted_iota(jnp.int32, (2 * LANES, 2 * LANES), 1) < LANES))

    def chunk_body(c, carry):
        r0 = pl.multiple_of(c * CHUNK, CHUNK)
        rows = pl.ds(r0, CHUNK)
        st_old = [st_scr[p] for p in range(H // 2)]
        st_new = []
        for p in range(H // 2):
            hs = (2 * p, 2 * p + 1)
            q = [proj_ref[_G_Q + h, rows, :] for h in hs]
            k = [k_scr[h, rows, :] for h in hs]
            vb = [proj_ref[_G_I + h, rows, :].astype(BF16) for h in hs]
            bcum = [bc_scr[h, rows, :] for h in hs]
            amat = _hgrn_intra_pair(q, k, bcum, ti, si, left, tcol_i)
            st = st_old[p]
            qg = jnp.concatenate([q[e] * jnp.exp(bcum[e]) for e in range(2)], axis=1)
            o = (_dot(amat.astype(BF16), _block_diag(vb[0], vb[1]))
                 + _dot_nt(qg.astype(BF16), st.astype(BF16)))
            last = [bcum[e][CHUNK - 1:CHUNK, :] for e in range(2)]
            kd = jnp.concatenate([k[e] * jnp.exp(last[e] - bcum[e]) for e in range(2)], axis=1)
            cross = _dot_tn(jnp.concatenate(vb, axis=1), kd.astype(BF16))
            st_new.append(st * jnp.exp(jnp.concatenate(last, axis=1)) + jnp.where(same_head, cross, 0.0))
            for e in range(2):
                h = hs[e]
                lo, hi = h * LANES, (h + 1) * LANES
                u_ref[rows, lo:hi] = (_rms(o[:, e * LANES:(e + 1) * LANES]) * nw_ref[:, lo:hi]
                                      * _sigmoid(proj_ref[_G_G + h, rows, :])).astype(BF16)
        for p in range(H // 2):
            st_scr[p] = st_new[p]
        return carry

    lax.fori_loop(0, nc, chunk_body, 0)

    @pl.when(j == nj - 1)
    def _():
        for p in range(H // 2):
            for e in range(2):
                blk = slice(e * LANES, (e + 1) * LANES)
                so_ref[0, 2 * p + e] = st_scr[p, blk, blk].T


def _odd_scan_call(proj, s0, lb, nw, *, ts, n_seq, steps, row_off):
    H = N_HEADS_ODD
    ng = proj.shape[0]
    st_spec = pl.BlockSpec((1, H, LANES, LANES), lambda b, j: (b, 0, 0, 0))
    return pl.pallas_call(
        functools.partial(_odd_scan_kernel, ts=ts),
        grid=(n_seq, steps),
        in_specs=[pl.BlockSpec((ng, ts, LANES), lambda b, j: (0, row_off + b * steps + j, 0)),
                  st_spec,
                  pl.BlockSpec((1, D_MODEL), lambda b, j: (0, 0)),
                  pl.BlockSpec((1, D_MODEL), lambda b, j: (0, 0))],
        out_specs=[pl.BlockSpec((ts, D_MODEL), lambda b, j: (b * steps + j, 0)), st_spec],
        out_shape=[jax.ShapeDtypeStruct((n_seq * steps * ts, D_MODEL), BF16),
                   jax.ShapeDtypeStruct(s0.shape, F32)],
        scratch_shapes=[pltpu.VMEM((H // 2, 2 * LANES, 2 * LANES), F32),
                        pltpu.VMEM((H, ts, LANES), F32),
                        pltpu.VMEM((H, ts, LANES), F32)],
        compiler_params=_cparams(("arbitrary", "arbitrary")),
        name="odd_scan",
    )(proj, s0, lb, nw)


def _post_kernel(x_ref, up_ref, us_ref, mod_ref, wout_ref, nw_ref, wr_ref, br_ref,
                 xo_ref, hf_ref, gate_ref, lpos_ref, cnt_ref, hf_scr,
                 *, tm, npc, n_prompt_tiles):
    i = pl.program_id(0)
    nch = tm // CHUNK

    u = jnp.where(i < n_prompt_tiles, up_ref[...], us_ref[...])
    y = _dot(u, wout_ref[...])
    for c in range(nch):
        seq = _seq_row(i, nch, c, npc)
        gm = mod_ref[pl.ds(seq, 1), 2 * D_MODEL:3 * D_MODEL]
        shf = mod_ref[pl.ds(seq, 1), 3 * D_MODEL:4 * D_MODEL]
        scf = mod_ref[pl.ds(seq, 1), 4 * D_MODEL:5 * D_MODEL]
        rs = slice(c * CHUNK, (c + 1) * CHUNK)
        xn = x_ref[rs, :] + gm * y[rs, :]
        xo_ref[rs, :] = xn
        hf = _rms(xn) * nw_ref[...] * (1.0 + scf) + shf
        hf_scr[rs, :] = hf
        hf_ref[rs, :] = hf.astype(BF16)

    logits = _dot3(hf_scr[...], wr_ref[...]) + br_ref[...]
    lane_i = lax.broadcasted_iota(jnp.int32, (tm, LANES), 1)
    lane_f = lane_i.astype(F32)
    vals, idxs = [], []
    cur = logits
    for _ in range(TOP_K):
        m = jnp.max(cur, axis=-1, keepdims=True)
        idx = jnp.min(jnp.where(cur == m, lane_f, float(LANES)), axis=-1, keepdims=True)
        vals.append(m)
        idxs.append(idx)
        cur = jnp.where(lane_f == idx, -jnp.inf, cur)
    exps = [jnp.exp(v - vals[0]) for v in vals]
    denom = exps[0] + exps[1] + exps[2] + exps[3]
    onehot = jnp.zeros((tm, LANES), F32)
    for idx in idxs:
        onehot = onehot + jnp.where(lane_f == idx, 1.0, 0.0)
    r = lax.broadcasted_iota(jnp.int32, (tm, tm), 0)
    cidx = lax.broadcasted_iota(jnp.int32, (tm, tm), 1)
    strict = jnp.where((cidx < r) & (cidx // DISPATCH_TILE == r // DISPATCH_TILE), 1.0, 0.0).astype(BF16)
    before = _dot(strict, onehot.astype(BF16))
    er = lax.broadcasted_iota(jnp.int32, (LANES, LANES), 0)
    ec = lax.broadcasted_iota(jnp.int32, (LANES, LANES), 1)
    lower_experts = jnp.where(er < ec, 1.0, 0.0).astype(BF16)
    pos_parts = []
    for s in range(tm // DISPATCH_TILE):
        rs = slice(s * DISPATCH_TILE, (s + 1) * DISPATCH_TILE)
        cnt = jnp.sum(onehot[rs, :], axis=0, keepdims=True)
        cnt_ref[s] = cnt
        n8 = jnp.floor((cnt + (SUBLANES - 1.0)) * (1.0 / SUBLANES)) * float(SUBLANES)
        run_start = _dot(jnp.broadcast_to(n8, (SUBLANES, LANES)).astype(BF16), lower_experts)[0:1, :]
        pos_parts.append(before[rs, :] + run_start)
    posmat = jnp.concatenate(pos_parts, axis=0)
    gate_o = jnp.zeros((tm, LANES), F32)
    lpos_o = jnp.zeros((tm, LANES), F32)
    for kk in range(TOP_K):
        lp = jnp.sum(jnp.where(lane_f == idxs[kk], posmat, 0.0), axis=-1, keepdims=True)
        gate_o = jnp.where(lane_i == kk, exps[kk] / denom, gate_o)
        lpos_o = jnp.where(lane_i == kk, lp, lpos_o)
    gate_ref[...] = gate_o
    lpos_ref[...] = lpos_o


def _post_call(x, u_p, u_s, mod_l, w_out, nw, w_r, b_r, npc):
    ttot = x.shape[0]
    tm = TOKEN_TILE
    npt = u_p.shape[0] // tm
    sub = tm // DISPATCH_TILE
    tile = lambda i: (i, 0)
    const = lambda i: (0, 0)
    return pl.pallas_call(
        functools.partial(_post_kernel, tm=tm, npc=npc, n_prompt_tiles=npt),
        grid=(ttot // tm,),
        in_specs=[pl.BlockSpec((tm, D_MODEL), tile),
                  pl.BlockSpec((tm, D_MODEL), lambda i: (jnp.minimum(i, npt - 1), 0)),
                  pl.BlockSpec((tm, D_MODEL), lambda i: (jnp.maximum(i - npt, 0), 0)),
                  pl.BlockSpec(mod_l.shape, const),
                  pl.BlockSpec(w_out.shape, const),
                  pl.BlockSpec((1, D_MODEL), const),
                  pl.BlockSpec(w_r.shape, const),
                  pl.BlockSpec((1, LANES), const)],
        out_specs=[pl.BlockSpec((tm, D_MODEL), tile), pl.BlockSpec((tm, D_MODEL), tile),
                   pl.BlockSpec((tm, LANES), tile), pl.BlockSpec((tm, LANES), tile),
                   pl.BlockSpec((sub, 1, LANES), lambda i: (i, 0, 0))],
        out_shape=[jax.ShapeDtypeStruct((ttot, D_MODEL), F32), jax.ShapeDtypeStruct((ttot, D_MODEL), BF16),
                   jax.ShapeDtypeStruct((ttot, LANES), F32), jax.ShapeDtypeStruct((ttot, LANES), F32),
                   jax.ShapeDtypeStruct((ttot // DISPATCH_TILE, 1, LANES), F32)],
        scratch_shapes=[pltpu.VMEM((tm, D_MODEL), F32)],
        compiler_params=_cparams(("arbitrary",)),
        name="post",
    )(x, u_p, u_s, mod_l, w_out, nw, w_r, b_r)


def _plan_kernel(cnt_ref, n8_ref, loff_ref, gbase_ref, blk_ref, ends_ref, *, nbp, ntp):
    cnt = cnt_ref[...]
    n8 = jnp.floor((cnt + (SUBLANES - 1.0)) * (1.0 / SUBLANES)) * float(SUBLANES)
    r = lax.broadcasted_iota(jnp.int32, (LANES, LANES), 0)
    c = lax.broadcasted_iota(jnp.int32, (LANES, LANES), 1)
    loff = _dot_sel_rhs(n8, jnp.where(r < c, 1.0, 0.0).astype(BF16))
    gtot = jnp.broadcast_to(jnp.sum(n8, axis=0, keepdims=True), (SUBLANES, LANES))
    nblk = jnp.floor((gtot + (EXPERT_ROWS - 1.0)) * (1.0 / EXPERT_ROWS))
    ends = _dot_sel_rhs(nblk, jnp.where(r <= c, 1.0, 0.0).astype(BF16))
    start_row = (ends[0:1, :] - nblk[0:1, :]) * float(EXPERT_ROWS)
    tr = lax.broadcasted_iota(jnp.int32, (ntp, ntp), 0)
    tc = lax.broadcasted_iota(jnp.int32, (ntp, ntp), 1)
    gbase = start_row + _dot_sel_lhs(jnp.where(tc < tr, 1.0, 0.0).astype(BF16), n8)
    n8_ref[...] = n8.astype(jnp.int32)
    loff_ref[...] = loff.astype(jnp.int32)
    gbase_ref[...] = gbase.astype(jnp.int32)
    bi = lax.broadcasted_iota(jnp.int32, (nbp, LANES), 0).astype(F32)
    li = lax.broadcasted_iota(jnp.int32, (nbp, LANES), 1)
    done = jnp.where((li < N_EXPERTS) & (ends[0:1, :] <= bi), 1.0, 0.0)
    be = jnp.minimum(jnp.sum(done, axis=-1, keepdims=True), N_EXPERTS - 1.0)
    blk_ref[...] = jnp.broadcast_to(be, (nbp, LANES)).astype(jnp.int32)
    r8 = lax.broadcasted_iota(jnp.int32, (SUBLANES, LANES), 0)
    tail_start = start_row + gtot[0:1, :]
    tail_len = nblk[0:1, :] * float(EXPERT_ROWS) - gtot[0:1, :]
    info = jnp.where(r8 == 0, ends, jnp.where(r8 == 1, tail_start, jnp.where(r8 == 2, tail_len, 0.0)))
    ends_ref[...] = info.astype(jnp.int32)


def _plan_call(cnt_tiles, nbp):
    ntp = cnt_tiles.shape[0]
    const = lambda i: (0, 0)
    tbl = jax.ShapeDtypeStruct((ntp, LANES), jnp.int32)
    return pl.pallas_call(
        functools.partial(_plan_kernel, nbp=nbp, ntp=ntp),
        grid=(1,),
        in_specs=[pl.BlockSpec((ntp, LANES), const)],
        out_specs=[pl.BlockSpec((ntp, LANES), const), pl.BlockSpec((ntp, LANES), const),
                   pl.BlockSpec((ntp, LANES), const),
                   pl.BlockSpec((nbp, LANES), const), pl.BlockSpec((SUBLANES, LANES), const)],
        out_shape=[tbl, tbl, tbl,
                   jax.ShapeDtypeStruct((nbp, LANES), jnp.int32),
                   jax.ShapeDtypeStruct((SUBLANES, LANES), jnp.int32)],
        compiler_params=_cparams(("arbitrary",)),
        name="plan",
    )(cnt_tiles)


_GROUP_BITS = tuple(range(3, DISPATCH_TILE.bit_length()))


_TOTAL_BITS = tuple(range(3, SORT_ROWS.bit_length()))


def _group_copies(n8_ref, loff_ref, gbase_ref, tile, make_copy, wait):
    if wait:
        total = lax.fori_loop(0, N_EXPERTS, lambda e, acc: acc + n8_ref[tile * N_EXPERTS + e], 0)
        for bit in _TOTAL_BITS:
            size = 1 << bit

            @pl.when((total & size) != 0)
            def _():
                make_copy(0, 0, size).wait()
        return

    def per_expert(e, carry):
        idx = tile * N_EXPERTS + e
        n = n8_ref[idx]
        off = loff_ref[idx]
        base = gbase_ref[idx]
        for bit in _GROUP_BITS:
            size = 1 << bit

            @pl.when((n & size) != 0)
            def _():
                done = n & ~(2 * size - 1)
                make_copy(pl.multiple_of(off + done, SUBLANES), pl.multiple_of(base + done, SUBLANES), size).start()
        return carry

    lax.fori_loop(0, N_EXPERTS, per_expert, 0)


def _zero_fill(tails_ref, nu_ref, n_blocks, make_zero_copy, wait):
    def finish(cp):
        cp.wait() if wait else cp.start()

    def per_expert(e, carry):
        base = tails_ref[e]
        n = tails_ref[N_EXPERTS + e]
        for bit in range(3, EXPERT_ROWS.bit_length() - 1):
            size = 1 << bit

            @pl.when((n & size) != 0)
            def _():
                done = n & ~(2 * size - 1)
                finish(make_zero_copy(pl.multiple_of(base + done, SUBLANES), size))
        return carry

    lax.fori_loop(0, N_EXPERTS, per_expert, 0)

    def per_block(blk, carry):
        finish(make_zero_copy(pl.multiple_of(blk * EXPERT_ROWS, EXPERT_ROWS), EXPERT_ROWS))
        return carry

    lax.fori_loop(nu_ref[0], n_blocks, per_block, 0)


def _dispatch_kernel(*refs, n_blocks, fresh):
    if fresh:
        n8_ref, loff_ref, gbase_ref, tails_ref, nu_ref, hf_ref, lpos_ref, gate_ref, xs_ref, sbuf, sem = refs
    else:
        (n8_ref, loff_ref, gbase_ref, tails_ref, nu_ref, hf_ref, lpos_ref, gate_ref, _, xs_ref,
         sbuf, sem) = refs
    i = pl.program_id(0)

    if fresh:
        @pl.when(i == 0)
        def _():
            sbuf[1] = jnp.zeros((SORT_ROWS, XS_WIDTH), F32)

            def make_zero_copy(dst_row, size):
                return pltpu.make_async_copy(sbuf.at[1, pl.ds(0, size), :],
                                             xs_ref.at[pl.ds(dst_row, size), :], sem.at[1])

            _zero_fill(tails_ref, nu_ref, n_blocks, make_zero_copy, wait=False)
            _zero_fill(tails_ref, nu_ref, n_blocks, make_zero_copy, wait=True)

    eye8 = jnp.where(lax.broadcasted_iota(jnp.int32, (SUBLANES, LANES), 0)
                     == lax.broadcasted_iota(jnp.int32, (SUBLANES, LANES), 1), 1.0, 0.0).astype(BF16)
    lpos_t = _dot_sel_nt(eye8, lpos_ref[...])
    gate_t = _dot_sel_nt(eye8, gate_ref[...])
    row = lax.broadcasted_iota(jnp.int32, (SORT_ROWS, DISPATCH_TILE), 0).astype(F32)
    perm = jnp.zeros((SORT_ROWS, DISPATCH_TILE), F32)
    wgate = jnp.zeros((SORT_ROWS, DISPATCH_TILE), F32)
    for kk in range(TOP_K):
        hit = row == lpos_t[kk:kk + 1, :]
        perm = jnp.where(hit, 1.0, perm)
        wgate = jnp.where(hit, gate_t[kk:kk + 1, :], wgate)
    slot = i % 2
    sbuf[slot, :, 0:D_MODEL] = _dot(perm.astype(BF16), hf_ref[...].astype(BF16))
    sbuf[slot, :, D_MODEL:XS_WIDTH] = _dot_sel_rhs(wgate, jnp.ones((DISPATCH_TILE, LANES), BF16))

    def copies(tile, buf_slot, wait):
        def make_copy(src_row, dst_row, size):
            return pltpu.make_async_copy(sbuf.at[buf_slot, pl.ds(src_row, size), :],
                                         xs_ref.at[pl.ds(dst_row, size), :], sem.at[buf_slot])
        _group_copies(n8_ref, loff_ref, gbase_ref, tile, make_copy, wait=wait)

    copies(i, slot, wait=False)

    @pl.when(i > 0)
    def _():
        copies(i - 1, 1 - slot, wait=True)

    @pl.when(i == pl.num_programs(0) - 1)
    def _():
        copies(i, slot, wait=True)


def _dispatch_call(tables, tails, n_used, hf, lpos, gate, n_blocks, xs_prev):
    ttot = hf.shape[0]
    tm = DISPATCH_TILE
    tile = lambda i, *_: (i, 0)
    fresh = xs_prev is None
    in_specs = [pl.BlockSpec((tm, D_MODEL), tile), pl.BlockSpec((tm, LANES), tile),
                pl.BlockSpec((tm, LANES), tile)]
    args = [*tables, tails, n_used, hf, lpos, gate]
    if not fresh:
        in_specs.append(pl.BlockSpec(memory_space=pl.ANY))
        args.append(xs_prev)
    grid_spec = pltpu.PrefetchScalarGridSpec(
        num_scalar_prefetch=5, grid=(ttot // tm,),
        in_specs=in_specs,
        out_specs=pl.BlockSpec(memory_space=pl.ANY),
        scratch_shapes=[pltpu.VMEM((2, SORT_ROWS, XS_WIDTH), F32), pltpu.SemaphoreType.DMA((2,))])
    return pl.pallas_call(
        functools.partial(_dispatch_kernel, n_blocks=n_blocks, fresh=fresh),
        grid_spec=grid_spec,
        out_shape=jax.ShapeDtypeStruct((n_blocks * EXPERT_ROWS, XS_WIDTH), F32),
        input_output_aliases={} if fresh else {len(args) - 1: 0},
        compiler_params=pltpu.CompilerParams(dimension_semantics=("arbitrary",),
                                             vmem_limit_bytes=VMEM_LIMIT, has_side_effects=True),
        name="dispatch",
    )(*args)


def _expert_kernel(be_ref, nu_ref, xs_ref, wgu_ref, bgu_ref, wdn_ref, bdn_ref, y_ref,
                   wgu_bf, wdn_bf):
    b = pl.program_id(0)

    @pl.when(b < nu_ref[0])
    def _():
        prev = be_ref[jnp.maximum(b - 1, 0)]

        @pl.when((b == 0) | (be_ref[b] != prev))
        def _():
            wgu_bf[...] = wgu_ref[0, 0].astype(BF16)
            wdn_bf[...] = wdn_ref[0, 0].astype(BF16)

        gu = _dot(xs_ref[:, 0:D_MODEL].astype(BF16), wgu_bf[...]) + bgu_ref[0, 0]
        g = jnp.minimum(gu[:, :D_MODEL], SWIGLU_LIMIT)
        u = jnp.clip(gu[:, D_MODEL:], -SWIGLU_LIMIT, SWIGLU_LIMIT)
        act = (u + 1.0) * (g * _sigmoid(SWIGLU_ALPHA * g))
        gate = xs_ref[:, D_MODEL:D_MODEL + 1]
        y_ref[...] = (_dot(act.astype(BF16), wdn_bf[...]) + bdn_ref[0, 0]) * gate

    @pl.when(b >= nu_ref[0])
    def _():
        y_ref[...] = jnp.zeros_like(y_ref)


def _expert_call(blk_expert, n_used, xs, w_gu, b_gu, w_dn, b_dn, layer):
    nb = xs.shape[0] // EXPERT_ROWS
    d_ff2 = w_gu.shape[3]
    blk = lambda b, be, nu: (jnp.minimum(b, nu[0] - 1), 0)
    blk_out = lambda b, be, nu: (b, 0)
    exp4 = lambda b, be, nu: (layer, be[jnp.minimum(b, nu[0] - 1)], 0, 0)
    grid_spec = pltpu.PrefetchScalarGridSpec(
        num_scalar_prefetch=2, grid=(nb,),
        in_specs=[pl.BlockSpec((EXPERT_ROWS, XS_WIDTH), blk),
                  pl.BlockSpec((1, 1, D_MODEL, d_ff2), exp4),
                  pl.BlockSpec((1, 1, 1, d_ff2), exp4),
                  pl.BlockSpec((1, 1, D_MODEL, D_MODEL), exp4),
                  pl.BlockSpec((1, 1, 1, D_MODEL), exp4)],
        out_specs=pl.BlockSpec((EXPERT_ROWS, D_MODEL), blk_out),
        scratch_shapes=[pltpu.VMEM((D_MODEL, d_ff2), BF16), pltpu.VMEM((D_MODEL, D_MODEL), BF16)])
    return pl.pallas_call(
        _expert_kernel,
        grid_spec=grid_spec,
        out_shape=jax.ShapeDtypeStruct((xs.shape[0], D_MODEL), F32),
        compiler_params=_cparams(("arbitrary",)),
        name="experts",
    )(blk_expert, n_used, xs, w_gu, b_gu, w_dn, b_dn)


def _combine_kernel(*refs, tm, npc, final):
    if final:
        (n8_ref, loff_ref, gbase_ref, x_ref, lpos_ref, mod_ref, fnw_ref, y_hbm,
         yp_ref, ys_ref, ybuf, sem, ynorm) = refs
    else:
        n8_ref, loff_ref, gbase_ref, x_ref, lpos_ref, mod_ref, y_hbm, xo_ref, ybuf, sem = refs
    i = pl.program_id(0)
    nch = tm // CHUNK

    slot = i % 2

    def copies(tile, buf_slot, wait):
        def make_copy(buf_row, src_row, size):
            return pltpu.make_async_copy(y_hbm.at[pl.ds(src_row, size), :],
                                         ybuf.at[buf_slot, pl.ds(buf_row, size), :], sem.at[buf_slot])
        _group_copies(n8_ref, loff_ref, gbase_ref, tile, make_copy, wait=wait)

    @pl.when(i == 0)
    def _():
        ybuf[...] = jnp.zeros_like(ybuf)
        copies(i, slot, wait=False)

    @pl.when(i + 1 < pl.num_programs(0))
    def _():
        copies(i + 1, 1 - slot, wait=False)

    copies(i, slot, wait=True)

    lpos = lpos_ref[...]
    col = lax.broadcasted_iota(jnp.int32, (tm, SORT_ROWS), 1).astype(F32)
    unperm = jnp.zeros((tm, SORT_ROWS), F32)
    for kk in range(TOP_K):
        unperm = jnp.where(col == lpos[:, kk:kk + 1], 1.0, unperm)
    acc = _dot_sel_lhs2(unperm.astype(BF16), ybuf[slot])
    for c in range(nch):
        seq = _seq_row(i, nch, c, npc)
        gf = mod_ref[pl.ds(seq, 1), 5 * D_MODEL:6 * D_MODEL]
        rs = slice(c * CHUNK, (c + 1) * CHUNK)
        xn = x_ref[rs, :] + gf * acc[rs, :]
        if final:
            ynorm[rs, :] = _rms(xn) * fnw_ref[...]
        else:
            xo_ref[rs, :] = xn
    if final:
        is_prompt = i < (npc * CHUNK) // tm

        @pl.when(is_prompt)
        def _():
            yp_ref[...] = ynorm[...]

        @pl.when(jnp.logical_not(is_prompt))
        def _():
            ys_ref[...] = ynorm[...]


def _combine_call(tables, x, lpos, mod_l, y_sorted, npc, final_w):
    ttot = x.shape[0]
    tm = DISPATCH_TILE
    final = final_w is not None
    npt = (npc * CHUNK) // tm
    tile = lambda i, *_: (i, 0)
    const = lambda i, *_: (0, 0)
    in_specs = [pl.BlockSpec((tm, D_MODEL), tile),
                pl.BlockSpec((tm, LANES), tile),
                pl.BlockSpec(mod_l.shape, const)]
    args = [x, lpos, mod_l]
    if final:
        in_specs.append(pl.BlockSpec((1, D_MODEL), const))
        args.append(final_w)
    in_specs.append(pl.BlockSpec(memory_space=pl.ANY))
    args.append(y_sorted)
    if final:
        out_specs = [pl.BlockSpec((tm, D_MODEL), lambda i, *_: (jnp.minimum(i, npt - 1), 0)),
                     pl.BlockSpec((tm, D_MODEL), lambda i, *_: (jnp.maximum(i - npt, 0), 0))]
        out_shape = [jax.ShapeDtypeStruct((npt * tm, D_MODEL), F32),
                     jax.ShapeDtypeStruct((ttot - npt * tm, D_MODEL), F32)]
    else:
        out_specs = [pl.BlockSpec((tm, D_MODEL), tile)]
        out_shape = [jax.ShapeDtypeStruct((ttot, D_MODEL), F32)]
    scratch = [pltpu.VMEM((2, SORT_ROWS, D_MODEL), F32), pltpu.SemaphoreType.DMA((2,))]
    if final:
        scratch.append(pltpu.VMEM((tm, D_MODEL), F32))
    grid_spec = pltpu.PrefetchScalarGridSpec(
        num_scalar_prefetch=3, grid=(ttot // tm,),
        in_specs=in_specs, out_specs=out_specs, scratch_shapes=scratch)
    return pl.pallas_call(
        functools.partial(_combine_kernel, tm=tm, npc=npc, final=final),
        grid_spec=grid_spec, out_shape=out_shape,
        compiler_params=_cparams(("arbitrary",)),
        name="combine",
    )(*tables, *args)


def kernel(x_prompt, x_sample, c_prompt, c_sample, state_mlstm_C, state_mlstm_n, state_mlstm_m, state_mlstm_conv, state_ret_S, state_hgrn_S, w_ada, b_ada, norm_mix_w, norm_ffn_w, final_norm_w, w_in_even, b_mlstm_i, b_mlstm_f, w_mlstm_conv, b_mlstm_conv, w_mlstm_q, w_mlstm_k, mlstm_skip, mlstm_norm_w, ret_norm_w, w_out_even, w_in_odd, hgrn_lb_logits, hgrn_norm_w, w_out_odd, moe_router_w, moe_router_b, moe_w_gate_up, moe_b_gate_up, moe_w_down, moe_b_down):
    bp, seq, d = x_prompt.shape
    bs, dseq, _ = x_sample.shape
    assert bp == 1 and d == D_MODEL and dseq == CHUNK
    assert seq % TOKEN_TILE == 0 and (bs * dseq) % TOKEN_TILE == 0
    depth = w_ada.shape[0]
    tp, tsmp = seq, bs * dseq
    ttot = tp + tsmp
    npc = tp // CHUNK
    he, ho = N_HEADS_EVEN, N_HEADS_ODD
    da = he * LANES
    past_len = 1024

    x = jnp.concatenate([x_prompt.reshape(tp, d), x_sample.reshape(tsmp, d)], axis=0)
    n_mod_rows = 2 * SUBLANES
    assert 1 + bs <= n_mod_rows
    c_all = jnp.zeros((n_mod_rows, d), F32).at[0:1].set(c_prompt).at[1:1 + bs].set(c_sample)
    mod = _ada_call(c_all, w_ada, b_ada)

    half = LANES // 2
    inv = ROPE_BASE ** (-jnp.arange(half, dtype=F32) / half)
    pos_all = jnp.concatenate([jnp.arange(tp, dtype=F32),
                               jnp.tile(past_len + jnp.arange(dseq, dtype=F32), bs)])
    ang = pos_all[:, None] * inv[None, :]
    cos_t = jnp.concatenate([jnp.cos(ang), jnp.cos(ang)], axis=-1)
    sin_t = jnp.concatenate([-jnp.sin(ang), jnp.sin(ang)], axis=-1)

    lb_p = jax.nn.softmax(hgrn_lb_logits.astype(F32), axis=0)
    lbs = jnp.cumsum(lb_p, axis=0) - lb_p[0]

    n_tiles = ttot // DISPATCH_TILE
    ntp = -(-n_tiles // LANES) * LANES
    max_rows = ttot * TOP_K + n_tiles * N_EXPERTS * (SUBLANES - 1)
    nb = -(-max_rows // EXPERT_ROWS) + N_EXPERTS
    nbp = -(-nb // SUBLANES) * SUBLANES
    xs = None

    ts_p = TOKEN_TILE
    steps_p = tp // ts_p
    even_out, odd_out = [], []
    y_final = None
    for l in range(depth):
        jl = l // 2
        mod_l = mod[l]
        if l % 2 == 0:
            w_in = w_in_even[jl]
            w_main = jnp.concatenate([w_in[:, da:3 * da], w_in[:, 3 * da + 2 * he:]], axis=1).astype(BF16)
            w_gate = jnp.zeros((d, da + LANES), F32).at[:, :da].set(w_in[:, :da])
            w_gate = w_gate.at[:, da:da + 2 * he].set(w_in[:, 3 * da:3 * da + 2 * he])
            proj, gates = _inproj_call(x, mod_l, norm_mix_w[l][None], w_main, w_gate, npc)
            gbias = jnp.zeros((1, LANES), F32).at[0, :he].set(b_mlstm_i[jl]).at[0, he:2 * he].set(b_mlstm_f[jl])
            weights = [w_mlstm_conv[jl], b_mlstm_conv[jl][None],
                       jnp.concatenate([w_mlstm_q[jl], w_mlstm_k[jl]], axis=-1), gbias,
                       mlstm_skip[jl][None], mlstm_norm_w[jl][None], ret_norm_w[jl][None]]
            zeros_p = (jnp.zeros((1, he, LANES, LANES), F32), jnp.zeros((1, he, 1, LANES), F32),
                       jnp.zeros((1, 1, LANES), F32), jnp.zeros((1, SUBLANES, da), F32),
                       jnp.zeros((1, he, LANES, LANES), F32))
            st_s = (state_mlstm_C[jl], state_mlstm_n[jl][:, :, None, :],
                    jnp.zeros((bs, 1, LANES), F32).at[:, 0, :he].set(state_mlstm_m[jl]),
                    jnp.zeros((bs, SUBLANES, da), F32).at[:, SUBLANES - (CONV_W - 1):].set(state_mlstm_conv[jl]),
                    state_ret_S[jl])
            res_p = _even_scan_call(proj, gates, cos_t, sin_t, zeros_p, weights,
                                    ts=ts_p, n_seq=1, steps=steps_p, row_off=0)
            res_s = _even_scan_call(proj, gates, cos_t, sin_t, st_s, weights,
                                    ts=CHUNK, n_seq=bs, steps=1, row_off=npc)
            u_p, u_s = res_p[0], res_s[0]
            even_out.append((res_p[1:], res_s[1:]))
            w_out = w_out_even[jl].astype(BF16)
        else:
            proj = _inproj_call(x, mod_l, norm_mix_w[l][None], w_in_odd[jl].astype(BF16), None, npc)[0]
            lb = lbs[l][None]
            nw = hgrn_norm_w[jl][None]
            u_p, sp = _odd_scan_call(proj, jnp.zeros((1, ho, LANES, LANES), F32), lb, nw,
                                     ts=ts_p, n_seq=1, steps=steps_p, row_off=0)
            u_s, ss = _odd_scan_call(proj, state_hgrn_S[jl], lb, nw,
                                     ts=CHUNK, n_seq=bs, steps=1, row_off=npc)
            odd_out.append((sp, ss))
            w_out = w_out_odd[jl].astype(BF16)

        w_r = jnp.zeros((d, LANES), F32).at[:, :N_EXPERTS].set(moe_router_w[l])
        b_r = jnp.full((1, LANES), -jnp.inf, F32).at[0, :N_EXPERTS].set(moe_router_b[l])
        x, hf, gate, lpos, cnt = _post_call(x, u_p, u_s, mod_l, w_out, norm_ffn_w[l][None], w_r, b_r, npc)
        cnt_tiles = jnp.zeros((ntp, LANES), F32).at[:n_tiles].set(cnt[:, 0, :])
        n8, loff, gbase, blk, ends = _plan_call(cnt_tiles, nbp)
        tables = [t[:n_tiles, :N_EXPERTS].reshape(-1) for t in (n8, loff, gbase)]
        blk_expert = blk[:nb, 0]
        n_used = ends[0, N_EXPERTS - 1:N_EXPERTS]
        tails = ends[1:3, :N_EXPERTS].reshape(-1)
        xs = _dispatch_call(tables, tails, n_used, hf, lpos, gate, nb, xs)
        y_sorted = _expert_call(blk_expert, n_used, xs, moe_w_gate_up, moe_b_gate_up[:, :, None, :],
                                moe_w_down, moe_b_down[:, :, None, :], l)
        if l == depth - 1:
            y_final = _combine_call(tables, x, lpos, mod_l, y_sorted, npc, final_norm_w[None])
        else:
            x = _combine_call(tables, x, lpos, mod_l, y_sorted, npc, None)[0]

    def even_states(which):
        cs = jnp.stack([e[which][0] for e in even_out])
        ns = jnp.stack([e[which][1][:, :, 0, :] for e in even_out])
        ms = jnp.stack([e[which][2][:, 0, :he] for e in even_out])
        cv = jnp.stack([e[which][3][:, SUBLANES - (CONV_W - 1):, :] for e in even_out])
        ss = jnp.stack([e[which][4] for e in even_out])
        return cs, ns, ms, cv, ss

    p_c, p_n, p_m, p_cv, p_s = even_states(0)
    s_c, s_n, s_m, s_cv, s_s = even_states(1)
    p_h = jnp.stack([o[0] for o in odd_out])
    s_h = jnp.stack([o[1] for o in odd_out])
    y_prompt = y_final[0].reshape(bp, seq, d)
    y_sample = y_final[1].reshape(bs, dseq, d)
    return (y_prompt, y_sample, p_c, p_n, p_m, p_cv, p_s, p_h, s_c, s_n, s_m, s_cv, s_s, s_h)
```

```python
import functools
import math

import jax
import jax.numpy as jnp
from jax import lax
from jax.experimental import pallas as pl
from jax.experimental.pallas import tpu as pltpu

F32 = jnp.float32
BF16 = jnp.bfloat16

CHUNK = 64
LANES = 128
SUBLANES = 8
D_MODEL = 1024
N_HEADS_EVEN = 4
N_HEADS_ODD = 8
CONV_W = 4
N_EXPERTS = 32
TOP_K = 4
SWIGLU_LIMIT = 7.0
SWIGLU_ALPHA = 1.702
EPS = 1e-6
ROPE_BASE = 10000.0

TOKEN_TILE = 512
INPROJ_TILE = 512
DISPATCH_TILE = 256
SORT_ROWS = DISPATCH_TILE * TOP_K + LANES * 2
XS_WIDTH = D_MODEL + LANES
EXPERT_ROWS = 512
VMEM_LIMIT = 56 * 1024 * 1024


def _cparams(sem, vmem=VMEM_LIMIT):
    return pltpu.CompilerParams(dimension_semantics=sem, vmem_limit_bytes=vmem)


def _dot(a, b):
    return jnp.dot(a, b, preferred_element_type=F32)


def _dot_nt(a, b):
    return lax.dot_general(a, b, (((1,), (1,)), ((), ())), preferred_element_type=F32)


def _dot_tn(a, b):
    return lax.dot_general(a, b, (((0,), (0,)), ((), ())), preferred_element_type=F32)


def _split3(x):
    p1 = x.astype(BF16)
    r1 = x - p1.astype(F32)
    p2 = r1.astype(BF16)
    p3 = (r1 - p2.astype(F32)).astype(BF16)
    return p1, p2, p3


def _dot3(a, b):
    ah = a.astype(BF16)
    al = (a - ah.astype(F32)).astype(BF16)
    bh = b.astype(BF16)
    bl = (b - bh.astype(F32)).astype(BF16)
    return _dot(ah, bh) + (_dot(ah, bl) + _dot(al, bh))


def _dot3_tn_fused(a, b):
    n = a.shape[1]
    ah = a.astype(BF16)
    al = (a - ah.astype(F32)).astype(BF16)
    bh = b.astype(BF16)
    bl = (b - bh.astype(F32)).astype(BF16)
    x = _dot_tn(jnp.concatenate([ah, al], axis=1), jnp.concatenate([bh, bl], axis=1))
    return x[:n, :n] + (x[:n, n:] + x[n:, :n])


def _block_diag(a, b):
    z = jnp.zeros_like(a)
    return jnp.concatenate([jnp.concatenate([a, z], axis=1), jnp.concatenate([z, b], axis=1)], axis=0)


def _dot_sel_lhs(sel_bf16, x):
    p1, p2, p3 = _split3(x)
    return _dot(sel_bf16, p1) + (_dot(sel_bf16, p2) + _dot(sel_bf16, p3))


def _dot_sel_nt(sel_bf16, x):
    p1, p2, p3 = _split3(x)
    return _dot_nt(sel_bf16, p1) + (_dot_nt(sel_bf16, p2) + _dot_nt(sel_bf16, p3))


def _dot_sel_rhs(x, sel_bf16):
    p1, p2, p3 = _split3(x)
    return _dot(p1, sel_bf16) + (_dot(p2, sel_bf16) + _dot(p3, sel_bf16))


def _dot_sel_lhs2(sel_bf16, x):
    hi = x.astype(BF16)
    lo = (x - hi.astype(F32)).astype(BF16)
    return _dot(sel_bf16, hi) + _dot(sel_bf16, lo)


def _sigmoid(x):
    return 1.0 / (1.0 + jnp.exp(-x))


def _log_sigmoid(x):
    return jnp.minimum(x, 0.0) - jnp.log1p(jnp.exp(-jnp.abs(x)))


def _rms(x):
    return x * lax.rsqrt(jnp.mean(x * x, axis=-1, keepdims=True) + EPS)


def _chunk_tri(n):
    r = lax.broadcasted_iota(jnp.int32, (n, n), 0)
    c = lax.broadcasted_iota(jnp.int32, (n, n), 1)
    return jnp.where((r // CHUNK == c // CHUNK) & (c <= r), 1.0, 0.0).astype(BF16)


def _chunk_cumsum(x):
    rows, n = x.shape
    vregs = CHUNK // SUBLANES
    x4 = x.reshape(rows // CHUNK, vregs, SUBLANES, n)
    sub = lax.broadcasted_iota(jnp.int32, x4.shape, 2)
    s = x4
    for shift in (1, 2, 4):
        s = s + jnp.where(sub >= shift, pltpu.roll(s, shift, 2), 0.0)
    outs, carry = [], None
    for v in range(vregs):
        cur = s[:, v] if carry is None else s[:, v] + carry
        outs.append(cur)
        carry = jnp.broadcast_to(cur[:, SUBLANES - 1:SUBLANES, :], cur.shape)
    return jnp.stack(outs, axis=1).reshape(rows, n)


def _seq_row(tile_idx, chunks_per_tile, c, n_prompt_chunks):
    return jnp.maximum(tile_idx * chunks_per_tile + c - (n_prompt_chunks - 1), 0)


def _ada_kernel(c_ref, w_ref, b_ref, o_ref):
    c = c_ref[...]
    o_ref[0] = _dot3(c * _sigmoid(c), w_ref[0]) + b_ref[0]


def _ada_call(c_all, w_ada, b_ada):
    depth = w_ada.shape[0]
    nrow = c_all.shape[0]
    ncol = w_ada.shape[2] // D_MODEL
    return pl.pallas_call(
        _ada_kernel,
        grid=(depth, ncol),
        in_specs=[pl.BlockSpec((nrow, D_MODEL), lambda l, j: (0, 0)),
                  pl.BlockSpec((1, D_MODEL, D_MODEL), lambda l, j: (l, 0, j)),
                  pl.BlockSpec((1, 1, D_MODEL), lambda l, j: (l, 0, j))],
        out_specs=pl.BlockSpec((1, nrow, D_MODEL), lambda l, j: (l, 0, j)),
        out_shape=jax.ShapeDtypeStruct((depth, nrow, ncol * D_MODEL), F32),
        compiler_params=_cparams(("arbitrary", "arbitrary")),
        name="ada",
    )(c_all, w_ada, b_ada.reshape(depth, 1, -1))


def _inproj_kernel(*refs, tm, ng, npc, n_hp):
    if n_hp:
        x_ref, mod_ref, nw_ref, w_ref, whp_ref, proj_ref, gates_ref, h_scr = refs
    else:
        x_ref, mod_ref, nw_ref, w_ref, proj_ref, h_scr = refs
    i = pl.program_id(0)
    nch = tm // CHUNK
    for c in range(nch):
        seq = _seq_row(i, nch, c, npc)
        sh = mod_ref[pl.ds(seq, 1), 0:D_MODEL]
        sc = mod_ref[pl.ds(seq, 1), D_MODEL:2 * D_MODEL]
        xc = x_ref[c * CHUNK:(c + 1) * CHUNK, :]
        h_scr[c * CHUNK:(c + 1) * CHUNK, :] = _rms(xc) * nw_ref[...] * (1.0 + sc) + sh
    h = h_scr[...]
    hb = h.astype(BF16)
    for g in range(0, ng - n_hp, 4):
        res = _dot(hb, w_ref[:, g * LANES:(g + 4) * LANES])
        for jj in range(4):
            proj_ref[n_hp + g + jj] = res[:, jj * LANES:(jj + 1) * LANES]
    if n_hp:
        res = _dot3(h, whp_ref[...])
        for jj in range(n_hp):
            proj_ref[jj] = res[:, jj * LANES:(jj + 1) * LANES]
        gates_ref[...] = res[:, n_hp * LANES:]


def _inproj_call(x, mod_l, nw, w_main, w_gate, npc):
    ttot = x.shape[0]
    tm = INPROJ_TILE
    has_gates = w_gate is not None
    n_hp = w_gate.shape[1] // LANES - 1 if has_gates else 0
    ng = w_main.shape[1] // LANES + n_hp
    in_specs = [pl.BlockSpec((tm, D_MODEL), lambda i: (i, 0)),
                pl.BlockSpec(mod_l.shape, lambda i: (0, 0)),
                pl.BlockSpec((1, D_MODEL), lambda i: (0, 0)),
                pl.BlockSpec(w_main.shape, lambda i: (0, 0))]
    out_specs = [pl.BlockSpec((ng, tm, LANES), lambda i: (0, i, 0))]
    out_shape = [jax.ShapeDtypeStruct((ng, ttot, LANES), F32)]
    args = [x, mod_l, nw, w_main]
    if has_gates:
        in_specs.append(pl.BlockSpec(w_gate.shape, lambda i: (0, 0)))
        out_specs.append(pl.BlockSpec((tm, LANES), lambda i: (i, 0)))
        out_shape.append(jax.ShapeDtypeStruct((ttot, LANES), F32))
        args.append(w_gate)
    return pl.pallas_call(
        functools.partial(_inproj_kernel, tm=tm, ng=ng, npc=npc, n_hp=n_hp),
        grid=(ttot // tm,),
        in_specs=in_specs, out_specs=out_specs, out_shape=out_shape,
        scratch_shapes=[pltpu.VMEM((tm, D_MODEL), F32)],
        compiler_params=_cparams(("arbitrary",)),
        name="inproj",
    )(*args)


_G_XM, _G_VA, _G_OA, _G_QB, _G_KB, _G_VB, _G_GB = 0, 4, 8, 12, 16, 20, 24


def _even_scan_kernel(proj_ref, gates_ref, cos_ref, sin_ref,
                      c0_ref, n0_ref, m0_ref, conv0_ref, s0_ref,
                      cw_ref, cb_ref, wqk_ref, gbias_ref, skip_ref, nwa_ref, nwb_ref,
                      u_ref, co_ref, no_ref, mo_ref, convo_ref, so_ref,
                      c_scr, n_scr, m_scr, conv_scr, s_scr,
                      xbuf, xc_scr, q_scr, k_scr, qr_scr, kr_scr,
                      gl_scr, bc_scr, rows_scr, dec_scr, *, ts):
    H = N_HEADS_EVEN
    j = pl.program_id(1)
    nj = pl.num_programs(1)
    nc = ts // CHUNK

    @pl.when(j == 0)
    def _():
        for p in range(H // 2):
            c_scr[p] = _block_diag(c0_ref[0, 2 * p], c0_ref[0, 2 * p + 1])
            s_scr[p] = _block_diag(s0_ref[0, 2 * p], s0_ref[0, 2 * p + 1])
        n_scr[...] = n0_ref[0]
        m_scr[...] = m0_ref[0]
        conv_scr[...] = conv0_ref[0]

    for g in range(H):
        lo, hi = g * LANES, (g + 1) * LANES
        x_g = proj_ref[_G_XM + g]
        xbuf[0:SUBLANES, :] = conv_scr[:, lo:hi]
        xbuf[SUBLANES:SUBLANES + ts, :] = x_g
        acc = cb_ref[:, lo:hi] + cw_ref[CONV_W - 1:CONV_W, lo:hi] * x_g
        for t in range(CONV_W - 1):
            off = SUBLANES - (CONV_W - 1) + t
            acc = acc + cw_ref[t:t + 1, lo:hi] * xbuf[off:off + ts, :]
        conv_scr[:, lo:hi] = xbuf[ts:ts + SUBLANES, :]
        xc = acc * _sigmoid(acc)
        xc_scr[g] = xc
        qk = _dot3(xc, wqk_ref[g])
        q_scr[g] = qk[:, 0:LANES]
        k_scr[g] = qk[:, LANES:2 * LANES] * (LANES ** -0.5)

    cosv = cos_ref[...]
    sinv = sin_ref[...]
    for g in range(H):
        qb = proj_ref[_G_QB + g]
        kb = proj_ref[_G_KB + g]
        qr_scr[g] = qb * cosv + pltpu.roll(qb, LANES // 2, 1) * sinv
        kr_scr[g] = (kb * cosv + pltpu.roll(kb, LANES // 2, 1) * sinv) * (LANES ** -0.5)

    gpre = gates_ref[...] + gbias_ref[...]
    lane = lax.broadcasted_iota(jnp.int32, (ts, LANES), 1)
    gl = jnp.where(lane < H, gpre, _log_sigmoid(gpre))
    gl_scr[...] = gl
    bc = _dot_sel_lhs(_chunk_tri(ts), gl)
    bc_scr[...] = bc
    comb = jnp.where(lane < H, gl, bc)
    even_head = (lane % 2) == 0
    comb_even = jnp.where(even_head, comb, 0.0)
    comb_odd = jnp.where(even_head, 0.0, comb)
    pr = lax.broadcasted_iota(jnp.int32, (SUBLANES, LANES), 0)
    pc = lax.broadcasted_iota(jnp.int32, (SUBLANES, LANES), 1)
    pair_sel = jnp.where((pc // 2 == pr) & (pc < 2 * H), 1.0, 0.0).astype(BF16)
    for c in range(nc):
        cs = slice(c * CHUNK, (c + 1) * CHUNK)
        rows_scr[c] = _dot_sel_nt(pair_sel, jnp.concatenate([comb_even[cs, :], comb_odd[cs, :]], axis=0))

    ti = lax.broadcasted_iota(jnp.int32, (CHUNK, LANES), 0)
    lane2 = lax.broadcasted_iota(jnp.int32, (CHUNK, LANES), 1)
    left = lane2 < CHUNK
    si = lane2 % CHUNK
    tril = ti >= si
    left_wide = lax.broadcasted_iota(jnp.int32, (CHUNK, 2 * LANES), 1) < LANES
    br = lax.broadcasted_iota(jnp.int32, (2 * LANES, 2 * LANES), 0) < LANES
    bcol_blk = lax.broadcasted_iota(jnp.int32, (2 * LANES, 2 * LANES), 1) < LANES
    tcol = lax.broadcasted_iota(jnp.int32, (CHUNK, 1), 0).astype(F32)
    log_gamma = [math.log1p(-2.0 ** (-5 - h)) for h in range(H)]
    for p in range(H // 2):
        lg2 = jnp.where(left, log_gamma[2 * p], log_gamma[2 * p + 1])
        dec_scr[p] = jnp.where(tril, jnp.exp((ti - si).astype(F32) * lg2), 0.0)

    def chunk_body(c, carry):
        r0 = pl.multiple_of(c * CHUNK, CHUNK)
        rows = pl.ds(r0, CHUNK)
        pair_rows = rows_scr[c]
        glc = gl_scr[rows, :]
        bcc = bc_scr[rows, :]
        m_all = m_scr[...]
        c_old = [c_scr[p] for p in range(H // 2)]
        s_old = [s_scr[p] for p in range(H // 2)]
        n_old = [n_scr[h] for h in range(H)]
        c_new, s_new, n_new = [], [], []
        m_next = m_all
        lane_row = lax.broadcasted_iota(jnp.int32, (1, LANES), 1)
        for p in range(H // 2):
            hs = (2 * p, 2 * p + 1)
            q = [q_scr[h, rows, :] for h in hs]
            k = [k_scr[h, rows, :] for h in hs]
            v = [proj_ref[_G_VA + h, rows, :] for h in hs]
            b_col = [bcc[:, H + h:H + h + 1] for h in hs]
            i_col = [glc[:, h:h + 1] for h in hs]
            m_prev = [m_all[:, h:h + 1] for h in hs]
            q2 = jnp.concatenate(q, axis=1).astype(BF16)
            smat = _dot_nt(q2, _block_diag(k[0].astype(BF16), k[1].astype(BF16)))
            dmat = jnp.where(tril, jnp.where(left, b_col[0], b_col[1])
                             - pair_rows[2 + p:3 + p, :] + pair_rows[p:p + 1, :], -jnp.inf)
            a = [jnp.max(jnp.where(left, dmat, -jnp.inf), axis=-1, keepdims=True),
                 jnp.max(jnp.where(left, -jnp.inf, dmat), axis=-1, keepdims=True)]
            inter = [b_col[e] + m_prev[e] for e in range(2)]
            m_t = [jnp.maximum(inter[e], a[e]) for e in range(2)]
            w_inter = [jnp.exp(inter[e] - m_t[e]) for e in range(2)]
            amat = smat * jnp.exp(dmat - jnp.where(left, m_t[0], m_t[1]))
            cst = c_old[p]
            num = (_dot(amat.astype(BF16), _block_diag(v[0].astype(BF16), v[1].astype(BF16)))
                   + jnp.where(left_wide, w_inter[0], w_inter[1]) * _dot(q2, cst.astype(BF16)))
            den = [jnp.sum(jnp.where(left, amat, 0.0), axis=-1, keepdims=True),
                   jnp.sum(jnp.where(left, 0.0, amat), axis=-1, keepdims=True)]
            den = [jnp.maximum(jnp.abs(den[e] + w_inter[e] * jnp.sum(q[e] * n_old[hs[e]], axis=-1, keepdims=True)),
                               jnp.exp(-m_t[e])) for e in range(2)]
            hout = num / jnp.where(left_wide, den[0], den[1])
            decay, upd = [], []
            for e in range(2):
                m_new = m_t[e][CHUNK - 1:CHUNK, :]
                b_last = b_col[e][CHUNK - 1:CHUNK, :]
                kw = k[e] * jnp.exp(b_last - b_col[e] + i_col[e] - m_new)
                decay.append(jnp.exp(b_last + m_prev[e] - m_new))
                upd.append(_dot3_tn_fused(kw, v[e]))
                n_new.append(decay[e] * n_old[hs[e]] + jnp.sum(kw, axis=0, keepdims=True))
                m_next = jnp.where(lane_row == hs[e], m_new, m_next)
            c_new.append(jnp.where(br, decay[0], decay[1]) * cst + _block_diag(upd[0], upd[1]))
            for e in range(2):
                h = hs[e]
                lo, hi = h * LANES, (h + 1) * LANES
                z = _sigmoid(proj_ref[_G_OA + h, rows, :]) * hout[:, e * LANES:(e + 1) * LANES]
                u_ref[rows, lo:hi] = (_rms(z) * nwa_ref[:, lo:hi]
                                      + skip_ref[:, lo:hi] * xc_scr[h, rows, :]).astype(BF16)
        for p in range(H // 2):
            hs = (2 * p, 2 * p + 1)
            lg = [log_gamma[h] for h in hs]
            kr = [kr_scr[h, rows, :] for h in hs]
            vb = [proj_ref[_G_VB + h, rows, :].astype(BF16) for h in hs]
            q2 = jnp.concatenate([qr_scr[h, rows, :] for h in hs], axis=1).astype(BF16)
            amat = _dot_nt(q2, _block_diag(kr[0].astype(BF16), kr[1].astype(BF16))) * dec_scr[p]
            sst = s_old[p]
            o = (_dot(amat.astype(BF16), _block_diag(vb[0], vb[1]))
                 + jnp.exp((tcol + 1.0) * jnp.where(left_wide, lg[0], lg[1])) * _dot(q2, sst.astype(BF16)))
            kws = jnp.concatenate([kr[e] * jnp.exp((CHUNK - 1.0 - tcol) * lg[e]) for e in range(2)], axis=1)
            cross = _dot_tn(kws.astype(BF16), jnp.concatenate(vb, axis=1))
            s_new.append(jnp.where(br, math.exp(CHUNK * lg[0]), math.exp(CHUNK * lg[1])) * sst
                         + jnp.where(br == bcol_blk, cross, 0.0))
            for e in range(2):
                h = hs[e]
                lo, hi = h * LANES, (h + 1) * LANES
                gate = proj_ref[_G_GB + h, rows, :]
                u_ref[rows, D_MODEL // 2 + lo:D_MODEL // 2 + hi] = (
                    _rms(o[:, e * LANES:(e + 1) * LANES]) * nwb_ref[:, lo:hi]
                    * (gate * _sigmoid(gate))).astype(BF16)
        for p in range(H // 2):
            c_scr[p] = c_new[p]
            s_scr[p] = s_new[p]
        for h in range(H):
            n_scr[h] = n_new[h]
        m_scr[...] = m_next
        return carry

    lax.fori_loop(0, nc, chunk_body, 0)

    @pl.when(j == nj - 1)
    def _():
        for p in range(H // 2):
            for e in range(2):
                blk = slice(e * LANES, (e + 1) * LANES)
                co_ref[0, 2 * p + e] = c_scr[p, blk, blk]
                so_ref[0, 2 * p + e] = s_scr[p, blk, blk]
        no_ref[0] = n_scr[...]
        mo_ref[0] = m_scr[...]
        convo_ref[0] = conv_scr[...]


def _even_scan_call(proj, gates, cos_t, sin_t, states, weights, *, ts, n_seq, steps, row_off):
    H = N_HEADS_EVEN
    c0, n0, m0, conv0, s0 = states
    ng = proj.shape[0]
    rows_idx = lambda b, j: (row_off + b * steps + j, 0)
    state_specs = [pl.BlockSpec((1, H, LANES, LANES), lambda b, j: (b, 0, 0, 0)),
                   pl.BlockSpec((1, H, 1, LANES), lambda b, j: (b, 0, 0, 0)),
                   pl.BlockSpec((1, 1, LANES), lambda b, j: (b, 0, 0)),
                   pl.BlockSpec((1, SUBLANES, H * LANES), lambda b, j: (b, 0, 0)),
                   pl.BlockSpec((1, H, LANES, LANES), lambda b, j: (b, 0, 0, 0))]
    in_specs = [pl.BlockSpec((ng, ts, LANES), lambda b, j: (0, row_off + b * steps + j, 0)),
                pl.BlockSpec((ts, LANES), rows_idx),
                pl.BlockSpec((ts, LANES), rows_idx),
                pl.BlockSpec((ts, LANES), rows_idx)] + state_specs
    for w in weights:
        in_specs.append(pl.BlockSpec(w.shape, functools.partial(lambda nd, b, j: (0,) * nd, w.ndim)))
    out_specs = [pl.BlockSpec((ts, D_MODEL), lambda b, j: (b * steps + j, 0))] + state_specs
    out_shape = [jax.ShapeDtypeStruct((n_seq * steps * ts, D_MODEL), BF16),
                 jax.ShapeDtypeStruct(c0.shape, F32), jax.ShapeDtypeStruct(n0.shape, F32),
                 jax.ShapeDtypeStruct(m0.shape, F32), jax.ShapeDtypeStruct(conv0.shape, F32),
                 jax.ShapeDtypeStruct(s0.shape, F32)]
    nc = ts // CHUNK
    pair_state = pltpu.VMEM((H // 2, 2 * LANES, 2 * LANES), F32)
    scratch = [pair_state, pltpu.VMEM((H, 1, LANES), F32),
               pltpu.VMEM((1, LANES), F32), pltpu.VMEM((SUBLANES, H * LANES), F32),
               pair_state,
               pltpu.VMEM((ts + 2 * SUBLANES, LANES), F32),
               pltpu.VMEM((H, ts, LANES), F32), pltpu.VMEM((H, ts, LANES), F32),
               pltpu.VMEM((H, ts, LANES), F32), pltpu.VMEM((H, ts, LANES), F32),
               pltpu.VMEM((H, ts, LANES), F32),
               pltpu.VMEM((ts, LANES), F32), pltpu.VMEM((ts, LANES), F32),
               pltpu.VMEM((nc, SUBLANES, LANES), F32),
               pltpu.VMEM((H // 2, CHUNK, LANES), F32)]
    return pl.pallas_call(
        functools.partial(_even_scan_kernel, ts=ts),
        grid=(n_seq, steps),
        in_specs=in_specs, out_specs=out_specs, out_shape=out_shape,
        scratch_shapes=scratch,
        compiler_params=_cparams(("arbitrary", "arbitrary")),
        name="even_scan",
    )(proj, gates, cos_t, sin_t, c0, n0, m0, conv0, s0, *weights)


_G_Q, _G_F, _G_I, _G_G = 0, 8, 16, 24


def _hgrn_intra_pair(q, k, bcum, ti, si, left, tcol_i):
    amat = jnp.zeros((CHUNK, LANES), F32)
    for b in (32, 16, 8):
        nb2 = CHUNK // (2 * b)
        upper = ((tcol_i // b) % 2) == 1
        ql, kl = [], []
        for e in range(2):
            parts = [jnp.broadcast_to(bcum[e][m * 2 * b + b - 1:m * 2 * b + b, :], (2 * b, LANES))
                     for m in range(nb2)]
            ref = parts[0] if nb2 == 1 else jnp.concatenate(parts, axis=0)
            ql.append(jnp.where(upper, q[e] * jnp.exp(bcum[e] - ref), 0.0).astype(BF16))
            kl.append(jnp.where(upper, 0.0, k[e] * jnp.exp(ref - bcum[e])).astype(BF16))
        al = _dot_nt(jnp.concatenate(ql, axis=1), _block_diag(kl[0], kl[1]))
        amat = amat + jnp.where((ti // (2 * b)) == (si // (2 * b)), al, 0.0)
    nblk = CHUNK // SUBLANES
    b3 = [x.reshape(nblk, SUBLANES, LANES) for x in bcum]
    k3 = [x.reshape(nblk, SUBLANES, LANES) for x in k]
    for jj in range(SUBLANES):
        col = []
        for e in range(2):
            bj = jnp.broadcast_to(b3[e][:, jj:jj + 1, :], (nblk, SUBLANES, LANES)).reshape(CHUNK, LANES)
            kj = jnp.broadcast_to(k3[e][:, jj:jj + 1, :], (nblk, SUBLANES, LANES)).reshape(CHUNK, LANES)
            col.append(jnp.sum(q[e] * kj * jnp.exp(bcum[e] - bj), axis=-1, keepdims=True))
        sel = (si == (ti // SUBLANES) * SUBLANES + jj) & ((ti % SUBLANES) >= jj)
        amat = jnp.where(sel, jnp.where(left, col[0], col[1]), amat)
    return amat


def _odd_scan_kernel(proj_ref, s0_ref, lb_ref, nw_ref, u_ref, so_ref,
                     st_scr, k_scr, bc_scr, *, ts):
    H = N_HEADS_ODD
    j = pl.program_id(1)
    nj = pl.num_programs(1)
    nc = ts // CHUNK

    @pl.when(j == 0)
    def _():
        for p in range(H // 2):
            st_scr[p] = _block_diag(s0_ref[0, 2 * p].T, s0_ref[0, 2 * p + 1].T)

    for h in range(H):
        lo, hi = h * LANES, (h + 1) * LANES
        lbv = lb_ref[:, lo:hi]
        fpre = proj_ref[_G_F + h]
        e = jnp.exp(-jnp.abs(fpre))
        one_e = 1.0 + e
        log_lb = jnp.log(lbv)
        log_rest = jnp.log1p(-lbv) + (jnp.minimum(fpre, 0.0) - jnp.log(one_e))
        logf = jnp.maximum(log_lb, log_rest) + jnp.log(1.0 + jnp.exp(-jnp.abs(log_lb - log_rest)))
        k_scr[h] = (1.0 - lbv) * (jnp.where(fpre >= 0.0, e, 1.0) / one_e)
        bc_scr[h] = _chunk_cumsum(logf)

    ti = lax.broadcasted_iota(jnp.int32, (CHUNK, LANES), 0)
    lane2 = lax.broadcasted_iota(jnp.int32, (CHUNK, LANES), 1)
    left = lane2 < CHUNK
    si = lane2 % CHUNK
    tcol_i = lax.broadcasted_iota(jnp.int32, (CHUNK, 1), 0)
    same_head = ((lax.broadcasted_iota(jnp.int32, (2 * LANES, 2 * LANES), 0) < LANES)
                 == (lax.broadcasted_iota(jnp.int32, (2 * LANES, 2 * LANES), 1) < LANES))

    def chunk_body(c, carry):
        r0 = pl.multiple_of(c * CHUNK, CHUNK)
        rows = pl.ds(r0, CHUNK)
        st_old = [st_scr[p] for p in range(H // 2)]
        st_new = []
        for p in range(H // 2):
            hs = (2 * p, 2 * p + 1)
            q = [proj_ref[_G_Q + h, rows, :] for h in hs]
            k = [k_scr[h, rows, :] for h in hs]
            vb = [proj_ref[_G_I + h, rows, :].astype(BF16) for h in hs]
            bcum = [bc_scr[h, rows, :] for h in hs]
            amat = _hgrn_intra_pair(q, k, bcum, ti, si, left, tcol_i)
            st = st_old[p]
            qg = jnp.concatenate([q[e] * jnp.exp(bcum[e]) for e in range(2)], axis=1)
            o = (_dot(amat.astype(BF16), _block_diag(vb[0], vb[1]))
                 + _dot_nt(qg.astype(BF16), st.astype(BF16)))
            last = [bcum[e][CHUNK - 1:CHUNK, :] for e in range(2)]
            kd = jnp.concatenate([k[e] * jnp.exp(last[e] - bcum[e]) for e in range(2)], axis=1)
            cross = _dot_tn(jnp.concatenate(vb, axis=1), kd.astype(BF16))
            st_new.append(st * jnp.exp(jnp.concatenate(last, axis=1)) + jnp.where(same_head, cross, 0.0))
            for e in range(2):
                h = hs[e]
                lo, hi = h * LANES, (h + 1) * LANES
                u_ref[rows, lo:hi] = (_rms(o[:, e * LANES:(e + 1) * LANES]) * nw_ref[:, lo:hi]
                                      * _sigmoid(proj_ref[_G_G + h, rows, :])).astype(BF16)
        for p in range(H // 2):
            st_scr[p] = st_new[p]
        return carry

    lax.fori_loop(0, nc, chunk_body, 0)

    @pl.when(j == nj - 1)
    def _():
        for p in range(H // 2):
            for e in range(2):
                blk = slice(e * LANES, (e + 1) * LANES)
                so_ref[0, 2 * p + e] = st_scr[p, blk, blk].T


def _odd_scan_call(proj, s0, lb, nw, *, ts, n_seq, steps, row_off):
    H = N_HEADS_ODD
    ng = proj.shape[0]
    st_spec = pl.BlockSpec((1, H, LANES, LANES), lambda b, j: (b, 0, 0, 0))
    return pl.pallas_call(
        functools.partial(_odd_scan_kernel, ts=ts),
        grid=(n_seq, steps),
        in_specs=[pl.BlockSpec((ng, ts, LANES), lambda b, j: (0, row_off + b * steps + j, 0)),
                  st_spec,
                  pl.BlockSpec((1, D_MODEL), lambda b, j: (0, 0)),
                  pl.BlockSpec((1, D_MODEL), lambda b, j: (0, 0))],
        out_specs=[pl.BlockSpec((ts, D_MODEL), lambda b, j: (b * steps + j, 0)), st_spec],
        out_shape=[jax.ShapeDtypeStruct((n_seq * steps * ts, D_MODEL), BF16),
                   jax.ShapeDtypeStruct(s0.shape, F32)],
        scratch_shapes=[pltpu.VMEM((H // 2, 2 * LANES, 2 * LANES), F32),
                        pltpu.VMEM((H, ts, LANES), F32),
                        pltpu.VMEM((H, ts, LANES), F32)],
        compiler_params=_cparams(("arbitrary", "arbitrary")),
        name="odd_scan",
    )(proj, s0, lb, nw)


def _post_kernel(x_ref, up_ref, us_ref, mod_ref, wout_ref, nw_ref, wr_ref, br_ref,
                 xo_ref, hf_ref, gate_ref, lpos_ref, cnt_ref, hf_scr,
                 *, tm, npc, n_prompt_tiles):
    i = pl.program_id(0)
    nch = tm // CHUNK

    u = jnp.where(i < n_prompt_tiles, up_ref[...], us_ref[...])
    y = _dot(u, wout_ref[...])
    for c in range(nch):
        seq = _seq_row(i, nch, c, npc)
        gm = mod_ref[pl.ds(seq, 1), 2 * D_MODEL:3 * D_MODEL]
        shf = mod_ref[pl.ds(seq, 1), 3 * D_MODEL:4 * D_MODEL]
        scf = mod_ref[pl.ds(seq, 1), 4 * D_MODEL:5 * D_MODEL]
        rs = slice(c * CHUNK, (c + 1) * CHUNK)
        xn = x_ref[rs, :] + gm * y[rs, :]
        xo_ref[rs, :] = xn
        hf = _rms(xn) * nw_ref[...] * (1.0 + scf) + shf
        hf_scr[rs, :] = hf
        hf_ref[rs, :] = hf.astype(BF16)

    logits = _dot3(hf_scr[...], wr_ref[...]) + br_ref[...]
    lane_i = lax.broadcasted_iota(jnp.int32, (tm, LANES), 1)
    lane_f = lane_i.astype(F32)
    vals, idxs = [], []
    cur = logits
    for _ in range(TOP_K):
        m = jnp.max(cur, axis=-1, keepdims=True)
        idx = jnp.min(jnp.where(cur == m, lane_f, float(LANES)), axis=-1, keepdims=True)
        vals.append(m)
        idxs.append(idx)
        cur = jnp.where(lane_f == idx, -jnp.inf, cur)
    exps = [jnp.exp(v - vals[0]) for v in vals]
    denom = exps[0] + exps[1] + exps[2] + exps[3]
    onehot = jnp.zeros((tm, LANES), F32)
    for idx in idxs:
        onehot = onehot + jnp.where(lane_f == idx, 1.0, 0.0)
    r = lax.broadcasted_iota(jnp.int32, (tm, tm), 0)
    cidx = lax.broadcasted_iota(jnp.int32, (tm, tm), 1)
    strict = jnp.where((cidx < r) & (cidx // DISPATCH_TILE == r // DISPATCH_TILE), 1.0, 0.0).astype(BF16)
    before = _dot(strict, onehot.astype(BF16))
    er = lax.broadcasted_iota(jnp.int32, (LANES, LANES), 0)
    ec = lax.broadcasted_iota(jnp.int32, (LANES, LANES), 1)
    lower_experts = jnp.where(er < ec, 1.0, 0.0).astype(BF16)
    pos_parts = []
    for s in range(tm // DISPATCH_TILE):
        rs = slice(s * DISPATCH_TILE, (s + 1) * DISPATCH_TILE)
        cnt = jnp.sum(onehot[rs, :], axis=0, keepdims=True)
        cnt_ref[s] = cnt
        n8 = jnp.floor((cnt + (SUBLANES - 1.0)) * (1.0 / SUBLANES)) * float(SUBLANES)
        run_start = _dot(jnp.broadcast_to(n8, (SUBLANES, LANES)).astype(BF16), lower_experts)[0:1, :]
        pos_parts.append(before[rs, :] + run_start)
    posmat = jnp.concatenate(pos_parts, axis=0)
    gate_o = jnp.zeros((tm, LANES), F32)
    lpos_o = jnp.zeros((tm, LANES), F32)
    for kk in range(TOP_K):
        lp = jnp.sum(jnp.where(lane_f == idxs[kk], posmat, 0.0), axis=-1, keepdims=True)
        gate_o = jnp.where(lane_i == kk, exps[kk] / denom, gate_o)
        lpos_o = jnp.where(lane_i == kk, lp, lpos_o)
    gate_ref[...] = gate_o
    lpos_ref[...] = lpos_o


def _post_call(x, u_p, u_s, mod_l, w_out, nw, w_r, b_r, npc):
    ttot = x.shape[0]
    tm = TOKEN_TILE
    npt = u_p.shape[0] // tm
    sub = tm // DISPATCH_TILE
    tile = lambda i: (i, 0)
    const = lambda i: (0, 0)
    return pl.pallas_call(
        functools.partial(_post_kernel, tm=tm, npc=npc, n_prompt_tiles=npt),
        grid=(ttot // tm,),
        in_specs=[pl.BlockSpec((tm, D_MODEL), tile),
                  pl.BlockSpec((tm, D_MODEL), lambda i: (jnp.minimum(i, npt - 1), 0)),
                  pl.BlockSpec((tm, D_MODEL), lambda i: (jnp.maximum(i - npt, 0), 0)),
                  pl.BlockSpec(mod_l.shape, const),
                  pl.BlockSpec(w_out.shape, const),
                  pl.BlockSpec((1, D_MODEL), const),
                  pl.BlockSpec(w_r.shape, const),
                  pl.BlockSpec((1, LANES), const)],
        out_specs=[pl.BlockSpec((tm, D_MODEL), tile), pl.BlockSpec((tm, D_MODEL), tile),
                   pl.BlockSpec((tm, LANES), tile), pl.BlockSpec((tm, LANES), tile),
                   pl.BlockSpec((sub, 1, LANES), lambda i: (i, 0, 0))],
        out_shape=[jax.ShapeDtypeStruct((ttot, D_MODEL), F32), jax.ShapeDtypeStruct((ttot, D_MODEL), BF16),
                   jax.ShapeDtypeStruct((ttot, LANES), F32), jax.ShapeDtypeStruct((ttot, LANES), F32),
                   jax.ShapeDtypeStruct((ttot // DISPATCH_TILE, 1, LANES), F32)],
        scratch_shapes=[pltpu.VMEM((tm, D_MODEL), F32)],
        compiler_params=_cparams(("arbitrary",)),
        name="post",
    )(x, u_p, u_s, mod_l, w_out, nw, w_r, b_r)


def _plan_kernel(cnt_ref, n8_ref, loff_ref, gbase_ref, blk_ref, ends_ref, *, nbp, ntp):
    cnt = cnt_ref[...]
    n8 = jnp.floor((cnt + (SUBLANES - 1.0)) * (1.0 / SUBLANES)) * float(SUBLANES)
    r = lax.broadcasted_iota(jnp.int32, (LANES, LANES), 0)
    c = lax.broadcasted_iota(jnp.int32, (LANES, LANES), 1)
    loff = _dot_sel_rhs(n8, jnp.where(r < c, 1.0, 0.0).astype(BF16))
    gtot = jnp.broadcast_to(jnp.sum(n8, axis=0, keepdims=True), (SUBLANES, LANES))
    nblk = jnp.floor((gtot + (EXPERT_ROWS - 1.0)) * (1.0 / EXPERT_ROWS))
    ends = _dot_sel_rhs(nblk, jnp.where(r <= c, 1.0, 0.0).astype(BF16))
    start_row = (ends[0:1, :] - nblk[0:1, :]) * float(EXPERT_ROWS)
    tr = lax.broadcasted_iota(jnp.int32, (ntp, ntp), 0)
    tc = lax.broadcasted_iota(jnp.int32, (ntp, ntp), 1)
    gbase = start_row + _dot_sel_lhs(jnp.where(tc < tr, 1.0, 0.0).astype(BF16), n8)
    n8_ref[...] = n8.astype(jnp.int32)
    loff_ref[...] = loff.astype(jnp.int32)
    gbase_ref[...] = gbase.astype(jnp.int32)
    bi = lax.broadcasted_iota(jnp.int32, (nbp, LANES), 0).astype(F32)
    li = lax.broadcasted_iota(jnp.int32, (nbp, LANES), 1)
    done = jnp.where((li < N_EXPERTS) & (ends[0:1, :] <= bi), 1.0, 0.0)
    be = jnp.minimum(jnp.sum(done, axis=-1, keepdims=True), N_EXPERTS - 1.0)
    blk_ref[...] = jnp.broadcast_to(be, (nbp, LANES)).astype(jnp.int32)
    r8 = lax.broadcasted_iota(jnp.int32, (SUBLANES, LANES), 0)
    tail_start = start_row + gtot[0:1, :]
    tail_len = nblk[0:1, :] * float(EXPERT_ROWS) - gtot[0:1, :]
    info = jnp.where(r8 == 0, ends, jnp.where(r8 == 1, tail_start, jnp.where(r8 == 2, tail_len, 0.0)))
    ends_ref[...] = info.astype(jnp.int32)


def _plan_call(cnt_tiles, nbp):
    ntp = cnt_tiles.shape[0]
    const = lambda i: (0, 0)
    tbl = jax.ShapeDtypeStruct((ntp, LANES), jnp.int32)
    return pl.pallas_call(
        functools.partial(_plan_kernel, nbp=nbp, ntp=ntp),
        grid=(1,),
        in_specs=[pl.BlockSpec((ntp, LANES), const)],
        out_specs=[pl.BlockSpec((ntp, LANES), const), pl.BlockSpec((ntp, LANES), const),
                   pl.BlockSpec((ntp, LANES), const),
                   pl.BlockSpec((nbp, LANES), const), pl.BlockSpec((SUBLANES, LANES), const)],
        out_shape=[tbl, tbl, tbl,
                   jax.ShapeDtypeStruct((nbp, LANES), jnp.int32),
                   jax.ShapeDtypeStruct((SUBLANES, LANES), jnp.int32)],
        compiler_params=_cparams(("arbitrary",)),
        name="plan",
    )(cnt_tiles)


_GROUP_BITS = tuple(range(3, DISPATCH_TILE.bit_length()))


_TOTAL_BITS = tuple(range(3, SORT_ROWS.bit_length()))


def _group_copies(n8_ref, loff_ref, gbase_ref, tile, make_copy, wait):
    if wait:
        total = lax.fori_loop(0, N_EXPERTS, lambda e, acc: acc + n8_ref[tile * N_EXPERTS + e], 0)
        for bit in _TOTAL_BITS:
            size = 1 << bit

            @pl.when((total & size) != 0)
            def _():
                make_copy(0, 0, size).wait()
        return

    def per_expert(e, carry):
        idx = tile * N_EXPERTS + e
        n = n8_ref[idx]
        off = loff_ref[idx]
        base = gbase_ref[idx]
        for bit in _GROUP_BITS:
            size = 1 << bit

            @pl.when((n & size) != 0)
            def _():
                done = n & ~(2 * size - 1)
                make_copy(pl.multiple_of(off + done, SUBLANES), pl.multiple_of(base + done, SUBLANES), size).start()
        return carry

    lax.fori_loop(0, N_EXPERTS, per_expert, 0)


def _zero_fill(tails_ref, nu_ref, n_blocks, make_zero_copy, wait):
    def finish(cp):
        cp.wait() if wait else cp.start()

    def per_expert(e, carry):
        base = tails_ref[e]
        n = tails_ref[N_EXPERTS + e]
        for bit in range(3, EXPERT_ROWS.bit_length() - 1):
            size = 1 << bit

            @pl.when((n & size) != 0)
            def _():
                done = n & ~(2 * size - 1)
                finish(make_zero_copy(pl.multiple_of(base + done, SUBLANES), size))
        return carry

    lax.fori_loop(0, N_EXPERTS, per_expert, 0)

    def per_block(blk, carry):
        finish(make_zero_copy(pl.multiple_of(blk * EXPERT_ROWS, EXPERT_ROWS), EXPERT_ROWS))
        return carry

    lax.fori_loop(nu_ref[0], n_blocks, per_block, 0)


def _dispatch_kernel(*refs, n_blocks, fresh):
    if fresh:
        n8_ref, loff_ref, gbase_ref, tails_ref, nu_ref, hf_ref, lpos_ref, gate_ref, xs_ref, sbuf, sem = refs
    else:
        (n8_ref, loff_ref, gbase_ref, tails_ref, nu_ref, hf_ref, lpos_ref, gate_ref, _, xs_ref,
         sbuf, sem) = refs
    i = pl.program_id(0)

    if fresh:
        @pl.when(i == 0)
        def _():
            sbuf[1] = jnp.zeros((SORT_ROWS, XS_WIDTH), F32)

            def make_zero_copy(dst_row, size):
                return pltpu.make_async_copy(sbuf.at[1, pl.ds(0, size), :],
                                             xs_ref.at[pl.ds(dst_row, size), :], sem.at[1])

            _zero_fill(tails_ref, nu_ref, n_blocks, make_zero_copy, wait=False)
            _zero_fill(tails_ref, nu_ref, n_blocks, make_zero_copy, wait=True)

    eye8 = jnp.where(lax.broadcasted_iota(jnp.int32, (SUBLANES, LANES), 0)
                     == lax.broadcasted_iota(jnp.int32, (SUBLANES, LANES), 1), 1.0, 0.0).astype(BF16)
    lpos_t = _dot_sel_nt(eye8, lpos_ref[...])
    gate_t = _dot_sel_nt(eye8, gate_ref[...])
    row = lax.broadcasted_iota(jnp.int32, (SORT_ROWS, DISPATCH_TILE), 0).astype(F32)
    perm = jnp.zeros((SORT_ROWS, DISPATCH_TILE), F32)
    wgate = jnp.zeros((SORT_ROWS, DISPATCH_TILE), F32)
    for kk in range(TOP_K):
        hit = row == lpos_t[kk:kk + 1, :]
        perm = jnp.where(hit, 1.0, perm)
        wgate = jnp.where(hit, gate_t[kk:kk + 1, :], wgate)
    slot = i % 2
    sbuf[slot, :, 0:D_MODEL] = _dot(perm.astype(BF16), hf_ref[...].astype(BF16))
    sbuf[slot, :, D_MODEL:XS_WIDTH] = jnp.broadcast_to(jnp.sum(wgate, axis=-1, keepdims=True),
                                                       (SORT_ROWS, LANES))

    def copies(tile, buf_slot, wait):
        def make_copy(src_row, dst_row, size):
            return pltpu.make_async_copy(sbuf.at[buf_slot, pl.ds(src_row, size), :],
                                         xs_ref.at[pl.ds(dst_row, size), :], sem.at[buf_slot])
        _group_copies(n8_ref, loff_ref, gbase_ref, tile, make_copy, wait=wait)

    copies(i, slot, wait=False)

    @pl.when(i > 0)
    def _():
        copies(i - 1, 1 - slot, wait=True)

    @pl.when(i == pl.num_programs(0) - 1)
    def _():
        copies(i, slot, wait=True)


def _dispatch_call(tables, tails, n_used, hf, lpos, gate, n_blocks, xs_prev):
    ttot = hf.shape[0]
    tm = DISPATCH_TILE
    tile = lambda i, *_: (i, 0)
    fresh = xs_prev is None
    in_specs = [pl.BlockSpec((tm, D_MODEL), tile), pl.BlockSpec((tm, LANES), tile),
                pl.BlockSpec((tm, LANES), tile)]
    args = [*tables, tails, n_used, hf, lpos, gate]
    if not fresh:
        in_specs.append(pl.BlockSpec(memory_space=pl.ANY))
        args.append(xs_prev)
    grid_spec = pltpu.PrefetchScalarGridSpec(
        num_scalar_prefetch=5, grid=(ttot // tm,),
        in_specs=in_specs,
        out_specs=pl.BlockSpec(memory_space=pl.ANY),
        scratch_shapes=[pltpu.VMEM((2, SORT_ROWS, XS_WIDTH), F32), pltpu.SemaphoreType.DMA((2,))])
    return pl.pallas_call(
        functools.partial(_dispatch_kernel, n_blocks=n_blocks, fresh=fresh),
        grid_spec=grid_spec,
        out_shape=jax.ShapeDtypeStruct((n_blocks * EXPERT_ROWS, XS_WIDTH), F32),
        input_output_aliases={} if fresh else {len(args) - 1: 0},
        compiler_params=pltpu.CompilerParams(dimension_semantics=("arbitrary",),
                                             vmem_limit_bytes=VMEM_LIMIT, has_side_effects=True),
        name="dispatch",
    )(*args)


def _expert_kernel(be_ref, nu_ref, xs_ref, wgu_ref, bgu_ref, wdn_ref, bdn_ref, y_ref,
                   wgu_bf, wdn_bf):
    b = pl.program_id(0)

    @pl.when(b < nu_ref[0])
    def _():
        prev = be_ref[jnp.maximum(b - 1, 0)]

        @pl.when((b == 0) | (be_ref[b] != prev))
        def _():
            wgu_bf[...] = wgu_ref[0, 0].astype(BF16)
            wdn_bf[...] = wdn_ref[0, 0].astype(BF16)

        gu = _dot(xs_ref[:, 0:D_MODEL].astype(BF16), wgu_bf[...]) + bgu_ref[0, 0]
        g = jnp.minimum(gu[:, :D_MODEL], SWIGLU_LIMIT)
        u = jnp.clip(gu[:, D_MODEL:], -SWIGLU_LIMIT, SWIGLU_LIMIT)
        act = (u + 1.0) * (g * _sigmoid(SWIGLU_ALPHA * g))
        gate = xs_ref[:, D_MODEL:D_MODEL + 1]
        y_ref[...] = (_dot(act.astype(BF16), wdn_bf[...]) + bdn_ref[0, 0]) * gate

    @pl.when(b >= nu_ref[0])
    def _():
        y_ref[...] = jnp.zeros_like(y_ref)


def _expert_call(blk_expert, n_used, xs, w_gu, b_gu, w_dn, b_dn, layer):
    nb = xs.shape[0] // EXPERT_ROWS
    d_ff2 = w_gu.shape[3]
    blk = lambda b, be, nu: (jnp.minimum(b, nu[0] - 1), 0)
    blk_out = lambda b, be, nu: (b, 0)
    exp4 = lambda b, be, nu: (layer, be[jnp.minimum(b, nu[0] - 1)], 0, 0)
    grid_spec = pltpu.PrefetchScalarGridSpec(
        num_scalar_prefetch=2, grid=(nb,),
        in_specs=[pl.BlockSpec((EXPERT_ROWS, XS_WIDTH), blk),
                  pl.BlockSpec((1, 1, D_MODEL, d_ff2), exp4),
                  pl.BlockSpec((1, 1, 1, d_ff2), exp4),
                  pl.BlockSpec((1, 1, D_MODEL, D_MODEL), exp4),
                  pl.BlockSpec((1, 1, 1, D_MODEL), exp4)],
        out_specs=pl.BlockSpec((EXPERT_ROWS, D_MODEL), blk_out),
        scratch_shapes=[pltpu.VMEM((D_MODEL, d_ff2), BF16), pltpu.VMEM((D_MODEL, D_MODEL), BF16)])
    return pl.pallas_call(
        _expert_kernel,
        grid_spec=grid_spec,
        out_shape=jax.ShapeDtypeStruct((xs.shape[0], D_MODEL), F32),
        compiler_params=_cparams(("arbitrary",)),
        name="experts",
    )(blk_expert, n_used, xs, w_gu, b_gu, w_dn, b_dn)


def _combine_kernel(*refs, tm, npc, final):
    if final:
        (n8_ref, loff_ref, gbase_ref, x_ref, lpos_ref, mod_ref, fnw_ref, y_hbm,
         yp_ref, ys_ref, ybuf, sem, ynorm) = refs
    else:
        n8_ref, loff_ref, gbase_ref, x_ref, lpos_ref, mod_ref, y_hbm, xo_ref, ybuf, sem = refs
    i = pl.program_id(0)
    nch = tm // CHUNK

    slot = i % 2

    def copies(tile, buf_slot, wait):
        def make_copy(buf_row, src_row, size):
            return pltpu.make_async_copy(y_hbm.at[pl.ds(src_row, size), :],
                                         ybuf.at[buf_slot, pl.ds(buf_row, size), :], sem.at[buf_slot])
        _group_copies(n8_ref, loff_ref, gbase_ref, tile, make_copy, wait=wait)

    @pl.when(i == 0)
    def _():
        ybuf[...] = jnp.zeros_like(ybuf)
        copies(i, slot, wait=False)

    @pl.when(i + 1 < pl.num_programs(0))
    def _():
        copies(i + 1, 1 - slot, wait=False)

    copies(i, slot, wait=True)

    lpos = lpos_ref[...]
    col = lax.broadcasted_iota(jnp.int32, (tm, SORT_ROWS), 1).astype(F32)
    unperm = jnp.zeros((tm, SORT_ROWS), F32)
    for kk in range(TOP_K):
        unperm = jnp.where(col == lpos[:, kk:kk + 1], 1.0, unperm)
    acc = _dot_sel_lhs2(unperm.astype(BF16), ybuf[slot])
    for c in range(nch):
        seq = _seq_row(i, nch, c, npc)
        gf = mod_ref[pl.ds(seq, 1), 5 * D_MODEL:6 * D_MODEL]
        rs = slice(c * CHUNK, (c + 1) * CHUNK)
        xn = x_ref[rs, :] + gf * acc[rs, :]
        if final:
            ynorm[rs, :] = _rms(xn) * fnw_ref[...]
        else:
            xo_ref[rs, :] = xn
    if final:
        is_prompt = i < (npc * CHUNK) // tm

        @pl.when(is_prompt)
        def _():
            yp_ref[...] = ynorm[...]

        @pl.when(jnp.logical_not(is_prompt))
        def _():
            ys_ref[...] = ynorm[...]


def _combine_call(tables, x, lpos, mod_l, y_sorted, npc, final_w):
    ttot = x.shape[0]
    tm = DISPATCH_TILE
    final = final_w is not None
    npt = (npc * CHUNK) // tm
    tile = lambda i, *_: (i, 0)
    const = lambda i, *_: (0, 0)
    in_specs = [pl.BlockSpec((tm, D_MODEL), tile),
                pl.BlockSpec((tm, LANES), tile),
                pl.BlockSpec(mod_l.shape, const)]
    args = [x, lpos, mod_l]
    if final:
        in_specs.append(pl.BlockSpec((1, D_MODEL), const))
        args.append(final_w)
    in_specs.append(pl.BlockSpec(memory_space=pl.ANY))
    args.append(y_sorted)
    if final:
        out_specs = [pl.BlockSpec((tm, D_MODEL), lambda i, *_: (jnp.minimum(i, npt - 1), 0)),
                     pl.BlockSpec((tm, D_MODEL), lambda i, *_: (jnp.maximum(i - npt, 0), 0))]
        out_shape = [jax.ShapeDtypeStruct((npt * tm, D_MODEL), F32),
                     jax.ShapeDtypeStruct((ttot - npt * tm, D_MODEL), F32)]
    else:
        out_specs = [pl.BlockSpec((tm, D_MODEL), tile)]
        out_shape = [jax.ShapeDtypeStruct((ttot, D_MODEL), F32)]
    scratch = [pltpu.VMEM((2, SORT_ROWS, D_MODEL), F32), pltpu.SemaphoreType.DMA((2,))]
    if final:
        scratch.append(pltpu.VMEM((tm, D_MODEL), F32))
    grid_spec = pltpu.PrefetchScalarGridSpec(
        num_scalar_prefetch=3, grid=(ttot // tm,),
        in_specs=in_specs, out_specs=out_specs, scratch_shapes=scratch)
    return pl.pallas_call(
        functools.partial(_combine_kernel, tm=tm, npc=npc, final=final),
        grid_spec=grid_spec, out_shape=out_shape,
        compiler_params=_cparams(("arbitrary",)),
        name="combine",
    )(*tables, *args)


def kernel(x_prompt, x_sample, c_prompt, c_sample, state_mlstm_C, state_mlstm_n, state_mlstm_m, state_mlstm_conv, state_ret_S, state_hgrn_S, w_ada, b_ada, norm_mix_w, norm_ffn_w, final_norm_w, w_in_even, b_mlstm_i, b_mlstm_f, w_mlstm_conv, b_mlstm_conv, w_mlstm_q, w_mlstm_k, mlstm_skip, mlstm_norm_w, ret_norm_w, w_out_even, w_in_odd, hgrn_lb_logits, hgrn_norm_w, w_out_odd, moe_router_w, moe_router_b, moe_w_gate_up, moe_b_gate_up, moe_w_down, moe_b_down):
    bp, seq, d = x_prompt.shape
    bs, dseq, _ = x_sample.shape
    assert bp == 1 and d == D_MODEL and dseq == CHUNK
    assert seq % TOKEN_TILE == 0 and (bs * dseq) % TOKEN_TILE == 0
    depth = w_ada.shape[0]
    tp, tsmp = seq, bs * dseq
    ttot = tp + tsmp
    npc = tp // CHUNK
    he, ho = N_HEADS_EVEN, N_HEADS_ODD
    da = he * LANES
    past_len = 1024

    x = jnp.concatenate([x_prompt.reshape(tp, d), x_sample.reshape(tsmp, d)], axis=0)
    n_mod_rows = 2 * SUBLANES
    assert 1 + bs <= n_mod_rows
    c_all = jnp.zeros((n_mod_rows, d), F32).at[0:1].set(c_prompt).at[1:1 + bs].set(c_sample)
    mod = _ada_call(c_all, w_ada, b_ada)

    half = LANES // 2
    inv = ROPE_BASE ** (-jnp.arange(half, dtype=F32) / half)
    pos_all = jnp.concatenate([jnp.arange(tp, dtype=F32),
                               jnp.tile(past_len + jnp.arange(dseq, dtype=F32), bs)])
    ang = pos_all[:, None] * inv[None, :]
    cos_t = jnp.concatenate([jnp.cos(ang), jnp.cos(ang)], axis=-1)
    sin_t = jnp.concatenate([-jnp.sin(ang), jnp.sin(ang)], axis=-1)

    lb_p = jax.nn.softmax(hgrn_lb_logits.astype(F32), axis=0)
    lbs = jnp.cumsum(lb_p, axis=0) - lb_p[0]

    n_tiles = ttot // DISPATCH_TILE
    ntp = -(-n_tiles // LANES) * LANES
    max_rows = ttot * TOP_K + n_tiles * N_EXPERTS * (SUBLANES - 1)
    nb = -(-max_rows // EXPERT_ROWS) + N_EXPERTS
    nbp = -(-nb // SUBLANES) * SUBLANES
    xs = None

    ts_p = TOKEN_TILE
    steps_p = tp // ts_p
    even_out, odd_out = [], []
    y_final = None
    for l in range(depth):
        jl = l // 2
        mod_l = mod[l]
        if l % 2 == 0:
            w_in = w_in_even[jl]
            w_main = jnp.concatenate([w_in[:, da:3 * da], w_in[:, 3 * da + 2 * he:]], axis=1).astype(BF16)
            w_gate = jnp.zeros((d, da + LANES), F32).at[:, :da].set(w_in[:, :da])
            w_gate = w_gate.at[:, da:da + 2 * he].set(w_in[:, 3 * da:3 * da + 2 * he])
            proj, gates = _inproj_call(x, mod_l, norm_mix_w[l][None], w_main, w_gate, npc)
            gbias = jnp.zeros((1, LANES), F32).at[0, :he].set(b_mlstm_i[jl]).at[0, he:2 * he].set(b_mlstm_f[jl])
            weights = [w_mlstm_conv[jl], b_mlstm_conv[jl][None],
                       jnp.concatenate([w_mlstm_q[jl], w_mlstm_k[jl]], axis=-1), gbias,
                       mlstm_skip[jl][None], mlstm_norm_w[jl][None], ret_norm_w[jl][None]]
            zeros_p = (jnp.zeros((1, he, LANES, LANES), F32), jnp.zeros((1, he, 1, LANES), F32),
                       jnp.zeros((1, 1, LANES), F32), jnp.zeros((1, SUBLANES, da), F32),
                       jnp.zeros((1, he, LANES, LANES), F32))
            st_s = (state_mlstm_C[jl], state_mlstm_n[jl][:, :, None, :],
                    jnp.zeros((bs, 1, LANES), F32).at[:, 0, :he].set(state_mlstm_m[jl]),
                    jnp.zeros((bs, SUBLANES, da), F32).at[:, SUBLANES - (CONV_W - 1):].set(state_mlstm_conv[jl]),
                    state_ret_S[jl])
            res_p = _even_scan_call(proj, gates, cos_t, sin_t, zeros_p, weights,
                                    ts=ts_p, n_seq=1, steps=steps_p, row_off=0)
            res_s = _even_scan_call(proj, gates, cos_t, sin_t, st_s, weights,
                                    ts=CHUNK, n_seq=bs, steps=1, row_off=npc)
            u_p, u_s = res_p[0], res_s[0]
            even_out.append((res_p[1:], res_s[1:]))
            w_out = w_out_even[jl].astype(BF16)
        else:
            proj = _inproj_call(x, mod_l, norm_mix_w[l][None], w_in_odd[jl].astype(BF16), None, npc)[0]
            lb = lbs[l][None]
            nw = hgrn_norm_w[jl][None]
            u_p, sp = _odd_scan_call(proj, jnp.zeros((1, ho, LANES, LANES), F32), lb, nw,
                                     ts=ts_p, n_seq=1, steps=steps_p, row_off=0)
            u_s, ss = _odd_scan_call(proj, state_hgrn_S[jl], lb, nw,
                                     ts=CHUNK, n_seq=bs, steps=1, row_off=npc)
            odd_out.append((sp, ss))
            w_out = w_out_odd[jl].astype(BF16)

        w_r = jnp.zeros((d, LANES), F32).at[:, :N_EXPERTS].set(moe_router_w[l])
        b_r = jnp.full((1, LANES), -jnp.inf, F32).at[0, :N_EXPERTS].set(moe_router_b[l])
        x, hf, gate, lpos, cnt = _post_call(x, u_p, u_s, mod_l, w_out, norm_ffn_w[l][None], w_r, b_r, npc)
        cnt_tiles = jnp.zeros((ntp, LANES), F32).at[:n_tiles].set(cnt[:, 0, :])
        n8, loff, gbase, blk, ends = _plan_call(cnt_tiles, nbp)
        tables = [t[:n_tiles, :N_EXPERTS].reshape(-1) for t in (n8, loff, gbase)]
        blk_expert = blk[:nb, 0]
        n_used = ends[0, N_EXPERTS - 1:N_EXPERTS]
        tails = ends[1:3, :N_EXPERTS].reshape(-1)
        xs = _dispatch_call(tables, tails, n_used, hf, lpos, gate, nb, xs)
        y_sorted = _expert_call(blk_expert, n_used, xs, moe_w_gate_up, moe_b_gate_up[:, :, None, :],
                                moe_w_down, moe_b_down[:, :, None, :], l)
        if l == depth - 1:
            y_final = _combine_call(tables, x, lpos, mod_l, y_sorted, npc, final_norm_w[None])
        else:
            x = _combine_call(tables, x, lpos, mod_l, y_sorted, npc, None)[0]

    def even_states(which):
        cs = jnp.stack([e[which][0] for e in even_out])
        ns = jnp.stack([e[which][1][:, :, 0, :] for e in even_out])
        ms = jnp.stack([e[which][2][:, 0, :he] for e in even_out])
        cv = jnp.stack([e[which][3][:, SUBLANES - (CONV_W - 1):, :] for e in even_out])
        ss = jnp.stack([e[which][4] for e in even_out])
        return cs, ns, ms, cv, ss

    p_c, p_n, p_m, p_cv, p_s = even_states(0)
    s_c, s_n, s_m, s_cv, s_s = even_states(1)
    p_h = jnp.stack([o[0] for o in odd_out])
    s_h = jnp.stack([o[1] for o in odd_out])
    y_prompt = y_final[0].reshape(bp, seq, d)
    y_sample = y_final[1].reshape(bs, dseq, d)
    return (y_prompt, y_sample, p_c, p_n, p_m, p_cv, p_s, p_h, s_c, s_n, s_m, s_cv, s_s, s_h)
```

```python
import functools
import math

import jax
import jax.numpy as jnp
from jax import lax
from jax.experimental import pallas as pl
from jax.experimental.pallas import tpu as pltpu

F32 = jnp.float32
BF16 = jnp.bfloat16

CHUNK = 64
LANES = 128
SUBLANES = 8
D_MODEL = 1024
N_HEADS_EVEN = 4
N_HEADS_ODD = 8
CONV_W = 4
N_EXPERTS = 32
TOP_K = 4
SWIGLU_LIMIT = 7.0
SWIGLU_ALPHA = 1.702
EPS = 1e-6
ROPE_BASE = 10000.0
PAST_LEN = 1024

TOKEN_TILE = 512
INPROJ_TILE = 512
DISPATCH_TILE = 256
SORT_ROWS = -(-(DISPATCH_TILE * TOP_K + N_EXPERTS * (SUBLANES - 1)) // (2 * LANES)) * (2 * LANES)
XS_WIDTH = D_MODEL + LANES
EXPERT_ROWS = 512
VMEM_LIMIT = 56 * 1024 * 1024


def _cparams(sem, vmem=VMEM_LIMIT):
    return pltpu.CompilerParams(dimension_semantics=sem, vmem_limit_bytes=vmem)


def _dot(a, b):
    return jnp.dot(a, b, preferred_element_type=F32)


def _dot_nt(a, b):
    return lax.dot_general(a, b, (((1,), (1,)), ((), ())), preferred_element_type=F32)


def _dot_tn(a, b):
    return lax.dot_general(a, b, (((0,), (0,)), ((), ())), preferred_element_type=F32)


def _split3(x):
    p1 = x.astype(BF16)
    r1 = x - p1.astype(F32)
    p2 = r1.astype(BF16)
    p3 = (r1 - p2.astype(F32)).astype(BF16)
    return p1, p2, p3


def _dot3(a, b):
    ah = a.astype(BF16)
    al = (a - ah.astype(F32)).astype(BF16)
    bh = b.astype(BF16)
    bl = (b - bh.astype(F32)).astype(BF16)
    return _dot(ah, bh) + (_dot(ah, bl) + _dot(al, bh))


def _dot3_tn_fused(a, b):
    n = a.shape[1]
    ah = a.astype(BF16)
    al = (a - ah.astype(F32)).astype(BF16)
    bh = b.astype(BF16)
    bl = (b - bh.astype(F32)).astype(BF16)
    x = _dot_tn(jnp.concatenate([ah, al], axis=1), jnp.concatenate([bh, bl], axis=1))
    return x[:n, :n] + (x[:n, n:] + x[n:, :n])


def _block_diag(a, b):
    z = jnp.zeros_like(a)
    return jnp.concatenate([jnp.concatenate([a, z], axis=1), jnp.concatenate([z, b], axis=1)], axis=0)


def _dot_sel_lhs(sel_bf16, x):
    p1, p2, p3 = _split3(x)
    return _dot(sel_bf16, p1) + (_dot(sel_bf16, p2) + _dot(sel_bf16, p3))


def _dot_sel_nt(sel_bf16, x):
    p1, p2, p3 = _split3(x)
    return _dot_nt(sel_bf16, p1) + (_dot_nt(sel_bf16, p2) + _dot_nt(sel_bf16, p3))


def _dot_sel_rhs(x, sel_bf16):
    p1, p2, p3 = _split3(x)
    return _dot(p1, sel_bf16) + (_dot(p2, sel_bf16) + _dot(p3, sel_bf16))


def _dot_sel_lhs2(sel_bf16, x):
    hi = x.astype(BF16)
    lo = (x - hi.astype(F32)).astype(BF16)
    return _dot(sel_bf16, hi) + _dot(sel_bf16, lo)


def _sigmoid(x):
    return 1.0 / (1.0 + jnp.exp(-x))


def _log_sigmoid(x):
    return jnp.minimum(x, 0.0) - jnp.log1p(jnp.exp(-jnp.abs(x)))


def _rms(x):
    return x * lax.rsqrt(jnp.mean(x * x, axis=-1, keepdims=True) + EPS)


def _chunk_cumsum(x):
    rows, n = x.shape
    vregs = CHUNK // SUBLANES
    x4 = x.reshape(rows // CHUNK, vregs, SUBLANES, n)
    sub = lax.broadcasted_iota(jnp.int32, x4.shape, 2)
    s = x4
    for shift in (1, 2, 4):
        s = s + jnp.where(sub >= shift, pltpu.roll(s, shift, 2), 0.0)
    outs, carry = [], None
    for v in range(vregs):
        cur = s[:, v] if carry is None else s[:, v] + carry
        outs.append(cur)
        carry = jnp.broadcast_to(cur[:, SUBLANES - 1:SUBLANES, :], cur.shape)
    return jnp.stack(outs, axis=1).reshape(rows, n)


def _seq_row(tile_idx, chunks_per_tile, c, n_prompt_chunks):
    return jnp.maximum(tile_idx * chunks_per_tile + c - (n_prompt_chunks - 1), 0)


def _ada_kernel(c_ref, w_ref, b_ref, o_ref):
    c = c_ref[...]
    o_ref[0] = _dot3(c * _sigmoid(c), w_ref[0]) + b_ref[0]


def _ada_call(c_all, w_ada, b_ada):
    depth = w_ada.shape[0]
    nrow = c_all.shape[0]
    ncol = w_ada.shape[2] // D_MODEL
    return pl.pallas_call(
        _ada_kernel,
        grid=(depth, ncol),
        in_specs=[pl.BlockSpec((nrow, D_MODEL), lambda l, j: (0, 0)),
                  pl.BlockSpec((1, D_MODEL, D_MODEL), lambda l, j: (l, 0, j)),
                  pl.BlockSpec((1, 1, D_MODEL), lambda l, j: (l, 0, j))],
        out_specs=pl.BlockSpec((1, nrow, D_MODEL), lambda l, j: (l, 0, j)),
        out_shape=jax.ShapeDtypeStruct((depth, nrow, ncol * D_MODEL), F32),
        compiler_params=_cparams(("arbitrary", "arbitrary")),
        name="ada",
    )(c_all, w_ada, b_ada.reshape(depth, 1, -1))


def _inproj_kernel(*refs, tm, ng, npc, n_hp):
    if n_hp:
        x_ref, mod_ref, nw_ref, w_ref, whp_ref, proj_ref, gates_ref, h_scr = refs
    else:
        x_ref, mod_ref, nw_ref, w_ref, proj_ref, h_scr = refs
    i = pl.program_id(0)
    nch = tm // CHUNK
    for c in range(nch):
        seq = _seq_row(i, nch, c, npc)
        sh = mod_ref[pl.ds(seq, 1), 0:D_MODEL]
        sc = mod_ref[pl.ds(seq, 1), D_MODEL:2 * D_MODEL]
        xc = x_ref[c * CHUNK:(c + 1) * CHUNK, :]
        h_scr[c * CHUNK:(c + 1) * CHUNK, :] = _rms(xc) * nw_ref[...] * (1.0 + sc) + sh
    h = h_scr[...]
    hb = h.astype(BF16)
    for g in range(0, ng - n_hp, 4):
        res = _dot(hb, w_ref[:, g * LANES:(g + 4) * LANES])
        for jj in range(4):
            proj_ref[n_hp + g + jj] = res[:, jj * LANES:(jj + 1) * LANES]
    if n_hp:
        res = _dot3(h, whp_ref[...])
        for jj in range(n_hp):
            proj_ref[jj] = res[:, jj * LANES:(jj + 1) * LANES]
        gates_ref[...] = res[:, n_hp * LANES:]


def _inproj_call(x, mod_l, nw, w_main, w_gate, npc):
    ttot = x.shape[0]
    tm = INPROJ_TILE
    has_gates = w_gate is not None
    n_hp = w_gate.shape[1] // LANES - 1 if has_gates else 0
    ng = w_main.shape[1] // LANES + n_hp
    in_specs = [pl.BlockSpec((tm, D_MODEL), lambda i: (i, 0)),
                pl.BlockSpec(mod_l.shape, lambda i: (0, 0)),
                pl.BlockSpec((1, D_MODEL), lambda i: (0, 0)),
                pl.BlockSpec(w_main.shape, lambda i: (0, 0))]
    out_specs = [pl.BlockSpec((ng, tm, LANES), lambda i: (0, i, 0))]
    out_shape = [jax.ShapeDtypeStruct((ng, ttot, LANES), F32)]
    args = [x, mod_l, nw, w_main]
    if has_gates:
        in_specs.append(pl.BlockSpec(w_gate.shape, lambda i: (0, 0)))
        out_specs.append(pl.BlockSpec((tm, LANES), lambda i: (i, 0)))
        out_shape.append(jax.ShapeDtypeStruct((ttot, LANES), F32))
        args.append(w_gate)
    return pl.pallas_call(
        functools.partial(_inproj_kernel, tm=tm, ng=ng, npc=npc, n_hp=n_hp),
        grid=(ttot // tm,),
        in_specs=in_specs, out_specs=out_specs, out_shape=out_shape,
        scratch_shapes=[pltpu.VMEM((tm, D_MODEL), F32)],
        compiler_params=_cparams(("arbitrary",)),
        name="inproj",
    )(*args)


_G_XM, _G_VA, _G_OA, _G_QB, _G_KB, _G_VB, _G_GB = 0, 4, 8, 12, 16, 20, 24


def _even_scan_kernel(proj_ref, gates_ref, cos_ref, sin_ref,
                      c0_ref, n0_ref, m0_ref, conv0_ref, s0_ref,
                      cw_ref, cb_ref, wqk_ref, gbias_ref, skip_ref, nwa_ref, nwb_ref,
                      u_ref, co_ref, no_ref, mo_ref, convo_ref, so_ref,
                      c_scr, n_scr, m_scr, conv_scr, s_scr,
                      xbuf, xc_scr, q_scr, k_scr, qr_scr, kr_scr,
                      gl_scr, bc_scr, rows_scr, dec_scr, *, ts):
    H = N_HEADS_EVEN
    j = pl.program_id(1)
    nj = pl.num_programs(1)
    nc = ts // CHUNK

    @pl.when(j == 0)
    def _():
        for p in range(H // 2):
            c_scr[p] = _block_diag(c0_ref[0, 2 * p], c0_ref[0, 2 * p + 1])
            s_scr[p] = _block_diag(s0_ref[0, 2 * p], s0_ref[0, 2 * p + 1])
        n_scr[...] = n0_ref[0]
        m_scr[...] = m0_ref[0]
        conv_scr[...] = conv0_ref[0]

    for g in range(H):
        lo, hi = g * LANES, (g + 1) * LANES
        x_g = proj_ref[_G_XM + g]
        xbuf[0:SUBLANES, :] = conv_scr[:, lo:hi]
        xbuf[SUBLANES:SUBLANES + ts, :] = x_g
        acc = cb_ref[:, lo:hi] + cw_ref[CONV_W - 1:CONV_W, lo:hi] * x_g
        for t in range(CONV_W - 1):
            off = SUBLANES - (CONV_W - 1) + t
            acc = acc + cw_ref[t:t + 1, lo:hi] * xbuf[off:off + ts, :]
        conv_scr[:, lo:hi] = xbuf[ts:ts + SUBLANES, :]
        xc = acc * _sigmoid(acc)
        xc_scr[g] = xc
        qk = _dot3(xc, wqk_ref[g])
        q_scr[g] = qk[:, 0:LANES]
        k_scr[g] = qk[:, LANES:2 * LANES] * (LANES ** -0.5)

    cosv = cos_ref[...]
    sinv = sin_ref[...]
    for g in range(H):
        qb = proj_ref[_G_QB + g]
        kb = proj_ref[_G_KB + g]
        qr_scr[g] = qb * cosv + pltpu.roll(qb, LANES // 2, 1) * sinv
        kr_scr[g] = (kb * cosv + pltpu.roll(kb, LANES // 2, 1) * sinv) * (LANES ** -0.5)

    gpre = gates_ref[...] + gbias_ref[...]
    lane = lax.broadcasted_iota(jnp.int32, (ts, LANES), 1)
    gl = jnp.where(lane < H, gpre, _log_sigmoid(gpre))
    gl_scr[...] = gl
    bc = _chunk_cumsum(gl)
    bc_scr[...] = bc
    comb = jnp.where(lane < H, gl, bc)
    even_head = (lane % 2) == 0
    comb_even = jnp.where(even_head, comb, 0.0)
    comb_odd = jnp.where(even_head, 0.0, comb)
    pr = lax.broadcasted_iota(jnp.int32, (SUBLANES, LANES), 0)
    pc = lax.broadcasted_iota(jnp.int32, (SUBLANES, LANES), 1)
    pair_sel = jnp.where((pc // 2 == pr) & (pc < 2 * H), 1.0, 0.0).astype(BF16)
    for c in range(nc):
        cs = slice(c * CHUNK, (c + 1) * CHUNK)
        rows_scr[c] = _dot_sel_nt(pair_sel, jnp.concatenate([comb_even[cs, :], comb_odd[cs, :]], axis=0))

    ti = lax.broadcasted_iota(jnp.int32, (CHUNK, LANES), 0)
    lane2 = lax.broadcasted_iota(jnp.int32, (CHUNK, LANES), 1)
    left = lane2 < CHUNK
    si = lane2 % CHUNK
    tril = ti >= si
    left_wide = lax.broadcasted_iota(jnp.int32, (CHUNK, 2 * LANES), 1) < LANES
    br = lax.broadcasted_iota(jnp.int32, (2 * LANES, 2 * LANES), 0) < LANES
    bcol_blk = lax.broadcasted_iota(jnp.int32, (2 * LANES, 2 * LANES), 1) < LANES
    tcol = lax.broadcasted_iota(jnp.int32, (CHUNK, 1), 0).astype(F32)
    log_gamma = [math.log1p(-2.0 ** (-5 - h)) for h in range(H)]
    for p in range(H // 2):
        lg2 = jnp.where(left, log_gamma[2 * p], log_gamma[2 * p + 1])
        dec_scr[p] = jnp.where(tril, jnp.exp((ti - si).astype(F32) * lg2), 0.0)

    def chunk_body(c, carry):
        r0 = pl.multiple_of(c * CHUNK, CHUNK)
        rows = pl.ds(r0, CHUNK)
        pair_rows = rows_scr[c]
        glc = gl_scr[rows, :]
        bcc = bc_scr[rows, :]
        m_all = m_scr[...]
        c_old = [c_scr[p] for p in range(H // 2)]
        s_old = [s_scr[p] for p in range(H // 2)]
        n_old = [n_scr[h] for h in range(H)]
        c_new, s_new, n_new = [], [], []
        m_next = m_all
        lane_row = lax.broadcasted_iota(jnp.int32, (1, LANES), 1)
        for p in range(H // 2):
            hs = (2 * p, 2 * p + 1)
            q = [q_scr[h, rows, :] for h in hs]
            k = [k_scr[h, rows, :] for h in hs]
            v = [proj_ref[_G_VA + h, rows, :] for h in hs]
            b_col = [bcc[:, H + h:H + h + 1] for h in hs]
            i_col = [glc[:, h:h + 1] for h in hs]
            m_prev = [m_all[:, h:h + 1] for h in hs]
            q2 = jnp.concatenate(q, axis=1).astype(BF16)
            smat = _dot_nt(q2, _block_diag(k[0].astype(BF16), k[1].astype(BF16)))
            dmat = jnp.where(tril, jnp.where(left, b_col[0], b_col[1])
                             - pair_rows[2 + p:3 + p, :] + pair_rows[p:p + 1, :], -jnp.inf)
            a = [jnp.max(jnp.where(left, dmat, -jnp.inf), axis=-1, keepdims=True),
                 jnp.max(jnp.where(left, -jnp.inf, dmat), axis=-1, keepdims=True)]
            inter = [b_col[e] + m_prev[e] for e in range(2)]
            m_t = [jnp.maximum(inter[e], a[e]) for e in range(2)]
            w_inter = [jnp.exp(inter[e] - m_t[e]) for e in range(2)]
            amat = smat * jnp.exp(dmat - jnp.where(left, m_t[0], m_t[1]))
            cst = c_old[p]
            num = (_dot(amat.astype(BF16), _block_diag(v[0].astype(BF16), v[1].astype(BF16)))
                   + jnp.where(left_wide, w_inter[0], w_inter[1]) * _dot(q2, cst.astype(BF16)))
            den = [jnp.sum(jnp.where(left, amat, 0.0), axis=-1, keepdims=True),
                   jnp.sum(jnp.where(left, 0.0, amat), axis=-1, keepdims=True)]
            den = [jnp.maximum(jnp.abs(den[e] + w_inter[e] * jnp.sum(q[e] * n_old[hs[e]], axis=-1, keepdims=True)),
                               jnp.exp(-m_t[e])) for e in range(2)]
            hout = num / jnp.where(left_wide, den[0], den[1])
            decay, upd = [], []
            for e in range(2):
                m_new = m_t[e][CHUNK - 1:CHUNK, :]
                b_last = b_col[e][CHUNK - 1:CHUNK, :]
                kw = k[e] * jnp.exp(b_last - b_col[e] + i_col[e] - m_new)
                decay.append(jnp.exp(b_last + m_prev[e] - m_new))
                upd.append(_dot3_tn_fused(kw, v[e]))
                n_new.append(decay[e] * n_old[hs[e]] + jnp.sum(kw, axis=0, keepdims=True))
                m_next = jnp.where(lane_row == hs[e], m_new, m_next)
            c_new.append(jnp.where(br, decay[0], decay[1]) * cst + _block_diag(upd[0], upd[1]))
            for e in range(2):
                h = hs[e]
                lo, hi = h * LANES, (h + 1) * LANES
                z = _sigmoid(proj_ref[_G_OA + h, rows, :]) * hout[:, e * LANES:(e + 1) * LANES]
                u_ref[rows, lo:hi] = (_rms(z) * nwa_ref[:, lo:hi]
                                      + skip_ref[:, lo:hi] * xc_scr[h, rows, :]).astype(BF16)
        for p in range(H // 2):
            hs = (2 * p, 2 * p + 1)
            lg = [log_gamma[h] for h in hs]
            kr = [kr_scr[h, rows, :] for h in hs]
            vb = [proj_ref[_G_VB + h, rows, :].astype(BF16) for h in hs]
            q2 = jnp.concatenate([qr_scr[h, rows, :] for h in hs], axis=1).astype(BF16)
            amat = _dot_nt(q2, _block_diag(kr[0].astype(BF16), kr[1].astype(BF16))) * dec_scr[p]
            sst = s_old[p]
            o = (_dot(amat.astype(BF16), _block_diag(vb[0], vb[1]))
                 + jnp.exp((tcol + 1.0) * jnp.where(left_wide, lg[0], lg[1])) * _dot(q2, sst.astype(BF16)))
            kws = jnp.concatenate([kr[e] * jnp.exp((CHUNK - 1.0 - tcol) * lg[e]) for e in range(2)], axis=1)
            cross = _dot_tn(kws.astype(BF16), jnp.concatenate(vb, axis=1))
            s_new.append(jnp.where(br, math.exp(CHUNK * lg[0]), math.exp(CHUNK * lg[1])) * sst
                         + jnp.where(br == bcol_blk, cross, 0.0))
            for e in range(2):
                h = hs[e]
                lo, hi = h * LANES, (h + 1) * LANES
                gate = proj_ref[_G_GB + h, rows, :]
                u_ref[rows, D_MODEL // 2 + lo:D_MODEL // 2 + hi] = (
                    _rms(o[:, e * LANES:(e + 1) * LANES]) * nwb_ref[:, lo:hi]
                    * (gate * _sigmoid(gate))).astype(BF16)
        for p in range(H // 2):
            c_scr[p] = c_new[p]
            s_scr[p] = s_new[p]
        for h in range(H):
            n_scr[h] = n_new[h]
        m_scr[...] = m_next
        return carry

    lax.fori_loop(0, nc, chunk_body, 0)

    @pl.when(j == nj - 1)
    def _():
        for p in range(H // 2):
            for e in range(2):
                blk = slice(e * LANES, (e + 1) * LANES)
                co_ref[0, 2 * p + e] = c_scr[p, blk, blk]
                so_ref[0, 2 * p + e] = s_scr[p, blk, blk]
        no_ref[0] = n_scr[...]
        mo_ref[0] = m_scr[...]
        convo_ref[0] = conv_scr[...]


def _even_scan_call(proj, gates, cos_t, sin_t, states, weights, *, ts, n_seq, steps, row_off):
    H = N_HEADS_EVEN
    c0, n0, m0, conv0, s0 = states
    ng = proj.shape[0]
    rows_idx = lambda b, j: (row_off + b * steps + j, 0)
    state_specs = [pl.BlockSpec((1, H, LANES, LANES), lambda b, j: (b, 0, 0, 0)),
                   pl.BlockSpec((1, H, 1, LANES), lambda b, j: (b, 0, 0, 0)),
                   pl.BlockSpec((1, 1, LANES), lambda b, j: (b, 0, 0)),
                   pl.BlockSpec((1, SUBLANES, H * LANES), lambda b, j: (b, 0, 0)),
                   pl.BlockSpec((1, H, LANES, LANES), lambda b, j: (b, 0, 0, 0))]
    in_specs = [pl.BlockSpec((ng, ts, LANES), lambda b, j: (0, row_off + b * steps + j, 0)),
                pl.BlockSpec((ts, LANES), rows_idx),
                pl.BlockSpec((ts, LANES), rows_idx),
                pl.BlockSpec((ts, LANES), rows_idx)] + state_specs
    for w in weights:
        in_specs.append(pl.BlockSpec(w.shape, functools.partial(lambda nd, b, j: (0,) * nd, w.ndim)))
    out_specs = [pl.BlockSpec((ts, D_MODEL), lambda b, j: (b * steps + j, 0))] + state_specs
    out_shape = [jax.ShapeDtypeStruct((n_seq * steps * ts, D_MODEL), BF16),
                 jax.ShapeDtypeStruct(c0.shape, F32), jax.ShapeDtypeStruct(n0.shape, F32),
                 jax.ShapeDtypeStruct(m0.shape, F32), jax.ShapeDtypeStruct(conv0.shape, F32),
                 jax.ShapeDtypeStruct(s0.shape, F32)]
    nc = ts // CHUNK
    pair_state = pltpu.VMEM((H // 2, 2 * LANES, 2 * LANES), F32)
    scratch = [pair_state, pltpu.VMEM((H, 1, LANES), F32),
               pltpu.VMEM((1, LANES), F32), pltpu.VMEM((SUBLANES, H * LANES), F32),
               pair_state,
               pltpu.VMEM((ts + 2 * SUBLANES, LANES), F32),
               pltpu.VMEM((H, ts, LANES), F32), pltpu.VMEM((H, ts, LANES), F32),
               pltpu.VMEM((H, ts, LANES), F32), pltpu.VMEM((H, ts, LANES), F32),
               pltpu.VMEM((H, ts, LANES), F32),
               pltpu.VMEM((ts, LANES), F32), pltpu.VMEM((ts, LANES), F32),
               pltpu.VMEM((nc, SUBLANES, LANES), F32),
               pltpu.VMEM((H // 2, CHUNK, LANES), F32)]
    return pl.pallas_call(
        functools.partial(_even_scan_kernel, ts=ts),
        grid=(n_seq, steps),
        in_specs=in_specs, out_specs=out_specs, out_shape=out_shape,
        scratch_shapes=scratch,
        compiler_params=_cparams(("arbitrary", "arbitrary")),
        name="even_scan",
    )(proj, gates, cos_t, sin_t, c0, n0, m0, conv0, s0, *weights)


_G_Q, _G_F, _G_I, _G_G = 0, 8, 16, 24


def _hgrn_intra_pair(q, k, bcum, ti, si, left, tcol_i):
    amat = jnp.zeros((CHUNK, LANES), F32)
    for b in (32, 16, 8):
        nb2 = CHUNK // (2 * b)
        upper = ((tcol_i // b) % 2) == 1
        ql, kl = [], []
        for e in range(2):
            parts = [jnp.broadcast_to(bcum[e][m * 2 * b + b - 1:m * 2 * b + b, :], (2 * b, LANES))
                     for m in range(nb2)]
            ref = parts[0] if nb2 == 1 else jnp.concatenate(parts, axis=0)
            ql.append(jnp.where(upper, q[e] * jnp.exp(bcum[e] - ref), 0.0).astype(BF16))
            kl.append(jnp.where(upper, 0.0, k[e] * jnp.exp(ref - bcum[e])).astype(BF16))
        al = _dot_nt(jnp.concatenate(ql, axis=1), _block_diag(kl[0], kl[1]))
        amat = amat + jnp.where((ti // (2 * b)) == (si // (2 * b)), al, 0.0)
    nblk = CHUNK // SUBLANES
    b3 = [x.reshape(nblk, SUBLANES, LANES) for x in bcum]
    k3 = [x.reshape(nblk, SUBLANES, LANES) for x in k]
    for jj in range(SUBLANES):
        col = []
        for e in range(2):
            bj = jnp.broadcast_to(b3[e][:, jj:jj + 1, :], (nblk, SUBLANES, LANES)).reshape(CHUNK, LANES)
            kj = jnp.broadcast_to(k3[e][:, jj:jj + 1, :], (nblk, SUBLANES, LANES)).reshape(CHUNK, LANES)
            col.append(jnp.sum(q[e] * kj * jnp.exp(bcum[e] - bj), axis=-1, keepdims=True))
        sel = (si == (ti // SUBLANES) * SUBLANES + jj) & ((ti % SUBLANES) >= jj)
        amat = jnp.where(sel, jnp.where(left, col[0], col[1]), amat)
    return amat


def _odd_scan_kernel(proj_ref, s0_ref, lb_ref, nw_ref, u_ref, so_ref,
                     st_scr, k_scr, bc_scr, *, ts):
    H = N_HEADS_ODD
    j = pl.program_id(1)
    nj = pl.num_programs(1)
    nc = ts // CHUNK

    @pl.when(j == 0)
    def _():
        for p in range(H // 2):
            st_scr[p] = _block_diag(s0_ref[0, 2 * p].T, s0_ref[0, 2 * p + 1].T)

    for h in range(H):
        lo, hi = h * LANES, (h + 1) * LANES
        lbv = lb_ref[:, lo:hi]
        fpre = proj_ref[_G_F + h]
        e = jnp.exp(-jnp.abs(fpre))
        one_e = 1.0 + e
        log_lb = jnp.log(lbv)
        log_rest = jnp.log1p(-lbv) + (jnp.minimum(fpre, 0.0) - jnp.log(one_e))
        logf = jnp.maximum(log_lb, log_rest) + jnp.log(1.0 + jnp.exp(-jnp.abs(log_lb - log_rest)))
        k_scr[h] = (1.0 - lbv) * (jnp.where(fpre >= 0.0, e, 1.0) / one_e)
        bc_scr[h] = _chunk_cumsum(logf)

    ti = lax.broadcasted_iota(jnp.int32, (CHUNK, LANES), 0)
    lane2 = lax.broadcasted_iota(jnp.int32, (CHUNK, LANES), 1)
    left = lane2 < CHUNK
    si = lane2 % CHUNK
    tcol_i = lax.broadcasted_iota(jnp.int32, (CHUNK, 1), 0)
    same_head = ((lax.broadcasted_iota(jnp.int32, (2 * LANES, 2 * LANES), 0) < LANES)
                 == (lax.broadcasted_iota(jnp.int32, (2 * LANES, 2 * LANES), 1) < LANES))

    def chunk_body(c, carry):
        r0 = pl.multiple_of(c * CHUNK, CHUNK)
        rows = pl.ds(r0, CHUNK)
        st_old = [st_scr[p] for p in range(H // 2)]
        st_new = []
        for p in range(H // 2):
            hs = (2 * p, 2 * p + 1)
            q = [proj_ref[_G_Q + h, rows, :] for h in hs]
            k = [k_scr[h, rows, :] for h in hs]
            vb = [proj_ref[_G_I + h, rows, :].astype(BF16) for h in hs]
            bcum = [bc_scr[h, rows, :] for h in hs]
            amat = _hgrn_intra_pair(q, k, bcum, ti, si, left, tcol_i)
            st = st_old[p]
            qg = jnp.concatenate([q[e] * jnp.exp(bcum[e]) for e in range(2)], axis=1)
            o = (_dot(amat.astype(BF16), _block_diag(vb[0], vb[1]))
                 + _dot_nt(qg.astype(BF16), st.astype(BF16)))
            last = [bcum[e][CHUNK - 1:CHUNK, :] for e in range(2)]
            kd = jnp.concatenate([k[e] * jnp.exp(last[e] - bcum[e]) for e in range(2)], axis=1)
            cross = _dot_tn(jnp.concatenate(vb, axis=1), kd.astype(BF16))
            st_new.append(st * jnp.exp(jnp.concatenate(last, axis=1)) + jnp.where(same_head, cross, 0.0))
            for e in range(2):
                h = hs[e]
                lo, hi = h * LANES, (h + 1) * LANES
                u_ref[rows, lo:hi] = (_rms(o[:, e * LANES:(e + 1) * LANES]) * nw_ref[:, lo:hi]
                                      * _sigmoid(proj_ref[_G_G + h, rows, :])).astype(BF16)
        for p in range(H // 2):
            st_scr[p] = st_new[p]
        return carry

    lax.fori_loop(0, nc, chunk_body, 0)

    @pl.when(j == nj - 1)
    def _():
        for p in range(H // 2):
            for e in range(2):
                blk = slice(e * LANES, (e + 1) * LANES)
                so_ref[0, 2 * p + e] = st_scr[p, blk, blk].T


def _odd_scan_call(proj, s0, lb, nw, *, ts, n_seq, steps, row_off):
    H = N_HEADS_ODD
    ng = proj.shape[0]
    st_spec = pl.BlockSpec((1, H, LANES, LANES), lambda b, j: (b, 0, 0, 0))
    return pl.pallas_call(
        functools.partial(_odd_scan_kernel, ts=ts),
        grid=(n_seq, steps),
        in_specs=[pl.BlockSpec((ng, ts, LANES), lambda b, j: (0, row_off + b * steps + j, 0)),
                  st_spec,
                  pl.BlockSpec((1, D_MODEL), lambda b, j: (0, 0)),
                  pl.BlockSpec((1, D_MODEL), lambda b, j: (0, 0))],
        out_specs=[pl.BlockSpec((ts, D_MODEL), lambda b, j: (b * steps + j, 0)), st_spec],
        out_shape=[jax.ShapeDtypeStruct((n_seq * steps * ts, D_MODEL), BF16),
                   jax.ShapeDtypeStruct(s0.shape, F32)],
        scratch_shapes=[pltpu.VMEM((H // 2, 2 * LANES, 2 * LANES), F32),
                        pltpu.VMEM((H, ts, LANES), F32),
                        pltpu.VMEM((H, ts, LANES), F32)],
        compiler_params=_cparams(("arbitrary", "arbitrary")),
        name="odd_scan",
    )(proj, s0, lb, nw)


def _post_kernel(x_ref, up_ref, us_ref, mod_ref, wout_ref, nw_ref, wr_ref, br_ref,
                 xo_ref, hf_ref, gate_ref, lpos_ref, cnt_ref, hf_scr,
                 *, tm, npc, n_prompt_tiles):
    i = pl.program_id(0)
    nch = tm // CHUNK

    u = jnp.where(i < n_prompt_tiles, up_ref[...], us_ref[...])
    y = _dot(u, wout_ref[...])
    for c in range(nch):
        seq = _seq_row(i, nch, c, npc)
        gm = mod_ref[pl.ds(seq, 1), 2 * D_MODEL:3 * D_MODEL]
        shf = mod_ref[pl.ds(seq, 1), 3 * D_MODEL:4 * D_MODEL]
        scf = mod_ref[pl.ds(seq, 1), 4 * D_MODEL:5 * D_MODEL]
        rs = slice(c * CHUNK, (c + 1) * CHUNK)
        xn = x_ref[rs, :] + gm * y[rs, :]
        xo_ref[rs, :] = xn
        hf = _rms(xn) * nw_ref[...] * (1.0 + scf) + shf
        hf_scr[rs, :] = hf
        hf_ref[rs, :] = hf.astype(BF16)

    logits = _dot3(hf_scr[...], wr_ref[...]) + br_ref[...]
    lane_i = lax.broadcasted_iota(jnp.int32, (tm, LANES), 1)
    lane_f = lane_i.astype(F32)
    vals, idxs = [], []
    cur = logits
    for _ in range(TOP_K):
        m = jnp.max(cur, axis=-1, keepdims=True)
        idx = jnp.min(jnp.where(cur == m, lane_f, float(LANES)), axis=-1, keepdims=True)
        vals.append(m)
        idxs.append(idx)
        cur = jnp.where(lane_f == idx, -jnp.inf, cur)
    exps = [jnp.exp(v - vals[0]) for v in vals]
    denom = exps[0] + exps[1] + exps[2] + exps[3]
    onehot = jnp.zeros((tm, LANES), F32)
    for idx in idxs:
        onehot = onehot + jnp.where(lane_f == idx, 1.0, 0.0)
    r = lax.broadcasted_iota(jnp.int32, (tm, tm), 0)
    cidx = lax.broadcasted_iota(jnp.int32, (tm, tm), 1)
    strict = jnp.where((cidx < r) & (cidx // DISPATCH_TILE == r // DISPATCH_TILE), 1.0, 0.0).astype(BF16)
    before = _dot(strict, onehot.astype(BF16))
    er = lax.broadcasted_iota(jnp.int32, (LANES, LANES), 0)
    ec = lax.broadcasted_iota(jnp.int32, (LANES, LANES), 1)
    lower_experts = jnp.where(er < ec, 1.0, 0.0).astype(BF16)
    pos_parts = []
    for s in range(tm // DISPATCH_TILE):
        rs = slice(s * DISPATCH_TILE, (s + 1) * DISPATCH_TILE)
        cnt = jnp.sum(onehot[rs, :], axis=0, keepdims=True)
        cnt_ref[s] = cnt
        n8 = jnp.floor((cnt + (SUBLANES - 1.0)) * (1.0 / SUBLANES)) * float(SUBLANES)
        run_start = _dot(jnp.broadcast_to(n8, (SUBLANES, LANES)).astype(BF16), lower_experts)[0:1, :]
        pos_parts.append(before[rs, :] + run_start)
    posmat = jnp.concatenate(pos_parts, axis=0)
    gate_o = jnp.zeros((tm, LANES), F32)
    lpos_o = jnp.zeros((tm, LANES), F32)
    for kk in range(TOP_K):
        lp = jnp.sum(jnp.where(lane_f == idxs[kk], posmat, 0.0), axis=-1, keepdims=True)
        gate_o = jnp.where(lane_i == kk, exps[kk] / denom, gate_o)
        lpos_o = jnp.where(lane_i == kk, lp, lpos_o)
    gate_ref[...] = gate_o
    lpos_ref[...] = lpos_o


def _post_call(x, u_p, u_s, mod_l, w_out, nw, w_r, b_r, npc):
    ttot = x.shape[0]
    tm = TOKEN_TILE
    npt = u_p.shape[0] // tm
    sub = tm // DISPATCH_TILE
    tile = lambda i: (i, 0)
    const = lambda i: (0, 0)
    return pl.pallas_call(
        functools.partial(_post_kernel, tm=tm, npc=npc, n_prompt_tiles=npt),
        grid=(ttot // tm,),
        in_specs=[pl.BlockSpec((tm, D_MODEL), tile),
                  pl.BlockSpec((tm, D_MODEL), lambda i: (jnp.minimum(i, npt - 1), 0)),
                  pl.BlockSpec((tm, D_MODEL), lambda i: (jnp.maximum(i - npt, 0), 0)),
                  pl.BlockSpec(mod_l.shape, const),
                  pl.BlockSpec(w_out.shape, const),
                  pl.BlockSpec((1, D_MODEL), const),
                  pl.BlockSpec(w_r.shape, const),
                  pl.BlockSpec((1, LANES), const)],
        out_specs=[pl.BlockSpec((tm, D_MODEL), tile), pl.BlockSpec((tm, D_MODEL), tile),
                   pl.BlockSpec((tm, LANES), tile), pl.BlockSpec((tm, LANES), tile),
                   pl.BlockSpec((sub, 1, LANES), lambda i: (i, 0, 0))],
        out_shape=[jax.ShapeDtypeStruct((ttot, D_MODEL), F32), jax.ShapeDtypeStruct((ttot, D_MODEL), BF16),
                   jax.ShapeDtypeStruct((ttot, LANES), F32), jax.ShapeDtypeStruct((ttot, LANES), F32),
                   jax.ShapeDtypeStruct((ttot // DISPATCH_TILE, 1, LANES), F32)],
        scratch_shapes=[pltpu.VMEM((tm, D_MODEL), F32)],
        compiler_params=_cparams(("arbitrary",)),
        name="post",
    )(x, u_p, u_s, mod_l, w_out, nw, w_r, b_r)


def _plan_kernel(cnt_ref, n8_ref, loff_ref, gbase_ref, blk_ref, ends_ref, *, nbp, ntp):
    cnt = cnt_ref[...]
    n8 = jnp.floor((cnt + (SUBLANES - 1.0)) * (1.0 / SUBLANES)) * float(SUBLANES)
    r = lax.broadcasted_iota(jnp.int32, (LANES, LANES), 0)
    c = lax.broadcasted_iota(jnp.int32, (LANES, LANES), 1)
    loff = _dot_sel_rhs(n8, jnp.where(r < c, 1.0, 0.0).astype(BF16))
    gtot = jnp.broadcast_to(jnp.sum(n8, axis=0, keepdims=True), (SUBLANES, LANES))
    nblk = jnp.floor((gtot + (EXPERT_ROWS - 1.0)) * (1.0 / EXPERT_ROWS))
    ends = _dot_sel_rhs(nblk, jnp.where(r <= c, 1.0, 0.0).astype(BF16))
    start_row = (ends[0:1, :] - nblk[0:1, :]) * float(EXPERT_ROWS)
    tr = lax.broadcasted_iota(jnp.int32, (ntp, ntp), 0)
    tc = lax.broadcasted_iota(jnp.int32, (ntp, ntp), 1)
    gbase = start_row + _dot_sel_lhs(jnp.where(tc < tr, 1.0, 0.0).astype(BF16), n8)
    n8_ref[...] = n8.astype(jnp.int32)
    loff_ref[...] = loff.astype(jnp.int32)
    gbase_ref[...] = gbase.astype(jnp.int32)
    bi = lax.broadcasted_iota(jnp.int32, (nbp, LANES), 0).astype(F32)
    li = lax.broadcasted_iota(jnp.int32, (nbp, LANES), 1)
    done = jnp.where((li < N_EXPERTS) & (ends[0:1, :] <= bi), 1.0, 0.0)
    be = jnp.minimum(jnp.sum(done, axis=-1, keepdims=True), N_EXPERTS - 1.0)
    blk_ref[...] = jnp.broadcast_to(be, (nbp, LANES)).astype(jnp.int32)
    r8 = lax.broadcasted_iota(jnp.int32, (SUBLANES, LANES), 0)
    tail_start = start_row + gtot[0:1, :]
    tail_len = nblk[0:1, :] * float(EXPERT_ROWS) - gtot[0:1, :]
    info = jnp.where(r8 == 0, ends, jnp.where(r8 == 1, tail_start, jnp.where(r8 == 2, tail_len, 0.0)))
    ends_ref[...] = info.astype(jnp.int32)


def _plan_call(cnt_tiles, nbp):
    ntp = cnt_tiles.shape[0]
    const = lambda i: (0, 0)
    tbl = jax.ShapeDtypeStruct((ntp, LANES), jnp.int32)
    return pl.pallas_call(
        functools.partial(_plan_kernel, nbp=nbp, ntp=ntp),
        grid=(1,),
        in_specs=[pl.BlockSpec((ntp, LANES), const)],
        out_specs=[pl.BlockSpec((ntp, LANES), const), pl.BlockSpec((ntp, LANES), const),
                   pl.BlockSpec((ntp, LANES), const),
                   pl.BlockSpec((nbp, LANES), const), pl.BlockSpec((SUBLANES, LANES), const)],
        out_shape=[tbl, tbl, tbl,
                   jax.ShapeDtypeStruct((nbp, LANES), jnp.int32),
                   jax.ShapeDtypeStruct((SUBLANES, LANES), jnp.int32)],
        compiler_params=_cparams(("arbitrary",)),
        name="plan",
    )(cnt_tiles)


_GROUP_BITS = tuple(range(3, DISPATCH_TILE.bit_length()))


_TOTAL_BITS = tuple(range(3, SORT_ROWS.bit_length()))


def _group_copies(n8_ref, loff_ref, gbase_ref, tile, make_copy, wait):
    if wait:
        total = lax.fori_loop(0, N_EXPERTS, lambda e, acc: acc + n8_ref[tile * N_EXPERTS + e], 0)
        for bit in _TOTAL_BITS:
            size = 1 << bit

            @pl.when((total & size) != 0)
            def _():
                make_copy(0, 0, size).wait()
        return

    def per_expert(e, carry):
        idx = tile * N_EXPERTS + e
        n = n8_ref[idx]
        off = loff_ref[idx]
        base = gbase_ref[idx]
        for bit in _GROUP_BITS:
            size = 1 << bit

            @pl.when((n & size) != 0)
            def _():
                done = n & ~(2 * size - 1)
                make_copy(pl.multiple_of(off + done, SUBLANES), pl.multiple_of(base + done, SUBLANES), size).start()
        return carry

    lax.fori_loop(0, N_EXPERTS, per_expert, 0)


def _zero_fill(tails_ref, nu_ref, n_blocks, make_zero_copy, wait):
    def finish(cp):
        cp.wait() if wait else cp.start()

    def per_expert(e, carry):
        base = tails_ref[e]
        n = tails_ref[N_EXPERTS + e]
        for bit in range(3, EXPERT_ROWS.bit_length() - 1):
            size = 1 << bit

            @pl.when((n & size) != 0)
            def _():
                done = n & ~(2 * size - 1)
                finish(make_zero_copy(pl.multiple_of(base + done, SUBLANES), size))
        return carry

    lax.fori_loop(0, N_EXPERTS, per_expert, 0)

    def per_block(blk, carry):
        finish(make_zero_copy(pl.multiple_of(blk * EXPERT_ROWS, EXPERT_ROWS), EXPERT_ROWS))
        return carry

    lax.fori_loop(nu_ref[0], n_blocks, per_block, 0)


def _dispatch_kernel(*refs, n_blocks, fresh):
    if fresh:
        n8_ref, loff_ref, gbase_ref, tails_ref, nu_ref, hf_ref, lpos_ref, gate_ref, xs_ref, sbuf, sem = refs
    else:
        (n8_ref, loff_ref, gbase_ref, tails_ref, nu_ref, hf_ref, lpos_ref, gate_ref, _, xs_ref,
         sbuf, sem) = refs
    i = pl.program_id(0)

    if fresh:
        @pl.when(i == 0)
        def _():
            sbuf[1] = jnp.zeros((SORT_ROWS, XS_WIDTH), F32)

            def make_zero_copy(dst_row, size):
                return pltpu.make_async_copy(sbuf.at[1, pl.ds(0, size), :],
                                             xs_ref.at[pl.ds(dst_row, size), :], sem.at[1])

            _zero_fill(tails_ref, nu_ref, n_blocks, make_zero_copy, wait=False)
            _zero_fill(tails_ref, nu_ref, n_blocks, make_zero_copy, wait=True)

    eye8 = jnp.where(lax.broadcasted_iota(jnp.int32, (SUBLANES, LANES), 0)
                     == lax.broadcasted_iota(jnp.int32, (SUBLANES, LANES), 1), 1.0, 0.0).astype(BF16)
    lpos_t = _dot_sel_nt(eye8, lpos_ref[...])
    gate_t = _dot_sel_nt(eye8, gate_ref[...])
    row = lax.broadcasted_iota(jnp.int32, (SORT_ROWS, DISPATCH_TILE), 0).astype(F32)
    perm = jnp.zeros((SORT_ROWS, DISPATCH_TILE), F32)
    wgate = jnp.zeros((SORT_ROWS, DISPATCH_TILE), F32)
    for kk in range(TOP_K):
        hit = row == lpos_t[kk:kk + 1, :]
        perm = jnp.where(hit, 1.0, perm)
        wgate = jnp.where(hit, gate_t[kk:kk + 1, :], wgate)
    slot = i % 2
    sbuf[slot, :, 0:D_MODEL] = _dot(perm.astype(BF16), hf_ref[...].astype(BF16))
    sbuf[slot, :, D_MODEL:XS_WIDTH] = jnp.broadcast_to(jnp.sum(wgate, axis=-1, keepdims=True),
                                                       (SORT_ROWS, LANES))

    def copies(tile, buf_slot, wait):
        def make_copy(src_row, dst_row, size):
            return pltpu.make_async_copy(sbuf.at[buf_slot, pl.ds(src_row, size), :],
                                         xs_ref.at[pl.ds(dst_row, size), :], sem.at[buf_slot])
        _group_copies(n8_ref, loff_ref, gbase_ref, tile, make_copy, wait=wait)

    copies(i, slot, wait=False)

    @pl.when(i > 0)
    def _():
        copies(i - 1, 1 - slot, wait=True)

    @pl.when(i == pl.num_programs(0) - 1)
    def _():
        copies(i, slot, wait=True)


def _dispatch_call(tables, tails, n_used, hf, lpos, gate, n_blocks, xs_prev):
    ttot = hf.shape[0]
    tm = DISPATCH_TILE
    tile = lambda i, *_: (i, 0)
    fresh = xs_prev is None
    in_specs = [pl.BlockSpec((tm, D_MODEL), tile), pl.BlockSpec((tm, LANES), tile),
                pl.BlockSpec((tm, LANES), tile)]
    args = [*tables, tails, n_used, hf, lpos, gate]
    if not fresh:
        in_specs.append(pl.BlockSpec(memory_space=pl.ANY))
        args.append(xs_prev)
    grid_spec = pltpu.PrefetchScalarGridSpec(
        num_scalar_prefetch=5, grid=(ttot // tm,),
        in_specs=in_specs,
        out_specs=pl.BlockSpec(memory_space=pl.ANY),
        scratch_shapes=[pltpu.VMEM((2, SORT_ROWS, XS_WIDTH), F32), pltpu.SemaphoreType.DMA((2,))])
    return pl.pallas_call(
        functools.partial(_dispatch_kernel, n_blocks=n_blocks, fresh=fresh),
        grid_spec=grid_spec,
        out_shape=jax.ShapeDtypeStruct((n_blocks * EXPERT_ROWS, XS_WIDTH), F32),
        input_output_aliases={} if fresh else {len(args) - 1: 0},
        compiler_params=pltpu.CompilerParams(dimension_semantics=("arbitrary",),
                                             vmem_limit_bytes=VMEM_LIMIT, has_side_effects=True),
        name="dispatch",
    )(*args)


def _expert_kernel(be_ref, nu_ref, xs_ref, wgu_ref, bgu_ref, wdn_ref, bdn_ref, y_ref,
                   wgu_bf, wdn_bf):
    b = pl.program_id(0)

    @pl.when(b < nu_ref[0])
    def _():
        prev = be_ref[jnp.maximum(b - 1, 0)]

        @pl.when((b == 0) | (be_ref[b] != prev))
        def _():
            wgu_bf[...] = wgu_ref[0, 0].astype(BF16)
            wdn_bf[...] = wdn_ref[0, 0].astype(BF16)

        gu = _dot(xs_ref[:, 0:D_MODEL].astype(BF16), wgu_bf[...]) + bgu_ref[0, 0]
        g = jnp.minimum(gu[:, :D_MODEL], SWIGLU_LIMIT)
        u = jnp.clip(gu[:, D_MODEL:], -SWIGLU_LIMIT, SWIGLU_LIMIT)
        act = (u + 1.0) * (g * _sigmoid(SWIGLU_ALPHA * g))
        gate = xs_ref[:, D_MODEL:D_MODEL + 1]
        y_ref[...] = (_dot(act.astype(BF16), wdn_bf[...]) + bdn_ref[0, 0]) * gate

    @pl.when(b >= nu_ref[0])
    def _():
        y_ref[...] = jnp.zeros_like(y_ref)


def _expert_call(blk_expert, n_used, xs, w_gu, b_gu, w_dn, b_dn, layer):
    nb = xs.shape[0] // EXPERT_ROWS
    d_ff2 = w_gu.shape[3]
    blk = lambda b, be, nu: (jnp.minimum(b, nu[0] - 1), 0)
    blk_out = lambda b, be, nu: (b, 0)
    exp4 = lambda b, be, nu: (layer, be[jnp.minimum(b, nu[0] - 1)], 0, 0)
    grid_spec = pltpu.PrefetchScalarGridSpec(
        num_scalar_prefetch=2, grid=(nb,),
        in_specs=[pl.BlockSpec((EXPERT_ROWS, XS_WIDTH), blk),
                  pl.BlockSpec((1, 1, D_MODEL, d_ff2), exp4),
                  pl.BlockSpec((1, 1, 1, d_ff2), exp4),
                  pl.BlockSpec((1, 1, D_MODEL, D_MODEL), exp4),
                  pl.BlockSpec((1, 1, 1, D_MODEL), exp4)],
        out_specs=pl.BlockSpec((EXPERT_ROWS, D_MODEL), blk_out),
        scratch_shapes=[pltpu.VMEM((D_MODEL, d_ff2), BF16), pltpu.VMEM((D_MODEL, D_MODEL), BF16)])
    return pl.pallas_call(
        _expert_kernel,
        grid_spec=grid_spec,
        out_shape=jax.ShapeDtypeStruct((xs.shape[0], D_MODEL), F32),
        compiler_params=_cparams(("arbitrary",)),
        name="experts",
    )(blk_expert, n_used, xs, w_gu, b_gu, w_dn, b_dn)


def _combine_kernel(*refs, tm, npc, final):
    if final:
        (n8_ref, loff_ref, gbase_ref, x_ref, lpos_ref, mod_ref, fnw_ref, y_hbm,
         yp_ref, ys_ref, ybuf, sem, ynorm) = refs
    else:
        n8_ref, loff_ref, gbase_ref, x_ref, lpos_ref, mod_ref, y_hbm, xo_ref, ybuf, sem = refs
    i = pl.program_id(0)
    nch = tm // CHUNK

    slot = i % 2

    def copies(tile, buf_slot, wait):
        def make_copy(buf_row, src_row, size):
            return pltpu.make_async_copy(y_hbm.at[pl.ds(src_row, size), :],
                                         ybuf.at[buf_slot, pl.ds(buf_row, size), :], sem.at[buf_slot])
        _group_copies(n8_ref, loff_ref, gbase_ref, tile, make_copy, wait=wait)

    @pl.when(i == 0)
    def _():
        ybuf[...] = jnp.zeros_like(ybuf)
        copies(i, slot, wait=False)

    @pl.when(i + 1 < pl.num_programs(0))
    def _():
        copies(i + 1, 1 - slot, wait=False)

    copies(i, slot, wait=True)

    lpos = lpos_ref[...]
    col = lax.broadcasted_iota(jnp.int32, (tm, SORT_ROWS), 1).astype(F32)
    unperm = jnp.zeros((tm, SORT_ROWS), F32)
    for kk in range(TOP_K):
        unperm = jnp.where(col == lpos[:, kk:kk + 1], 1.0, unperm)
    acc = _dot_sel_lhs2(unperm.astype(BF16), ybuf[slot])
    for c in range(nch):
        seq = _seq_row(i, nch, c, npc)
        gf = mod_ref[pl.ds(seq, 1), 5 * D_MODEL:6 * D_MODEL]
        rs = slice(c * CHUNK, (c + 1) * CHUNK)
        xn = x_ref[rs, :] + gf * acc[rs, :]
        if final:
            ynorm[rs, :] = _rms(xn) * fnw_ref[...]
        else:
            xo_ref[rs, :] = xn
    if final:
        is_prompt = i < (npc * CHUNK) // tm

        @pl.when(is_prompt)
        def _():
            yp_ref[...] = ynorm[...]

        @pl.when(jnp.logical_not(is_prompt))
        def _():
            ys_ref[...] = ynorm[...]


def _combine_call(tables, x, lpos, mod_l, y_sorted, npc, final_w):
    ttot = x.shape[0]
    tm = DISPATCH_TILE
    final = final_w is not None
    npt = (npc * CHUNK) // tm
    tile = lambda i, *_: (i, 0)
    const = lambda i, *_: (0, 0)
    in_specs = [pl.BlockSpec((tm, D_MODEL), tile),
                pl.BlockSpec((tm, LANES), tile),
                pl.BlockSpec(mod_l.shape, const)]
    args = [x, lpos, mod_l]
    if final:
        in_specs.append(pl.BlockSpec((1, D_MODEL), const))
        args.append(final_w)
    in_specs.append(pl.BlockSpec(memory_space=pl.ANY))
    args.append(y_sorted)
    if final:
        out_specs = [pl.BlockSpec((tm, D_MODEL), lambda i, *_: (jnp.minimum(i, npt - 1), 0)),
                     pl.BlockSpec((tm, D_MODEL), lambda i, *_: (jnp.maximum(i - npt, 0), 0))]
        out_shape = [jax.ShapeDtypeStruct((npt * tm, D_MODEL), F32),
                     jax.ShapeDtypeStruct((ttot - npt * tm, D_MODEL), F32)]
    else:
        out_specs = [pl.BlockSpec((tm, D_MODEL), tile)]
        out_shape = [jax.ShapeDtypeStruct((ttot, D_MODEL), F32)]
    scratch = [pltpu.VMEM((2, SORT_ROWS, D_MODEL), F32), pltpu.SemaphoreType.DMA((2,))]
    if final:
        scratch.append(pltpu.VMEM((tm, D_MODEL), F32))
    grid_spec = pltpu.PrefetchScalarGridSpec(
        num_scalar_prefetch=3, grid=(ttot // tm,),
        in_specs=in_specs, out_specs=out_specs, scratch_shapes=scratch)
    return pl.pallas_call(
        functools.partial(_combine_kernel, tm=tm, npc=npc, final=final),
        grid_spec=grid_spec, out_shape=out_shape,
        compiler_params=_cparams(("arbitrary",)),
        name="combine",
    )(*tables, *args)


def kernel(x_prompt, x_sample, c_prompt, c_sample, state_mlstm_C, state_mlstm_n, state_mlstm_m, state_mlstm_conv, state_ret_S, state_hgrn_S, w_ada, b_ada, norm_mix_w, norm_ffn_w, final_norm_w, w_in_even, b_mlstm_i, b_mlstm_f, w_mlstm_conv, b_mlstm_conv, w_mlstm_q, w_mlstm_k, mlstm_skip, mlstm_norm_w, ret_norm_w, w_out_even, w_in_odd, hgrn_lb_logits, hgrn_norm_w, w_out_odd, moe_router_w, moe_router_b, moe_w_gate_up, moe_b_gate_up, moe_w_down, moe_b_down):
    bp, seq, d = x_prompt.shape
    bs, dseq, _ = x_sample.shape
    assert bp == 1 and d == D_MODEL and dseq == CHUNK
    assert seq % TOKEN_TILE == 0 and (bs * dseq) % TOKEN_TILE == 0
    depth = w_ada.shape[0]
    tp, tsmp = seq, bs * dseq
    ttot = tp + tsmp
    npc = tp // CHUNK
    he, ho = N_HEADS_EVEN, N_HEADS_ODD
    da = he * LANES

    x = jnp.concatenate([x_prompt.reshape(tp, d), x_sample.reshape(tsmp, d)], axis=0)
    n_mod_rows = 2 * SUBLANES
    assert 1 + bs <= n_mod_rows
    c_all = jnp.zeros((n_mod_rows, d), F32).at[0:1].set(c_prompt).at[1:1 + bs].set(c_sample)
    mod = _ada_call(c_all, w_ada, b_ada)

    half = LANES // 2
    inv = ROPE_BASE ** (-jnp.arange(half, dtype=F32) / half)
    pos_all = jnp.concatenate([jnp.arange(tp, dtype=F32),
                               jnp.tile(PAST_LEN + jnp.arange(dseq, dtype=F32), bs)])
    ang = pos_all[:, None] * inv[None, :]
    cos_t = jnp.concatenate([jnp.cos(ang), jnp.cos(ang)], axis=-1)
    sin_t = jnp.concatenate([-jnp.sin(ang), jnp.sin(ang)], axis=-1)

    lb_p = jax.nn.softmax(hgrn_lb_logits.astype(F32), axis=0)
    lbs = jnp.cumsum(lb_p, axis=0) - lb_p[0]

    n_tiles = ttot // DISPATCH_TILE
    ntp = -(-n_tiles // LANES) * LANES
    max_rows = ttot * TOP_K + n_tiles * N_EXPERTS * (SUBLANES - 1)
    nb = -(-max_rows // EXPERT_ROWS) + N_EXPERTS
    nbp = -(-nb // SUBLANES) * SUBLANES
    xs = None

    ts_p = TOKEN_TILE
    steps_p = tp // ts_p
    even_out, odd_out = [], []
    y_final = None
    for l in range(depth):
        jl = l // 2
        mod_l = mod[l]
        if l % 2 == 0:
            w_in = w_in_even[jl]
            w_main = jnp.concatenate([w_in[:, da:3 * da], w_in[:, 3 * da + 2 * he:]], axis=1).astype(BF16)
            w_gate = jnp.zeros((d, da + LANES), F32).at[:, :da].set(w_in[:, :da])
            w_gate = w_gate.at[:, da:da + 2 * he].set(w_in[:, 3 * da:3 * da + 2 * he])
            proj, gates = _inproj_call(x, mod_l, norm_mix_w[l][None], w_main, w_gate, npc)
            gbias = jnp.zeros((1, LANES), F32).at[0, :he].set(b_mlstm_i[jl]).at[0, he:2 * he].set(b_mlstm_f[jl])
            weights = [w_mlstm_conv[jl], b_mlstm_conv[jl][None],
                       jnp.concatenate([w_mlstm_q[jl], w_mlstm_k[jl]], axis=-1), gbias,
                       mlstm_skip[jl][None], mlstm_norm_w[jl][None], ret_norm_w[jl][None]]
            zeros_p = (jnp.zeros((1, he, LANES, LANES), F32), jnp.zeros((1, he, 1, LANES), F32),
                       jnp.zeros((1, 1, LANES), F32), jnp.zeros((1, SUBLANES, da), F32),
                       jnp.zeros((1, he, LANES, LANES), F32))
            st_s = (state_mlstm_C[jl], state_mlstm_n[jl][:, :, None, :],
                    jnp.zeros((bs, 1, LANES), F32).at[:, 0, :he].set(state_mlstm_m[jl]),
                    jnp.zeros((bs, SUBLANES, da), F32).at[:, SUBLANES - (CONV_W - 1):].set(state_mlstm_conv[jl]),
                    state_ret_S[jl])
            res_p = _even_scan_call(proj, gates, cos_t, sin_t, zeros_p, weights,
                                    ts=ts_p, n_seq=1, steps=steps_p, row_off=0)
            res_s = _even_scan_call(proj, gates, cos_t, sin_t, st_s, weights,
                                    ts=CHUNK, n_seq=bs, steps=1, row_off=npc)
            u_p, u_s = res_p[0], res_s[0]
            even_out.append((res_p[1:], res_s[1:]))
            w_out = w_out_even[jl].astype(BF16)
        else:
            proj = _inproj_call(x, mod_l, norm_mix_w[l][None], w_in_odd[jl].astype(BF16), None, npc)[0]
            lb = lbs[l][None]
            nw = hgrn_norm_w[jl][None]
            u_p, sp = _odd_scan_call(proj, jnp.zeros((1, ho, LANES, LANES), F32), lb, nw,
                                     ts=ts_p, n_seq=1, steps=steps_p, row_off=0)
            u_s, ss = _odd_scan_call(proj, state_hgrn_S[jl], lb, nw,
                                     ts=CHUNK, n_seq=bs, steps=1, row_off=npc)
            odd_out.append((sp, ss))
            w_out = w_out_odd[jl].astype(BF16)

        w_r = jnp.zeros((d, LANES), F32).at[:, :N_EXPERTS].set(moe_router_w[l])
        b_r = jnp.full((1, LANES), -jnp.inf, F32).at[0, :N_EXPERTS].set(moe_router_b[l])
        x, hf, gate, lpos, cnt = _post_call(x, u_p, u_s, mod_l, w_out, norm_ffn_w[l][None], w_r, b_r, npc)
        cnt_tiles = jnp.zeros((ntp, LANES), F32).at[:n_tiles].set(cnt[:, 0, :])
        n8, loff, gbase, blk, ends = _plan_call(cnt_tiles, nbp)
        tables = [t[:n_tiles, :N_EXPERTS].reshape(-1) for t in (n8, loff, gbase)]
        blk_expert = blk[:nb, 0]
        n_used = ends[0, N_EXPERTS - 1:N_EXPERTS]
        tails = ends[1:3, :N_EXPERTS].reshape(-1)
        xs = _dispatch_call(tables, tails, n_used, hf, lpos, gate, nb, xs)
        y_sorted = _expert_call(blk_expert, n_used, xs, moe_w_gate_up, moe_b_gate_up[:, :, None, :],
                                moe_w_down, moe_b_down[:, :, None, :], l)
        if l == depth - 1:
            y_final = _combine_call(tables, x, lpos, mod_l, y_sorted, npc, final_norm_w[None])
        else:
            x = _combine_call(tables, x, lpos, mod_l, y_sorted, npc, None)[0]

    def even_states(which):
        cs = jnp.stack([e[which][0] for e in even_out])
        ns = jnp.stack([e[which][1][:, :, 0, :] for e in even_out])
        ms = jnp.stack([e[which][2][:, 0, :he] for e in even_out])
        cv = jnp.stack([e[which][3][:, SUBLANES - (CONV_W - 1):, :] for e in even_out])
        ss = jnp.stack([e[which][4] for e in even_out])
        return cs, ns, ms, cv, ss

    p_c, p_n, p_m, p_cv, p_s = even_states(0)
    s_c, s_n, s_m, s_cv, s_s = even_states(1)
    p_h = jnp.stack([o[0] for o in odd_out])
    s_h = jnp.stack([o[1] for o in odd_out])
    y_prompt = y_final[0].reshape(bp, seq, d)
    y_sample = y_final[1].reshape(bs, dseq, d)
    return (y_prompt, y_sample, p_c, p_n, p_m, p_cv, p_s, p_h, s_c, s_n, s_m, s_cv, s_s, s_h)
```

```python
import functools
import math

import jax
import jax.numpy as jnp
from jax import lax
from jax.experimental import pallas as pl
from jax.experimental.pallas import tpu as pltpu

F32 = jnp.float32
BF16 = jnp.bfloat16

CHUNK = 64
LANES = 128
SUBLANES = 8
D_MODEL = 1024
N_HEADS_EVEN = 4
N_HEADS_ODD = 8
CONV_W = 4
N_EXPERTS = 32
TOP_K = 4
SWIGLU_LIMIT = 7.0
SWIGLU_ALPHA = 1.702
EPS = 1e-6
ROPE_BASE = 10000.0
PAST_LEN = 1024

TOKEN_TILE = 512
INPROJ_TILE = 512
DISPATCH_TILE = 256
SORT_ROWS = -(-(DISPATCH_TILE * TOP_K + N_EXPERTS * (SUBLANES - 1)) // (2 * LANES)) * (2 * LANES)
XS_WIDTH = D_MODEL + LANES
EXPERT_ROWS = 512
VMEM_LIMIT = 56 * 1024 * 1024


def _cparams(sem, vmem=VMEM_LIMIT):
    return pltpu.CompilerParams(dimension_semantics=sem, vmem_limit_bytes=vmem)


def _dot(a, b):
    return jnp.dot(a, b, preferred_element_type=F32)


def _dot_nt(a, b):
    return lax.dot_general(a, b, (((1,), (1,)), ((), ())), preferred_element_type=F32)


def _dot_tn(a, b):
    return lax.dot_general(a, b, (((0,), (0,)), ((), ())), preferred_element_type=F32)


def _split3(x):
    p1 = x.astype(BF16)
    r1 = x - p1.astype(F32)
    p2 = r1.astype(BF16)
    p3 = (r1 - p2.astype(F32)).astype(BF16)
    return p1, p2, p3


def _dot3(a, b):
    ah = a.astype(BF16)
    al = (a - ah.astype(F32)).astype(BF16)
    bh = b.astype(BF16)
    bl = (b - bh.astype(F32)).astype(BF16)
    return _dot(ah, bh) + (_dot(ah, bl) + _dot(al, bh))


def _dot3_tn_fused(a, b):
    n = a.shape[1]
    ah = a.astype(BF16)
    al = (a - ah.astype(F32)).astype(BF16)
    bh = b.astype(BF16)
    bl = (b - bh.astype(F32)).astype(BF16)
    x = _dot_tn(jnp.concatenate([ah, al], axis=1), jnp.concatenate([bh, bl], axis=1))
    return x[:n, :n] + (x[:n, n:] + x[n:, :n])


def _block_diag(a, b):
    z = jnp.zeros_like(a)
    return jnp.concatenate([jnp.concatenate([a, z], axis=1), jnp.concatenate([z, b], axis=1)], axis=0)


def _dot_sel_lhs(sel_bf16, x):
    p1, p2, p3 = _split3(x)
    return _dot(sel_bf16, p1) + (_dot(sel_bf16, p2) + _dot(sel_bf16, p3))


def _dot_sel_nt(sel_bf16, x):
    p1, p2, p3 = _split3(x)
    return _dot_nt(sel_bf16, p1) + (_dot_nt(sel_bf16, p2) + _dot_nt(sel_bf16, p3))


def _dot_sel_rhs(x, sel_bf16):
    p1, p2, p3 = _split3(x)
    return _dot(p1, sel_bf16) + (_dot(p2, sel_bf16) + _dot(p3, sel_bf16))


def _dot_sel_lhs2(sel_bf16, x):
    hi = x.astype(BF16)
    lo = (x - hi.astype(F32)).astype(BF16)
    return _dot(sel_bf16, hi) + _dot(sel_bf16, lo)


def _sigmoid(x):
    return 1.0 / (1.0 + jnp.exp(-x))


def _log_sigmoid(x):
    return jnp.minimum(x, 0.0) - jnp.log1p(jnp.exp(-jnp.abs(x)))


def _rms(x):
    return x * lax.rsqrt(jnp.mean(x * x, axis=-1, keepdims=True) + EPS)


def _chunk_cumsum(x):
    rows, n = x.shape
    vregs = CHUNK // SUBLANES
    x4 = x.reshape(rows // CHUNK, vregs, SUBLANES, n)
    sub = lax.broadcasted_iota(jnp.int32, x4.shape, 2)
    s = x4
    for shift in (1, 2, 4):
        s = s + jnp.where(sub >= shift, pltpu.roll(s, shift, 2), 0.0)
    outs, carry = [], None
    for v in range(vregs):
        cur = s[:, v] if carry is None else s[:, v] + carry
        outs.append(cur)
        carry = jnp.broadcast_to(cur[:, SUBLANES - 1:SUBLANES, :], cur.shape)
    return jnp.stack(outs, axis=1).reshape(rows, n)


def _seq_row(tile_idx, chunks_per_tile, c, n_prompt_chunks):
    return jnp.maximum(tile_idx * chunks_per_tile + c - (n_prompt_chunks - 1), 0)


def _ada_kernel(c_ref, w_ref, b_ref, o_ref):
    c = c_ref[...]
    o_ref[0] = _dot3(c * _sigmoid(c), w_ref[0]) + b_ref[0]


def _ada_call(c_all, w_ada, b_ada):
    depth = w_ada.shape[0]
    nrow = c_all.shape[0]
    ncol = w_ada.shape[2] // D_MODEL
    return pl.pallas_call(
        _ada_kernel,
        grid=(depth, ncol),
        in_specs=[pl.BlockSpec((nrow, D_MODEL), lambda l, j: (0, 0)),
                  pl.BlockSpec((1, D_MODEL, D_MODEL), lambda l, j: (l, 0, j)),
                  pl.BlockSpec((1, 1, D_MODEL), lambda l, j: (l, 0, j))],
        out_specs=pl.BlockSpec((1, nrow, D_MODEL), lambda l, j: (l, 0, j)),
        out_shape=jax.ShapeDtypeStruct((depth, nrow, ncol * D_MODEL), F32),
        compiler_params=_cparams(("arbitrary", "arbitrary")),
        name="ada",
    )(c_all, w_ada, b_ada.reshape(depth, 1, -1))


def _inproj_kernel(*refs, tm, ng, npc, n_hp):
    if n_hp:
        x_ref, mod_ref, nw_ref, w_ref, whp_ref, proj_ref, gates_ref, h_scr = refs
    else:
        x_ref, mod_ref, nw_ref, w_ref, proj_ref, h_scr = refs
    i = pl.program_id(0)
    nch = tm // CHUNK
    for c in range(nch):
        seq = _seq_row(i, nch, c, npc)
        sh = mod_ref[pl.ds(seq, 1), 0:D_MODEL]
        sc = mod_ref[pl.ds(seq, 1), D_MODEL:2 * D_MODEL]
        xc = x_ref[c * CHUNK:(c + 1) * CHUNK, :]
        h_scr[c * CHUNK:(c + 1) * CHUNK, :] = _rms(xc) * nw_ref[...] * (1.0 + sc) + sh
    h = h_scr[...]
    hb = h.astype(BF16)
    for g in range(0, ng - n_hp, 4):
        res = _dot(hb, w_ref[:, g * LANES:(g + 4) * LANES])
        for jj in range(4):
            proj_ref[n_hp + g + jj] = res[:, jj * LANES:(jj + 1) * LANES]
    if n_hp:
        res = _dot3(h, whp_ref[...])
        for jj in range(n_hp):
            proj_ref[jj] = res[:, jj * LANES:(jj + 1) * LANES]
        gates_ref[...] = res[:, n_hp * LANES:]


def _inproj_call(x, mod_l, nw, w_main, w_gate, npc):
    ttot = x.shape[0]
    tm = INPROJ_TILE
    has_gates = w_gate is not None
    n_hp = w_gate.shape[1] // LANES - 1 if has_gates else 0
    ng = w_main.shape[1] // LANES + n_hp
    in_specs = [pl.BlockSpec((tm, D_MODEL), lambda i: (i, 0)),
                pl.BlockSpec(mod_l.shape, lambda i: (0, 0)),
                pl.BlockSpec((1, D_MODEL), lambda i: (0, 0)),
                pl.BlockSpec(w_main.shape, lambda i: (0, 0))]
    out_specs = [pl.BlockSpec((ng, tm, LANES), lambda i: (0, i, 0))]
    out_shape = [jax.ShapeDtypeStruct((ng, ttot, LANES), F32)]
    args = [x, mod_l, nw, w_main]
    if has_gates:
        in_specs.append(pl.BlockSpec(w_gate.shape, lambda i: (0, 0)))
        out_specs.append(pl.BlockSpec((tm, LANES), lambda i: (i, 0)))
        out_shape.append(jax.ShapeDtypeStruct((ttot, LANES), F32))
        args.append(w_gate)
    return pl.pallas_call(
        functools.partial(_inproj_kernel, tm=tm, ng=ng, npc=npc, n_hp=n_hp),
        grid=(ttot // tm,),
        in_specs=in_specs, out_specs=out_specs, out_shape=out_shape,
        scratch_shapes=[pltpu.VMEM((tm, D_MODEL), F32)],
        compiler_params=_cparams(("arbitrary",)),
        name="inproj",
    )(*args)


_G_XM, _G_VA, _G_OA, _G_QB, _G_KB, _G_VB, _G_GB = 0, 4, 8, 12, 16, 20, 24


def _even_scan_kernel(proj_ref, gates_ref, cos_ref, sin_ref,
                      c0_ref, n0_ref, m0_ref, conv0_ref, s0_ref,
                      cw_ref, cb_ref, wqk_ref, gbias_ref, skip_ref, nwa_ref, nwb_ref,
                      u_ref, co_ref, no_ref, mo_ref, convo_ref, so_ref,
                      c_scr, n_scr, m_scr, conv_scr, s_scr,
                      xbuf, xc_scr, q_scr, k_scr, qr_scr, kr_scr,
                      gl_scr, bc_scr, rows_scr, dec_scr, *, ts):
    H = N_HEADS_EVEN
    j = pl.program_id(1)
    nj = pl.num_programs(1)
    nc = ts // CHUNK

    @pl.when(j == 0)
    def _():
        for p in range(H // 2):
            c_scr[p] = _block_diag(c0_ref[0, 2 * p], c0_ref[0, 2 * p + 1])
            s_scr[p] = _block_diag(s0_ref[0, 2 * p], s0_ref[0, 2 * p + 1])
        n_scr[...] = n0_ref[0]
        m_scr[...] = m0_ref[0]
        conv_scr[...] = conv0_ref[0]

    for g in range(H):
        lo, hi = g * LANES, (g + 1) * LANES
        x_g = proj_ref[_G_XM + g]
        xbuf[0:SUBLANES, :] = conv_scr[:, lo:hi]
        xbuf[SUBLANES:SUBLANES + ts, :] = x_g
        acc = cb_ref[:, lo:hi] + cw_ref[CONV_W - 1:CONV_W, lo:hi] * x_g
        for t in range(CONV_W - 1):
            off = SUBLANES - (CONV_W - 1) + t
            acc = acc + cw_ref[t:t + 1, lo:hi] * xbuf[off:off + ts, :]
        conv_scr[:, lo:hi] = xbuf[ts:ts + SUBLANES, :]
        xc = acc * _sigmoid(acc)
        xc_scr[g] = xc
        qk = _dot3(xc, wqk_ref[g])
        q_scr[g] = qk[:, 0:LANES]
        k_scr[g] = qk[:, LANES:2 * LANES] * (LANES ** -0.5)

    cosv = cos_ref[...]
    sinv = sin_ref[...]
    for g in range(H):
        qb = proj_ref[_G_QB + g]
        kb = proj_ref[_G_KB + g]
        qr_scr[g] = qb * cosv + pltpu.roll(qb, LANES // 2, 1) * sinv
        kr_scr[g] = (kb * cosv + pltpu.roll(kb, LANES // 2, 1) * sinv) * (LANES ** -0.5)

    gpre = gates_ref[...] + gbias_ref[...]
    lane = lax.broadcasted_iota(jnp.int32, (ts, LANES), 1)
    gl = jnp.where(lane < H, gpre, _log_sigmoid(gpre))
    gl_scr[...] = gl
    bc = _chunk_cumsum(gl)
    bc_scr[...] = bc
    comb = jnp.where(lane < H, gl, bc)
    even_head = (lane % 2) == 0
    comb_even = jnp.where(even_head, comb, 0.0)
    comb_odd = jnp.where(even_head, 0.0, comb)
    pr = lax.broadcasted_iota(jnp.int32, (SUBLANES, LANES), 0)
    pc = lax.broadcasted_iota(jnp.int32, (SUBLANES, LANES), 1)
    pair_sel = jnp.where((pc // 2 == pr) & (pc < 2 * H), 1.0, 0.0).astype(BF16)
    for c in range(nc):
        cs = slice(c * CHUNK, (c + 1) * CHUNK)
        rows_scr[c] = _dot_sel_nt(pair_sel, jnp.concatenate([comb_even[cs, :], comb_odd[cs, :]], axis=0))

    ti = lax.broadcasted_iota(jnp.int32, (CHUNK, LANES), 0)
    lane2 = lax.broadcasted_iota(jnp.int32, (CHUNK, LANES), 1)
    left = lane2 < CHUNK
    si = lane2 % CHUNK
    tril = ti >= si
    left_wide = lax.broadcasted_iota(jnp.int32, (CHUNK, 2 * LANES), 1) < LANES
    br = lax.broadcasted_iota(jnp.int32, (2 * LANES, 2 * LANES), 0) < LANES
    bcol_blk = lax.broadcasted_iota(jnp.int32, (2 * LANES, 2 * LANES), 1) < LANES
    tcol = lax.broadcasted_iota(jnp.int32, (CHUNK, 1), 0).astype(F32)
    log_gamma = [math.log1p(-2.0 ** (-5 - h)) for h in range(H)]
    for p in range(H // 2):
        lg2 = jnp.where(left, log_gamma[2 * p], log_gamma[2 * p + 1])
        dec_scr[p] = jnp.where(tril, jnp.exp((ti - si).astype(F32) * lg2), 0.0)

    def chunk_body(c, carry):
        r0 = pl.multiple_of(c * CHUNK, CHUNK)
        rows = pl.ds(r0, CHUNK)
        pair_rows = rows_scr[c]
        glc = gl_scr[rows, :]
        bcc = bc_scr[rows, :]
        m_all = m_scr[...]
        c_old = [c_scr[p] for p in range(H // 2)]
        s_old = [s_scr[p] for p in range(H // 2)]
        n_old = [n_scr[h] for h in range(H)]
        c_new, s_new, n_new = [], [], []
        m_next = m_all
        lane_row = lax.broadcasted_iota(jnp.int32, (1, LANES), 1)
        for p in range(H // 2):
            hs = (2 * p, 2 * p + 1)
            q = [q_scr[h, rows, :] for h in hs]
            k = [k_scr[h, rows, :] for h in hs]
            v = [proj_ref[_G_VA + h, rows, :] for h in hs]
            b_col = [bcc[:, H + h:H + h + 1] for h in hs]
            i_col = [glc[:, h:h + 1] for h in hs]
            m_prev = [m_all[:, h:h + 1] for h in hs]
            q2 = jnp.concatenate(q, axis=1).astype(BF16)
            smat = _dot_nt(q2, _block_diag(k[0].astype(BF16), k[1].astype(BF16)))
            dmat = jnp.where(tril, jnp.where(left, b_col[0], b_col[1])
                             - pair_rows[2 + p:3 + p, :] + pair_rows[p:p + 1, :], -jnp.inf)
            a = [jnp.max(jnp.where(left, dmat, -jnp.inf), axis=-1, keepdims=True),
                 jnp.max(jnp.where(left, -jnp.inf, dmat), axis=-1, keepdims=True)]
            inter = [b_col[e] + m_prev[e] for e in range(2)]
            m_t = [jnp.maximum(inter[e], a[e]) for e in range(2)]
            w_inter = [jnp.exp(inter[e] - m_t[e]) for e in range(2)]
            amat = smat * jnp.exp(dmat - jnp.where(left, m_t[0], m_t[1]))
            cst = c_old[p]
            num = (_dot(amat.astype(BF16), _block_diag(v[0].astype(BF16), v[1].astype(BF16)))
                   + jnp.where(left_wide, w_inter[0], w_inter[1]) * _dot(q2, cst.astype(BF16)))
            den = [jnp.sum(jnp.where(left, amat, 0.0), axis=-1, keepdims=True),
                   jnp.sum(jnp.where(left, 0.0, amat), axis=-1, keepdims=True)]
            den = [jnp.maximum(jnp.abs(den[e] + w_inter[e] * jnp.sum(q[e] * n_old[hs[e]], axis=-1, keepdims=True)),
                               jnp.exp(-m_t[e])) for e in range(2)]
            hout = num / jnp.where(left_wide, den[0], den[1])
            decay, upd = [], []
            for e in range(2):
                m_new = m_t[e][CHUNK - 1:CHUNK, :]
                b_last = b_col[e][CHUNK - 1:CHUNK, :]
                kw = k[e] * jnp.exp(b_last - b_col[e] + i_col[e] - m_new)
                decay.append(jnp.exp(b_last + m_prev[e] - m_new))
                upd.append(_dot3_tn_fused(kw, v[e]))
                n_new.append(decay[e] * n_old[hs[e]] + jnp.sum(kw, axis=0, keepdims=True))
                m_next = jnp.where(lane_row == hs[e], m_new, m_next)
            c_new.append(jnp.where(br, decay[0], decay[1]) * cst + _block_diag(upd[0], upd[1]))
            for e in range(2):
                h = hs[e]
                lo, hi = h * LANES, (h + 1) * LANES
                z = _sigmoid(proj_ref[_G_OA + h, rows, :]) * hout[:, e * LANES:(e + 1) * LANES]
                u_ref[rows, lo:hi] = (_rms(z) * nwa_ref[:, lo:hi]
                                      + skip_ref[:, lo:hi] * xc_scr[h, rows, :]).astype(BF16)
        for p in range(H // 2):
            hs = (2 * p, 2 * p + 1)
            lg = [log_gamma[h] for h in hs]
            kr = [kr_scr[h, rows, :] for h in hs]
            vb = [proj_ref[_G_VB + h, rows, :].astype(BF16) for h in hs]
            q2 = jnp.concatenate([qr_scr[h, rows, :] for h in hs], axis=1).astype(BF16)
            amat = _dot_nt(q2, _block_diag(kr[0].astype(BF16), kr[1].astype(BF16))) * dec_scr[p]
            sst = s_old[p]
            o = (_dot(amat.astype(BF16), _block_diag(vb[0], vb[1]))
                 + jnp.exp((tcol + 1.0) * jnp.where(left_wide, lg[0], lg[1])) * _dot(q2, sst.astype(BF16)))
            kws = jnp.concatenate([kr[e] * jnp.exp((CHUNK - 1.0 - tcol) * lg[e]) for e in range(2)], axis=1)
            cross = _dot_tn(kws.astype(BF16), jnp.concatenate(vb, axis=1))
            s_new.append(jnp.where(br, math.exp(CHUNK * lg[0]), math.exp(CHUNK * lg[1])) * sst
                         + jnp.where(br == bcol_blk, cross, 0.0))
            for e in range(2):
                h = hs[e]
                lo, hi = h * LANES, (h + 1) * LANES
                gate = proj_ref[_G_GB + h, rows, :]
                u_ref[rows, D_MODEL // 2 + lo:D_MODEL // 2 + hi] = (
                    _rms(o[:, e * LANES:(e + 1) * LANES]) * nwb_ref[:, lo:hi]
                    * (gate * _sigmoid(gate))).astype(BF16)
        for p in range(H // 2):
            c_scr[p] = c_new[p]
            s_scr[p] = s_new[p]
        for h in range(H):
            n_scr[h] = n_new[h]
        m_scr[...] = m_next
        return carry

    lax.fori_loop(0, nc, chunk_body, 0)

    @pl.when(j == nj - 1)
    def _():
        for p in range(H // 2):
            for e in range(2):
                blk = slice(e * LANES, (e + 1) * LANES)
                co_ref[0, 2 * p + e] = c_scr[p, blk, blk]
                so_ref[0, 2 * p + e] = s_scr[p, blk, blk]
        no_ref[0] = n_scr[...]
        mo_ref[0] = m_scr[...]
        convo_ref[0] = conv_scr[...]


def _even_scan_call(proj, gates, cos_t, sin_t, states, weights, *, ts, n_seq, steps, row_off):
    H = N_HEADS_EVEN
    c0, n0, m0, conv0, s0 = states
    ng = proj.shape[0]
    rows_idx = lambda b, j: (row_off + b * steps + j, 0)
    state_specs = [pl.BlockSpec((1, H, LANES, LANES), lambda b, j: (b, 0, 0, 0)),
                   pl.BlockSpec((1, H, 1, LANES), lambda b, j: (b, 0, 0, 0)),
                   pl.BlockSpec((1, 1, LANES), lambda b, j: (b, 0, 0)),
                   pl.BlockSpec((1, SUBLANES, H * LANES), lambda b, j: (b, 0, 0)),
                   pl.BlockSpec((1, H, LANES, LANES), lambda b, j: (b, 0, 0, 0))]
    in_specs = [pl.BlockSpec((ng, ts, LANES), lambda b, j: (0, row_off + b * steps + j, 0)),
                pl.BlockSpec((ts, LANES), rows_idx),
                pl.BlockSpec((ts, LANES), rows_idx),
                pl.BlockSpec((ts, LANES), rows_idx)] + state_specs
    for w in weights:
        in_specs.append(pl.BlockSpec(w.shape, functools.partial(lambda nd, b, j: (0,) * nd, w.ndim)))
    out_specs = [pl.BlockSpec((ts, D_MODEL), lambda b, j: (b * steps + j, 0))] + state_specs
    out_shape = [jax.ShapeDtypeStruct((n_seq * steps * ts, D_MODEL), BF16),
                 jax.ShapeDtypeStruct(c0.shape, F32), jax.ShapeDtypeStruct(n0.shape, F32),
                 jax.ShapeDtypeStruct(m0.shape, F32), jax.ShapeDtypeStruct(conv0.shape, F32),
                 jax.ShapeDtypeStruct(s0.shape, F32)]
    nc = ts // CHUNK
    pair_state = pltpu.VMEM((H // 2, 2 * LANES, 2 * LANES), F32)
    scratch = [pair_state, pltpu.VMEM((H, 1, LANES), F32),
               pltpu.VMEM((1, LANES), F32), pltpu.VMEM((SUBLANES, H * LANES), F32),
               pair_state,
               pltpu.VMEM((ts + 2 * SUBLANES, LANES), F32),
               pltpu.VMEM((H, ts, LANES), F32), pltpu.VMEM((H, ts, LANES), F32),
               pltpu.VMEM((H, ts, LANES), F32), pltpu.VMEM((H, ts, LANES), F32),
               pltpu.VMEM((H, ts, LANES), F32),
               pltpu.VMEM((ts, LANES), F32), pltpu.VMEM((ts, LANES), F32),
               pltpu.VMEM((nc, SUBLANES, LANES), F32),
               pltpu.VMEM((H // 2, CHUNK, LANES), F32)]
    return pl.pallas_call(
        functools.partial(_even_scan_kernel, ts=ts),
        grid=(n_seq, steps),
        in_specs=in_specs, out_specs=out_specs, out_shape=out_shape,
        scratch_shapes=scratch,
        compiler_params=_cparams(("arbitrary", "arbitrary")),
        name="even_scan",
    )(proj, gates, cos_t, sin_t, c0, n0, m0, conv0, s0, *weights)


_G_Q, _G_F, _G_I, _G_G = 0, 8, 16, 24


def _hgrn_intra_pair(q, k, bcum, ti, si, left, tcol_i):
    amat = jnp.zeros((CHUNK, LANES), F32)
    for b in (32, 16, 8):
        nb2 = CHUNK // (2 * b)
        upper = ((tcol_i // b) % 2) == 1
        ql, kl = [], []
        for e in range(2):
            parts = [jnp.broadcast_to(bcum[e][m * 2 * b + b - 1:m * 2 * b + b, :], (2 * b, LANES))
                     for m in range(nb2)]
            ref = parts[0] if nb2 == 1 else jnp.concatenate(parts, axis=0)
            ql.append(jnp.where(upper, q[e] * jnp.exp(bcum[e] - ref), 0.0).astype(BF16))
            kl.append(jnp.where(upper, 0.0, k[e] * jnp.exp(ref - bcum[e])).astype(BF16))
        al = _dot_nt(jnp.concatenate(ql, axis=1), _block_diag(kl[0], kl[1]))
        amat = amat + jnp.where((ti // (2 * b)) == (si // (2 * b)), al, 0.0)
    nblk = CHUNK // SUBLANES
    b3 = [x.reshape(nblk, SUBLANES, LANES) for x in bcum]
    k3 = [x.reshape(nblk, SUBLANES, LANES) for x in k]
    for jj in range(SUBLANES):
        col = []
        for e in range(2):
            bj = jnp.broadcast_to(b3[e][:, jj:jj + 1, :], (nblk, SUBLANES, LANES)).reshape(CHUNK, LANES)
            kj = jnp.broadcast_to(k3[e][:, jj:jj + 1, :], (nblk, SUBLANES, LANES)).reshape(CHUNK, LANES)
            col.append(jnp.sum(q[e] * kj * jnp.exp(bcum[e] - bj), axis=-1, keepdims=True))
        sel = (si == (ti // SUBLANES) * SUBLANES + jj) & ((ti % SUBLANES) >= jj)
        amat = jnp.where(sel, jnp.where(left, col[0], col[1]), amat)
    return amat


def _odd_scan_kernel(proj_ref, s0_ref, lb_ref, nw_ref, u_ref, so_ref,
                     st_scr, k_scr, bc_scr, *, ts):
    H = N_HEADS_ODD
    j = pl.program_id(1)
    nj = pl.num_programs(1)
    nc = ts // CHUNK

    @pl.when(j == 0)
    def _():
        for p in range(H // 2):
            st_scr[p] = _block_diag(s0_ref[0, 2 * p].T, s0_ref[0, 2 * p + 1].T)

    for h in range(H):
        lo, hi = h * LANES, (h + 1) * LANES
        lbv = lb_ref[:, lo:hi]
        fpre = proj_ref[_G_F + h]
        e = jnp.exp(-jnp.abs(fpre))
        one_e = 1.0 + e
        log_lb = jnp.log(lbv)
        log_rest = jnp.log1p(-lbv) + (jnp.minimum(fpre, 0.0) - jnp.log(one_e))
        logf = jnp.maximum(log_lb, log_rest) + jnp.log(1.0 + jnp.exp(-jnp.abs(log_lb - log_rest)))
        k_scr[h] = (1.0 - lbv) * (jnp.where(fpre >= 0.0, e, 1.0) / one_e)
        bc_scr[h] = _chunk_cumsum(logf)

    ti = lax.broadcasted_iota(jnp.int32, (CHUNK, LANES), 0)
    lane2 = lax.broadcasted_iota(jnp.int32, (CHUNK, LANES), 1)
    left = lane2 < CHUNK
    si = lane2 % CHUNK
    tcol_i = lax.broadcasted_iota(jnp.int32, (CHUNK, 1), 0)
    same_head = ((lax.broadcasted_iota(jnp.int32, (2 * LANES, 2 * LANES), 0) < LANES)
                 == (lax.broadcasted_iota(jnp.int32, (2 * LANES, 2 * LANES), 1) < LANES))

    def chunk_body(c, carry):
        r0 = pl.multiple_of(c * CHUNK, CHUNK)
        rows = pl.ds(r0, CHUNK)
        st_old = [st_scr[p] for p in range(H // 2)]
        st_new = []
        for p in range(H // 2):
            hs = (2 * p, 2 * p + 1)
            q = [proj_ref[_G_Q + h, rows, :] for h in hs]
            k = [k_scr[h, rows, :] for h in hs]
            vb = [proj_ref[_G_I + h, rows, :].astype(BF16) for h in hs]
            bcum = [bc_scr[h, rows, :] for h in hs]
            amat = _hgrn_intra_pair(q, k, bcum, ti, si, left, tcol_i)
            st = st_old[p]
            qg = jnp.concatenate([q[e] * jnp.exp(bcum[e]) for e in range(2)], axis=1)
            o = (_dot(amat.astype(BF16), _block_diag(vb[0], vb[1]))
                 + _dot_nt(qg.astype(BF16), st.astype(BF16)))
            last = [bcum[e][CHUNK - 1:CHUNK, :] for e in range(2)]
            kd = jnp.concatenate([k[e] * jnp.exp(last[e] - bcum[e]) for e in range(2)], axis=1)
            cross = _dot_tn(jnp.concatenate(vb, axis=1), kd.astype(BF16))
            st_new.append(st * jnp.exp(jnp.concatenate(last, axis=1)) + jnp.where(same_head, cross, 0.0))
            for e in range(2):
                h = hs[e]
                lo, hi = h * LANES, (h + 1) * LANES
                u_ref[rows, lo:hi] = (_rms(o[:, e * LANES:(e + 1) * LANES]) * nw_ref[:, lo:hi]
                                      * _sigmoid(proj_ref[_G_G + h, rows, :])).astype(BF16)
        for p in range(H // 2):
            st_scr[p] = st_new[p]
        return carry

    lax.fori_loop(0, nc, chunk_body, 0)

    @pl.when(j == nj - 1)
    def _():
        for p in range(H // 2):
            for e in range(2):
                blk = slice(e * LANES, (e + 1) * LANES)
                so_ref[0, 2 * p + e] = st_scr[p, blk, blk].T


def _odd_scan_call(proj, s0, lb, nw, *, ts, n_seq, steps, row_off):
    H = N_HEADS_ODD
    ng = proj.shape[0]
    st_spec = pl.BlockSpec((1, H, LANES, LANES), lambda b, j: (b, 0, 0, 0))
    return pl.pallas_call(
        functools.partial(_odd_scan_kernel, ts=ts),
        grid=(n_seq, steps),
        in_specs=[pl.BlockSpec((ng, ts, LANES), lambda b, j: (0, row_off + b * steps + j, 0)),
                  st_spec,
                  pl.BlockSpec((1, D_MODEL), lambda b, j: (0, 0)),
                  pl.BlockSpec((1, D_MODEL), lambda b, j: (0, 0))],
        out_specs=[pl.BlockSpec((ts, D_MODEL), lambda b, j: (b * steps + j, 0)), st_spec],
        out_shape=[jax.ShapeDtypeStruct((n_seq * steps * ts, D_MODEL), BF16),
                   jax.ShapeDtypeStruct(s0.shape, F32)],
        scratch_shapes=[pltpu.VMEM((H // 2, 2 * LANES, 2 * LANES), F32),
                        pltpu.VMEM((H, ts, LANES), F32),
                        pltpu.VMEM((H, ts, LANES), F32)],
        compiler_params=_cparams(("arbitrary", "arbitrary")),
        name="odd_scan",
    )(proj, s0, lb, nw)


def _post_kernel(x_ref, up_ref, us_ref, mod_ref, wout_ref, nw_ref, wr_ref, br_ref,
                 xo_ref, hf_ref, gate_ref, lpos_ref, cnt_ref, hf_scr,
                 *, tm, npc, n_prompt_tiles):
    i = pl.program_id(0)
    nch = tm // CHUNK

    u = jnp.where(i < n_prompt_tiles, up_ref[...], us_ref[...])
    y = _dot(u, wout_ref[...])
    for c in range(nch):
        seq = _seq_row(i, nch, c, npc)
        gm = mod_ref[pl.ds(seq, 1), 2 * D_MODEL:3 * D_MODEL]
        shf = mod_ref[pl.ds(seq, 1), 3 * D_MODEL:4 * D_MODEL]
        scf = mod_ref[pl.ds(seq, 1), 4 * D_MODEL:5 * D_MODEL]
        rs = slice(c * CHUNK, (c + 1) * CHUNK)
        xn = x_ref[rs, :] + gm * y[rs, :]
        xo_ref[rs, :] = xn
        hf = _rms(xn) * nw_ref[...] * (1.0 + scf) + shf
        hf_scr[rs, :] = hf
        hf_ref[rs, :] = hf.astype(BF16)

    logits = _dot3(hf_scr[...], wr_ref[...]) + br_ref[...]
    lane_i = lax.broadcasted_iota(jnp.int32, (tm, LANES), 1)
    lane_f = lane_i.astype(F32)
    vals, idxs = [], []
    cur = logits
    for _ in range(TOP_K):
        m = jnp.max(cur, axis=-1, keepdims=True)
        idx = jnp.min(jnp.where(cur == m, lane_f, float(LANES)), axis=-1, keepdims=True)
        vals.append(m)
        idxs.append(idx)
        cur = jnp.where(lane_f == idx, -jnp.inf, cur)
    exps = [jnp.exp(v - vals[0]) for v in vals]
    denom = exps[0] + exps[1] + exps[2] + exps[3]
    onehot = jnp.zeros((tm, LANES), F32)
    for idx in idxs:
        onehot = onehot + jnp.where(lane_f == idx, 1.0, 0.0)
    r = lax.broadcasted_iota(jnp.int32, (tm, tm), 0)
    cidx = lax.broadcasted_iota(jnp.int32, (tm, tm), 1)
    strict = jnp.where((cidx < r) & (cidx // DISPATCH_TILE == r // DISPATCH_TILE), 1.0, 0.0).astype(BF16)
    before = _dot(strict, onehot.astype(BF16))
    er = lax.broadcasted_iota(jnp.int32, (LANES, LANES), 0)
    ec = lax.broadcasted_iota(jnp.int32, (LANES, LANES), 1)
    lower_experts = jnp.where(er < ec, 1.0, 0.0).astype(BF16)
    pos_parts = []
    for s in range(tm // DISPATCH_TILE):
        rs = slice(s * DISPATCH_TILE, (s + 1) * DISPATCH_TILE)
        cnt = jnp.sum(onehot[rs, :], axis=0, keepdims=True)
        cnt_ref[s] = cnt
        n8 = jnp.floor((cnt + (SUBLANES - 1.0)) * (1.0 / SUBLANES)) * float(SUBLANES)
        run_start = _dot(jnp.broadcast_to(n8, (SUBLANES, LANES)).astype(BF16), lower_experts)[0:1, :]
        pos_parts.append(before[rs, :] + run_start)
    posmat = jnp.concatenate(pos_parts, axis=0)
    gate_o = jnp.zeros((tm, LANES), F32)
    lpos_o = jnp.zeros((tm, LANES), F32)
    for kk in range(TOP_K):
        lp = jnp.sum(jnp.where(lane_f == idxs[kk], posmat, 0.0), axis=-1, keepdims=True)
        gate_o = jnp.where(lane_i == kk, exps[kk] / denom, gate_o)
        lpos_o = jnp.where(lane_i == kk, lp, lpos_o)
    gate_ref[...] = gate_o
    lpos_ref[...] = lpos_o


def _post_call(x, u_p, u_s, mod_l, w_out, nw, w_r, b_r, npc):
    ttot = x.shape[0]
    tm = TOKEN_TILE
    npt = u_p.shape[0] // tm
    sub = tm // DISPATCH_TILE
    tile = lambda i: (i, 0)
    const = lambda i: (0, 0)
    return pl.pallas_call(
        functools.partial(_post_kernel, tm=tm, npc=npc, n_prompt_tiles=npt),
        grid=(ttot // tm,),
        in_specs=[pl.BlockSpec((tm, D_MODEL), tile),
                  pl.BlockSpec((tm, D_MODEL), lambda i: (jnp.minimum(i, npt - 1), 0)),
                  pl.BlockSpec((tm, D_MODEL), lambda i: (jnp.maximum(i - npt, 0), 0)),
                  pl.BlockSpec(mod_l.shape, const),
                  pl.BlockSpec(w_out.shape, const),
                  pl.BlockSpec((1, D_MODEL), const),
                  pl.BlockSpec(w_r.shape, const),
                  pl.BlockSpec((1, LANES), const)],
        out_specs=[pl.BlockSpec((tm, D_MODEL), tile), pl.BlockSpec((tm, D_MODEL), tile),
                   pl.BlockSpec((tm, LANES), tile), pl.BlockSpec((tm, LANES), tile),
                   pl.BlockSpec((sub, 1, LANES), lambda i: (i, 0, 0))],
        out_shape=[jax.ShapeDtypeStruct((ttot, D_MODEL), F32), jax.ShapeDtypeStruct((ttot, D_MODEL), BF16),
                   jax.ShapeDtypeStruct((ttot, LANES), F32), jax.ShapeDtypeStruct((ttot, LANES), F32),
                   jax.ShapeDtypeStruct((ttot // DISPATCH_TILE, 1, LANES), F32)],
        scratch_shapes=[pltpu.VMEM((tm, D_MODEL), F32)],
        compiler_params=_cparams(("arbitrary",)),
        name="post",
    )(x, u_p, u_s, mod_l, w_out, nw, w_r, b_r)


def _plan_kernel(cnt_ref, n8_ref, loff_ref, gbase_ref, blk_ref, ends_ref, *, nbp, ntp):
    cnt = cnt_ref[...]
    n8 = jnp.floor((cnt + (SUBLANES - 1.0)) * (1.0 / SUBLANES)) * float(SUBLANES)
    r = lax.broadcasted_iota(jnp.int32, (LANES, LANES), 0)
    c = lax.broadcasted_iota(jnp.int32, (LANES, LANES), 1)
    loff = _dot_sel_rhs(n8, jnp.where(r < c, 1.0, 0.0).astype(BF16))
    gtot = jnp.broadcast_to(jnp.sum(n8, axis=0, keepdims=True), (SUBLANES, LANES))
    nblk = jnp.floor((gtot + (EXPERT_ROWS - 1.0)) * (1.0 / EXPERT_ROWS))
    ends = _dot_sel_rhs(nblk, jnp.where(r <= c, 1.0, 0.0).astype(BF16))
    start_row = (ends[0:1, :] - nblk[0:1, :]) * float(EXPERT_ROWS)
    tr = lax.broadcasted_iota(jnp.int32, (ntp, ntp), 0)
    tc = lax.broadcasted_iota(jnp.int32, (ntp, ntp), 1)
    gbase = start_row + _dot_sel_lhs(jnp.where(tc < tr, 1.0, 0.0).astype(BF16), n8)
    n8_ref[...] = n8.astype(jnp.int32)
    loff_ref[...] = loff.astype(jnp.int32)
    gbase_ref[...] = gbase.astype(jnp.int32)
    bi = lax.broadcasted_iota(jnp.int32, (nbp, LANES), 0).astype(F32)
    li = lax.broadcasted_iota(jnp.int32, (nbp, LANES), 1)
    done = jnp.where((li < N_EXPERTS) & (ends[0:1, :] <= bi), 1.0, 0.0)
    be = jnp.minimum(jnp.sum(done, axis=-1, keepdims=True), N_EXPERTS - 1.0)
    blk_ref[...] = jnp.broadcast_to(be, (nbp, LANES)).astype(jnp.int32)
    r8 = lax.broadcasted_iota(jnp.int32, (SUBLANES, LANES), 0)
    tail_start = start_row + gtot[0:1, :]
    tail_len = nblk[0:1, :] * float(EXPERT_ROWS) - gtot[0:1, :]
    info = jnp.where(r8 == 0, ends, jnp.where(r8 == 1, tail_start, jnp.where(r8 == 2, tail_len, 0.0)))
    ends_ref[...] = info.astype(jnp.int32)


def _plan_call(cnt_tiles, nbp):
    ntp = cnt_tiles.shape[0]
    const = lambda i: (0, 0)
    tbl = jax.ShapeDtypeStruct((ntp, LANES), jnp.int32)
    return pl.pallas_call(
        functools.partial(_plan_kernel, nbp=nbp, ntp=ntp),
        grid=(1,),
        in_specs=[pl.BlockSpec((ntp, LANES), const)],
        out_specs=[pl.BlockSpec((ntp, LANES), const), pl.BlockSpec((ntp, LANES), const),
                   pl.BlockSpec((ntp, LANES), const),
                   pl.BlockSpec((nbp, LANES), const), pl.BlockSpec((SUBLANES, LANES), const)],
        out_shape=[tbl, tbl, tbl,
                   jax.ShapeDtypeStruct((nbp, LANES), jnp.int32),
                   jax.ShapeDtypeStruct((SUBLANES, LANES), jnp.int32)],
        compiler_params=_cparams(("arbitrary",)),
        name="plan",
    )(cnt_tiles)


_GROUP_BITS = tuple(range(3, DISPATCH_TILE.bit_length()))


_TOTAL_BITS = tuple(range(3, SORT_ROWS.bit_length()))


def _group_copies(n8_ref, loff_ref, gbase_ref, tile, make_copy, wait):
    if wait:
        total = lax.fori_loop(0, N_EXPERTS, lambda e, acc: acc + n8_ref[tile * N_EXPERTS + e], 0)
        for bit in _TOTAL_BITS:
            size = 1 << bit

            @pl.when((total & size) != 0)
            def _():
                make_copy(0, 0, size).wait()
        return

    def per_expert(e, carry):
        idx = tile * N_EXPERTS + e
        n = n8_ref[idx]
        off = loff_ref[idx]
        base = gbase_ref[idx]
        for bit in _GROUP_BITS:
            size = 1 << bit

            @pl.when((n & size) != 0)
            def _():
                done = n & ~(2 * size - 1)
                make_copy(pl.multiple_of(off + done, SUBLANES), pl.multiple_of(base + done, SUBLANES), size).start()
        return carry

    lax.fori_loop(0, N_EXPERTS, per_expert, 0)


def _zero_fill(tails_ref, nu_ref, n_blocks, make_zero_copy, wait):
    def finish(cp):
        cp.wait() if wait else cp.start()

    def per_expert(e, carry):
        base = tails_ref[e]
        n = tails_ref[N_EXPERTS + e]
        for bit in range(3, EXPERT_ROWS.bit_length() - 1):
            size = 1 << bit

            @pl.when((n & size) != 0)
            def _():
                done = n & ~(2 * size - 1)
                finish(make_zero_copy(pl.multiple_of(base + done, SUBLANES), size))
        return carry

    lax.fori_loop(0, N_EXPERTS, per_expert, 0)

    def per_block(blk, carry):
        finish(make_zero_copy(pl.multiple_of(blk * EXPERT_ROWS, EXPERT_ROWS), EXPERT_ROWS))
        return carry

    lax.fori_loop(nu_ref[0], n_blocks, per_block, 0)


def _dispatch_kernel(*refs, n_blocks, fresh):
    if fresh:
        (n8_ref, loff_ref, gbase_ref, tails_ref, nu_ref, hf_ref, lpos_ref, gate_ref, xs_ref,
         sbuf, sem, zbuf, zsem) = refs
    else:
        (n8_ref, loff_ref, gbase_ref, tails_ref, nu_ref, hf_ref, lpos_ref, gate_ref, _, xs_ref,
         sbuf, sem) = refs
    i = pl.program_id(0)

    if fresh:
        def make_zero_copy(dst_row, size):
            return pltpu.make_async_copy(zbuf.at[pl.ds(0, size), :], xs_ref.at[pl.ds(dst_row, size), :], zsem)

        @pl.when(i == 0)
        def _():
            zbuf[...] = jnp.zeros_like(zbuf)
            _zero_fill(tails_ref, nu_ref, n_blocks, make_zero_copy, wait=False)

    eye8 = jnp.where(lax.broadcasted_iota(jnp.int32, (SUBLANES, LANES), 0)
                     == lax.broadcasted_iota(jnp.int32, (SUBLANES, LANES), 1), 1.0, 0.0).astype(BF16)
    lpos_t = _dot_sel_nt(eye8, lpos_ref[...])
    gate_t = _dot_sel_nt(eye8, gate_ref[...])
    row = lax.broadcasted_iota(jnp.int32, (SORT_ROWS, DISPATCH_TILE), 0).astype(F32)
    perm = jnp.zeros((SORT_ROWS, DISPATCH_TILE), F32)
    wgate = jnp.zeros((SORT_ROWS, DISPATCH_TILE), F32)
    for kk in range(TOP_K):
        hit = row == lpos_t[kk:kk + 1, :]
        perm = jnp.where(hit, 1.0, perm)
        wgate = jnp.where(hit, gate_t[kk:kk + 1, :], wgate)
    slot = i % 2
    sbuf[slot, :, 0:D_MODEL] = _dot(perm.astype(BF16), hf_ref[...].astype(BF16))
    sbuf[slot, :, D_MODEL:XS_WIDTH] = jnp.broadcast_to(jnp.sum(wgate, axis=-1, keepdims=True),
                                                       (SORT_ROWS, LANES))

    def copies(tile, buf_slot, wait):
        def make_copy(src_row, dst_row, size):
            return pltpu.make_async_copy(sbuf.at[buf_slot, pl.ds(src_row, size), :],
                                         xs_ref.at[pl.ds(dst_row, size), :], sem.at[buf_slot])
        _group_copies(n8_ref, loff_ref, gbase_ref, tile, make_copy, wait=wait)

    copies(i, slot, wait=False)

    @pl.when(i > 0)
    def _():
        copies(i - 1, 1 - slot, wait=True)

    @pl.when(i == pl.num_programs(0) - 1)
    def _():
        copies(i, slot, wait=True)
        if fresh:
            _zero_fill(tails_ref, nu_ref, n_blocks, make_zero_copy, wait=True)


def _dispatch_call(tables, tails, n_used, hf, lpos, gate, n_blocks, xs_prev):
    ttot = hf.shape[0]
    tm = DISPATCH_TILE
    tile = lambda i, *_: (i, 0)
    fresh = xs_prev is None
    in_specs = [pl.BlockSpec((tm, D_MODEL), tile), pl.BlockSpec((tm, LANES), tile),
                pl.BlockSpec((tm, LANES), tile)]
    args = [*tables, tails, n_used, hf, lpos, gate]
    if not fresh:
        in_specs.append(pl.BlockSpec(memory_space=pl.ANY))
        args.append(xs_prev)
    scratch = [pltpu.VMEM((2, SORT_ROWS, XS_WIDTH), F32), pltpu.SemaphoreType.DMA((2,))]
    if fresh:
        scratch += [pltpu.VMEM((EXPERT_ROWS, XS_WIDTH), F32), pltpu.SemaphoreType.DMA(())]
    grid_spec = pltpu.PrefetchScalarGridSpec(
        num_scalar_prefetch=5, grid=(ttot // tm,),
        in_specs=in_specs,
        out_specs=pl.BlockSpec(memory_space=pl.ANY),
        scratch_shapes=scratch)
    return pl.pallas_call(
        functools.partial(_dispatch_kernel, n_blocks=n_blocks, fresh=fresh),
        grid_spec=grid_spec,
        out_shape=jax.ShapeDtypeStruct((n_blocks * EXPERT_ROWS, XS_WIDTH), F32),
        input_output_aliases={} if fresh else {len(args) - 1: 0},
        compiler_params=pltpu.CompilerParams(dimension_semantics=("arbitrary",),
                                             vmem_limit_bytes=VMEM_LIMIT, has_side_effects=True),
        name="dispatch",
    )(*args)


def _expert_kernel(be_ref, nu_ref, xs_ref, wgu_ref, bgu_ref, wdn_ref, bdn_ref, y_ref,
                   wgu_bf, wdn_bf):
    b = pl.program_id(0)

    @pl.when(b < nu_ref[0])
    def _():
        prev = be_ref[jnp.maximum(b - 1, 0)]

        @pl.when((b == 0) | (be_ref[b] != prev))
        def _():
            wgu_bf[...] = wgu_ref[0, 0].astype(BF16)
            wdn_bf[...] = wdn_ref[0, 0].astype(BF16)

        gu = _dot(xs_ref[:, 0:D_MODEL].astype(BF16), wgu_bf[...]) + bgu_ref[0, 0]
        g = jnp.minimum(gu[:, :D_MODEL], SWIGLU_LIMIT)
        u = jnp.clip(gu[:, D_MODEL:], -SWIGLU_LIMIT, SWIGLU_LIMIT)
        act = (u + 1.0) * (g * _sigmoid(SWIGLU_ALPHA * g))
        gate = xs_ref[:, D_MODEL:D_MODEL + 1]
        y_ref[...] = (_dot(act.astype(BF16), wdn_bf[...]) + bdn_ref[0, 0]) * gate

    @pl.when(b >= nu_ref[0])
    def _():
        y_ref[...] = jnp.zeros_like(y_ref)


def _expert_call(blk_expert, n_used, xs, w_gu, b_gu, w_dn, b_dn, layer):
    nb = xs.shape[0] // EXPERT_ROWS
    d_ff2 = w_gu.shape[3]
    blk = lambda b, be, nu: (jnp.minimum(b, nu[0] - 1), 0)
    blk_out = lambda b, be, nu: (b, 0)
    exp4 = lambda b, be, nu: (layer, be[jnp.minimum(b, nu[0] - 1)], 0, 0)
    grid_spec = pltpu.PrefetchScalarGridSpec(
        num_scalar_prefetch=2, grid=(nb,),
        in_specs=[pl.BlockSpec((EXPERT_ROWS, XS_WIDTH), blk),
                  pl.BlockSpec((1, 1, D_MODEL, d_ff2), exp4),
                  pl.BlockSpec((1, 1, 1, d_ff2), exp4),
                  pl.BlockSpec((1, 1, D_MODEL, D_MODEL), exp4),
                  pl.BlockSpec((1, 1, 1, D_MODEL), exp4)],
        out_specs=pl.BlockSpec((EXPERT_ROWS, D_MODEL), blk_out),
        scratch_shapes=[pltpu.VMEM((D_MODEL, d_ff2), BF16), pltpu.VMEM((D_MODEL, D_MODEL), BF16)])
    return pl.pallas_call(
        _expert_kernel,
        grid_spec=grid_spec,
        out_shape=jax.ShapeDtypeStruct((xs.shape[0], D_MODEL), F32),
        compiler_params=_cparams(("arbitrary",)),
        name="experts",
    )(blk_expert, n_used, xs, w_gu, b_gu, w_dn, b_dn)


def _combine_kernel(*refs, tm, npc, final):
    if final:
        (n8_ref, loff_ref, gbase_ref, x_ref, lpos_ref, mod_ref, fnw_ref, y_hbm,
         yp_ref, ys_ref, ybuf, sem, ynorm) = refs
    else:
        n8_ref, loff_ref, gbase_ref, x_ref, lpos_ref, mod_ref, y_hbm, xo_ref, ybuf, sem = refs
    i = pl.program_id(0)
    nch = tm // CHUNK

    slot = i % 2

    def copies(tile, buf_slot, wait):
        def make_copy(buf_row, src_row, size):
            return pltpu.make_async_copy(y_hbm.at[pl.ds(src_row, size), :],
                                         ybuf.at[buf_slot, pl.ds(buf_row, size), :], sem.at[buf_slot])
        _group_copies(n8_ref, loff_ref, gbase_ref, tile, make_copy, wait=wait)

    @pl.when(i == 0)
    def _():
        ybuf[...] = jnp.zeros_like(ybuf)
        copies(i, slot, wait=False)

    @pl.when(i + 1 < pl.num_programs(0))
    def _():
        copies(i + 1, 1 - slot, wait=False)

    copies(i, slot, wait=True)

    lpos = lpos_ref[...]
    col = lax.broadcasted_iota(jnp.int32, (tm, SORT_ROWS), 1).astype(F32)
    unperm = jnp.zeros((tm, SORT_ROWS), F32)
    for kk in range(TOP_K):
        unperm = jnp.where(col == lpos[:, kk:kk + 1], 1.0, unperm)
    acc = _dot_sel_lhs2(unperm.astype(BF16), ybuf[slot])
    for c in range(nch):
        seq = _seq_row(i, nch, c, npc)
        gf = mod_ref[pl.ds(seq, 1), 5 * D_MODEL:6 * D_MODEL]
        rs = slice(c * CHUNK, (c + 1) * CHUNK)
        xn = x_ref[rs, :] + gf * acc[rs, :]
        if final:
            ynorm[rs, :] = _rms(xn) * fnw_ref[...]
        else:
            xo_ref[rs, :] = xn
    if final:
        is_prompt = i < (npc * CHUNK) // tm

        @pl.when(is_prompt)
        def _():
            yp_ref[...] = ynorm[...]

        @pl.when(jnp.logical_not(is_prompt))
        def _():
            ys_ref[...] = ynorm[...]


def _combine_call(tables, x, lpos, mod_l, y_sorted, npc, final_w):
    ttot = x.shape[0]
    tm = DISPATCH_TILE
    final = final_w is not None
    npt = (npc * CHUNK) // tm
    tile = lambda i, *_: (i, 0)
    const = lambda i, *_: (0, 0)
    in_specs = [pl.BlockSpec((tm, D_MODEL), tile),
                pl.BlockSpec((tm, LANES), tile),
                pl.BlockSpec(mod_l.shape, const)]
    args = [x, lpos, mod_l]
    if final:
        in_specs.append(pl.BlockSpec((1, D_MODEL), const))
        args.append(final_w)
    in_specs.append(pl.BlockSpec(memory_space=pl.ANY))
    args.append(y_sorted)
    if final:
        out_specs = [pl.BlockSpec((tm, D_MODEL), lambda i, *_: (jnp.minimum(i, npt - 1), 0)),
                     pl.BlockSpec((tm, D_MODEL), lambda i, *_: (jnp.maximum(i - npt, 0), 0))]
        out_shape = [jax.ShapeDtypeStruct((npt * tm, D_MODEL), F32),
                     jax.ShapeDtypeStruct((ttot - npt * tm, D_MODEL), F32)]
    else:
        out_specs = [pl.BlockSpec((tm, D_MODEL), tile)]
        out_shape = [jax.ShapeDtypeStruct((ttot, D_MODEL), F32)]
    scratch = [pltpu.VMEM((2, SORT_ROWS, D_MODEL), F32), pltpu.SemaphoreType.DMA((2,))]
    if final:
        scratch.append(pltpu.VMEM((tm, D_MODEL), F32))
    grid_spec = pltpu.PrefetchScalarGridSpec(
        num_scalar_prefetch=3, grid=(ttot // tm,),
        in_specs=in_specs, out_specs=out_specs, scratch_shapes=scratch)
    return pl.pallas_call(
        functools.partial(_combine_kernel, tm=tm, npc=npc, final=final),
        grid_spec=grid_spec, out_shape=out_shape,
        compiler_params=_cparams(("arbitrary",)),
        name="combine",
    )(*tables, *args)


def kernel(x_prompt, x_sample, c_prompt, c_sample, state_mlstm_C, state_mlstm_n, state_mlstm_m, state_mlstm_conv, state_ret_S, state_hgrn_S, w_ada, b_ada, norm_mix_w, norm_ffn_w, final_norm_w, w_in_even, b_mlstm_i, b_mlstm_f, w_mlstm_conv, b_mlstm_conv, w_mlstm_q, w_mlstm_k, mlstm_skip, mlstm_norm_w, ret_norm_w, w_out_even, w_in_odd, hgrn_lb_logits, hgrn_norm_w, w_out_odd, moe_router_w, moe_router_b, moe_w_gate_up, moe_b_gate_up, moe_w_down, moe_b_down):
    bp, seq, d = x_prompt.shape
    bs, dseq, _ = x_sample.shape
    assert bp == 1 and d == D_MODEL and dseq == CHUNK
    assert seq % TOKEN_TILE == 0 and (bs * dseq) % TOKEN_TILE == 0
    depth = w_ada.shape[0]
    tp, tsmp = seq, bs * dseq
    ttot = tp + tsmp
    npc = tp // CHUNK
    he, ho = N_HEADS_EVEN, N_HEADS_ODD
    da = he * LANES

    x = jnp.concatenate([x_prompt.reshape(tp, d), x_sample.reshape(tsmp, d)], axis=0)
    n_mod_rows = 2 * SUBLANES
    assert 1 + bs <= n_mod_rows
    c_all = jnp.zeros((n_mod_rows, d), F32).at[0:1].set(c_prompt).at[1:1 + bs].set(c_sample)
    mod = _ada_call(c_all, w_ada, b_ada)

    half = LANES // 2
    inv = ROPE_BASE ** (-jnp.arange(half, dtype=F32) / half)
    pos_all = jnp.concatenate([jnp.arange(tp, dtype=F32),
                               jnp.tile(PAST_LEN + jnp.arange(dseq, dtype=F32), bs)])
    ang = pos_all[:, None] * inv[None, :]
    cos_t = jnp.concatenate([jnp.cos(ang), jnp.cos(ang)], axis=-1)
    sin_t = jnp.concatenate([-jnp.sin(ang), jnp.sin(ang)], axis=-1)

    lb_p = jax.nn.softmax(hgrn_lb_logits.astype(F32), axis=0)
    lbs = jnp.cumsum(lb_p, axis=0) - lb_p[0]

    n_tiles = ttot // DISPATCH_TILE
    ntp = -(-n_tiles // LANES) * LANES
    max_rows = ttot * TOP_K + n_tiles * N_EXPERTS * (SUBLANES - 1)
    nb = -(-max_rows // EXPERT_ROWS) + N_EXPERTS
    nbp = -(-nb // SUBLANES) * SUBLANES
    xs = None

    ts_p = TOKEN_TILE
    steps_p = tp // ts_p
    even_out, odd_out = [], []
    y_final = None
    for l in range(depth):
        jl = l // 2
        mod_l = mod[l]
        if l % 2 == 0:
            w_in = w_in_even[jl]
            w_main = jnp.concatenate([w_in[:, da:3 * da], w_in[:, 3 * da + 2 * he:]], axis=1).astype(BF16)
            w_gate = jnp.zeros((d, da + LANES), F32).at[:, :da].set(w_in[:, :da])
            w_gate = w_gate.at[:, da:da + 2 * he].set(w_in[:, 3 * da:3 * da + 2 * he])
            proj, gates = _inproj_call(x, mod_l, norm_mix_w[l][None], w_main, w_gate, npc)
            gbias = jnp.zeros((1, LANES), F32).at[0, :he].set(b_mlstm_i[jl]).at[0, he:2 * he].set(b_mlstm_f[jl])
            weights = [w_mlstm_conv[jl], b_mlstm_conv[jl][None],
                       jnp.concatenate([w_mlstm_q[jl], w_mlstm_k[jl]], axis=-1), gbias,
                       mlstm_skip[jl][None], mlstm_norm_w[jl][None], ret_norm_w[jl][None]]
            zeros_p = (jnp.zeros((1, he, LANES, LANES), F32), jnp.zeros((1, he, 1, LANES), F32),
                       jnp.zeros((1, 1, LANES), F32), jnp.zeros((1, SUBLANES, da), F32),
                       jnp.zeros((1, he, LANES, LANES), F32))
            st_s = (state_mlstm_C[jl], state_mlstm_n[jl][:, :, None, :],
                    jnp.zeros((bs, 1, LANES), F32).at[:, 0, :he].set(state_mlstm_m[jl]),
                    jnp.zeros((bs, SUBLANES, da), F32).at[:, SUBLANES - (CONV_W - 1):].set(state_mlstm_conv[jl]),
                    state_ret_S[jl])
            res_p = _even_scan_call(proj, gates, cos_t, sin_t, zeros_p, weights,
                                    ts=ts_p, n_seq=1, steps=steps_p, row_off=0)
            res_s = _even_scan_call(proj, gates, cos_t, sin_t, st_s, weights,
                                    ts=CHUNK, n_seq=bs, steps=1, row_off=npc)
            u_p, u_s = res_p[0], res_s[0]
            even_out.append((res_p[1:], res_s[1:]))
            w_out = w_out_even[jl].astype(BF16)
        else:
            proj = _inproj_call(x, mod_l, norm_mix_w[l][None], w_in_odd[jl].astype(BF16), None, npc)[0]
            lb = lbs[l][None]
            nw = hgrn_norm_w[jl][None]
            u_p, sp = _odd_scan_call(proj, jnp.zeros((1, ho, LANES, LANES), F32), lb, nw,
                                     ts=ts_p, n_seq=1, steps=steps_p, row_off=0)
            u_s, ss = _odd_scan_call(proj, state_hgrn_S[jl], lb, nw,
                                     ts=CHUNK, n_seq=bs, steps=1, row_off=npc)
            odd_out.append((sp, ss))
            w_out = w_out_odd[jl].astype(BF16)

        w_r = jnp.zeros((d, LANES), F32).at[:, :N_EXPERTS].set(moe_router_w[l])
        b_r = jnp.full((1, LANES), -jnp.inf, F32).at[0, :N_EXPERTS].set(moe_router_b[l])
        x, hf, gate, lpos, cnt = _post_call(x, u_p, u_s, mod_l, w_out, norm_ffn_w[l][None], w_r, b_r, npc)
        cnt_tiles = jnp.zeros((ntp, LANES), F32).at[:n_tiles].set(cnt[:, 0, :])
        n8, loff, gbase, blk, ends = _plan_call(cnt_tiles, nbp)
        tables = [t[:n_tiles, :N_EXPERTS].reshape(-1) for t in (n8, loff, gbase)]
        blk_expert = blk[:nb, 0]
        n_used = ends[0, N_EXPERTS - 1:N_EXPERTS]
        tails = ends[1:3, :N_EXPERTS].reshape(-1)
        xs = _dispatch_call(tables, tails, n_used, hf, lpos, gate, nb, xs)
        y_sorted = _expert_call(blk_expert, n_used, xs, moe_w_gate_up, moe_b_gate_up[:, :, None, :],
                                moe_w_down, moe_b_down[:, :, None, :], l)
        if l == depth - 1:
            y_final = _combine_call(tables, x, lpos, mod_l, y_sorted, npc, final_norm_w[None])
        else:
            x = _combine_call(tables, x, lpos, mod_l, y_sorted, npc, None)[0]

    def even_states(which):
        cs = jnp.stack([e[which][0] for e in even_out])
        ns = jnp.stack([e[which][1][:, :, 0, :] for e in even_out])
        ms = jnp.stack([e[which][2][:, 0, :he] for e in even_out])
        cv = jnp.stack([e[which][3][:, SUBLANES - (CONV_W - 1):, :] for e in even_out])
        ss = jnp.stack([e[which][4] for e in even_out])
        return cs, ns, ms, cv, ss

    p_c, p_n, p_m, p_cv, p_s = even_states(0)
    s_c, s_n, s_m, s_cv, s_s = even_states(1)
    p_h = jnp.stack([o[0] for o in odd_out])
    s_h = jnp.stack([o[1] for o in odd_out])
    y_prompt = y_final[0].reshape(bp, seq, d)
    y_sample = y_final[1].reshape(bs, dseq, d)
    return (y_prompt, y_sample, p_c, p_n, p_m, p_cv, p_s, p_h, s_c, s_n, s_m, s_cv, s_s, s_h)
```

```python
import functools
import math

import jax
import jax.numpy as jnp
from jax import lax
from jax.experimental import pallas as pl
from jax.experimental.pallas import tpu as pltpu

F32 = jnp.float32
BF16 = jnp.bfloat16

CHUNK = 64
LANES = 128
SUBLANES = 8
D_MODEL = 1024
N_HEADS_EVEN = 4
N_HEADS_ODD = 8
CONV_W = 4
N_EXPERTS = 32
TOP_K = 4
SWIGLU_LIMIT = 7.0
SWIGLU_ALPHA = 1.702
EPS = 1e-6
ROPE_BASE = 10000.0
PAST_LEN = 1024

TOKEN_TILE = 512
INPROJ_TILE = 512
DISPATCH_TILE = 256
SORT_ROWS = -(-(DISPATCH_TILE * TOP_K + N_EXPERTS * (SUBLANES - 1)) // (2 * LANES)) * (2 * LANES)
XS_WIDTH = D_MODEL + LANES
EXPERT_ROWS = 512
VMEM_LIMIT = 56 * 1024 * 1024


def _cparams(sem, vmem=VMEM_LIMIT):
    return pltpu.CompilerParams(dimension_semantics=sem, vmem_limit_bytes=vmem)


def _dot(a, b):
    return jnp.dot(a, b, preferred_element_type=F32)


def _dot_nt(a, b):
    return lax.dot_general(a, b, (((1,), (1,)), ((), ())), preferred_element_type=F32)


def _dot_tn(a, b):
    return lax.dot_general(a, b, (((0,), (0,)), ((), ())), preferred_element_type=F32)


def _split3(x):
    p1 = x.astype(BF16)
    r1 = x - p1.astype(F32)
    p2 = r1.astype(BF16)
    p3 = (r1 - p2.astype(F32)).astype(BF16)
    return p1, p2, p3


def _dot3(a, b):
    ah = a.astype(BF16)
    al = (a - ah.astype(F32)).astype(BF16)
    bh = b.astype(BF16)
    bl = (b - bh.astype(F32)).astype(BF16)
    return _dot(ah, bh) + (_dot(ah, bl) + _dot(al, bh))


def _dot3_tn_fused(a, b):
    n = a.shape[1]
    ah = a.astype(BF16)
    al = (a - ah.astype(F32)).astype(BF16)
    bh = b.astype(BF16)
    bl = (b - bh.astype(F32)).astype(BF16)
    x = _dot_tn(jnp.concatenate([ah, al], axis=1), jnp.concatenate([bh, bl], axis=1))
    return x[:n, :n] + (x[:n, n:] + x[n:, :n])


def _block_diag(a, b):
    z = jnp.zeros_like(a)
    return jnp.concatenate([jnp.concatenate([a, z], axis=1), jnp.concatenate([z, b], axis=1)], axis=0)


def _dot_sel_lhs(sel_bf16, x):
    p1, p2, p3 = _split3(x)
    return _dot(sel_bf16, p1) + (_dot(sel_bf16, p2) + _dot(sel_bf16, p3))


def _dot_sel_nt(sel_bf16, x):
    p1, p2, p3 = _split3(x)
    return _dot_nt(sel_bf16, p1) + (_dot_nt(sel_bf16, p2) + _dot_nt(sel_bf16, p3))


def _dot_sel_rhs(x, sel_bf16):
    p1, p2, p3 = _split3(x)
    return _dot(p1, sel_bf16) + (_dot(p2, sel_bf16) + _dot(p3, sel_bf16))


def _dot_sel_lhs2(sel_bf16, x):
    hi = x.astype(BF16)
    lo = (x - hi.astype(F32)).astype(BF16)
    return _dot(sel_bf16, hi) + _dot(sel_bf16, lo)


def _sigmoid(x):
    return 1.0 / (1.0 + jnp.exp(-x))


def _log_sigmoid(x):
    return jnp.minimum(x, 0.0) - jnp.log1p(jnp.exp(-jnp.abs(x)))


def _rms(x):
    return x * lax.rsqrt(jnp.mean(x * x, axis=-1, keepdims=True) + EPS)


def _chunk_cumsum(x):
    rows, n = x.shape
    vregs = CHUNK // SUBLANES
    x4 = x.reshape(rows // CHUNK, vregs, SUBLANES, n)
    sub = lax.broadcasted_iota(jnp.int32, x4.shape, 2)
    s = x4
    for shift in (1, 2, 4):
        s = s + jnp.where(sub >= shift, pltpu.roll(s, shift, 2), 0.0)
    outs, carry = [], None
    for v in range(vregs):
        cur = s[:, v] if carry is None else s[:, v] + carry
        outs.append(cur)
        carry = jnp.broadcast_to(cur[:, SUBLANES - 1:SUBLANES, :], cur.shape)
    return jnp.stack(outs, axis=1).reshape(rows, n)


def _seq_row(tile_idx, chunks_per_tile, c, n_prompt_chunks):
    return jnp.maximum(tile_idx * chunks_per_tile + c - (n_prompt_chunks - 1), 0)


def _ada_kernel(c_ref, w_ref, b_ref, o_ref):
    c = c_ref[...]
    o_ref[0] = _dot3(c * _sigmoid(c), w_ref[0]) + b_ref[0]


def _ada_call(c_all, w_ada, b_ada):
    depth = w_ada.shape[0]
    nrow = c_all.shape[0]
    ncol = w_ada.shape[2] // D_MODEL
    return pl.pallas_call(
        _ada_kernel,
        grid=(depth, ncol),
        in_specs=[pl.BlockSpec((nrow, D_MODEL), lambda l, j: (0, 0)),
                  pl.BlockSpec((1, D_MODEL, D_MODEL), lambda l, j: (l, 0, j)),
                  pl.BlockSpec((1, 1, D_MODEL), lambda l, j: (l, 0, j))],
        out_specs=pl.BlockSpec((1, nrow, D_MODEL), lambda l, j: (l, 0, j)),
        out_shape=jax.ShapeDtypeStruct((depth, nrow, ncol * D_MODEL), F32),
        compiler_params=_cparams(("arbitrary", "arbitrary")),
        name="ada",
    )(c_all, w_ada, b_ada.reshape(depth, 1, -1))


def _inproj_kernel(*refs, tm, ng, npc, n_hp):
    if n_hp:
        x_ref, mod_ref, nw_ref, w_ref, whp_ref, proj_ref, gates_ref, h_scr = refs
    else:
        x_ref, mod_ref, nw_ref, w_ref, proj_ref, h_scr = refs
    i = pl.program_id(0)
    nch = tm // CHUNK
    for c in range(nch):
        seq = _seq_row(i, nch, c, npc)
        sh = mod_ref[pl.ds(seq, 1), 0:D_MODEL]
        sc = mod_ref[pl.ds(seq, 1), D_MODEL:2 * D_MODEL]
        xc = x_ref[c * CHUNK:(c + 1) * CHUNK, :]
        h_scr[c * CHUNK:(c + 1) * CHUNK, :] = _rms(xc) * nw_ref[...] * (1.0 + sc) + sh
    h = h_scr[...]
    hb = h.astype(BF16)
    for g in range(0, ng - n_hp, 4):
        res = _dot(hb, w_ref[:, g * LANES:(g + 4) * LANES])
        for jj in range(4):
            proj_ref[n_hp + g + jj] = res[:, jj * LANES:(jj + 1) * LANES]
    if n_hp:
        res = _dot3(h, whp_ref[...])
        for jj in range(n_hp):
            proj_ref[jj] = res[:, jj * LANES:(jj + 1) * LANES]
        gates_ref[...] = res[:, n_hp * LANES:]


def _inproj_call(x, mod_l, nw, w_main, w_gate, npc):
    ttot = x.shape[0]
    tm = INPROJ_TILE
    has_gates = w_gate is not None
    n_hp = w_gate.shape[1] // LANES - 1 if has_gates else 0
    ng = w_main.shape[1] // LANES + n_hp
    in_specs = [pl.BlockSpec((tm, D_MODEL), lambda i: (i, 0)),
                pl.BlockSpec(mod_l.shape, lambda i: (0, 0)),
                pl.BlockSpec((1, D_MODEL), lambda i: (0, 0)),
                pl.BlockSpec(w_main.shape, lambda i: (0, 0))]
    out_specs = [pl.BlockSpec((ng, tm, LANES), lambda i: (0, i, 0))]
    out_shape = [jax.ShapeDtypeStruct((ng, ttot, LANES), F32)]
    args = [x, mod_l, nw, w_main]
    if has_gates:
        in_specs.append(pl.BlockSpec(w_gate.shape, lambda i: (0, 0)))
        out_specs.append(pl.BlockSpec((tm, LANES), lambda i: (i, 0)))
        out_shape.append(jax.ShapeDtypeStruct((ttot, LANES), F32))
        args.append(w_gate)
    return pl.pallas_call(
        functools.partial(_inproj_kernel, tm=tm, ng=ng, npc=npc, n_hp=n_hp),
        grid=(ttot // tm,),
        in_specs=in_specs, out_specs=out_specs, out_shape=out_shape,
        scratch_shapes=[pltpu.VMEM((tm, D_MODEL), F32)],
        compiler_params=_cparams(("arbitrary",)),
        name="inproj",
    )(*args)


_G_XM, _G_VA, _G_OA, _G_QB, _G_KB, _G_VB, _G_GB = 0, 4, 8, 12, 16, 20, 24


def _even_scan_kernel(proj_ref, gates_ref, cos_ref, sin_ref,
                      c0_ref, n0_ref, m0_ref, conv0_ref, s0_ref,
                      cw_ref, cb_ref, wqk_ref, gbias_ref, skip_ref, nwa_ref, nwb_ref,
                      u_ref, co_ref, no_ref, mo_ref, convo_ref, so_ref,
                      c_scr, n_scr, m_scr, conv_scr, s_scr,
                      xbuf, xc_scr, q_scr, k_scr, qr_scr, kr_scr,
                      gl_scr, bc_scr, rows_scr, dec_scr, *, ts):
    H = N_HEADS_EVEN
    j = pl.program_id(1)
    nj = pl.num_programs(1)
    nc = ts // CHUNK

    @pl.when(j == 0)
    def _():
        for p in range(H // 2):
            c_scr[p] = _block_diag(c0_ref[0, 2 * p], c0_ref[0, 2 * p + 1])
            s_scr[p] = _block_diag(s0_ref[0, 2 * p], s0_ref[0, 2 * p + 1])
        n_scr[...] = n0_ref[0]
        m_scr[...] = m0_ref[0]
        conv_scr[...] = conv0_ref[0]

    for g in range(H):
        lo, hi = g * LANES, (g + 1) * LANES
        x_g = proj_ref[_G_XM + g]
        xbuf[0:SUBLANES, :] = conv_scr[:, lo:hi]
        xbuf[SUBLANES:SUBLANES + ts, :] = x_g
        acc = cb_ref[:, lo:hi] + cw_ref[CONV_W - 1:CONV_W, lo:hi] * x_g
        for t in range(CONV_W - 1):
            off = SUBLANES - (CONV_W - 1) + t
            acc = acc + cw_ref[t:t + 1, lo:hi] * xbuf[off:off + ts, :]
        conv_scr[:, lo:hi] = xbuf[ts:ts + SUBLANES, :]
        xc = acc * _sigmoid(acc)
        xc_scr[g] = xc
        qk = _dot3(xc, wqk_ref[g])
        q_scr[g] = qk[:, 0:LANES]
        k_scr[g] = qk[:, LANES:2 * LANES] * (LANES ** -0.5)

    cosv = cos_ref[...]
    sinv = sin_ref[...]
    for g in range(H):
        qb = proj_ref[_G_QB + g]
        kb = proj_ref[_G_KB + g]
        qr_scr[g] = qb * cosv + pltpu.roll(qb, LANES // 2, 1) * sinv
        kr_scr[g] = (kb * cosv + pltpu.roll(kb, LANES // 2, 1) * sinv) * (LANES ** -0.5)

    gpre = gates_ref[...] + gbias_ref[...]
    lane = lax.broadcasted_iota(jnp.int32, (ts, LANES), 1)
    gl = jnp.where(lane < H, gpre, _log_sigmoid(gpre))
    gl_scr[...] = gl
    bc = _chunk_cumsum(gl)
    bc_scr[...] = bc
    comb = jnp.where(lane < H, gl, bc)
    even_head = (lane % 2) == 0
    comb_even = jnp.where(even_head, comb, 0.0)
    comb_odd = jnp.where(even_head, 0.0, comb)
    pr = lax.broadcasted_iota(jnp.int32, (SUBLANES, LANES), 0)
    pc = lax.broadcasted_iota(jnp.int32, (SUBLANES, LANES), 1)
    pair_sel = jnp.where((pc // 2 == pr) & (pc < 2 * H), 1.0, 0.0).astype(BF16)
    for c in range(nc):
        cs = slice(c * CHUNK, (c + 1) * CHUNK)
        rows_scr[c] = _dot_sel_nt(pair_sel, jnp.concatenate([comb_even[cs, :], comb_odd[cs, :]], axis=0))

    ti = lax.broadcasted_iota(jnp.int32, (CHUNK, LANES), 0)
    lane2 = lax.broadcasted_iota(jnp.int32, (CHUNK, LANES), 1)
    left = lane2 < CHUNK
    si = lane2 % CHUNK
    tril = ti >= si
    left_wide = lax.broadcasted_iota(jnp.int32, (CHUNK, 2 * LANES), 1) < LANES
    br = lax.broadcasted_iota(jnp.int32, (2 * LANES, 2 * LANES), 0) < LANES
    bcol_blk = lax.broadcasted_iota(jnp.int32, (2 * LANES, 2 * LANES), 1) < LANES
    tcol = lax.broadcasted_iota(jnp.int32, (CHUNK, 1), 0).astype(F32)
    log_gamma = [math.log1p(-2.0 ** (-5 - h)) for h in range(H)]
    for p in range(H // 2):
        lg2 = jnp.where(left, log_gamma[2 * p], log_gamma[2 * p + 1])
        dec_scr[p] = jnp.where(tril, jnp.exp((ti - si).astype(F32) * lg2), 0.0)

    def chunk_body(c, carry):
        r0 = pl.multiple_of(c * CHUNK, CHUNK)
        rows = pl.ds(r0, CHUNK)
        pair_rows = rows_scr[c]
        glc = gl_scr[rows, :]
        bcc = bc_scr[rows, :]
        m_all = m_scr[...]
        c_old = [c_scr[p] for p in range(H // 2)]
        s_old = [s_scr[p] for p in range(H // 2)]
        n_old = [n_scr[h] for h in range(H)]
        c_new, s_new, n_new = [], [], []
        m_next = m_all
        lane_row = lax.broadcasted_iota(jnp.int32, (1, LANES), 1)
        for p in range(H // 2):
            hs = (2 * p, 2 * p + 1)
            q = [q_scr[h, rows, :] for h in hs]
            k = [k_scr[h, rows, :] for h in hs]
            v = [proj_ref[_G_VA + h, rows, :] for h in hs]
            b_col = [bcc[:, H + h:H + h + 1] for h in hs]
            i_col = [glc[:, h:h + 1] for h in hs]
            m_prev = [m_all[:, h:h + 1] for h in hs]
            q2 = jnp.concatenate(q, axis=1).astype(BF16)
            smat = _dot_nt(q2, _block_diag(k[0].astype(BF16), k[1].astype(BF16)))
            dmat = jnp.where(tril, jnp.where(left, b_col[0], b_col[1])
                             - pair_rows[2 + p:3 + p, :] + pair_rows[p:p + 1, :], -jnp.inf)
            a = [jnp.max(jnp.where(left, dmat, -jnp.inf), axis=-1, keepdims=True),
                 jnp.max(jnp.where(left, -jnp.inf, dmat), axis=-1, keepdims=True)]
            inter = [b_col[e] + m_prev[e] for e in range(2)]
            m_t = [jnp.maximum(inter[e], a[e]) for e in range(2)]
            w_inter = [jnp.exp(inter[e] - m_t[e]) for e in range(2)]
            amat = smat * jnp.exp(dmat - jnp.where(left, m_t[0], m_t[1]))
            cst = c_old[p]
            num = (_dot(amat.astype(BF16), _block_diag(v[0].astype(BF16), v[1].astype(BF16)))
                   + jnp.where(left_wide, w_inter[0], w_inter[1]) * _dot(q2, cst.astype(BF16)))
            den = [jnp.sum(jnp.where(left, amat, 0.0), axis=-1, keepdims=True),
                   jnp.sum(jnp.where(left, 0.0, amat), axis=-1, keepdims=True)]
            den = [jnp.maximum(jnp.abs(den[e] + w_inter[e] * jnp.sum(q[e] * n_old[hs[e]], axis=-1, keepdims=True)),
                               jnp.exp(-m_t[e])) for e in range(2)]
            hout = num / jnp.where(left_wide, den[0], den[1])
            decay, upd = [], []
            for e in range(2):
                m_new = m_t[e][CHUNK - 1:CHUNK, :]
                b_last = b_col[e][CHUNK - 1:CHUNK, :]
                kw = k[e] * jnp.exp(b_last - b_col[e] + i_col[e] - m_new)
                decay.append(jnp.exp(b_last + m_prev[e] - m_new))
                upd.append(_dot3_tn_fused(kw, v[e]))
                n_new.append(decay[e] * n_old[hs[e]] + jnp.sum(kw, axis=0, keepdims=True))
                m_next = jnp.where(lane_row == hs[e], m_new, m_next)
            c_new.append(jnp.where(br, decay[0], decay[1]) * cst + _block_diag(upd[0], upd[1]))
            for e in range(2):
                h = hs[e]
                lo, hi = h * LANES, (h + 1) * LANES
                z = _sigmoid(proj_ref[_G_OA + h, rows, :]) * hout[:, e * LANES:(e + 1) * LANES]
                u_ref[rows, lo:hi] = (_rms(z) * nwa_ref[:, lo:hi]
                                      + skip_ref[:, lo:hi] * xc_scr[h, rows, :]).astype(BF16)
        for p in range(H // 2):
            hs = (2 * p, 2 * p + 1)
            lg = [log_gamma[h] for h in hs]
            kr = [kr_scr[h, rows, :] for h in hs]
            vb = [proj_ref[_G_VB + h, rows, :].astype(BF16) for h in hs]
            q2 = jnp.concatenate([qr_scr[h, rows, :] for h in hs], axis=1).astype(BF16)
            amat = _dot_nt(q2, _block_diag(kr[0].astype(BF16), kr[1].astype(BF16))) * dec_scr[p]
            sst = s_old[p]
            o = (_dot(amat.astype(BF16), _block_diag(vb[0], vb[1]))
                 + jnp.exp((tcol + 1.0) * jnp.where(left_wide, lg[0], lg[1])) * _dot(q2, sst.astype(BF16)))
            kws = jnp.concatenate([kr[e] * jnp.exp((CHUNK - 1.0 - tcol) * lg[e]) for e in range(2)], axis=1)
            cross = _dot_tn(kws.astype(BF16), jnp.concatenate(vb, axis=1))
            s_new.append(jnp.where(br, math.exp(CHUNK * lg[0]), math.exp(CHUNK * lg[1])) * sst
                         + jnp.where(br == bcol_blk, cross, 0.0))
            for e in range(2):
                h = hs[e]
                lo, hi = h * LANES, (h + 1) * LANES
                gate = proj_ref[_G_GB + h, rows, :]
                u_ref[rows, D_MODEL // 2 + lo:D_MODEL // 2 + hi] = (
                    _rms(o[:, e * LANES:(e + 1) * LANES]) * nwb_ref[:, lo:hi]
                    * (gate * _sigmoid(gate))).astype(BF16)
        for p in range(H // 2):
            c_scr[p] = c_new[p]
            s_scr[p] = s_new[p]
        for h in range(H):
            n_scr[h] = n_new[h]
        m_scr[...] = m_next
        return carry

    lax.fori_loop(0, nc, chunk_body, 0)

    @pl.when(j == nj - 1)
    def _():
        for p in range(H // 2):
            for e in range(2):
                blk = slice(e * LANES, (e + 1) * LANES)
                co_ref[0, 2 * p + e] = c_scr[p, blk, blk]
                so_ref[0, 2 * p + e] = s_scr[p, blk, blk]
        no_ref[0] = n_scr[...]
        mo_ref[0] = m_scr[...]
        convo_ref[0] = conv_scr[...]


def _even_scan_call(proj, gates, cos_t, sin_t, states, weights, *, ts, n_seq, steps, row_off):
    H = N_HEADS_EVEN
    c0, n0, m0, conv0, s0 = states
    ng = proj.shape[0]
    rows_idx = lambda b, j: (row_off + b * steps + j, 0)
    state_specs = [pl.BlockSpec((1, H, LANES, LANES), lambda b, j: (b, 0, 0, 0)),
                   pl.BlockSpec((1, H, 1, LANES), lambda b, j: (b, 0, 0, 0)),
                   pl.BlockSpec((1, 1, LANES), lambda b, j: (b, 0, 0)),
                   pl.BlockSpec((1, SUBLANES, H * LANES), lambda b, j: (b, 0, 0)),
                   pl.BlockSpec((1, H, LANES, LANES), lambda b, j: (b, 0, 0, 0))]
    in_specs = [pl.BlockSpec((ng, ts, LANES), lambda b, j: (0, row_off + b * steps + j, 0)),
                pl.BlockSpec((ts, LANES), rows_idx),
                pl.BlockSpec((ts, LANES), rows_idx),
                pl.BlockSpec((ts, LANES), rows_idx)] + state_specs
    for w in weights:
        in_specs.append(pl.BlockSpec(w.shape, functools.partial(lambda nd, b, j: (0,) * nd, w.ndim)))
    out_specs = [pl.BlockSpec((ts, D_MODEL), lambda b, j: (b * steps + j, 0))] + state_specs
    out_shape = [jax.ShapeDtypeStruct((n_seq * steps * ts, D_MODEL), BF16),
                 jax.ShapeDtypeStruct(c0.shape, F32), jax.ShapeDtypeStruct(n0.shape, F32),
                 jax.ShapeDtypeStruct(m0.shape, F32), jax.ShapeDtypeStruct(conv0.shape, F32),
                 jax.ShapeDtypeStruct(s0.shape, F32)]
    nc = ts // CHUNK
    pair_state = pltpu.VMEM((H // 2, 2 * LANES, 2 * LANES), F32)
    scratch = [pair_state, pltpu.VMEM((H, 1, LANES), F32),
               pltpu.VMEM((1, LANES), F32), pltpu.VMEM((SUBLANES, H * LANES), F32),
               pair_state,
               pltpu.VMEM((ts + 2 * SUBLANES, LANES), F32),
               pltpu.VMEM((H, ts, LANES), F32), pltpu.VMEM((H, ts, LANES), F32),
               pltpu.VMEM((H, ts, LANES), F32), pltpu.VMEM((H, ts, LANES), F32),
               pltpu.VMEM((H, ts, LANES), F32),
               pltpu.VMEM((ts, LANES), F32), pltpu.VMEM((ts, LANES), F32),
               pltpu.VMEM((nc, SUBLANES, LANES), F32),
               pltpu.VMEM((H // 2, CHUNK, LANES), F32)]
    return pl.pallas_call(
        functools.partial(_even_scan_kernel, ts=ts),
        grid=(n_seq, steps),
        in_specs=in_specs, out_specs=out_specs, out_shape=out_shape,
        scratch_shapes=scratch,
        compiler_params=_cparams(("arbitrary", "arbitrary")),
        name="even_scan",
    )(proj, gates, cos_t, sin_t, c0, n0, m0, conv0, s0, *weights)


_G_Q, _G_F, _G_I, _G_G = 0, 8, 16, 24


def _hgrn_intra_pair(q, k, bcum, ti, si, left, tcol_i):
    amat = jnp.zeros((CHUNK, LANES), F32)
    for b in (32, 16, 8):
        nb2 = CHUNK // (2 * b)
        upper = ((tcol_i // b) % 2) == 1
        ql, kl = [], []
        for e in range(2):
            parts = [jnp.broadcast_to(bcum[e][m * 2 * b + b - 1:m * 2 * b + b, :], (2 * b, LANES))
                     for m in range(nb2)]
            ref = parts[0] if nb2 == 1 else jnp.concatenate(parts, axis=0)
            ql.append(jnp.where(upper, q[e] * jnp.exp(bcum[e] - ref), 0.0).astype(BF16))
            kl.append(jnp.where(upper, 0.0, k[e] * jnp.exp(ref - bcum[e])).astype(BF16))
        al = _dot_nt(jnp.concatenate(ql, axis=1), _block_diag(kl[0], kl[1]))
        amat = amat + jnp.where((ti // (2 * b)) == (si // (2 * b)), al, 0.0)
    nblk = CHUNK // SUBLANES
    b3 = [x.reshape(nblk, SUBLANES, LANES) for x in bcum]
    k3 = [x.reshape(nblk, SUBLANES, LANES) for x in k]
    for jj in range(SUBLANES):
        col = []
        for e in range(2):
            bj = jnp.broadcast_to(b3[e][:, jj:jj + 1, :], (nblk, SUBLANES, LANES)).reshape(CHUNK, LANES)
            kj = jnp.broadcast_to(k3[e][:, jj:jj + 1, :], (nblk, SUBLANES, LANES)).reshape(CHUNK, LANES)
            col.append(jnp.sum(q[e] * kj * jnp.exp(bcum[e] - bj), axis=-1, keepdims=True))
        sel = (si == (ti // SUBLANES) * SUBLANES + jj) & ((ti % SUBLANES) >= jj)
        amat = jnp.where(sel, jnp.where(left, col[0], col[1]), amat)
    return amat


def _odd_scan_kernel(proj_ref, s0_ref, lb_ref, nw_ref, u_ref, so_ref,
                     st_scr, k_scr, bc_scr, *, ts):
    H = N_HEADS_ODD
    j = pl.program_id(1)
    nj = pl.num_programs(1)
    nc = ts // CHUNK

    @pl.when(j == 0)
    def _():
        for p in range(H // 2):
            st_scr[p] = _block_diag(s0_ref[0, 2 * p].T, s0_ref[0, 2 * p + 1].T)

    for h in range(H):
        lo, hi = h * LANES, (h + 1) * LANES
        lbv = lb_ref[:, lo:hi]
        fpre = proj_ref[_G_F + h]
        e = jnp.exp(-jnp.abs(fpre))
        one_e = 1.0 + e
        log_lb = jnp.log(lbv)
        log_rest = jnp.log1p(-lbv) + (jnp.minimum(fpre, 0.0) - jnp.log(one_e))
        logf = jnp.maximum(log_lb, log_rest) + jnp.log(1.0 + jnp.exp(-jnp.abs(log_lb - log_rest)))
        k_scr[h] = (1.0 - lbv) * (jnp.where(fpre >= 0.0, e, 1.0) / one_e)
        bc_scr[h] = _chunk_cumsum(logf)

    ti = lax.broadcasted_iota(jnp.int32, (CHUNK, LANES), 0)
    lane2 = lax.broadcasted_iota(jnp.int32, (CHUNK, LANES), 1)
    left = lane2 < CHUNK
    si = lane2 % CHUNK
    tcol_i = lax.broadcasted_iota(jnp.int32, (CHUNK, 1), 0)
    same_head = ((lax.broadcasted_iota(jnp.int32, (2 * LANES, 2 * LANES), 0) < LANES)
                 == (lax.broadcasted_iota(jnp.int32, (2 * LANES, 2 * LANES), 1) < LANES))

    def chunk_body(c, carry):
        r0 = pl.multiple_of(c * CHUNK, CHUNK)
        rows = pl.ds(r0, CHUNK)
        st_old = [st_scr[p] for p in range(H // 2)]
        st_new = []
        for p in range(H // 2):
            hs = (2 * p, 2 * p + 1)
            q = [proj_ref[_G_Q + h, rows, :] for h in hs]
            k = [k_scr[h, rows, :] for h in hs]
            vb = [proj_ref[_G_I + h, rows, :].astype(BF16) for h in hs]
            bcum = [bc_scr[h, rows, :] for h in hs]
            amat = _hgrn_intra_pair(q, k, bcum, ti, si, left, tcol_i)
            st = st_old[p]
            qg = jnp.concatenate([q[e] * jnp.exp(bcum[e]) for e in range(2)], axis=1)
            o = (_dot(amat.astype(BF16), _block_diag(vb[0], vb[1]))
                 + _dot_nt(qg.astype(BF16), st.astype(BF16)))
            last = [bcum[e][CHUNK - 1:CHUNK, :] for e in range(2)]
            kd = jnp.concatenate([k[e] * jnp.exp(last[e] - bcum[e]) for e in range(2)], axis=1)
            cross = _dot_tn(jnp.concatenate(vb, axis=1), kd.astype(BF16))
            st_new.append(st * jnp.exp(jnp.concatenate(last, axis=1)) + jnp.where(same_head, cross, 0.0))
            for e in range(2):
                h = hs[e]
                lo, hi = h * LANES, (h + 1) * LANES
                u_ref[rows, lo:hi] = (_rms(o[:, e * LANES:(e + 1) * LANES]) * nw_ref[:, lo:hi]
                                      * _sigmoid(proj_ref[_G_G + h, rows, :])).astype(BF16)
        for p in range(H // 2):
            st_scr[p] = st_new[p]
        return carry

    lax.fori_loop(0, nc, chunk_body, 0)

    @pl.when(j == nj - 1)
    def _():
        for p in range(H // 2):
            for e in range(2):
                blk = slice(e * LANES, (e + 1) * LANES)
                so_ref[0, 2 * p + e] = st_scr[p, blk, blk].T


def _odd_scan_call(proj, s0, lb, nw, *, ts, n_seq, steps, row_off):
    H = N_HEADS_ODD
    ng = proj.shape[0]
    st_spec = pl.BlockSpec((1, H, LANES, LANES), lambda b, j: (b, 0, 0, 0))
    return pl.pallas_call(
        functools.partial(_odd_scan_kernel, ts=ts),
        grid=(n_seq, steps),
        in_specs=[pl.BlockSpec((ng, ts, LANES), lambda b, j: (0, row_off + b * steps + j, 0)),
                  st_spec,
                  pl.BlockSpec((1, D_MODEL), lambda b, j: (0, 0)),
                  pl.BlockSpec((1, D_MODEL), lambda b, j: (0, 0))],
        out_specs=[pl.BlockSpec((ts, D_MODEL), lambda b, j: (b * steps + j, 0)), st_spec],
        out_shape=[jax.ShapeDtypeStruct((n_seq * steps * ts, D_MODEL), BF16),
                   jax.ShapeDtypeStruct(s0.shape, F32)],
        scratch_shapes=[pltpu.VMEM((H // 2, 2 * LANES, 2 * LANES), F32),
                        pltpu.VMEM((H, ts, LANES), F32),
                        pltpu.VMEM((H, ts, LANES), F32)],
        compiler_params=_cparams(("arbitrary", "arbitrary")),
        name="odd_scan",
    )(proj, s0, lb, nw)


def _post_kernel(x_ref, up_ref, us_ref, mod_ref, wout_ref, nw_ref, wr_ref, br_ref,
                 xo_ref, hf_ref, gate_ref, lpos_ref, cnt_ref, hf_scr,
                 *, tm, npc, n_prompt_tiles):
    i = pl.program_id(0)
    nch = tm // CHUNK

    u = jnp.where(i < n_prompt_tiles, up_ref[...], us_ref[...])
    y = _dot(u, wout_ref[...])
    for c in range(nch):
        seq = _seq_row(i, nch, c, npc)
        gm = mod_ref[pl.ds(seq, 1), 2 * D_MODEL:3 * D_MODEL]
        shf = mod_ref[pl.ds(seq, 1), 3 * D_MODEL:4 * D_MODEL]
        scf = mod_ref[pl.ds(seq, 1), 4 * D_MODEL:5 * D_MODEL]
        rs = slice(c * CHUNK, (c + 1) * CHUNK)
        xn = x_ref[rs, :] + gm * y[rs, :]
        xo_ref[rs, :] = xn
        hf = _rms(xn) * nw_ref[...] * (1.0 + scf) + shf
        hf_scr[rs, :] = hf
        hf_ref[rs, :] = hf.astype(BF16)

    logits = _dot3(hf_scr[...], wr_ref[...]) + br_ref[...]
    lane_i = lax.broadcasted_iota(jnp.int32, (tm, LANES), 1)
    lane_f = lane_i.astype(F32)
    vals, idxs = [], []
    cur = logits
    for _ in range(TOP_K):
        m = jnp.max(cur, axis=-1, keepdims=True)
        idx = jnp.min(jnp.where(cur == m, lane_f, float(LANES)), axis=-1, keepdims=True)
        vals.append(m)
        idxs.append(idx)
        cur = jnp.where(lane_f == idx, -jnp.inf, cur)
    exps = [jnp.exp(v - vals[0]) for v in vals]
    denom = exps[0] + exps[1] + exps[2] + exps[3]
    onehot = jnp.zeros((tm, LANES), F32)
    for idx in idxs:
        onehot = onehot + jnp.where(lane_f == idx, 1.0, 0.0)
    r = lax.broadcasted_iota(jnp.int32, (tm, tm), 0)
    cidx = lax.broadcasted_iota(jnp.int32, (tm, tm), 1)
    strict = jnp.where((cidx < r) & (cidx // DISPATCH_TILE == r // DISPATCH_TILE), 1.0, 0.0).astype(BF16)
    before = _dot(strict, onehot.astype(BF16))
    er = lax.broadcasted_iota(jnp.int32, (LANES, LANES), 0)
    ec = lax.broadcasted_iota(jnp.int32, (LANES, LANES), 1)
    lower_experts = jnp.where(er < ec, 1.0, 0.0).astype(BF16)
    pos_parts = []
    for s in range(tm // DISPATCH_TILE):
        rs = slice(s * DISPATCH_TILE, (s + 1) * DISPATCH_TILE)
        cnt = jnp.sum(onehot[rs, :], axis=0, keepdims=True)
        cnt_ref[s] = cnt
        n8 = jnp.floor((cnt + (SUBLANES - 1.0)) * (1.0 / SUBLANES)) * float(SUBLANES)
        run_start = _dot(jnp.broadcast_to(n8, (SUBLANES, LANES)).astype(BF16), lower_experts)[0:1, :]
        pos_parts.append(before[rs, :] + run_start)
    posmat = jnp.concatenate(pos_parts, axis=0)
    gate_o = jnp.zeros((tm, LANES), F32)
    lpos_o = jnp.zeros((tm, LANES), F32)
    for kk in range(TOP_K):
        lp = jnp.sum(jnp.where(lane_f == idxs[kk], posmat, 0.0), axis=-1, keepdims=True)
        gate_o = jnp.where(lane_i == kk, exps[kk] / denom, gate_o)
        lpos_o = jnp.where(lane_i == kk, lp, lpos_o)
    gate_ref[...] = gate_o
    lpos_ref[...] = lpos_o


def _post_call(x, u_p, u_s, mod_l, w_out, nw, w_r, b_r, npc):
    ttot = x.shape[0]
    tm = TOKEN_TILE
    npt = u_p.shape[0] // tm
    sub = tm // DISPATCH_TILE
    tile = lambda i: (i, 0)
    const = lambda i: (0, 0)
    return pl.pallas_call(
        functools.partial(_post_kernel, tm=tm, npc=npc, n_prompt_tiles=npt),
        grid=(ttot // tm,),
        in_specs=[pl.BlockSpec((tm, D_MODEL), tile),
                  pl.BlockSpec((tm, D_MODEL), lambda i: (jnp.minimum(i, npt - 1), 0)),
                  pl.BlockSpec((tm, D_MODEL), lambda i: (jnp.maximum(i - npt, 0), 0)),
                  pl.BlockSpec(mod_l.shape, const),
                  pl.BlockSpec(w_out.shape, const),
                  pl.BlockSpec((1, D_MODEL), const),
                  pl.BlockSpec(w_r.shape, const),
                  pl.BlockSpec((1, LANES), const)],
        out_specs=[pl.BlockSpec((tm, D_MODEL), tile), pl.BlockSpec((tm, D_MODEL), tile),
                   pl.BlockSpec((tm, LANES), tile), pl.BlockSpec((tm, LANES), tile),
                   pl.BlockSpec((sub, 1, LANES), lambda i: (i, 0, 0))],
        out_shape=[jax.ShapeDtypeStruct((ttot, D_MODEL), F32), jax.ShapeDtypeStruct((ttot, D_MODEL), BF16),
                   jax.ShapeDtypeStruct((ttot, LANES), F32), jax.ShapeDtypeStruct((ttot, LANES), F32),
                   jax.ShapeDtypeStruct((ttot // DISPATCH_TILE, 1, LANES), F32)],
        scratch_shapes=[pltpu.VMEM((tm, D_MODEL), F32)],
        compiler_params=_cparams(("arbitrary",)),
        name="post",
    )(x, u_p, u_s, mod_l, w_out, nw, w_r, b_r)


def _plan_kernel(cnt_ref, n8_ref, loff_ref, gbase_ref, blk_ref, ends_ref, *, nbp, ntp):
    cnt = cnt_ref[...]
    n8 = jnp.floor((cnt + (SUBLANES - 1.0)) * (1.0 / SUBLANES)) * float(SUBLANES)
    r = lax.broadcasted_iota(jnp.int32, (LANES, LANES), 0)
    c = lax.broadcasted_iota(jnp.int32, (LANES, LANES), 1)
    loff = _dot_sel_rhs(n8, jnp.where(r < c, 1.0, 0.0).astype(BF16))
    gtot = jnp.broadcast_to(jnp.sum(n8, axis=0, keepdims=True), (SUBLANES, LANES))
    nblk = jnp.floor((gtot + (EXPERT_ROWS - 1.0)) * (1.0 / EXPERT_ROWS))
    ends = _dot_sel_rhs(nblk, jnp.where(r <= c, 1.0, 0.0).astype(BF16))
    start_row = (ends[0:1, :] - nblk[0:1, :]) * float(EXPERT_ROWS)
    tr = lax.broadcasted_iota(jnp.int32, (ntp, ntp), 0)
    tc = lax.broadcasted_iota(jnp.int32, (ntp, ntp), 1)
    gbase = start_row + _dot_sel_lhs(jnp.where(tc < tr, 1.0, 0.0).astype(BF16), n8)
    n8_ref[...] = n8.astype(jnp.int32)
    loff_ref[...] = loff.astype(jnp.int32)
    gbase_ref[...] = gbase.astype(jnp.int32)
    bi = lax.broadcasted_iota(jnp.int32, (nbp, LANES), 0).astype(F32)
    li = lax.broadcasted_iota(jnp.int32, (nbp, LANES), 1)
    done = jnp.where((li < N_EXPERTS) & (ends[0:1, :] <= bi), 1.0, 0.0)
    be = jnp.minimum(jnp.sum(done, axis=-1, keepdims=True), N_EXPERTS - 1.0)
    blk_ref[...] = jnp.broadcast_to(be, (nbp, LANES)).astype(jnp.int32)
    r8 = lax.broadcasted_iota(jnp.int32, (SUBLANES, LANES), 0)
    tail_start = start_row + gtot[0:1, :]
    tail_len = nblk[0:1, :] * float(EXPERT_ROWS) - gtot[0:1, :]
    info = jnp.where(r8 == 0, ends, jnp.where(r8 == 1, tail_start, jnp.where(r8 == 2, tail_len, 0.0)))
    ends_ref[...] = info.astype(jnp.int32)


def _plan_call(cnt_tiles, nbp):
    ntp = cnt_tiles.shape[0]
    const = lambda i: (0, 0)
    tbl = jax.ShapeDtypeStruct((ntp, LANES), jnp.int32)
    return pl.pallas_call(
        functools.partial(_plan_kernel, nbp=nbp, ntp=ntp),
        grid=(1,),
        in_specs=[pl.BlockSpec((ntp, LANES), const)],
        out_specs=[pl.BlockSpec((ntp, LANES), const), pl.BlockSpec((ntp, LANES), const),
                   pl.BlockSpec((ntp, LANES), const),
                   pl.BlockSpec((nbp, LANES), const), pl.BlockSpec((SUBLANES, LANES), const)],
        out_shape=[tbl, tbl, tbl,
                   jax.ShapeDtypeStruct((nbp, LANES), jnp.int32),
                   jax.ShapeDtypeStruct((SUBLANES, LANES), jnp.int32)],
        compiler_params=_cparams(("arbitrary",)),
        name="plan",
    )(cnt_tiles)


_GROUP_BITS = tuple(range(3, DISPATCH_TILE.bit_length()))


_TOTAL_BITS = tuple(range(3, SORT_ROWS.bit_length()))


def _group_copies(n8_ref, loff_ref, gbase_ref, tile, make_copy, wait):
    if wait:
        total = lax.fori_loop(0, N_EXPERTS, lambda e, acc: acc + n8_ref[tile * N_EXPERTS + e], 0)
        for bit in _TOTAL_BITS:
            size = 1 << bit

            @pl.when((total & size) != 0)
            def _():
                make_copy(0, 0, size).wait()
        return

    def per_expert(e, carry):
        idx = tile * N_EXPERTS + e
        n = n8_ref[idx]
        off = loff_ref[idx]
        base = gbase_ref[idx]
        for bit in _GROUP_BITS:
            size = 1 << bit

            @pl.when((n & size) != 0)
            def _():
                done = n & ~(2 * size - 1)
                make_copy(pl.multiple_of(off + done, SUBLANES), pl.multiple_of(base + done, SUBLANES), size).start()
        return carry

    lax.fori_loop(0, N_EXPERTS, per_expert, 0)


def _zero_fill(tails_ref, nu_ref, n_blocks, make_zero_copy, wait):
    def finish(cp):
        cp.wait() if wait else cp.start()

    def per_expert(e, carry):
        base = tails_ref[e]
        n = tails_ref[N_EXPERTS + e]
        for bit in range(3, EXPERT_ROWS.bit_length() - 1):
            size = 1 << bit

            @pl.when((n & size) != 0)
            def _():
                done = n & ~(2 * size - 1)
                finish(make_zero_copy(pl.multiple_of(base + done, SUBLANES), size))
        return carry

    lax.fori_loop(0, N_EXPERTS, per_expert, 0)

    def per_block(blk, carry):
        finish(make_zero_copy(pl.multiple_of(blk * EXPERT_ROWS, EXPERT_ROWS), EXPERT_ROWS))
        return carry

    lax.fori_loop(nu_ref[0], n_blocks, per_block, 0)


def _dispatch_kernel(*refs, n_blocks, fresh):
    if fresh:
        (n8_ref, loff_ref, gbase_ref, tails_ref, nu_ref, hf_ref, lpos_ref, gate_ref, xs_ref,
         sbuf, sem, zbuf, zsem) = refs
    else:
        (n8_ref, loff_ref, gbase_ref, tails_ref, nu_ref, hf_ref, lpos_ref, gate_ref, _, xs_ref,
         sbuf, sem) = refs
    i = pl.program_id(0)

    if fresh:
        def make_zero_copy(dst_row, size):
            return pltpu.make_async_copy(zbuf.at[pl.ds(0, size), :], xs_ref.at[pl.ds(dst_row, size), :], zsem)

        @pl.when(i == 0)
        def _():
            zbuf[...] = jnp.zeros_like(zbuf)
            _zero_fill(tails_ref, nu_ref, n_blocks, make_zero_copy, wait=False)

    eye8 = jnp.where(lax.broadcasted_iota(jnp.int32, (SUBLANES, LANES), 0)
                     == lax.broadcasted_iota(jnp.int32, (SUBLANES, LANES), 1), 1.0, 0.0).astype(BF16)
    lpos_t = _dot_sel_nt(eye8, lpos_ref[...])
    gate_t = _dot_sel_nt(eye8, gate_ref[...])
    row = lax.broadcasted_iota(jnp.int32, (SORT_ROWS, DISPATCH_TILE), 0).astype(F32)
    perm = jnp.zeros((SORT_ROWS, DISPATCH_TILE), F32)
    wgate = jnp.zeros((SORT_ROWS, DISPATCH_TILE), F32)
    for kk in range(TOP_K):
        hit = row == lpos_t[kk:kk + 1, :]
        perm = jnp.where(hit, 1.0, perm)
        wgate = jnp.where(hit, gate_t[kk:kk + 1, :], wgate)
    slot = i % 2
    sbuf[slot, :, 0:D_MODEL] = _dot(perm.astype(BF16), hf_ref[...].astype(BF16))
    sbuf[slot, :, D_MODEL:XS_WIDTH] = jnp.broadcast_to(jnp.sum(wgate, axis=-1, keepdims=True),
                                                       (SORT_ROWS, LANES))

    def copies(tile, buf_slot, wait):
        def make_copy(src_row, dst_row, size):
            return pltpu.make_async_copy(sbuf.at[buf_slot, pl.ds(src_row, size), :],
                                         xs_ref.at[pl.ds(dst_row, size), :], sem.at[buf_slot])
        _group_copies(n8_ref, loff_ref, gbase_ref, tile, make_copy, wait=wait)

    copies(i, slot, wait=False)

    @pl.when(i > 0)
    def _():
        copies(i - 1, 1 - slot, wait=True)

    @pl.when(i == pl.num_programs(0) - 1)
    def _():
        copies(i, slot, wait=True)
        if fresh:
            _zero_fill(tails_ref, nu_ref, n_blocks, make_zero_copy, wait=True)


def _dispatch_call(tables, tails, n_used, hf, lpos, gate, n_blocks, xs_prev):
    ttot = hf.shape[0]
    tm = DISPATCH_TILE
    tile = lambda i, *_: (i, 0)
    fresh = xs_prev is None
    in_specs = [pl.BlockSpec((tm, D_MODEL), tile), pl.BlockSpec((tm, LANES), tile),
                pl.BlockSpec((tm, LANES), tile)]
    args = [*tables, tails, n_used, hf, lpos, gate]
    if not fresh:
        in_specs.append(pl.BlockSpec(memory_space=pl.ANY))
        args.append(xs_prev)
    scratch = [pltpu.VMEM((2, SORT_ROWS, XS_WIDTH), F32), pltpu.SemaphoreType.DMA((2,))]
    if fresh:
        scratch += [pltpu.VMEM((EXPERT_ROWS, XS_WIDTH), F32), pltpu.SemaphoreType.DMA(())]
    grid_spec = pltpu.PrefetchScalarGridSpec(
        num_scalar_prefetch=5, grid=(ttot // tm,),
        in_specs=in_specs,
        out_specs=pl.BlockSpec(memory_space=pl.ANY),
        scratch_shapes=scratch)
    return pl.pallas_call(
        functools.partial(_dispatch_kernel, n_blocks=n_blocks, fresh=fresh),
        grid_spec=grid_spec,
        out_shape=jax.ShapeDtypeStruct((n_blocks * EXPERT_ROWS, XS_WIDTH), F32),
        input_output_aliases={} if fresh else {len(args) - 1: 0},
        compiler_params=pltpu.CompilerParams(dimension_semantics=("arbitrary",),
                                             vmem_limit_bytes=VMEM_LIMIT, has_side_effects=True),
        name="dispatch",
    )(*args)


def _expert_kernel(be_ref, nu_ref, eend_ref, xs_ref, wgu_hbm, bgu_ref, wdn_hbm, bdn_ref, y_ref,
                   wgu_bf, wdn_bf, wgu_f32, wdn_f32, wsem, slot_scr, *, layer, n_blocks):
    b = pl.program_id(0)

    def weight_copies(e, slot):
        return (pltpu.make_async_copy(wgu_hbm.at[layer, e], wgu_f32.at[slot], wsem.at[slot]),
                pltpu.make_async_copy(wdn_hbm.at[layer, e], wdn_f32.at[slot], wsem.at[slot]))

    @pl.when(b < nu_ref[0])
    def _():
        e = be_ref[b]
        prev = be_ref[jnp.maximum(b - 1, 0)]

        @pl.when(b == 0)
        def _():
            slot_scr[0] = 0
            for cp in weight_copies(e, 0):
                cp.start()

        @pl.when((b == 0) | (e != prev))
        def _():
            slot = jnp.where(b == 0, 0, 1 - slot_scr[0])
            slot_scr[0] = slot
            for cp in weight_copies(e, slot):
                cp.wait()
            wgu_bf[...] = wgu_f32[slot].astype(BF16)
            wdn_bf[...] = wdn_f32[slot].astype(BF16)
            next_block = eend_ref[e]

            @pl.when(next_block < nu_ref[0])
            def _():
                for cp in weight_copies(be_ref[jnp.minimum(next_block, n_blocks - 1)], 1 - slot):
                    cp.start()

        gu = _dot(xs_ref[:, 0:D_MODEL].astype(BF16), wgu_bf[...]) + bgu_ref[0, 0]
        g = jnp.minimum(gu[:, :D_MODEL], SWIGLU_LIMIT)
        u = jnp.clip(gu[:, D_MODEL:], -SWIGLU_LIMIT, SWIGLU_LIMIT)
        act = (u + 1.0) * (g * _sigmoid(SWIGLU_ALPHA * g))
        gate = xs_ref[:, D_MODEL:D_MODEL + 1]
        y_ref[...] = (_dot(act.astype(BF16), wdn_bf[...]) + bdn_ref[0, 0]) * gate

    @pl.when(b >= nu_ref[0])
    def _():
        y_ref[...] = jnp.zeros_like(y_ref)


def _expert_call(blk_expert, n_used, expert_end, xs, w_gu, b_gu, w_dn, b_dn, layer):
    nb = xs.shape[0] // EXPERT_ROWS
    d_ff2 = w_gu.shape[3]
    blk = lambda b, be, nu, ee: (jnp.minimum(b, nu[0] - 1), 0)
    blk_out = lambda b, be, nu, ee: (b, 0)
    exp4 = lambda b, be, nu, ee: (layer, be[jnp.minimum(b, nu[0] - 1)], 0, 0)
    grid_spec = pltpu.PrefetchScalarGridSpec(
        num_scalar_prefetch=3, grid=(nb,),
        in_specs=[pl.BlockSpec((EXPERT_ROWS, XS_WIDTH), blk),
                  pl.BlockSpec(memory_space=pl.ANY),
                  pl.BlockSpec((1, 1, 1, d_ff2), exp4),
                  pl.BlockSpec(memory_space=pl.ANY),
                  pl.BlockSpec((1, 1, 1, D_MODEL), exp4)],
        out_specs=pl.BlockSpec((EXPERT_ROWS, D_MODEL), blk_out),
        scratch_shapes=[pltpu.VMEM((D_MODEL, d_ff2), BF16), pltpu.VMEM((D_MODEL, D_MODEL), BF16),
                        pltpu.VMEM((2, D_MODEL, d_ff2), F32), pltpu.VMEM((2, D_MODEL, D_MODEL), F32),
                        pltpu.SemaphoreType.DMA((2,)), pltpu.SMEM((1,), jnp.int32)])
    return pl.pallas_call(
        functools.partial(_expert_kernel, layer=layer, n_blocks=nb),
        grid_spec=grid_spec,
        out_shape=jax.ShapeDtypeStruct((xs.shape[0], D_MODEL), F32),
        compiler_params=_cparams(("arbitrary",)),
        name="experts",
    )(blk_expert, n_used, expert_end, xs, w_gu, b_gu, w_dn, b_dn)


def _combine_kernel(*refs, tm, npc, final):
    if final:
        (n8_ref, loff_ref, gbase_ref, x_ref, lpos_ref, mod_ref, fnw_ref, y_hbm,
         yp_ref, ys_ref, ybuf, sem, ynorm) = refs
    else:
        n8_ref, loff_ref, gbase_ref, x_ref, lpos_ref, mod_ref, y_hbm, xo_ref, ybuf, sem = refs
    i = pl.program_id(0)
    nch = tm // CHUNK

    slot = i % 2

    def copies(tile, buf_slot, wait):
        def make_copy(buf_row, src_row, size):
            return pltpu.make_async_copy(y_hbm.at[pl.ds(src_row, size), :],
                                         ybuf.at[buf_slot, pl.ds(buf_row, size), :], sem.at[buf_slot])
        _group_copies(n8_ref, loff_ref, gbase_ref, tile, make_copy, wait=wait)

    @pl.when(i == 0)
    def _():
        ybuf[...] = jnp.zeros_like(ybuf)
        copies(i, slot, wait=False)

    @pl.when(i + 1 < pl.num_programs(0))
    def _():
        copies(i + 1, 1 - slot, wait=False)

    copies(i, slot, wait=True)

    lpos = lpos_ref[...]
    col = lax.broadcasted_iota(jnp.int32, (tm, SORT_ROWS), 1).astype(F32)
    unperm = jnp.zeros((tm, SORT_ROWS), F32)
    for kk in range(TOP_K):
        unperm = jnp.where(col == lpos[:, kk:kk + 1], 1.0, unperm)
    acc = _dot_sel_lhs2(unperm.astype(BF16), ybuf[slot])
    for c in range(nch):
        seq = _seq_row(i, nch, c, npc)
        gf = mod_ref[pl.ds(seq, 1), 5 * D_MODEL:6 * D_MODEL]
        rs = slice(c * CHUNK, (c + 1) * CHUNK)
        xn = x_ref[rs, :] + gf * acc[rs, :]
        if final:
            ynorm[rs, :] = _rms(xn) * fnw_ref[...]
        else:
            xo_ref[rs, :] = xn
    if final:
        is_prompt = i < (npc * CHUNK) // tm

        @pl.when(is_prompt)
        def _():
            yp_ref[...] = ynorm[...]

        @pl.when(jnp.logical_not(is_prompt))
        def _():
            ys_ref[...] = ynorm[...]


def _combine_call(tables, x, lpos, mod_l, y_sorted, npc, final_w):
    ttot = x.shape[0]
    tm = DISPATCH_TILE
    final = final_w is not None
    npt = (npc * CHUNK) // tm
    tile = lambda i, *_: (i, 0)
    const = lambda i, *_: (0, 0)
    in_specs = [pl.BlockSpec((tm, D_MODEL), tile),
                pl.BlockSpec((tm, LANES), tile),
                pl.BlockSpec(mod_l.shape, const)]
    args = [x, lpos, mod_l]
    if final:
        in_specs.append(pl.BlockSpec((1, D_MODEL), const))
        args.append(final_w)
    in_specs.append(pl.BlockSpec(memory_space=pl.ANY))
    args.append(y_sorted)
    if final:
        out_specs = [pl.BlockSpec((tm, D_MODEL), lambda i, *_: (jnp.minimum(i, npt - 1), 0)),
                     pl.BlockSpec((tm, D_MODEL), lambda i, *_: (jnp.maximum(i - npt, 0), 0))]
        out_shape = [jax.ShapeDtypeStruct((npt * tm, D_MODEL), F32),
                     jax.ShapeDtypeStruct((ttot - npt * tm, D_MODEL), F32)]
    else:
        out_specs = [pl.BlockSpec((tm, D_MODEL), tile)]
        out_shape = [jax.ShapeDtypeStruct((ttot, D_MODEL), F32)]
    scratch = [pltpu.VMEM((2, SORT_ROWS, D_MODEL), F32), pltpu.SemaphoreType.DMA((2,))]
    if final:
        scratch.append(pltpu.VMEM((tm, D_MODEL), F32))
    grid_spec = pltpu.PrefetchScalarGridSpec(
        num_scalar_prefetch=3, grid=(ttot // tm,),
        in_specs=in_specs, out_specs=out_specs, scratch_shapes=scratch)
    return pl.pallas_call(
        functools.partial(_combine_kernel, tm=tm, npc=npc, final=final),
        grid_spec=grid_spec, out_shape=out_shape,
        compiler_params=_cparams(("arbitrary",)),
        name="combine",
    )(*tables, *args)


def kernel(x_prompt, x_sample, c_prompt, c_sample, state_mlstm_C, state_mlstm_n, state_mlstm_m, state_mlstm_conv, state_ret_S, state_hgrn_S, w_ada, b_ada, norm_mix_w, norm_ffn_w, final_norm_w, w_in_even, b_mlstm_i, b_mlstm_f, w_mlstm_conv, b_mlstm_conv, w_mlstm_q, w_mlstm_k, mlstm_skip, mlstm_norm_w, ret_norm_w, w_out_even, w_in_odd, hgrn_lb_logits, hgrn_norm_w, w_out_odd, moe_router_w, moe_router_b, moe_w_gate_up, moe_b_gate_up, moe_w_down, moe_b_down):
    bp, seq, d = x_prompt.shape
    bs, dseq, _ = x_sample.shape
    assert bp == 1 and d == D_MODEL and dseq == CHUNK
    assert seq % TOKEN_TILE == 0 and (bs * dseq) % TOKEN_TILE == 0
    depth = w_ada.shape[0]
    tp, tsmp = seq, bs * dseq
    ttot = tp + tsmp
    npc = tp // CHUNK
    he, ho = N_HEADS_EVEN, N_HEADS_ODD
    da = he * LANES

    x = jnp.concatenate([x_prompt.reshape(tp, d), x_sample.reshape(tsmp, d)], axis=0)
    n_mod_rows = 2 * SUBLANES
    assert 1 + bs <= n_mod_rows
    c_all = jnp.zeros((n_mod_rows, d), F32).at[0:1].set(c_prompt).at[1:1 + bs].set(c_sample)
    mod = _ada_call(c_all, w_ada, b_ada)

    half = LANES // 2
    inv = ROPE_BASE ** (-jnp.arange(half, dtype=F32) / half)
    pos_all = jnp.concatenate([jnp.arange(tp, dtype=F32),
                               jnp.tile(PAST_LEN + jnp.arange(dseq, dtype=F32), bs)])
    ang = pos_all[:, None] * inv[None, :]
    cos_t = jnp.concatenate([jnp.cos(ang), jnp.cos(ang)], axis=-1)
    sin_t = jnp.concatenate([-jnp.sin(ang), jnp.sin(ang)], axis=-1)

    lb_p = jax.nn.softmax(hgrn_lb_logits.astype(F32), axis=0)
    lbs = jnp.cumsum(lb_p, axis=0) - lb_p[0]

    n_tiles = ttot // DISPATCH_TILE
    ntp = -(-n_tiles // LANES) * LANES
    max_rows = ttot * TOP_K + n_tiles * N_EXPERTS * (SUBLANES - 1)
    nb = -(-max_rows // EXPERT_ROWS) + N_EXPERTS
    nbp = -(-nb // SUBLANES) * SUBLANES
    xs = None

    ts_p = TOKEN_TILE
    steps_p = tp // ts_p
    even_out, odd_out = [], []
    y_final = None
    for l in range(depth):
        jl = l // 2
        mod_l = mod[l]
        if l % 2 == 0:
            w_in = w_in_even[jl]
            w_main = jnp.concatenate([w_in[:, da:3 * da], w_in[:, 3 * da + 2 * he:]], axis=1).astype(BF16)
            w_gate = jnp.zeros((d, da + LANES), F32).at[:, :da].set(w_in[:, :da])
            w_gate = w_gate.at[:, da:da + 2 * he].set(w_in[:, 3 * da:3 * da + 2 * he])
            proj, gates = _inproj_call(x, mod_l, norm_mix_w[l][None], w_main, w_gate, npc)
            gbias = jnp.zeros((1, LANES), F32).at[0, :he].set(b_mlstm_i[jl]).at[0, he:2 * he].set(b_mlstm_f[jl])
            weights = [w_mlstm_conv[jl], b_mlstm_conv[jl][None],
                       jnp.concatenate([w_mlstm_q[jl], w_mlstm_k[jl]], axis=-1), gbias,
                       mlstm_skip[jl][None], mlstm_norm_w[jl][None], ret_norm_w[jl][None]]
            zeros_p = (jnp.zeros((1, he, LANES, LANES), F32), jnp.zeros((1, he, 1, LANES), F32),
                       jnp.zeros((1, 1, LANES), F32), jnp.zeros((1, SUBLANES, da), F32),
                       jnp.zeros((1, he, LANES, LANES), F32))
            st_s = (state_mlstm_C[jl], state_mlstm_n[jl][:, :, None, :],
                    jnp.zeros((bs, 1, LANES), F32).at[:, 0, :he].set(state_mlstm_m[jl]),
                    jnp.zeros((bs, SUBLANES, da), F32).at[:, SUBLANES - (CONV_W - 1):].set(state_mlstm_conv[jl]),
                    state_ret_S[jl])
            res_p = _even_scan_call(proj, gates, cos_t, sin_t, zeros_p, weights,
                                    ts=ts_p, n_seq=1, steps=steps_p, row_off=0)
            res_s = _even_scan_call(proj, gates, cos_t, sin_t, st_s, weights,
                                    ts=CHUNK, n_seq=bs, steps=1, row_off=npc)
            u_p, u_s = res_p[0], res_s[0]
            even_out.append((res_p[1:], res_s[1:]))
            w_out = w_out_even[jl].astype(BF16)
        else:
            proj = _inproj_call(x, mod_l, norm_mix_w[l][None], w_in_odd[jl].astype(BF16), None, npc)[0]
            lb = lbs[l][None]
            nw = hgrn_norm_w[jl][None]
            u_p, sp = _odd_scan_call(proj, jnp.zeros((1, ho, LANES, LANES), F32), lb, nw,
                                     ts=ts_p, n_seq=1, steps=steps_p, row_off=0)
            u_s, ss = _odd_scan_call(proj, state_hgrn_S[jl], lb, nw,
                                     ts=CHUNK, n_seq=bs, steps=1, row_off=npc)
            odd_out.append((sp, ss))
            w_out = w_out_odd[jl].astype(BF16)

        w_r = jnp.zeros((d, LANES), F32).at[:, :N_EXPERTS].set(moe_router_w[l])
        b_r = jnp.full((1, LANES), -jnp.inf, F32).at[0, :N_EXPERTS].set(moe_router_b[l])
        x, hf, gate, lpos, cnt = _post_call(x, u_p, u_s, mod_l, w_out, norm_ffn_w[l][None], w_r, b_r, npc)
        cnt_tiles = jnp.zeros((ntp, LANES), F32).at[:n_tiles].set(cnt[:, 0, :])
        n8, loff, gbase, blk, ends = _plan_call(cnt_tiles, nbp)
        tables = [t[:n_tiles, :N_EXPERTS].reshape(-1) for t in (n8, loff, gbase)]
        blk_expert = blk[:nb, 0]
        n_used = ends[0, N_EXPERTS - 1:N_EXPERTS]
        tails = ends[1:3, :N_EXPERTS].reshape(-1)
        xs = _dispatch_call(tables, tails, n_used, hf, lpos, gate, nb, xs)
        y_sorted = _expert_call(blk_expert, n_used, ends[0, :N_EXPERTS], xs, moe_w_gate_up, moe_b_gate_up[:, :, None, :],
                                moe_w_down, moe_b_down[:, :, None, :], l)
        if l == depth - 1:
            y_final = _combine_call(tables, x, lpos, mod_l, y_sorted, npc, final_norm_w[None])
        else:
            x = _combine_call(tables, x, lpos, mod_l, y_sorted, npc, None)[0]

    def even_states(which):
        cs = jnp.stack([e[which][0] for e in even_out])
        ns = jnp.stack([e[which][1][:, :, 0, :] for e in even_out])
        ms = jnp.stack([e[which][2][:, 0, :he] for e in even_out])
        cv = jnp.stack([e[which][3][:, SUBLANES - (CONV_W - 1):, :] for e in even_out])
        ss = jnp.stack([e[which][4] for e in even_out])
        return cs, ns, ms, cv, ss

    p_c, p_n, p_m, p_cv, p_s = even_states(0)
    s_c, s_n, s_m, s_cv, s_s = even_states(1)
    p_h = jnp.stack([o[0] for o in odd_out])
    s_h = jnp.stack([o[1] for o in odd_out])
    y_prompt = y_final[0].reshape(bp, seq, d)
    y_sample = y_final[1].reshape(bs, dseq, d)
    return (y_prompt, y_sample, p_c, p_n, p_m, p_cv, p_s, p_h, s_c, s_n, s_m, s_cv, s_s, s_h)
```
